```python
import math
import jax, jax.numpy as jnp
from jax import lax
import numpy as np

D_MODEL = 1024
BATCH = 4
SEQ = 8192
DEPTH = 1

D_MIX = D_MODEL
D_LRU = D_MIX // 2
D_HY = D_MIX - D_LRU
LRU_HEADS = 8
LRU_HEAD_DIM = D_LRU // LRU_HEADS
LRU_CONV = 4
LRU_C = 8.0
HY_ORDER = 2
HY_CONV = 3
HY_BANDS = 16
HY_EMB = 2 * HY_BANDS + 1
HY_FFN = 64
HY_FAST_DECAY_PCT = 0.3
HY_SLOW_DECAY_PCT = 1.5
HY_DECAY_TARGET = 1e-2
D_IN = 2 * D_LRU + (HY_ORDER + 1) * D_HY
N_GROUPS = 4
EXPERTS_PER_GROUP = 8
N_EXPERTS = N_GROUPS * EXPERTS_PER_GROUP
TOP_K = 2
D_EXPERT = D_MODEL // 2
MOE_BLOCK = 128
EPS = 1e-6

kernel_name = "hybrid_lru_hyena_hmoe_encoder"


def _rmsnorm(x, g):
    x32 = x.astype(jnp.float32)
    y = x32 * lax.rsqrt(jnp.mean(x32 * x32, axis=-1, keepdims=True) + EPS)
    return (y * g.astype(jnp.float32)).astype(x.dtype)


def _dwconv(x, w, b, pad):
    c = x.shape[-1]
    y = lax.conv_general_dilated(x, w[:, None, :].astype(x.dtype), window_strides=(1,), padding=[pad], dimension_numbers=("NWC", "WIO", "NWC"), feature_group_count=c)
    return y + b.astype(x.dtype)


def _lin_combine(c1, c2):
    a1, b1 = c1
    a2, b2 = c2
    return a1 * a2, a2 * b1 + b2


def _rg_lru(xc, wr, br, wi, bi, lam, reverse):
    bsz, seq, c = xc.shape
    xh = xc.reshape(bsz, seq, LRU_HEADS, LRU_HEAD_DIM)
    r = jax.nn.sigmoid((jnp.einsum("bshi,hij->bshj", xh, wr).reshape(bsz, seq, c) + br).astype(jnp.float32))
    i = jax.nn.sigmoid((jnp.einsum("bshi,hij->bshj", xh, wi).reshape(bsz, seq, c) + bi).astype(jnp.float32))
    log_a = -LRU_C * r * jax.nn.softplus(-lam.astype(jnp.float32))
    a = jnp.exp(log_a)
    b = jnp.sqrt(-jnp.expm1(2.0 * log_a)) * (i * xc.astype(jnp.float32))
    _, h = lax.associative_scan(_lin_combine, (a, b), axis=1, reverse=reverse)
    return h


def _hyena_filters(seq, fw1, fb1, ff1, fw2, fb2, ff2, fw3):
    f32 = jnp.float32
    pos = jnp.arange(seq, dtype=f32)
    t = jnp.linspace(0.0, 1.0, seq, dtype=f32)
    w = (2.0 * math.pi / seq) * pos
    bands = jnp.linspace(1e-4, HY_BANDS - 1, HY_BANDS, dtype=f32)
    ang = w[:, None] * bands[None, :]
    feats = jnp.concatenate([t[:, None], jnp.cos(ang), jnp.sin(ang)], axis=-1)
    z = jnp.sin(ff1.astype(f32) * (feats @ fw1.astype(f32) + fb1.astype(f32)))
    z = jnp.sin(ff2.astype(f32) * (z @ fw2.astype(f32) + fb2.astype(f32)))
    h = (z @ fw3.astype(f32)).reshape(seq, 2, HY_ORDER, D_HY)
    deltas = jnp.abs(jnp.linspace(math.log(HY_DECAY_TARGET) / HY_SLOW_DECAY_PCT, math.log(HY_DECAY_TARGET) / HY_FAST_DECAY_PCT, D_HY, dtype=f32))
    h = h * jnp.exp(-t[:, None, None, None] * deltas)
    hf, hb = h[:, 0], h[:, 1]
    k = jnp.concatenate([hf, jnp.zeros((1, HY_ORDER, D_HY), f32), hb[1:][::-1]], axis=0)
    k = k / (jnp.sum(jnp.abs(k), axis=0, keepdims=True) + EPS)
    return jnp.fft.rfft(k, axis=0)


def _long_conv(z, kf, skip):
    seq = z.shape[1]
    zf = jnp.fft.rfft(z, n=2 * seq, axis=1)
    y = jnp.fft.irfft(zf * kf[None], n=2 * seq, axis=1)[:, :seq]
    return y + z * skip


def _hyena(u, conv_w, conv_b, kf, skip):
    uc = _dwconv(u, conv_w, conv_b, (1, 1)).astype(jnp.float32)
    v, x1, x2 = uc[..., :D_HY], uc[..., D_HY:2 * D_HY], uc[..., 2 * D_HY:]
    skip = skip.astype(jnp.float32)
    z = x1 * _long_conv(v, kf[:, 0], skip[0])
    z = x2 * _long_conv(z, kf[:, 1], skip[1])
    return z


def _hier_moe(h, wg, bg, we, be, w1, w3, w2):
    bsz, seq, d = h.shape
    n_tok = bsz * seq
    hf = h.reshape(n_tok, d)
    lg = (hf @ wg + bg).astype(jnp.float32)
    pg = jax.nn.softmax(lg, axis=-1)
    g_idx = jnp.argmax(lg, axis=-1)
    g_p = jnp.take_along_axis(pg, g_idx[:, None], axis=1)[:, 0]
    le = (hf @ we + be).astype(jnp.float32).reshape(n_tok, N_GROUPS, EXPERTS_PER_GROUP)
    le = jnp.take_along_axis(le, g_idx[:, None, None], axis=1)[:, 0]
    pe = jax.nn.softmax(le, axis=-1)
    top_p, top_i = lax.top_k(pe, TOP_K)
    gate = g_p[:, None] * top_p / jnp.sum(top_p, axis=-1, keepdims=True)
    eid = (g_idx[:, None] * EXPERTS_PER_GROUP + top_i).astype(jnp.int32)
    n_assign = n_tok * TOP_K
    flat_e = eid.reshape(n_assign)
    order = jnp.argsort(flat_e)
    sorted_e = flat_e[order]
    sorted_tok = (order // TOP_K).astype(jnp.int32)
    sorted_gate = gate.reshape(n_assign)[order]
    counts = jnp.bincount(flat_e, length=N_EXPERTS).astype(jnp.int32)
    padded = (counts + MOE_BLOCK - 1) // MOE_BLOCK * MOE_BLOCK
    raw_start = jnp.cumsum(counts) - counts
    pad_end = jnp.cumsum(padded)
    pad_start = pad_end - padded
    dest = pad_start[sorted_e] + jnp.arange(n_assign, dtype=jnp.int32) - raw_start[sorted_e]
    n_blocks = (n_assign + MOE_BLOCK - 1) // MOE_BLOCK + N_EXPERTS
    n_slots = n_blocks * MOE_BLOCK
    slot_tok = jnp.full((n_slots,), n_tok, jnp.int32).at[dest].set(sorted_tok)
    block_e = jnp.minimum(jnp.searchsorted(pad_end, jnp.arange(n_blocks, dtype=jnp.int32) * MOE_BLOCK, side="right"), N_EXPERTS - 1)
    h_pad = jnp.concatenate([hf, jnp.zeros((1, d), hf.dtype)], axis=0)
    xb = h_pad[slot_tok].reshape(n_blocks, MOE_BLOCK, d)

    def expert_block(args):
        xblk, e = args
        a = jax.nn.silu(xblk @ w1[e]) * (xblk @ w3[e])
        return a @ w2[e]

    yb = lax.map(expert_block, (xb, block_e)).reshape(n_slots, d)
    y = jnp.zeros((n_tok, d), h.dtype).at[sorted_tok].add(yb[dest] * sorted_gate[:, None].astype(h.dtype))
    return y.reshape(bsz, seq, d)


def setup_inputs(seed: int = 0) -> dict:
    key = jax.random.key(seed)
    ks = iter(jax.random.split(key, 48))

    def nrm(shape, scale):
        return jax.random.normal(next(ks), shape, jnp.float32) * scale

    def gain(shape):
        return 1.0 + nrm(shape, 0.02)

    def lam_init():
        u = jax.random.uniform(next(ks), (DEPTH, D_LRU), jnp.float32, minval=0.9, maxval=0.999)
        a = u ** (1.0 / LRU_C)
        return jnp.log(a) - jnp.log1p(-a)

    hd = LRU_HEAD_DIM
    return {
        "x": nrm((BATCH, SEQ, D_MODEL), 1.0),
        "norm_mix_g": gain((DEPTH, D_MODEL)),
        "w_in": nrm((DEPTH, D_MODEL, D_IN), D_MODEL ** -0.5),
        "lru_conv_w": nrm((DEPTH, LRU_CONV, D_LRU), LRU_CONV ** -0.5),
        "lru_conv_b": nrm((DEPTH, D_LRU), 0.01),
        "lru_wr_f": nrm((DEPTH, LRU_HEADS, hd, hd), hd ** -0.5),
        "lru_br_f": nrm((DEPTH, D_LRU), 0.01),
        "lru_wi_f": nrm((DEPTH, LRU_HEADS, hd, hd), hd ** -0.5),
        "lru_bi_f": nrm((DEPTH, D_LRU), 0.01),
        "lru_lambda_f": lam_init(),
        "lru_wr_b": nrm((DEPTH, LRU_HEADS, hd, hd), hd ** -0.5),
        "lru_br_b": nrm((DEPTH, D_LRU), 0.01),
        "lru_wi_b": nrm((DEPTH, LRU_HEADS, hd, hd), hd ** -0.5),
        "lru_bi_b": nrm((DEPTH, D_LRU), 0.01),
        "lru_lambda_b": lam_init(),
        "hy_conv_w": nrm((DEPTH, HY_CONV, (HY_ORDER + 1) * D_HY), HY_CONV ** -0.5),
        "hy_conv_b": nrm((DEPTH, (HY_ORDER + 1) * D_HY), 0.01),
        "hy_filt_w1": nrm((DEPTH, HY_EMB, HY_FFN), HY_EMB ** -0.5),
        "hy_filt_b1": nrm((DEPTH, HY_FFN), 0.1),
        "hy_filt_freq1": gain((DEPTH, HY_FFN)),
        "hy_filt_w2": nrm((DEPTH, HY_FFN, HY_FFN), HY_FFN ** -0.5),
        "hy_filt_b2": nrm((DEPTH, HY_FFN), 0.1),
        "hy_filt_freq2": gain((DEPTH, HY_FFN)),
        "hy_filt_w3": nrm((DEPTH, HY_FFN, 2 * HY_ORDER * D_HY), HY_FFN ** -0.5),
        "hy_skip": nrm((DEPTH, HY_ORDER, D_HY), 1.0),
        "grp_norm_lru_g": gain((DEPTH, D_LRU)),
        "grp_norm_hy_g": gain((DEPTH, D_HY)),
        "w_out": nrm((DEPTH, D_MIX, D_MODEL), D_MIX ** -0.5),
        "norm_ffn_g": gain((DEPTH, D_MODEL)),
        "router_group_w": nrm((DEPTH, D_MODEL, N_GROUPS), D_MODEL ** -0.5),
        "router_group_b": nrm((DEPTH, N_GROUPS), 0.01),
        "router_expert_w": nrm((DEPTH, D_MODEL, N_EXPERTS), D_MODEL ** -0.5),
        "router_expert_b": nrm((DEPTH, N_EXPERTS), 0.01),
        "exp_w1": nrm((DEPTH, N_EXPERTS, D_MODEL, D_EXPERT), D_MODEL ** -0.5),
        "exp_w3": nrm((DEPTH, N_EXPERTS, D_MODEL, D_EXPERT), D_MODEL ** -0.5),
        "exp_w2": nrm((DEPTH, N_EXPERTS, D_EXPERT, D_MODEL), D_EXPERT ** -0.5),
        "norm_final_g": gain((D_MODEL,)),
    }


def reference(x, norm_mix_g, w_in, lru_conv_w, lru_conv_b, lru_wr_f, lru_br_f, lru_wi_f, lru_bi_f, lru_lambda_f, lru_wr_b, lru_br_b, lru_wi_b, lru_bi_b, lru_lambda_b, hy_conv_w, hy_conv_b, hy_filt_w1, hy_filt_b1, hy_filt_freq1, hy_filt_w2, hy_filt_b2, hy_filt_freq2, hy_filt_w3, hy_skip, grp_norm_lru_g, grp_norm_hy_g, w_out, norm_ffn_g, router_group_w, router_group_b, router_expert_w, router_expert_b, exp_w1, exp_w3, exp_w2, norm_final_g):
    seq = x.shape[1]
    for l in range(DEPTH):
        hn = _rmsnorm(x, norm_mix_g[l])
        proj = hn @ w_in[l]
        lru_x = proj[..., :D_LRU]
        lru_gate = proj[..., D_LRU:2 * D_LRU]
        hy_u = proj[..., 2 * D_LRU:]
        xc = _dwconv(lru_x, lru_conv_w[l], lru_conv_b[l], (2, 1))
        h_lru = _rg_lru(xc, lru_wr_f[l], lru_br_f[l], lru_wi_f[l], lru_bi_f[l], lru_lambda_f[l], False) + _rg_lru(xc, lru_wr_b[l], lru_br_b[l], lru_wi_b[l], lru_bi_b[l], lru_lambda_b[l], True)
        y_lru = (h_lru * jax.nn.gelu(lru_gate.astype(jnp.float32))).astype(x.dtype)
        kf = _hyena_filters(seq, hy_filt_w1[l], hy_filt_b1[l], hy_filt_freq1[l], hy_filt_w2[l], hy_filt_b2[l], hy_filt_freq2[l], hy_filt_w3[l])
        y_hy = _hyena(hy_u, hy_conv_w[l], hy_conv_b[l], kf, hy_skip[l]).astype(x.dtype)
        y_cat = jnp.concatenate([_rmsnorm(y_lru, grp_norm_lru_g[l]), _rmsnorm(y_hy, grp_norm_hy_g[l])], axis=-1)
        x = x + y_cat @ w_out[l]
        x = x + _hier_moe(_rmsnorm(x, norm_ffn_g[l]), router_group_w[l], router_group_b[l], router_expert_w[l], router_expert_b[l], exp_w1[l], exp_w3[l], exp_w2[l])
    return _rmsnorm(x, norm_final_g)
```

```python
import functools
import math

import numpy as np
import jax
import jax.numpy as jnp
from jax import lax
from jax.experimental import pallas as pl
from jax.experimental.pallas import tpu as pltpu

F32 = jnp.float32
BF16 = jnp.bfloat16
I32 = jnp.int32

D_MODEL = 1024
D_LRU = 512
D_HY = 512
LRU_HEADS = 8
LRU_HEAD_DIM = D_LRU // LRU_HEADS
LRU_CONV = 4
LRU_C = 8.0
HY_ORDER = 2
HY_CONV = 3
HY_BANDS = 16
HY_EMB = 2 * HY_BANDS + 1
HY_EMB_PAD = 40
HY_FFN = 64
HY_FAST_DECAY_PCT = 0.3
HY_SLOW_DECAY_PCT = 1.5
HY_DECAY_TARGET = 1e-2
N_GROUPS = 4
EXPERTS_PER_GROUP = 8
N_EXPERTS = N_GROUPS * EXPERTS_PER_GROUP
D_EXPERT = D_MODEL // 2
EPS = 1e-6

LANES = 128
SUBLANES = 8
FFT_R = 128
VMEM_LIMIT = 56 * 1024 * 1024

TM_IN = 512
TC_LRU = 512
RC_HY = 16
RF_KF = 16
RB_FILT = 64
TM_MIX = 512
TQ_ROW = 256
MOE_BLK = 256
NEG_BIG = -1e30


def _params(sem, vmem=VMEM_LIMIT):
    return pltpu.CompilerParams(dimension_semantics=sem, vmem_limit_bytes=vmem)


def _rms(x, g):
    return x * lax.rsqrt(jnp.mean(x * x, axis=-1, keepdims=True) + EPS) * g


def _sigmoid(x):
    return 1.0 / (1.0 + jnp.exp(-x))


def _split_bf16(a):
    hi = a.astype(BF16)
    lo = (a - hi.astype(F32)).astype(BF16)
    return hi, lo


def _dot(a, b):
    return jnp.dot(a, b, preferred_element_type=F32)


def _dot3(a, b):
    ah, al = _split_bf16(a)
    bh, bl = _split_bf16(b)
    return _dot(ah, bh) + _dot(al, bh) + _dot(ah, bl)


def _inproj_body(x_ref, g_ref, wl_ref, wht_ref, lru_ref, hy_ref):
    hn = _rms(x_ref[0], g_ref[...]).astype(BF16)
    lru_ref[0] = _dot(hn, wl_ref[...])
    hyt = lax.dot_general(wht_ref[...], hn, (((1,), (1,)), ((), ())), preferred_element_type=F32)
    nrow = hyt.shape[0] // SUBLANES
    for j in range(hyt.shape[1] // LANES):
        hy_ref[0, :, j, :, :] = hyt[:, LANES * j:LANES * (j + 1)].reshape(nrow, SUBLANES, LANES)


def _inproj(x, g, wl, wht):
    bsz, seq, d = x.shape
    nl = wl.shape[1]
    nh = wht.shape[0]
    tm = TM_IN
    return pl.pallas_call(
        _inproj_body,
        grid=(bsz, seq // tm),
        in_specs=[
            pl.BlockSpec((1, tm, d), lambda b, i: (b, i, 0)),
            pl.BlockSpec((1, d), lambda b, i: (0, 0)),
            pl.BlockSpec((d, nl), lambda b, i: (0, 0)),
            pl.BlockSpec((nh, d), lambda b, i: (0, 0)),
        ],
        out_specs=[
            pl.BlockSpec((1, tm, nl), lambda b, i: (b, i, 0)),
            pl.BlockSpec((1, nh // SUBLANES, tm // LANES, SUBLANES, LANES), lambda b, i: (b, 0, i, 0, 0)),
        ],
        out_shape=[
            jax.ShapeDtypeStruct((bsz, seq, nl), F32),
            jax.ShapeDtypeStruct((bsz, nh // SUBLANES, seq // LANES, SUBLANES, LANES), F32),
        ],
        compiler_params=_params(("parallel", "parallel")),
        name="inproj",
    )(x, g, wl, wht)


def _neg_expm1(x):
    series = -x * (1.0 + x * (1.0 / 2.0) * (1.0 + x * (1.0 / 3.0) * (1.0 + x * (1.0 / 4.0) * (
        1.0 + x * (1.0 / 5.0) * (1.0 + x * (1.0 / 6.0))))))
    return jnp.where(x > -0.1, series, 1.0 - jnp.exp(x))


def _group_scan(a3, b3, reverse):
    sub = lax.broadcasted_iota(I32, a3.shape, 1)
    for s in (1, 2, 4):
        if reverse:
            a_sh = pltpu.roll(a3, SUBLANES - s, axis=1)
            b_sh = pltpu.roll(b3, SUBLANES - s, axis=1)
            m = sub < SUBLANES - s
        else:
            a_sh = pltpu.roll(a3, s, axis=1)
            b_sh = pltpu.roll(b3, s, axis=1)
            m = sub >= s
        b3 = jnp.where(m, a3 * b_sh + b3, b3)
        a3 = jnp.where(m, a3 * a_sh, a3)
    return a3, b3


def _lru_body(cur_ref, prev_ref, next_ref, cw_ref, cb_ref, wg_ref, br_ref, bi_ref, lam_ref, o_ref,
              a_sc, b_sc, carry_sc, *, nt, tc):
    d = pl.program_id(1)
    i = pl.program_id(2)
    c = jnp.where(d == 0, i, nt - 1 - i)

    @pl.when(i == 0)
    def _():
        carry_sc[...] = jnp.zeros_like(carry_sc)

    prev = jnp.where(c > 0, prev_ref[0], 0.0)
    nxt = jnp.where(c < nt - 1, next_ref[0], 0.0)
    xfull = jnp.concatenate([prev, cur_ref[0], nxt], axis=0)
    cw = cw_ref[...]
    xc = cb_ref[...] + cw[0:1] * xfull[6:6 + tc]
    for k in range(1, LRU_CONV):
        xc = xc + cw[k:k + 1] * xfull[6 + k:6 + k + tc]

    lam = lam_ref[0]
    nlam = -lam
    softplus = jnp.maximum(nlam, 0.0) + jnp.log1p(jnp.exp(-jnp.abs(nlam)))
    ng = tc // SUBLANES
    half = D_LRU // 2

    def gates(hh):
        sl = slice(half * hh, half * (hh + 1))
        xh = xc[:, sl]
        logits = _dot(xh.astype(BF16), wg_ref[0, hh])
        r = _sigmoid(logits[:, :half] + br_ref[0][:, sl])
        gi = _sigmoid(logits[:, half:] + bi_ref[0][:, sl])
        log_a = (-LRU_C) * r * softplus[:, sl]
        a = jnp.exp(log_a)
        b = jnp.sqrt(_neg_expm1(2.0 * log_a)) * (gi * xh)
        return a.reshape(ng, SUBLANES, half), b.reshape(ng, SUBLANES, half)

    def run(reverse):
        for hh in range(2):
            a3, b3 = gates(hh)
            a3, b3 = _group_scan(a3, b3, reverse)
            a_sc[:, :, half * hh:half * (hh + 1)] = a3
            b_sc[:, :, half * hh:half * (hh + 1)] = b3
        last = 0 if reverse else SUBLANES - 1

        def step(s, hc):
            g = (ng - 1 - s) if reverse else s
            h = a_sc[g] * hc + b_sc[g]
            o_ref[0, 0, pl.ds(pl.multiple_of(g * SUBLANES, SUBLANES), SUBLANES), :] = h
            return jnp.broadcast_to(h[last:last + 1, :], h.shape)

        hc = lax.fori_loop(0, ng, step, carry_sc[...], unroll=8)
        carry_sc[...] = hc

    @pl.when(d == 0)
    def _():
        run(False)

    @pl.when(d == 1)
    def _():
        run(True)


def _lru(lru, cw, cb, wg, br, bi, lam):
    bsz, seq, _ = lru.shape
    tc = TC_LRU
    nt = seq // tc
    r8 = tc // SUBLANES
    nrow8 = seq // SUBLANES

    def cidx(d, i):
        return jnp.where(d == 0, i, nt - 1 - i)

    return pl.pallas_call(
        functools.partial(_lru_body, nt=nt, tc=tc),
        grid=(bsz, 2, nt),
        in_specs=[
            pl.BlockSpec((1, tc, D_LRU), lambda b, d, i: (b, cidx(d, i), 0)),
            pl.BlockSpec((1, SUBLANES, D_LRU), lambda b, d, i: (b, jnp.maximum(cidx(d, i) * r8 - 1, 0), 0)),
            pl.BlockSpec((1, SUBLANES, D_LRU), lambda b, d, i: (b, jnp.minimum((cidx(d, i) + 1) * r8, nrow8 - 1), 0)),
            pl.BlockSpec((LRU_CONV, D_LRU), lambda b, d, i: (0, 0)),
            pl.BlockSpec((1, D_LRU), lambda b, d, i: (0, 0)),
            pl.BlockSpec((1, 2, D_LRU // 2, D_LRU), lambda b, d, i: (d, 0, 0, 0)),
            pl.BlockSpec((1, 1, D_LRU), lambda b, d, i: (d, 0, 0)),
            pl.BlockSpec((1, 1, D_LRU), lambda b, d, i: (d, 0, 0)),
            pl.BlockSpec((1, 1, D_LRU), lambda b, d, i: (d, 0, 0)),
        ],
        out_specs=pl.BlockSpec((1, 1, tc, D_LRU), lambda b, d, i: (d, b, cidx(d, i), 0)),
        out_shape=jax.ShapeDtypeStruct((2, bsz, seq, D_LRU), F32),
        scratch_shapes=[
            pltpu.VMEM((tc // SUBLANES, SUBLANES, D_LRU), F32),
            pltpu.VMEM((tc // SUBLANES, SUBLANES, D_LRU), F32),
            pltpu.VMEM((SUBLANES, D_LRU), F32),
        ],
        compiler_params=_params(("parallel", "arbitrary", "arbitrary")),
        name="rg_lru",
    )(lru, lru, lru, cw, cb, wg, br, bi, lam)


def _fft_tables():
    r = FFT_R
    n = r * r
    idx = np.arange(r, dtype=np.float64)
    ang = 2.0 * np.pi * np.outer(idx, idx) / r
    wr, wi = np.cos(ang), -np.sin(ang)
    angt = 2.0 * np.pi * np.outer(idx, idx) / n
    tr, ti = np.cos(angt), -np.sin(angt)
    f1 = np.concatenate([wr, wi], axis=1)
    f2 = np.block([[wr, wi], [-wi, wr]])
    f2c = np.block([[wr, -wi], [wi, wr]])
    f3 = np.concatenate([wr, wi], axis=0) / n
    as_bf16 = lambda a: jnp.asarray(a, F32).astype(BF16)
    return (as_bf16(f1), as_bf16(f2), as_bf16(f2c), as_bf16(f3), jnp.asarray(tr, F32), jnp.asarray(ti, F32))


def _table_specs(nargs_fn):
    r = FFT_R
    shapes = [(r, 2 * r), (2 * r, 2 * r), (2 * r, 2 * r), (2 * r, r), (r, r), (r, r)]
    return [pl.BlockSpec(s, nargs_fn(len(s))) for s in shapes]


def _fft_fwd(x3, f1, f2, tr, ti):
    g = x3.shape[0]
    r = FFT_R
    xt = jnp.swapaxes(x3, 1, 2).reshape(g * r, r)
    a = _dot(xt.astype(BF16), f1)
    ar = a[:, :r].reshape(g, r, r)
    ai = a[:, r:].reshape(g, r, r)
    pr = ar * tr - ai * ti
    pi = ar * ti + ai * tr
    pt = jnp.concatenate([jnp.swapaxes(pr, 1, 2), jnp.swapaxes(pi, 1, 2)], axis=-1).reshape(g * r, 2 * r)
    x = _dot(pt.astype(BF16), f2)
    return x[:, :r], x[:, r:]


def _fft_inv(yr, yi, f2c, f3, tr, ti):
    r = FFT_R
    g = yr.shape[0] // r
    b = _dot(jnp.concatenate([yr, yi], axis=-1).astype(BF16), f2c)
    br = b[:, :r].reshape(g, r, r)
    bi = b[:, r:].reshape(g, r, r)
    cr = br * tr + bi * ti
    ci = bi * tr - br * ti
    ct = jnp.concatenate([jnp.swapaxes(cr, 1, 2), jnp.swapaxes(ci, 1, 2)], axis=-1).reshape(g * r, 2 * r)
    yt = _dot(ct.astype(BF16), f3)
    return jnp.swapaxes(yt.reshape(g, r, r), 1, 2)


def _filt_mlp_body(ft_ref, w1_ref, b1_ref, f1_ref, w2_ref, b2_ref, f2_ref, o_ref):
    z = jnp.sin(f1_ref[...] * (_dot3(w1_ref[...], ft_ref[0]) + b1_ref[...]))
    o_ref[0] = jnp.sin(f2_ref[...] * (_dot3(w2_ref[...], z) + b2_ref[...]))


def _filt_mlp(featst, w1t, b1, f1, w2t, b2, f2):
    _, ke, seq = featst.shape
    col = lambda d: (0, 0)
    return pl.pallas_call(
        _filt_mlp_body,
        grid=(2,),
        in_specs=[
            pl.BlockSpec((1, ke, seq), lambda d: (d, 0, 0)),
            pl.BlockSpec((HY_FFN, ke), col),
            pl.BlockSpec((HY_FFN, 1), col),
            pl.BlockSpec((HY_FFN, 1), col),
            pl.BlockSpec((HY_FFN, HY_FFN), col),
            pl.BlockSpec((HY_FFN, 1), col),
            pl.BlockSpec((HY_FFN, 1), col),
        ],
        out_specs=pl.BlockSpec((1, HY_FFN, seq), lambda d: (d, 0, 0)),
        out_shape=jax.ShapeDtypeStruct((2, HY_FFN, seq), F32),
        compiler_params=_params(("parallel",)),
        name="hyena_filter_mlp",
    )(featst, w1t, b1, f1, w2t, b2, f2)


def _filt_time_body(z_ref, wf_ref, wb_ref, dl_ref, tf_ref, tb_ref, o_ref):
    seq = z_ref.shape[2]
    hf = _dot3(wf_ref[...], z_ref[0]) * jnp.exp(-tf_ref[...] * dl_ref[...])
    hb = _dot3(wb_ref[...], z_ref[1]) * jnp.exp(-tb_ref[...] * dl_ref[...])
    lane = lax.broadcasted_iota(I32, hb.shape, 1)
    hb = jnp.where(lane == 0, 0.0, hb)
    norm = jnp.sum(jnp.abs(hf), axis=-1, keepdims=True) + jnp.sum(jnp.abs(hb), axis=-1, keepdims=True) + EPS
    hf = hf / norm
    hb = hb / norm
    nrow = hf.shape[0] // SUBLANES
    nch = seq // LANES
    for j in range(nch):
        o_ref[:, j, :, :] = hf[:, LANES * j:LANES * (j + 1)].reshape(nrow, SUBLANES, LANES)
        o_ref[:, nch + j, :, :] = hb[:, LANES * j:LANES * (j + 1)].reshape(nrow, SUBLANES, LANES)


def _filt_time(z2t, w3f, w3b, delta_rows, t_f, t_b):
    nrows = w3f.shape[0]
    seq = z2t.shape[2]
    rb = RB_FILT
    return pl.pallas_call(
        _filt_time_body,
        grid=(nrows // rb,),
        in_specs=[
            pl.BlockSpec((2, HY_FFN, seq), lambda r: (0, 0, 0)),
            pl.BlockSpec((rb, HY_FFN), lambda r: (r, 0)),
            pl.BlockSpec((rb, HY_FFN), lambda r: (r, 0)),
            pl.BlockSpec((rb, 1), lambda r: (r, 0)),
            pl.BlockSpec((1, seq), lambda r: (0, 0)),
            pl.BlockSpec((1, seq), lambda r: (0, 0)),
        ],
        out_specs=pl.BlockSpec((rb // SUBLANES, 2 * seq // LANES, SUBLANES, LANES), lambda r: (r, 0, 0, 0)),
        out_shape=jax.ShapeDtypeStruct((nrows // SUBLANES, 2 * seq // LANES, SUBLANES, LANES), F32),
        compiler_params=_params(("parallel",)),
        name="hyena_filter_time",
    )(z2t, w3f, w3b, delta_rows, t_f, t_b)


def _filt_fft_body(k_ref, f1_ref, f2_ref, f2c_ref, f3_ref, tr_ref, ti_ref, kr_ref, ki_ref):
    tr = tr_ref[...]
    ti = ti_ref[...]
    for g in range(k_ref.shape[0]):
        x3 = jnp.stack([k_ref[g, :, ci, :] for ci in range(SUBLANES)], axis=0)
        xr, xi = _fft_fwd(x3, f1_ref[...], f2_ref[...], tr, ti)
        kr_ref[SUBLANES * g:SUBLANES * (g + 1)] = xr.reshape(SUBLANES, FFT_R, FFT_R)
        ki_ref[SUBLANES * g:SUBLANES * (g + 1)] = xi.reshape(SUBLANES, FFT_R, FFT_R)


def _filt_fft(ktile, tables):
    n8, r, _, _ = ktile.shape
    nrows = n8 * SUBLANES
    rf = RF_KF
    out_spec = pl.BlockSpec((rf, r, r), lambda i: (i, 0, 0))
    return pl.pallas_call(
        _filt_fft_body,
        grid=(nrows // rf,),
        in_specs=[pl.BlockSpec((rf // SUBLANES, r, SUBLANES, LANES), lambda i: (i, 0, 0, 0))]
        + _table_specs(lambda nd: (lambda i: (0,) * nd)),
        out_specs=[out_spec, out_spec],
        out_shape=[jax.ShapeDtypeStruct((nrows, r, r), F32)] * 2,
        compiler_params=_params(("parallel",)),
        name="hyena_filter_fft",
    )(ktile, *tables)


def _conv3_time(x, w0, w1, w2, bias):
    row = lax.broadcasted_iota(I32, x.shape, 0)
    lane = lax.broadcasted_iota(I32, x.shape, 1)
    nrow = x.shape[0]
    xm = pltpu.roll(x, 1, axis=1)
    xm_up = pltpu.roll(xm, 1, axis=0)
    xm = jnp.where(lane == 0, jnp.where(row == 0, 0.0, xm_up), xm)
    xp = pltpu.roll(x, LANES - 1, axis=1)
    xp_dn = pltpu.roll(xp, nrow - 1, axis=0)
    xp = jnp.where(lane == LANES - 1, jnp.where(row == nrow - 1, 0.0, xp_dn), xp)
    return w0 * xm + w1 * x + w2 * xp + bias


def _hyena_body(cw_ref, cb_ref, sk_ref, v_ref, x1_ref, x2_ref, k0r_ref, k0i_ref, k1r_ref, k1i_ref,
                f1_ref, f2_ref, f2c_ref, f3_ref, tr_ref, ti_ref, o_ref, *, rc):
    ct = pl.program_id(0)
    tr = tr_ref[...]
    ti = ti_ref[...]
    np_ = v_ref.shape[2]

    def long_conv(z3, kr_ref, ki_ref, g):
        zpad = jnp.concatenate([z3, jnp.zeros_like(z3)], axis=1)
        xr, xi = _fft_fwd(zpad, f1_ref[...], f2_ref[...], tr, ti)
        kr = kr_ref[pl.ds(g * SUBLANES, SUBLANES)].reshape(SUBLANES * FFT_R, FFT_R)
        ki = ki_ref[pl.ds(g * SUBLANES, SUBLANES)].reshape(SUBLANES * FFT_R, FFT_R)
        yr = xr * kr - xi * ki
        yi = xr * ki + xi * kr
        y3 = _fft_inv(yr, yi, f2c_ref[...], f3_ref[...], tr, ti)
        return y3[:, :np_, :]

    def group(g, carry):
        cbase = ct * rc + g * SUBLANES
        vs, sv, x1s, x2s, s1 = [], [], [], [], []
        for ci in range(SUBLANES):
            ch = cbase + ci

            def conv(ref, off):
                c = off + ch
                return _conv3_time(ref[0, g, :, ci, :], cw_ref[0, c], cw_ref[1, c], cw_ref[2, c], cb_ref[c])

            v = conv(v_ref, 0)
            vs.append(v)
            sv.append(sk_ref[0, ch] * v)
            x1s.append(conv(x1_ref, D_HY))
            x2s.append(conv(x2_ref, 2 * D_HY))
            s1.append(jnp.full(v.shape, sk_ref[1, ch], F32))
        v3 = jnp.stack(vs, axis=0)
        z1 = jnp.stack(x1s, axis=0) * (long_conv(v3, k0r_ref, k0i_ref, g) + jnp.stack(sv, axis=0))
        out = jnp.stack(x2s, axis=0) * (long_conv(z1, k1r_ref, k1i_ref, g) + jnp.stack(s1, axis=0) * z1)
        for ci in range(SUBLANES):
            o_ref[0, g, :, ci, :] = out[ci]
        return carry

    lax.fori_loop(0, rc // SUBLANES, group, 0)


def _hyena(hy5, cw, cb, skip, kfr, kfi, tables):
    bsz, n8, np_, _, _ = hy5.shape
    rc = RC_HY
    r8 = rc // SUBLANES
    nct = D_HY // rc
    r = FFT_R
    smem = pl.BlockSpec(memory_space=pltpu.SMEM)

    def xspec(off):
        return pl.BlockSpec((1, r8, np_, SUBLANES, LANES), lambda c, b: (b, c + off * nct, 0, 0, 0))

    def kspec(order):
        return pl.BlockSpec((rc, r, r), lambda c, b: (c + order * nct, 0, 0))

    return pl.pallas_call(
        functools.partial(_hyena_body, rc=rc),
        grid=(nct, bsz),
        in_specs=[smem, smem, smem, xspec(0), xspec(1), xspec(2), kspec(0), kspec(0), kspec(1), kspec(1)]
        + _table_specs(lambda nd: (lambda c, b: (0,) * nd)),
        out_specs=pl.BlockSpec((1, r8, np_, SUBLANES, LANES), lambda c, b: (b, c, 0, 0, 0)),
        out_shape=jax.ShapeDtypeStruct((bsz, D_HY // SUBLANES, np_, SUBLANES, LANES), F32),
        compiler_params=_params(("parallel", "parallel")),
        name="hyena",
    )(cw, cb, skip, hy5, hy5, hy5, kfr, kfi, kfr, kfi, *tables)


def _gelu_tanh(x):
    return 0.5 * x * (1.0 + jnp.tanh(math.sqrt(2.0 / math.pi) * (x + 0.044715 * (x * x * x))))


def _mix_body(hf_ref, hb_ref, gate_ref, yhy_ref, x_ref, gl_ref, gh_ref, wl_ref, wh_ref, gffn_ref,
              wrh_ref, wrl_ref, rb_ref, tri_ref, xmid_ref, h_ref, route_ref, cnt_ref, carry_sc):
    first = jnp.logical_and(pl.program_id(0) == 0, pl.program_id(1) == 0)

    @pl.when(first)
    def _():
        carry_sc[...] = jnp.zeros_like(carry_sc)

    tm = x_ref.shape[1]
    y_lru = (hf_ref[0, 0] + hb_ref[0, 0]) * _gelu_tanh(gate_ref[0])
    nl = _rms(y_lru, gl_ref[...]).astype(BF16)
    gh = gh_ref[...]
    for j in range(tm // LANES):
        yt = yhy_ref[0, :, j, :, :].reshape(D_HY, LANES)
        nt = (yt * lax.rsqrt(jnp.mean(yt * yt, axis=0, keepdims=True) + EPS) * gh).astype(BF16)
        rows = slice(LANES * j, LANES * (j + 1))
        mix = _dot(nl[rows], wl_ref[...]) + lax.dot_general(
            nt, wh_ref[...], (((0,), (0,)), ((), ())), preferred_element_type=F32)
        xmid_ref[0, rows, :] = x_ref[0, rows, :] + mix
    h = _rms(xmid_ref[0], gffn_ref[...])
    h_ref[...] = h

    hh, hl = _split_bf16(h)
    logits = _dot(hh, wrh_ref[...]) + _dot(hl, wrh_ref[...]) + _dot(hh, wrl_ref[...]) + rb_ref[...]
    lane = lax.broadcasted_iota(I32, logits.shape, 1)
    big = jnp.int32(1 << 20)
    is_g = lane < N_GROUPS
    lg = jnp.where(is_g, logits, NEG_BIG)
    mg = jnp.max(lg, axis=-1, keepdims=True)
    gidx = jnp.min(jnp.where(lg == mg, lane, big), axis=-1, keepdims=True)
    g_p = 1.0 / jnp.sum(jnp.where(is_g, jnp.exp(lg - mg), 0.0), axis=-1, keepdims=True)
    e_lo = N_GROUPS + EXPERTS_PER_GROUP * gidx
    is_e = jnp.logical_and(lane >= e_lo, lane < e_lo + EXPERTS_PER_GROUP)
    le = jnp.where(is_e, logits, NEG_BIG)
    m1 = jnp.max(le, axis=-1, keepdims=True)
    i1 = jnp.min(jnp.where(le == m1, lane, big), axis=-1, keepdims=True)
    le2 = jnp.where(lane == i1, NEG_BIG, le)
    m2 = jnp.max(le2, axis=-1, keepdims=True)
    i2 = jnp.min(jnp.where(le2 == m2, lane, big), axis=-1, keepdims=True)
    ratio = jnp.exp(m2 - m1)
    gate1 = g_p / (1.0 + ratio)
    gate2 = g_p * ratio / (1.0 + ratio)
    oh1 = lane == i1
    oh2 = lane == i2
    oh = jnp.where(oh1, 1.0, jnp.where(oh2, 1.0, 0.0))
    tot = carry_sc[...] + _dot(tri_ref[...], oh.astype(BF16))
    rank1 = jnp.sum(jnp.where(oh1, tot, 0.0), axis=-1, keepdims=True)
    rank2 = jnp.sum(jnp.where(oh2, tot, 0.0), axis=-1, keepdims=True)
    carry_sc[...] = carry_sc[...] + jnp.sum(oh, axis=0, keepdims=True)
    cnt_ref[...] = carry_sc[...]
    cols = [(i1 - N_GROUPS).astype(F32), (i2 - N_GROUPS).astype(F32), rank1, rank2, gate1, gate2]
    route = jnp.zeros(logits.shape, F32)
    for k, col in enumerate(cols):
        route = jnp.where(lane == k, col, route)
    route_ref[...] = route[:, :SUBLANES]


def _mix_router(h4, lru, yhy5, x, gl, gh, wl, wh, gffn, wrh, wrl, rb, tri):
    bsz, seq, d = x.shape
    tm = TM_MIX
    nt = seq // tm
    n_tok = bsz * seq
    c2 = lambda b, i: (0, 0)
    return pl.pallas_call(
        _mix_body,
        grid=(bsz, nt),
        in_specs=[
            pl.BlockSpec((1, 1, tm, D_LRU), lambda b, i: (0, b, i, 0)),
            pl.BlockSpec((1, 1, tm, D_LRU), lambda b, i: (1, b, i, 0)),
            pl.BlockSpec((1, tm, D_LRU), lambda b, i: (b, i, 1)),
            pl.BlockSpec((1, D_HY // SUBLANES, tm // LANES, SUBLANES, LANES), lambda b, i: (b, 0, i, 0, 0)),
            pl.BlockSpec((1, tm, d), lambda b, i: (b, i, 0)),
            pl.BlockSpec((1, D_LRU), c2),
            pl.BlockSpec((D_HY, 1), c2),
            pl.BlockSpec((D_LRU, d), c2),
            pl.BlockSpec((D_HY, d), c2),
            pl.BlockSpec((1, d), c2),
            pl.BlockSpec((d, LANES), c2),
            pl.BlockSpec((d, LANES), c2),
            pl.BlockSpec((1, LANES), c2),
            pl.BlockSpec((tm, tm), c2),
        ],
        out_specs=[
            pl.BlockSpec((1, tm, d), lambda b, i: (b, i, 0)),
            pl.BlockSpec((tm, d), lambda b, i: (b * nt + i, 0)),
            pl.BlockSpec((tm, SUBLANES), lambda b, i: (b * nt + i, 0)),
            pl.BlockSpec((1, LANES), c2),
        ],
        out_shape=[
            jax.ShapeDtypeStruct((bsz, seq, d), F32),
            jax.ShapeDtypeStruct((n_tok, d), F32),
            jax.ShapeDtypeStruct((n_tok, SUBLANES), F32),
            jax.ShapeDtypeStruct((1, LANES), F32),
        ],
        scratch_shapes=[pltpu.VMEM((1, LANES), F32)],
        compiler_params=_params(("arbitrary", "arbitrary")),
        name="mix_router",
    )(h4, h4, lru, yhy5, x, gl, gh, wl, wh, gffn, wrh, wrl, rb, tri)


def _row_copy_wait(src_ref, dst_ref, sem, n):
    def wait(t, c):
        pltpu.make_async_copy(src_ref.at[pl.ds(0, 1)], dst_ref.at[pl.ds(0, 1)], sem).wait()
        return c

    lax.fori_loop(0, n, wait, 0)


def _dispatch_body(d0_ref, d1_ref, h_ref, xin_ref, xb_ref, sem):
    del xin_ref
    tq = h_ref.shape[0]

    def issue(t, c):
        src = h_ref.at[pl.ds(t, 1)]
        pltpu.make_async_copy(src, xb_ref.at[pl.ds(d0_ref[0, 0, t], 1)], sem).start()
        pltpu.make_async_copy(src, xb_ref.at[pl.ds(d1_ref[0, 0, t], 1)], sem).start()
        return c

    lax.fori_loop(0, tq, issue, 0)
    _row_copy_wait(h_ref, xb_ref, sem, 2 * tq)


def _dispatch(dest0, dest1, h, xb_init):
    n_tok, d = h.shape
    tq = TQ_ROW
    sidx = pl.BlockSpec((1, 1, tq), lambda i: (i, 0, 0), memory_space=pltpu.SMEM)
    return pl.pallas_call(
        _dispatch_body,
        grid=(n_tok // tq,),
        in_specs=[sidx, sidx, pl.BlockSpec((tq, d), lambda i: (i, 0)), pl.BlockSpec(memory_space=pl.ANY)],
        out_specs=pl.BlockSpec(memory_space=pl.ANY),
        out_shape=jax.ShapeDtypeStruct(xb_init.shape, F32),
        scratch_shapes=[pltpu.SemaphoreType.DMA(())],
        input_output_aliases={3: 0},
        compiler_params=_params(("arbitrary",)),
        name="moe_dispatch",
    )(dest0, dest1, h, xb_init)


def _expert_body(be_ref, fl_ref, nu_ref, xb_ref, w1_ref, w3_ref, w2_ref, yb_ref, w1_sc, w3_sc, w2_sc):
    del be_ref
    j = pl.program_id(0)
    used = j < nu_ref[0]

    @pl.when(jnp.logical_and(used, fl_ref[j] == 1))
    def _():
        w1_sc[...] = w1_ref[...].astype(BF16)
        w3_sc[...] = w3_ref[...].astype(BF16)
        w2_sc[...] = w2_ref[...].astype(BF16)

    @pl.when(used)
    def _():
        xb = xb_ref[...].astype(BF16)
        a = _dot(xb, w1_sc[...])
        b = _dot(xb, w3_sc[...])
        act = (a * _sigmoid(a) * b).astype(BF16)
        yb_ref[...] = _dot(act, w2_sc[...])

    @pl.when(jnp.logical_not(used))
    def _():
        yb_ref[...] = jnp.zeros_like(yb_ref)


def _experts(block_e, first_flag, n_used, xb, w1, w3, w2):
    n_slots, d = xb.shape
    blk = MOE_BLK
    de = w1.shape[2]
    grid_spec = pltpu.PrefetchScalarGridSpec(
        num_scalar_prefetch=3,
        grid=(n_slots // blk,),
        in_specs=[
            pl.BlockSpec((blk, d), lambda j, be, fl, nu: (j, 0)),
            pl.BlockSpec((None, d, de), lambda j, be, fl, nu: (be[j], 0, 0)),
            pl.BlockSpec((None, d, de), lambda j, be, fl, nu: (be[j], 0, 0)),
            pl.BlockSpec((None, de, d), lambda j, be, fl, nu: (be[j], 0, 0)),
        ],
        out_specs=pl.BlockSpec((blk, d), lambda j, be, fl, nu: (j, 0)),
        scratch_shapes=[pltpu.VMEM((d, de), BF16), pltpu.VMEM((d, de), BF16), pltpu.VMEM((de, d), BF16)],
    )
    return pl.pallas_call(
        _expert_body,
        grid_spec=grid_spec,
        out_shape=jax.ShapeDtypeStruct((n_slots, d), F32),
        compiler_params=_params(("arbitrary",)),
        name="moe_experts",
    )(block_e, first_flag, n_used, xb, w1, w3, w2)


def _combine_body(d0_ref, d1_ref, xmid_ref, route_ref, g_ref, yb_ref, o_ref, buf, sem):
    tq = xmid_ref.shape[0]

    def issue(t, c):
        pltpu.make_async_copy(yb_ref.at[pl.ds(d0_ref[0, 0, t], 1)], buf.at[0, pl.ds(t, 1)], sem).start()
        pltpu.make_async_copy(yb_ref.at[pl.ds(d1_ref[0, 0, t], 1)], buf.at[1, pl.ds(t, 1)], sem).start()
        return c

    lax.fori_loop(0, tq, issue, 0)
    _row_copy_wait(yb_ref, buf.at[0], sem, 2 * tq)
    route = route_ref[...]
    y = xmid_ref[...] + route[:, 4:5] * buf[0] + route[:, 5:6] * buf[1]
    o_ref[...] = _rms(y, g_ref[...])


def _combine(dest0, dest1, xmid, route, g, yb):
    n_tok, d = xmid.shape
    tq = TQ_ROW
    sidx = pl.BlockSpec((1, 1, tq), lambda i: (i, 0, 0), memory_space=pltpu.SMEM)
    return pl.pallas_call(
        _combine_body,
        grid=(n_tok // tq,),
        in_specs=[
            sidx, sidx,
            pl.BlockSpec((tq, d), lambda i: (i, 0)),
            pl.BlockSpec((tq, SUBLANES), lambda i: (i, 0)),
            pl.BlockSpec((1, d), lambda i: (0, 0)),
            pl.BlockSpec(memory_space=pl.ANY),
        ],
        out_specs=pl.BlockSpec((tq, d), lambda i: (i, 0)),
        out_shape=jax.ShapeDtypeStruct((n_tok, d), F32),
        scratch_shapes=[pltpu.VMEM((2, tq, d), F32), pltpu.SemaphoreType.DMA(())],
        compiler_params=_params(("arbitrary",)),
        name="moe_combine",
    )(dest0, dest1, xmid, route, g, yb)


def _filter_features(seq):
    pos = jnp.arange(seq, dtype=F32)
    t = jnp.linspace(0.0, 1.0, seq, dtype=F32)
    w = (2.0 * math.pi / seq) * pos
    bands = jnp.linspace(1e-4, HY_BANDS - 1, HY_BANDS, dtype=F32)
    ang = w[:, None] * bands[None, :]
    feats = jnp.concatenate([t[:, None], jnp.cos(ang), jnp.sin(ang)], axis=-1)
    rev = lambda a: jnp.concatenate([jnp.zeros_like(a[:1]), a[1:][::-1]], axis=0)
    pad = lambda a: jnp.pad(a, ((0, 0), (0, HY_EMB_PAD - HY_EMB)))
    featst = jnp.stack([pad(feats).T, pad(rev(feats)).T], axis=0)
    return featst, t[None, :], rev(t[:, None]).T


def _gate_weights(wr, wi):
    half = D_LRU // 2
    hph = half // LRU_HEAD_DIM
    out = []
    for hh in range(2):
        bd_r = jax.scipy.linalg.block_diag(*[wr[hh * hph + k] for k in range(hph)])
        bd_i = jax.scipy.linalg.block_diag(*[wi[hh * hph + k] for k in range(hph)])
        out.append(jnp.concatenate([bd_r, bd_i], axis=1))
    return jnp.stack(out, axis=0).astype(BF16)


def _layer(x, l, p):
    bsz, seq, d = x.shape
    n_tok = bsz * seq
    row = lambda a: a.reshape(1, -1).astype(F32)
    col = lambda a: a.reshape(-1, 1).astype(F32)

    w_in = p["w_in"][l]
    wl = w_in[:, :2 * D_LRU].astype(BF16)
    wht = w_in[:, 2 * D_LRU:].T.astype(BF16)
    lru, hy5 = _inproj(x, row(p["norm_mix_g"][l]), wl, wht)

    wg = jnp.stack([_gate_weights(p["lru_wr_f"][l], p["lru_wi_f"][l]),
                    _gate_weights(p["lru_wr_b"][l], p["lru_wi_b"][l])], axis=0)
    stack2 = lambda a, b: jnp.stack([a.reshape(1, -1), b.reshape(1, -1)], axis=0).astype(F32)
    h4 = _lru(lru, p["lru_conv_w"][l], row(p["lru_conv_b"][l]), wg,
              stack2(p["lru_br_f"][l], p["lru_br_b"][l]), stack2(p["lru_bi_f"][l], p["lru_bi_b"][l]),
              stack2(p["lru_lambda_f"][l], p["lru_lambda_b"][l]))

    tables = _fft_tables()
    featst, t_f, t_b = _filter_features(seq)
    w1t = jnp.pad(p["hy_filt_w1"][l].T, ((0, 0), (0, HY_EMB_PAD - HY_EMB)))
    z2t = _filt_mlp(featst, w1t, col(p["hy_filt_b1"][l]), col(p["hy_filt_freq1"][l]),
                    p["hy_filt_w2"][l].T, col(p["hy_filt_b2"][l]), col(p["hy_filt_freq2"][l]))
    w3 = p["hy_filt_w3"][l]
    nfr = HY_ORDER * D_HY
    deltas = jnp.abs(jnp.linspace(math.log(HY_DECAY_TARGET) / HY_SLOW_DECAY_PCT,
                                  math.log(HY_DECAY_TARGET) / HY_FAST_DECAY_PCT, D_HY, dtype=F32))
    ktile = _filt_time(z2t, w3[:, :nfr].T, w3[:, nfr:].T, col(jnp.tile(deltas, HY_ORDER)), t_f, t_b)
    kfr, kfi = _filt_fft(ktile, tables)
    yhy5 = _hyena(hy5, p["hy_conv_w"][l], p["hy_conv_b"][l], p["hy_skip"][l], kfr, kfi, tables)

    w_out = p["w_out"][l]
    wr_cat = jnp.concatenate([p["router_group_w"][l], p["router_expert_w"][l]], axis=1)
    wr_cat = jnp.pad(wr_cat, ((0, 0), (0, LANES - wr_cat.shape[1])))
    wrh, wrl = _split_bf16(wr_cat)
    rb = jnp.concatenate([p["router_group_b"][l], p["router_expert_b"][l]])
    rb = jnp.pad(rb, (0, LANES - rb.shape[0])).reshape(1, LANES)
    tri = (jnp.arange(TM_MIX)[:, None] > jnp.arange(TM_MIX)[None, :]).astype(BF16)
    xmid, h, route, counts = _mix_router(
        h4, lru, yhy5, x, row(p["grp_norm_lru_g"][l]), col(p["grp_norm_hy_g"][l]),
        w_out[:D_LRU].astype(BF16), w_out[D_LRU:].astype(BF16), row(p["norm_ffn_g"][l]), wrh, wrl, rb, tri)

    blk = MOE_BLK
    n_blocks = (2 * n_tok) // blk + N_EXPERTS
    cnt = counts[0, N_GROUPS:N_GROUPS + N_EXPERTS].astype(I32)
    padded = (cnt + blk - 1) // blk * blk
    pad_end = jnp.cumsum(padded)
    pad_start = pad_end - padded
    eid = route[:, 0:2].astype(I32)
    dest = pad_start[eid] + route[:, 2:4].astype(I32)
    dest0 = dest[:, 0].reshape(n_tok // TQ_ROW, 1, TQ_ROW)
    dest1 = dest[:, 1].reshape(n_tok // TQ_ROW, 1, TQ_ROW)
    block_start = jnp.arange(n_blocks, dtype=I32) * blk
    block_e = jnp.minimum(jnp.searchsorted(pad_end, block_start, side="right"), N_EXPERTS - 1).astype(I32)
    first_flag = jnp.concatenate([jnp.ones((1,), I32), (block_e[1:] != block_e[:-1]).astype(I32)])
    n_used = (pad_end[-1:] // blk).astype(I32)

    xb = _dispatch(dest0, dest1, h, jnp.zeros((n_blocks * blk, d), F32))
    yb = _experts(block_e, first_flag, n_used, xb, p["exp_w1"][l], p["exp_w3"][l], p["exp_w2"][l])
    return xmid.reshape(n_tok, d), route, dest0, dest1, yb


def kernel(x, norm_mix_g, w_in, lru_conv_w, lru_conv_b, lru_wr_f, lru_br_f, lru_wi_f, lru_bi_f, lru_lambda_f, lru_wr_b, lru_br_b, lru_wi_b, lru_bi_b, lru_lambda_b, hy_conv_w, hy_conv_b, hy_filt_w1, hy_filt_b1, hy_filt_freq1, hy_filt_w2, hy_filt_b2, hy_filt_freq2, hy_filt_w3, hy_skip, grp_norm_lru_g, grp_norm_hy_g, w_out, norm_ffn_g, router_group_w, router_group_b, router_expert_w, router_expert_b, exp_w1, exp_w3, exp_w2, norm_final_g):
    p = dict(locals())
    bsz, seq, d = x.shape
    assert d == D_MODEL and 2 * seq == FFT_R * FFT_R and w_in.shape[0] == 1
    xmid, route, dest0, dest1, yb = _layer(x, 0, p)
    out = _combine(dest0, dest1, xmid, route, norm_final_g.reshape(1, d), yb)
    return out.reshape(bsz, seq, d)
```

```python
import functools
import math

import numpy as np
import jax
import jax.numpy as jnp
from jax import lax
from jax.experimental import pallas as pl
from jax.experimental.pallas import tpu as pltpu

F32 = jnp.float32
BF16 = jnp.bfloat16
I32 = jnp.int32

D_MODEL = 1024
D_LRU = 512
D_HY = 512
LRU_HEADS = 8
LRU_HEAD_DIM = D_LRU // LRU_HEADS
LRU_CONV = 4
LRU_C = 8.0
HY_ORDER = 2
HY_CONV = 3
HY_BANDS = 16
HY_EMB = 2 * HY_BANDS + 1
HY_EMB_PAD = 40
HY_FFN = 64
HY_FAST_DECAY_PCT = 0.3
HY_SLOW_DECAY_PCT = 1.5
HY_DECAY_TARGET = 1e-2
N_GROUPS = 4
EXPERTS_PER_GROUP = 8
N_EXPERTS = N_GROUPS * EXPERTS_PER_GROUP
D_EXPERT = D_MODEL // 2
EPS = 1e-6

LANES = 128
SUBLANES = 8
FFT_R = 128
VMEM_LIMIT = 56 * 1024 * 1024

TM_IN = 512
TC_LRU = 512
RC_HY = 16
RF_KF = 16
RB_FILT = 64
TM_MIX = 512
MOE_BLK = 256
ROW_CHUNK = 16
LOCAL_ROWS = -(-(2 * TM_MIX + N_EXPERTS * (ROW_CHUNK - 1)) // LANES) * LANES
NEG_BIG = -1e30


def _params(sem, vmem=VMEM_LIMIT):
    return pltpu.CompilerParams(dimension_semantics=sem, vmem_limit_bytes=vmem)


def _rms(x, g):
    return x * lax.rsqrt(jnp.mean(x * x, axis=-1, keepdims=True) + EPS) * g


def _sigmoid(x):
    return 1.0 / (1.0 + jnp.exp(-x))


def _split_bf16(a):
    hi = a.astype(BF16)
    lo = (a - hi.astype(F32)).astype(BF16)
    return hi, lo


def _dot(a, b):
    return jnp.dot(a, b, preferred_element_type=F32)


def _dot3(a, b):
    ah, al = _split_bf16(a)
    bh, bl = _split_bf16(b)
    return _dot(ah, bh) + _dot(al, bh) + _dot(ah, bl)


def _inproj_body(x_ref, g_ref, wl_ref, wht_ref, lru_ref, hy_ref):
    hn = _rms(x_ref[0], g_ref[...]).astype(BF16)
    lru_ref[0] = _dot(hn, wl_ref[...])
    hyt = lax.dot_general(wht_ref[...], hn, (((1,), (1,)), ((), ())), preferred_element_type=F32)
    nrow = hyt.shape[0] // SUBLANES
    for j in range(hyt.shape[1] // LANES):
        hy_ref[0, :, j, :, :] = hyt[:, LANES * j:LANES * (j + 1)].reshape(nrow, SUBLANES, LANES)


def _inproj(x, g, wl, wht):
    bsz, seq, d = x.shape
    nl = wl.shape[1]
    nh = wht.shape[0]
    tm = TM_IN
    return pl.pallas_call(
        _inproj_body,
        grid=(bsz, seq // tm),
        in_specs=[
            pl.BlockSpec((1, tm, d), lambda b, i: (b, i, 0)),
            pl.BlockSpec((1, d), lambda b, i: (0, 0)),
            pl.BlockSpec((d, nl), lambda b, i: (0, 0)),
            pl.BlockSpec((nh, d), lambda b, i: (0, 0)),
        ],
        out_specs=[
            pl.BlockSpec((1, tm, nl), lambda b, i: (b, i, 0)),
            pl.BlockSpec((1, nh // SUBLANES, tm // LANES, SUBLANES, LANES), lambda b, i: (b, 0, i, 0, 0)),
        ],
        out_shape=[
            jax.ShapeDtypeStruct((bsz, seq, nl), F32),
            jax.ShapeDtypeStruct((bsz, nh // SUBLANES, seq // LANES, SUBLANES, LANES), F32),
        ],
        compiler_params=_params(("parallel", "parallel")),
        name="inproj",
    )(x, g, wl, wht)


def _neg_expm1(x):
    series = -x * (1.0 + x * (1.0 / 2.0) * (1.0 + x * (1.0 / 3.0) * (1.0 + x * (1.0 / 4.0) * (
        1.0 + x * (1.0 / 5.0) * (1.0 + x * (1.0 / 6.0))))))
    return jnp.where(x > -0.1, series, 1.0 - jnp.exp(x))


def _group_scan(a3, b3, reverse):
    sub = lax.broadcasted_iota(I32, a3.shape, 1)
    for s in (1, 2, 4):
        if reverse:
            a_sh = pltpu.roll(a3, SUBLANES - s, axis=1)
            b_sh = pltpu.roll(b3, SUBLANES - s, axis=1)
            m = sub < SUBLANES - s
        else:
            a_sh = pltpu.roll(a3, s, axis=1)
            b_sh = pltpu.roll(b3, s, axis=1)
            m = sub >= s
        b3 = jnp.where(m, a3 * b_sh + b3, b3)
        a3 = jnp.where(m, a3 * a_sh, a3)
    return a3, b3


def _lru_body(cur_ref, prev_ref, next_ref, cw_ref, cb_ref, wg_ref, br_ref, bi_ref, lam_ref, o_ref,
              a_sc, b_sc, carry_sc, *, nt, tc):
    d = pl.program_id(1)
    i = pl.program_id(2)
    c = jnp.where(d == 0, i, nt - 1 - i)

    @pl.when(i == 0)
    def _():
        carry_sc[...] = jnp.zeros_like(carry_sc)

    prev = jnp.where(c > 0, prev_ref[0], 0.0)
    nxt = jnp.where(c < nt - 1, next_ref[0], 0.0)
    xfull = jnp.concatenate([prev, cur_ref[0], nxt], axis=0)
    cw = cw_ref[...]
    xc = cb_ref[...] + cw[0:1] * xfull[6:6 + tc]
    for k in range(1, LRU_CONV):
        xc = xc + cw[k:k + 1] * xfull[6 + k:6 + k + tc]

    lam = lam_ref[0]
    nlam = -lam
    softplus = jnp.maximum(nlam, 0.0) + jnp.log1p(jnp.exp(-jnp.abs(nlam)))
    ng = tc // SUBLANES
    half = D_LRU // 2

    def gates(hh):
        sl = slice(half * hh, half * (hh + 1))
        xh = xc[:, sl]
        logits = _dot(xh.astype(BF16), wg_ref[0, hh])
        r = _sigmoid(logits[:, :half] + br_ref[0][:, sl])
        gi = _sigmoid(logits[:, half:] + bi_ref[0][:, sl])
        log_a = (-LRU_C) * r * softplus[:, sl]
        a = jnp.exp(log_a)
        b = jnp.sqrt(_neg_expm1(2.0 * log_a)) * (gi * xh)
        return a.reshape(ng, SUBLANES, half), b.reshape(ng, SUBLANES, half)

    def run(reverse):
        for hh in range(2):
            a3, b3 = gates(hh)
            a3, b3 = _group_scan(a3, b3, reverse)
            a_sc[:, :, half * hh:half * (hh + 1)] = a3
            b_sc[:, :, half * hh:half * (hh + 1)] = b3
        last = 0 if reverse else SUBLANES - 1

        def step(s, hc):
            g = (ng - 1 - s) if reverse else s
            h = a_sc[g] * hc + b_sc[g]
            o_ref[0, 0, pl.ds(pl.multiple_of(g * SUBLANES, SUBLANES), SUBLANES), :] = h
            return jnp.broadcast_to(h[last:last + 1, :], h.shape)

        hc = lax.fori_loop(0, ng, step, carry_sc[...], unroll=8)
        carry_sc[...] = hc

    @pl.when(d == 0)
    def _():
        run(False)

    @pl.when(d == 1)
    def _():
        run(True)


def _lru(lru, cw, cb, wg, br, bi, lam):
    bsz, seq, _ = lru.shape
    tc = TC_LRU
    nt = seq // tc
    r8 = tc // SUBLANES
    nrow8 = seq // SUBLANES

    def cidx(d, i):
        return jnp.where(d == 0, i, nt - 1 - i)

    return pl.pallas_call(
        functools.partial(_lru_body, nt=nt, tc=tc),
        grid=(bsz, 2, nt),
        in_specs=[
            pl.BlockSpec((1, tc, D_LRU), lambda b, d, i: (b, cidx(d, i), 0)),
            pl.BlockSpec((1, SUBLANES, D_LRU), lambda b, d, i: (b, jnp.maximum(cidx(d, i) * r8 - 1, 0), 0)),
            pl.BlockSpec((1, SUBLANES, D_LRU), lambda b, d, i: (b, jnp.minimum((cidx(d, i) + 1) * r8, nrow8 - 1), 0)),
            pl.BlockSpec((LRU_CONV, D_LRU), lambda b, d, i: (0, 0)),
            pl.BlockSpec((1, D_LRU), lambda b, d, i: (0, 0)),
            pl.BlockSpec((1, 2, D_LRU // 2, D_LRU), lambda b, d, i: (d, 0, 0, 0)),
            pl.BlockSpec((1, 1, D_LRU), lambda b, d, i: (d, 0, 0)),
            pl.BlockSpec((1, 1, D_LRU), lambda b, d, i: (d, 0, 0)),
            pl.BlockSpec((1, 1, D_LRU), lambda b, d, i: (d, 0, 0)),
        ],
        out_specs=pl.BlockSpec((1, 1, tc, D_LRU), lambda b, d, i: (d, b, cidx(d, i), 0)),
        out_shape=jax.ShapeDtypeStruct((2, bsz, seq, D_LRU), F32),
        scratch_shapes=[
            pltpu.VMEM((tc // SUBLANES, SUBLANES, D_LRU), F32),
            pltpu.VMEM((tc // SUBLANES, SUBLANES, D_LRU), F32),
            pltpu.VMEM((SUBLANES, D_LRU), F32),
        ],
        compiler_params=_params(("parallel", "arbitrary", "arbitrary")),
        name="rg_lru",
    )(lru, lru, lru, cw, cb, wg, br, bi, lam)


def _fft_tables():
    r = FFT_R
    n = r * r
    idx = np.arange(r, dtype=np.float64)
    ang = 2.0 * np.pi * np.outer(idx, idx) / r
    wr, wi = np.cos(ang), -np.sin(ang)
    angt = 2.0 * np.pi * np.outer(idx, idx) / n
    tr, ti = np.cos(angt), -np.sin(angt)
    f1 = np.concatenate([wr, wi], axis=1)
    f2 = np.block([[wr, wi], [-wi, wr]])
    f2c = np.block([[wr, -wi], [wi, wr]])
    f3 = np.concatenate([wr, wi], axis=0) / n
    as_bf16 = lambda a: jnp.asarray(a, F32).astype(BF16)
    return (as_bf16(f1), as_bf16(f2), as_bf16(f2c), as_bf16(f3), jnp.asarray(tr, F32), jnp.asarray(ti, F32))


def _table_specs(nargs_fn):
    r = FFT_R
    shapes = [(r, 2 * r), (2 * r, 2 * r), (2 * r, 2 * r), (2 * r, r), (r, r), (r, r)]
    return [pl.BlockSpec(s, nargs_fn(len(s))) for s in shapes]


def _fft_fwd(x3, f1, f2, tr, ti):
    g = x3.shape[0]
    r = FFT_R
    xt = jnp.swapaxes(x3, 1, 2).reshape(g * r, r)
    a = _dot(xt.astype(BF16), f1)
    ar = a[:, :r].reshape(g, r, r)
    ai = a[:, r:].reshape(g, r, r)
    pr = ar * tr - ai * ti
    pi = ar * ti + ai * tr
    pt = jnp.concatenate([jnp.swapaxes(pr, 1, 2), jnp.swapaxes(pi, 1, 2)], axis=-1).reshape(g * r, 2 * r)
    x = _dot(pt.astype(BF16), f2)
    return x[:, :r], x[:, r:]


def _fft_inv(yr, yi, f2c, f3, tr, ti):
    r = FFT_R
    g = yr.shape[0] // r
    b = _dot(jnp.concatenate([yr, yi], axis=-1).astype(BF16), f2c)
    br = b[:, :r].reshape(g, r, r)
    bi = b[:, r:].reshape(g, r, r)
    cr = br * tr + bi * ti
    ci = bi * tr - br * ti
    ct = jnp.concatenate([jnp.swapaxes(cr, 1, 2), jnp.swapaxes(ci, 1, 2)], axis=-1).reshape(g * r, 2 * r)
    yt = _dot(ct.astype(BF16), f3)
    return jnp.swapaxes(yt.reshape(g, r, r), 1, 2)


def _filt_mlp_body(ft_ref, w1_ref, b1_ref, f1_ref, w2_ref, b2_ref, f2_ref, o_ref):
    z = jnp.sin(f1_ref[...] * (_dot3(w1_ref[...], ft_ref[0]) + b1_ref[...]))
    o_ref[0] = jnp.sin(f2_ref[...] * (_dot3(w2_ref[...], z) + b2_ref[...]))


def _filt_mlp(featst, w1t, b1, f1, w2t, b2, f2):
    _, ke, seq = featst.shape
    col = lambda d: (0, 0)
    return pl.pallas_call(
        _filt_mlp_body,
        grid=(2,),
        in_specs=[
            pl.BlockSpec((1, ke, seq), lambda d: (d, 0, 0)),
            pl.BlockSpec((HY_FFN, ke), col),
            pl.BlockSpec((HY_FFN, 1), col),
            pl.BlockSpec((HY_FFN, 1), col),
            pl.BlockSpec((HY_FFN, HY_FFN), col),
            pl.BlockSpec((HY_FFN, 1), col),
            pl.BlockSpec((HY_FFN, 1), col),
        ],
        out_specs=pl.BlockSpec((1, HY_FFN, seq), lambda d: (d, 0, 0)),
        out_shape=jax.ShapeDtypeStruct((2, HY_FFN, seq), F32),
        compiler_params=_params(("parallel",)),
        name="hyena_filter_mlp",
    )(featst, w1t, b1, f1, w2t, b2, f2)


def _filt_time_body(z_ref, wf_ref, wb_ref, dl_ref, tf_ref, tb_ref, o_ref):
    seq = z_ref.shape[2]
    hf = _dot3(wf_ref[...], z_ref[0]) * jnp.exp(-tf_ref[...] * dl_ref[...])
    hb = _dot3(wb_ref[...], z_ref[1]) * jnp.exp(-tb_ref[...] * dl_ref[...])
    lane = lax.broadcasted_iota(I32, hb.shape, 1)
    hb = jnp.where(lane == 0, 0.0, hb)
    norm = jnp.sum(jnp.abs(hf), axis=-1, keepdims=True) + jnp.sum(jnp.abs(hb), axis=-1, keepdims=True) + EPS
    hf = hf / norm
    hb = hb / norm
    nrow = hf.shape[0] // SUBLANES
    nch = seq // LANES
    for j in range(nch):
        o_ref[:, j, :, :] = hf[:, LANES * j:LANES * (j + 1)].reshape(nrow, SUBLANES, LANES)
        o_ref[:, nch + j, :, :] = hb[:, LANES * j:LANES * (j + 1)].reshape(nrow, SUBLANES, LANES)


def _filt_time(z2t, w3f, w3b, delta_rows, t_f, t_b):
    nrows = w3f.shape[0]
    seq = z2t.shape[2]
    rb = RB_FILT
    return pl.pallas_call(
        _filt_time_body,
        grid=(nrows // rb,),
        in_specs=[
            pl.BlockSpec((2, HY_FFN, seq), lambda r: (0, 0, 0)),
            pl.BlockSpec((rb, HY_FFN), lambda r: (r, 0)),
            pl.BlockSpec((rb, HY_FFN), lambda r: (r, 0)),
            pl.BlockSpec((rb, 1), lambda r: (r, 0)),
            pl.BlockSpec((1, seq), lambda r: (0, 0)),
            pl.BlockSpec((1, seq), lambda r: (0, 0)),
        ],
        out_specs=pl.BlockSpec((rb // SUBLANES, 2 * seq // LANES, SUBLANES, LANES), lambda r: (r, 0, 0, 0)),
        out_shape=jax.ShapeDtypeStruct((nrows // SUBLANES, 2 * seq // LANES, SUBLANES, LANES), F32),
        compiler_params=_params(("parallel",)),
        name="hyena_filter_time",
    )(z2t, w3f, w3b, delta_rows, t_f, t_b)


def _filt_fft_body(k_ref, f1_ref, f2_ref, f2c_ref, f3_ref, tr_ref, ti_ref, kr_ref, ki_ref):
    tr = tr_ref[...]
    ti = ti_ref[...]
    for g in range(k_ref.shape[0]):
        x3 = jnp.stack([k_ref[g, :, ci, :] for ci in range(SUBLANES)], axis=0)
        xr, xi = _fft_fwd(x3, f1_ref[...], f2_ref[...], tr, ti)
        kr_ref[SUBLANES * g:SUBLANES * (g + 1)] = xr.reshape(SUBLANES, FFT_R, FFT_R)
        ki_ref[SUBLANES * g:SUBLANES * (g + 1)] = xi.reshape(SUBLANES, FFT_R, FFT_R)


def _filt_fft(ktile, tables):
    n8, r, _, _ = ktile.shape
    nrows = n8 * SUBLANES
    rf = RF_KF
    out_spec = pl.BlockSpec((rf, r, r), lambda i: (i, 0, 0))
    return pl.pallas_call(
        _filt_fft_body,
        grid=(nrows // rf,),
        in_specs=[pl.BlockSpec((rf // SUBLANES, r, SUBLANES, LANES), lambda i: (i, 0, 0, 0))]
        + _table_specs(lambda nd: (lambda i: (0,) * nd)),
        out_specs=[out_spec, out_spec],
        out_shape=[jax.ShapeDtypeStruct((nrows, r, r), F32)] * 2,
        compiler_params=_params(("parallel",)),
        name="hyena_filter_fft",
    )(ktile, *tables)


def _conv3_time(x, w0, w1, w2, bias):
    row = lax.broadcasted_iota(I32, x.shape, 0)
    lane = lax.broadcasted_iota(I32, x.shape, 1)
    nrow = x.shape[0]
    xm = pltpu.roll(x, 1, axis=1)
    xm_up = pltpu.roll(xm, 1, axis=0)
    xm = jnp.where(lane == 0, jnp.where(row == 0, 0.0, xm_up), xm)
    xp = pltpu.roll(x, LANES - 1, axis=1)
    xp_dn = pltpu.roll(xp, nrow - 1, axis=0)
    xp = jnp.where(lane == LANES - 1, jnp.where(row == nrow - 1, 0.0, xp_dn), xp)
    return w0 * xm + w1 * x + w2 * xp + bias


def _hyena_body(cw_ref, cb_ref, sk_ref, v_ref, x1_ref, x2_ref, k0r_ref, k0i_ref, k1r_ref, k1i_ref,
                f1_ref, f2_ref, f2c_ref, f3_ref, tr_ref, ti_ref, o_ref, *, rc):
    ct = pl.program_id(0)
    tr = tr_ref[...]
    ti = ti_ref[...]
    np_ = v_ref.shape[2]

    def long_conv(z3, kr_ref, ki_ref, g):
        zpad = jnp.concatenate([z3, jnp.zeros_like(z3)], axis=1)
        xr, xi = _fft_fwd(zpad, f1_ref[...], f2_ref[...], tr, ti)
        kr = kr_ref[pl.ds(g * SUBLANES, SUBLANES)].reshape(SUBLANES * FFT_R, FFT_R)
        ki = ki_ref[pl.ds(g * SUBLANES, SUBLANES)].reshape(SUBLANES * FFT_R, FFT_R)
        yr = xr * kr - xi * ki
        yi = xr * ki + xi * kr
        y3 = _fft_inv(yr, yi, f2c_ref[...], f3_ref[...], tr, ti)
        return y3[:, :np_, :]

    def group(g, carry):
        cbase = ct * rc + g * SUBLANES
        vs, sv, x1s, x2s, s1 = [], [], [], [], []
        for ci in range(SUBLANES):
            ch = cbase + ci

            def conv(ref, off):
                c = off + ch
                return _conv3_time(ref[0, g, :, ci, :], cw_ref[0, c], cw_ref[1, c], cw_ref[2, c], cb_ref[c])

            v = conv(v_ref, 0)
            vs.append(v)
            sv.append(sk_ref[0, ch] * v)
            x1s.append(conv(x1_ref, D_HY))
            x2s.append(conv(x2_ref, 2 * D_HY))
            s1.append(jnp.full(v.shape, sk_ref[1, ch], F32))
        v3 = jnp.stack(vs, axis=0)
        z1 = jnp.stack(x1s, axis=0) * (long_conv(v3, k0r_ref, k0i_ref, g) + jnp.stack(sv, axis=0))
        out = jnp.stack(x2s, axis=0) * (long_conv(z1, k1r_ref, k1i_ref, g) + jnp.stack(s1, axis=0) * z1)
        for ci in range(SUBLANES):
            o_ref[0, g, :, ci, :] = out[ci]
        return carry

    lax.fori_loop(0, rc // SUBLANES, group, 0)


def _hyena(hy5, cw, cb, skip, kfr, kfi, tables):
    bsz, n8, np_, _, _ = hy5.shape
    rc = RC_HY
    r8 = rc // SUBLANES
    nct = D_HY // rc
    r = FFT_R
    smem = pl.BlockSpec(memory_space=pltpu.SMEM)

    def xspec(off):
        return pl.BlockSpec((1, r8, np_, SUBLANES, LANES), lambda c, b: (b, c + off * nct, 0, 0, 0))

    def kspec(order):
        return pl.BlockSpec((rc, r, r), lambda c, b: (c + order * nct, 0, 0))

    return pl.pallas_call(
        functools.partial(_hyena_body, rc=rc),
        grid=(nct, bsz),
        in_specs=[smem, smem, smem, xspec(0), xspec(1), xspec(2), kspec(0), kspec(0), kspec(1), kspec(1)]
        + _table_specs(lambda nd: (lambda c, b: (0,) * nd)),
        out_specs=pl.BlockSpec((1, r8, np_, SUBLANES, LANES), lambda c, b: (b, c, 0, 0, 0)),
        out_shape=jax.ShapeDtypeStruct((bsz, D_HY // SUBLANES, np_, SUBLANES, LANES), F32),
        compiler_params=_params(("parallel", "parallel")),
        name="hyena",
    )(cw, cb, skip, hy5, hy5, hy5, kfr, kfi, kfr, kfi, *tables)


def _gelu_tanh(x):
    return 0.5 * x * (1.0 + jnp.tanh(math.sqrt(2.0 / math.pi) * (x + 0.044715 * (x * x * x))))


def _mix_body(hf_ref, hb_ref, gate_ref, yhy_ref, x_ref, gl_ref, gh_ref, wl_ref, wh_ref, gffn_ref,
              wrh_ref, wrl_ref, rb_ref, tri_ref, xmid_ref, h_ref, route_ref, cnt_ref):
    tm = x_ref.shape[1]
    y_lru = (hf_ref[0, 0] + hb_ref[0, 0]) * _gelu_tanh(gate_ref[0])
    nl = _rms(y_lru, gl_ref[...]).astype(BF16)
    gh = gh_ref[...]
    for j in range(tm // LANES):
        yt = yhy_ref[0, :, j, :, :].reshape(D_HY, LANES)
        nt = (yt * lax.rsqrt(jnp.mean(yt * yt, axis=0, keepdims=True) + EPS) * gh).astype(BF16)
        rows = slice(LANES * j, LANES * (j + 1))
        mix = _dot(nl[rows], wl_ref[...]) + lax.dot_general(
            nt, wh_ref[...], (((0,), (0,)), ((), ())), preferred_element_type=F32)
        xmid_ref[0, rows, :] = x_ref[0, rows, :] + mix
    h = _rms(xmid_ref[0], gffn_ref[...])
    h_ref[...] = h.astype(BF16)

    hh, hl = _split_bf16(h)
    logits = _dot(hh, wrh_ref[...]) + _dot(hl, wrh_ref[...]) + _dot(hh, wrl_ref[...]) + rb_ref[...]
    lane = lax.broadcasted_iota(I32, logits.shape, 1)
    big = jnp.int32(1 << 20)
    is_g = lane < N_GROUPS
    lg = jnp.where(is_g, logits, NEG_BIG)
    mg = jnp.max(lg, axis=-1, keepdims=True)
    gidx = jnp.min(jnp.where(lg == mg, lane, big), axis=-1, keepdims=True)
    g_p = 1.0 / jnp.sum(jnp.where(is_g, jnp.exp(lg - mg), 0.0), axis=-1, keepdims=True)
    e_lo = N_GROUPS + EXPERTS_PER_GROUP * gidx
    is_e = jnp.logical_and(lane >= e_lo, lane < e_lo + EXPERTS_PER_GROUP)
    le = jnp.where(is_e, logits, NEG_BIG)
    m1 = jnp.max(le, axis=-1, keepdims=True)
    i1 = jnp.min(jnp.where(le == m1, lane, big), axis=-1, keepdims=True)
    le2 = jnp.where(lane == i1, NEG_BIG, le)
    m2 = jnp.max(le2, axis=-1, keepdims=True)
    i2 = jnp.min(jnp.where(le2 == m2, lane, big), axis=-1, keepdims=True)
    ratio = jnp.exp(m2 - m1)
    gate1 = g_p / (1.0 + ratio)
    gate2 = g_p * ratio / (1.0 + ratio)
    oh1 = lane == i1
    oh2 = lane == i2
    oh = jnp.where(oh1, 1.0, jnp.where(oh2, 1.0, 0.0))
    tot = _dot(tri_ref[...], oh.astype(BF16))
    rank1 = jnp.sum(jnp.where(oh1, tot, 0.0), axis=-1, keepdims=True)
    rank2 = jnp.sum(jnp.where(oh2, tot, 0.0), axis=-1, keepdims=True)
    cnt_ref[0] = jnp.sum(oh, axis=0, keepdims=True)
    cols = [(i1 - N_GROUPS).astype(F32), (i2 - N_GROUPS).astype(F32), rank1, rank2, gate1, gate2]
    route = jnp.zeros(logits.shape, F32)
    for k, col in enumerate(cols):
        route = jnp.where(lane == k, col, route)
    route_ref[...] = route[:, :SUBLANES]


def _mix_router(h4, lru, yhy5, x, gl, gh, wl, wh, gffn, wrh, wrl, rb, tri):
    bsz, seq, d = x.shape
    tm = TM_MIX
    nt = seq // tm
    n_tok = bsz * seq
    c2 = lambda b, i: (0, 0)
    return pl.pallas_call(
        _mix_body,
        grid=(bsz, nt),
        in_specs=[
            pl.BlockSpec((1, 1, tm, D_LRU), lambda b, i: (0, b, i, 0)),
            pl.BlockSpec((1, 1, tm, D_LRU), lambda b, i: (1, b, i, 0)),
            pl.BlockSpec((1, tm, D_LRU), lambda b, i: (b, i, 1)),
            pl.BlockSpec((1, D_HY // SUBLANES, tm // LANES, SUBLANES, LANES), lambda b, i: (b, 0, i, 0, 0)),
            pl.BlockSpec((1, tm, d), lambda b, i: (b, i, 0)),
            pl.BlockSpec((1, D_LRU), c2),
            pl.BlockSpec((D_HY, 1), c2),
            pl.BlockSpec((D_LRU, d), c2),
            pl.BlockSpec((D_HY, d), c2),
            pl.BlockSpec((1, d), c2),
            pl.BlockSpec((d, LANES), c2),
            pl.BlockSpec((d, LANES), c2),
            pl.BlockSpec((1, LANES), c2),
            pl.BlockSpec((tm, tm), c2),
        ],
        out_specs=[
            pl.BlockSpec((1, tm, d), lambda b, i: (b, i, 0)),
            pl.BlockSpec((tm, d), lambda b, i: (b * nt + i, 0)),
            pl.BlockSpec((tm, SUBLANES), lambda b, i: (b * nt + i, 0)),
            pl.BlockSpec((1, 1, LANES), lambda b, i: (b * nt + i, 0, 0)),
        ],
        out_shape=[
            jax.ShapeDtypeStruct((bsz, seq, d), F32),
            jax.ShapeDtypeStruct((n_tok, d), BF16),
            jax.ShapeDtypeStruct((n_tok, SUBLANES), F32),
            jax.ShapeDtypeStruct((n_tok // tm, 1, LANES), F32),
        ],
        compiler_params=_params(("parallel", "parallel")),
        name="mix_router",
    )(h4, h4, lru, yhy5, x, gl, gh, wl, wh, gffn, wrh, wrl, rb, tri)


def _local_slots(route, lsl):
    lane = lax.broadcasted_iota(I32, (route.shape[0], LANES), 1)
    pos = []
    for k in range(2):
        start = jnp.sum(jnp.where(lane == route[:, k:k + 1].astype(I32), lsl, 0.0), axis=-1, keepdims=True)
        pos.append((start + route[:, 2 + k:3 + k]).astype(I32))
    return pos


def _run_chunks(i, nch_ref, gs_ref, ls_ref, make_copy):
    def per_expert(e, total):
        idx = i * N_EXPERTS + e
        n = nch_ref[idx]
        gs = gs_ref[idx]
        ls = ls_ref[idx]

        def per_chunk(k, c):
            make_copy(ls + k * ROW_CHUNK, gs + k * ROW_CHUNK).start()
            return c

        lax.fori_loop(0, n, per_chunk, 0)
        return total + n

    total = lax.fori_loop(0, N_EXPERTS, per_expert, 0)

    def wait(k, c):
        make_copy(0, 0).wait()
        return c

    lax.fori_loop(0, total, wait, 0)


def _dispatch_body(nch_ref, gs_ref, ls_ref, tn_ref, ts_ref, nu_ref, h_ref, route_ref, lsl_ref, xb_ref, buf, zbuf, sem):
    i = pl.program_id(0)
    tm = h_ref.shape[0]
    lbuf = buf.shape[0]
    n_blocks = xb_ref.shape[0] // MOE_BLK

    def aligned(r):
        return pl.ds(pl.multiple_of(r, ROW_CHUNK), ROW_CHUNK)

    @pl.when(i == 0)
    def _():
        zbuf[...] = jnp.zeros_like(zbuf)

        def zero_chunk(r):
            return pltpu.make_async_copy(zbuf.at[pl.ds(0, ROW_CHUNK)], xb_ref.at[aligned(r)], sem)

        def zero_block(j):
            return pltpu.make_async_copy(zbuf, xb_ref.at[pl.ds(pl.multiple_of(j * MOE_BLK, MOE_BLK), MOE_BLK)], sem)

        def per_expert(e, total):
            def per_chunk(k, c):
                zero_chunk(ts_ref[e] + k * ROW_CHUNK).start()
                return c

            lax.fori_loop(0, tn_ref[e], per_chunk, 0)
            return total + tn_ref[e]

        total = lax.fori_loop(0, N_EXPERTS, per_expert, 0)

        def start_block(j, c):
            zero_block(j).start()
            return c

        lax.fori_loop(nu_ref[0], n_blocks, start_block, 0)

        def wait_chunk(k, c):
            zero_chunk(0).wait()
            return c

        lax.fori_loop(0, total, wait_chunk, 0)

        def wait_block(j, c):
            zero_block(0).wait()
            return c

        lax.fori_loop(nu_ref[0], n_blocks, wait_block, 0)

    l0, l1 = _local_slots(route_ref[...], lsl_ref[0])
    lane = lax.broadcasted_iota(I32, (tm, LANES), 1)
    slots_t = jnp.where(lane == 0, l0, jnp.where(lane == 1, l1, 0)).astype(F32).T
    l0t = slots_t[0:1].astype(I32)
    l1t = slots_t[1:2].astype(I32)
    row = lax.broadcasted_iota(I32, (lbuf, tm), 0)
    perm = jnp.where(row == l0t, 1.0, jnp.where(row == l1t, 1.0, 0.0)).astype(BF16)
    buf[...] = _dot(perm, h_ref[...]).astype(BF16)

    def make_copy(src_row, dst_row):
        return pltpu.make_async_copy(buf.at[aligned(src_row)], xb_ref.at[aligned(dst_row)], sem)

    _run_chunks(i, nch_ref, gs_ref, ls_ref, make_copy)


def _dispatch(nch, gstart, lstart, tail_n, tail_s, n_used, h, route, lsl, n_slots):
    n_tok, d = h.shape
    tm = TM_MIX
    grid_spec = pltpu.PrefetchScalarGridSpec(
        num_scalar_prefetch=6,
        grid=(n_tok // tm,),
        in_specs=[
            pl.BlockSpec((tm, d), lambda i, *_: (i, 0)),
            pl.BlockSpec((tm, SUBLANES), lambda i, *_: (i, 0)),
            pl.BlockSpec((1, 1, LANES), lambda i, *_: (i, 0, 0)),
        ],
        out_specs=pl.BlockSpec(memory_space=pl.ANY),
        scratch_shapes=[pltpu.VMEM((LOCAL_ROWS, d), BF16), pltpu.VMEM((MOE_BLK, d), BF16),
                        pltpu.SemaphoreType.DMA(())],
    )
    return pl.pallas_call(
        _dispatch_body,
        grid_spec=grid_spec,
        out_shape=jax.ShapeDtypeStruct((n_slots, d), BF16),
        compiler_params=_params(("arbitrary",)),
        name="moe_dispatch",
    )(nch, gstart, lstart, tail_n, tail_s, n_used, h, route, lsl)


def _expert_body(be_ref, fl_ref, nu_ref, xb_ref, w1_ref, w3_ref, w2_ref, yb_ref, w1_sc, w3_sc, w2_sc):
    del be_ref
    j = pl.program_id(0)
    used = j < nu_ref[0]

    @pl.when(jnp.logical_and(used, fl_ref[j] == 1))
    def _():
        w1_sc[...] = w1_ref[...].astype(BF16)
        w3_sc[...] = w3_ref[...].astype(BF16)
        w2_sc[...] = w2_ref[...].astype(BF16)

    @pl.when(used)
    def _():
        xb = xb_ref[...]
        a = _dot(xb, w1_sc[...])
        b = _dot(xb, w3_sc[...])
        act = (a * _sigmoid(a) * b).astype(BF16)
        yb_ref[...] = _dot(act, w2_sc[...]).astype(BF16)

    @pl.when(jnp.logical_not(used))
    def _():
        yb_ref[...] = jnp.zeros_like(yb_ref)


def _experts(block_e, first_flag, n_used, xb, w1, w3, w2):
    n_slots, d = xb.shape
    blk = MOE_BLK
    de = w1.shape[2]
    grid_spec = pltpu.PrefetchScalarGridSpec(
        num_scalar_prefetch=3,
        grid=(n_slots // blk,),
        in_specs=[
            pl.BlockSpec((blk, d), lambda j, be, fl, nu: (j, 0)),
            pl.BlockSpec((None, d, de), lambda j, be, fl, nu: (be[j], 0, 0)),
            pl.BlockSpec((None, d, de), lambda j, be, fl, nu: (be[j], 0, 0)),
            pl.BlockSpec((None, de, d), lambda j, be, fl, nu: (be[j], 0, 0)),
        ],
        out_specs=pl.BlockSpec((blk, d), lambda j, be, fl, nu: (j, 0)),
        scratch_shapes=[pltpu.VMEM((d, de), BF16), pltpu.VMEM((d, de), BF16), pltpu.VMEM((de, d), BF16)],
    )
    return pl.pallas_call(
        _expert_body,
        grid_spec=grid_spec,
        out_shape=jax.ShapeDtypeStruct((n_slots, d), BF16),
        compiler_params=_params(("arbitrary",)),
        name="moe_experts",
    )(block_e, first_flag, n_used, xb, w1, w3, w2)


def _combine_body(nch_ref, gs_ref, ls_ref, xmid_ref, route_ref, lsl_ref, g_ref, yb_ref, o_ref, buf, sem):
    i = pl.program_id(0)
    tm = xmid_ref.shape[0]
    lbuf = buf.shape[0]

    @pl.when(i == 0)
    def _():
        buf[...] = jnp.zeros_like(buf)

    def aligned(r):
        return pl.ds(pl.multiple_of(r, ROW_CHUNK), ROW_CHUNK)

    def make_copy(local_row, slot_row):
        return pltpu.make_async_copy(yb_ref.at[aligned(slot_row)], buf.at[aligned(local_row)], sem)

    _run_chunks(i, nch_ref, gs_ref, ls_ref, make_copy)
    route = route_ref[...]
    l0, l1 = _local_slots(route, lsl_ref[0])
    lane = lax.broadcasted_iota(I32, (tm, lbuf), 1)
    gmat = (jnp.where(lane == l0, route[:, 4:5], 0.0) + jnp.where(lane == l1, route[:, 5:6], 0.0)).astype(BF16)
    y = xmid_ref[...] + _dot(gmat, buf[...])
    o_ref[...] = _rms(y, g_ref[...])


def _combine(nch, gstart, lstart, xmid, route, lsl, g, yb):
    n_tok, d = xmid.shape
    tm = TM_MIX
    grid_spec = pltpu.PrefetchScalarGridSpec(
        num_scalar_prefetch=3,
        grid=(n_tok // tm,),
        in_specs=[
            pl.BlockSpec((tm, d), lambda i, *_: (i, 0)),
            pl.BlockSpec((tm, SUBLANES), lambda i, *_: (i, 0)),
            pl.BlockSpec((1, 1, LANES), lambda i, *_: (i, 0, 0)),
            pl.BlockSpec((1, d), lambda i, *_: (0, 0)),
            pl.BlockSpec(memory_space=pl.ANY),
        ],
        out_specs=pl.BlockSpec((tm, d), lambda i, *_: (i, 0)),
        scratch_shapes=[pltpu.VMEM((LOCAL_ROWS, d), BF16), pltpu.SemaphoreType.DMA(())],
    )
    return pl.pallas_call(
        _combine_body,
        grid_spec=grid_spec,
        out_shape=jax.ShapeDtypeStruct((n_tok, d), F32),
        compiler_params=_params(("arbitrary",)),
        name="moe_combine",
    )(nch, gstart, lstart, xmid, route, lsl, g, yb)


def _filter_features(seq):
    pos = jnp.arange(seq, dtype=F32)
    t = jnp.linspace(0.0, 1.0, seq, dtype=F32)
    w = (2.0 * math.pi / seq) * pos
    bands = jnp.linspace(1e-4, HY_BANDS - 1, HY_BANDS, dtype=F32)
    ang = w[:, None] * bands[None, :]
    feats = jnp.concatenate([t[:, None], jnp.cos(ang), jnp.sin(ang)], axis=-1)
    rev = lambda a: jnp.concatenate([jnp.zeros_like(a[:1]), a[1:][::-1]], axis=0)
    pad = lambda a: jnp.pad(a, ((0, 0), (0, HY_EMB_PAD - HY_EMB)))
    featst = jnp.stack([pad(feats).T, pad(rev(feats)).T], axis=0)
    return featst, t[None, :], rev(t[:, None]).T


def _gate_weights(wr, wi):
    half = D_LRU // 2
    hph = half // LRU_HEAD_DIM
    eye = jnp.eye(hph, dtype=wr.dtype)[:, None, :, None]

    def block_diag(w):
        return (eye * w[:, :, None, :]).reshape(half, half)

    out = [jnp.concatenate([block_diag(wr[hh * hph:(hh + 1) * hph]), block_diag(wi[hh * hph:(hh + 1) * hph])], axis=1)
           for hh in range(2)]
    return jnp.stack(out, axis=0).astype(BF16)


def _layer(x, l, p):
    bsz, seq, d = x.shape
    n_tok = bsz * seq
    row = lambda a: a.reshape(1, -1).astype(F32)
    col = lambda a: a.reshape(-1, 1).astype(F32)

    w_in = p["w_in"][l]
    wl = w_in[:, :2 * D_LRU].astype(BF16)
    wht = w_in[:, 2 * D_LRU:].T.astype(BF16)
    lru, hy5 = _inproj(x, row(p["norm_mix_g"][l]), wl, wht)

    wg = jnp.stack([_gate_weights(p["lru_wr_f"][l], p["lru_wi_f"][l]),
                    _gate_weights(p["lru_wr_b"][l], p["lru_wi_b"][l])], axis=0)
    stack2 = lambda a, b: jnp.stack([a.reshape(1, -1), b.reshape(1, -1)], axis=0).astype(F32)
    h4 = _lru(lru, p["lru_conv_w"][l], row(p["lru_conv_b"][l]), wg,
              stack2(p["lru_br_f"][l], p["lru_br_b"][l]), stack2(p["lru_bi_f"][l], p["lru_bi_b"][l]),
              stack2(p["lru_lambda_f"][l], p["lru_lambda_b"][l]))

    tables = _fft_tables()
    featst, t_f, t_b = _filter_features(seq)
    w1t = jnp.pad(p["hy_filt_w1"][l].T, ((0, 0), (0, HY_EMB_PAD - HY_EMB)))
    z2t = _filt_mlp(featst, w1t, col(p["hy_filt_b1"][l]), col(p["hy_filt_freq1"][l]),
                    p["hy_filt_w2"][l].T, col(p["hy_filt_b2"][l]), col(p["hy_filt_freq2"][l]))
    w3 = p["hy_filt_w3"][l]
    nfr = HY_ORDER * D_HY
    deltas = jnp.abs(jnp.linspace(math.log(HY_DECAY_TARGET) / HY_SLOW_DECAY_PCT,
                                  math.log(HY_DECAY_TARGET) / HY_FAST_DECAY_PCT, D_HY, dtype=F32))
    ktile = _filt_time(z2t, w3[:, :nfr].T, w3[:, nfr:].T, col(jnp.tile(deltas, HY_ORDER)), t_f, t_b)
    kfr, kfi = _filt_fft(ktile, tables)
    yhy5 = _hyena(hy5, p["hy_conv_w"][l], p["hy_conv_b"][l], p["hy_skip"][l], kfr, kfi, tables)

    w_out = p["w_out"][l]
    wr_cat = jnp.concatenate([p["router_group_w"][l], p["router_expert_w"][l]], axis=1)
    wr_cat = jnp.pad(wr_cat, ((0, 0), (0, LANES - wr_cat.shape[1])))
    wrh, wrl = _split_bf16(wr_cat)
    rb = jnp.concatenate([p["router_group_b"][l], p["router_expert_b"][l]])
    rb = jnp.pad(rb, (0, LANES - rb.shape[0])).reshape(1, LANES)
    tri = (jnp.arange(TM_MIX)[:, None] > jnp.arange(TM_MIX)[None, :]).astype(BF16)
    xmid, h, route, tile_cnt = _mix_router(
        h4, lru, yhy5, x, row(p["grp_norm_lru_g"][l]), col(p["grp_norm_hy_g"][l]),
        w_out[:D_LRU].astype(BF16), w_out[D_LRU:].astype(BF16), row(p["norm_ffn_g"][l]), wrh, wrl, rb, tri)

    blk = MOE_BLK
    ch = ROW_CHUNK
    n_tiles = n_tok // TM_MIX
    n_blocks = (2 * n_tok + n_tiles * N_EXPERTS * (ch - 1)) // blk + N_EXPERTS
    cnt = tile_cnt[:, 0, N_GROUPS:N_GROUPS + N_EXPERTS].astype(I32)
    run = (cnt + ch - 1) // ch * ch
    tot_e = jnp.sum(run, axis=0)
    padded_e = (tot_e + blk - 1) // blk * blk
    pad_end = jnp.cumsum(padded_e)
    pad_start = pad_end - padded_e
    gstart = pad_start[None, :] + jnp.cumsum(run, axis=0) - run
    lstart = jnp.cumsum(run, axis=1) - run
    lsl = jnp.pad(lstart.astype(F32), ((0, 0), (0, LANES - N_EXPERTS))).reshape(n_tiles, 1, LANES)
    flat = lambda a: a.reshape(-1).astype(I32)
    nch, gstart, lstart = flat(run // ch), flat(gstart), flat(lstart)
    tail_n = flat((padded_e - tot_e) // ch)
    tail_s = flat(pad_start + tot_e)
    block_start = jnp.arange(n_blocks, dtype=I32) * blk
    block_e = jnp.minimum(jnp.searchsorted(pad_end, block_start, side="right"), N_EXPERTS - 1).astype(I32)
    first_flag = jnp.concatenate([jnp.ones((1,), I32), (block_e[1:] != block_e[:-1]).astype(I32)])
    n_used = (pad_end[-1:] // blk).astype(I32)

    xb = _dispatch(nch, gstart, lstart, tail_n, tail_s, n_used, h, route, lsl, n_blocks * blk)
    yb = _experts(block_e, first_flag, n_used, xb, p["exp_w1"][l], p["exp_w3"][l], p["exp_w2"][l])
    return xmid.reshape(n_tok, d), route, (nch, gstart, lstart, lsl), yb


def kernel(x, norm_mix_g, w_in, lru_conv_w, lru_conv_b, lru_wr_f, lru_br_f, lru_wi_f, lru_bi_f, lru_lambda_f, lru_wr_b, lru_br_b, lru_wi_b, lru_bi_b, lru_lambda_b, hy_conv_w, hy_conv_b, hy_filt_w1, hy_filt_b1, hy_filt_freq1, hy_filt_w2, hy_filt_b2, hy_filt_freq2, hy_filt_w3, hy_skip, grp_norm_lru_g, grp_norm_hy_g, w_out, norm_ffn_g, router_group_w, router_group_b, router_expert_w, router_expert_b, exp_w1, exp_w3, exp_w2, norm_final_g):
    p = dict(locals())
    bsz, seq, d = x.shape
    assert d == D_MODEL and 2 * seq == FFT_R * FFT_R and w_in.shape[0] == 1
    xmid, route, (nch, gstart, lstart, lsl), yb = _layer(x, 0, p)
    out = _combine(nch, gstart, lstart, xmid, route, lsl, norm_final_g.reshape(1, d), yb)
    return out.reshape(bsz, seq, d)
```

```python
import functools
import math

import numpy as np
import jax
import jax.numpy as jnp
from jax import lax
from jax.experimental import pallas as pl
from jax.experimental.pallas import tpu as pltpu

F32 = jnp.float32
BF16 = jnp.bfloat16
I32 = jnp.int32

D_MODEL = 1024
D_LRU = 512
D_HY = 512
LRU_HEADS = 8
LRU_HEAD_DIM = D_LRU // LRU_HEADS
LRU_CONV = 4
LRU_C = 8.0
HY_ORDER = 2
HY_CONV = 3
HY_BANDS = 16
HY_EMB = 2 * HY_BANDS + 1
HY_EMB_PAD = 40
HY_FFN = 64
HY_FAST_DECAY_PCT = 0.3
HY_SLOW_DECAY_PCT = 1.5
HY_DECAY_TARGET = 1e-2
N_GROUPS = 4
EXPERTS_PER_GROUP = 8
N_EXPERTS = N_GROUPS * EXPERTS_PER_GROUP
D_EXPERT = D_MODEL // 2
EPS = 1e-6

LANES = 128
SUBLANES = 8
FFT_R = 128
VMEM_LIMIT = 56 * 1024 * 1024

TM_IN = 512
TC_LRU = 512
RC_HY = 16
RF_KF = 16
RB_FILT = 64
TM_MIX = 512
MOE_BLK = 256
ROW_CHUNK = 16
LOCAL_ROWS = -(-(2 * TM_MIX + N_EXPERTS * (ROW_CHUNK - 1)) // LANES) * LANES
NEG_BIG = -1e30


def _params(sem, vmem=VMEM_LIMIT):
    return pltpu.CompilerParams(dimension_semantics=sem, vmem_limit_bytes=vmem)


def _rms(x, g):
    return x * lax.rsqrt(jnp.mean(x * x, axis=-1, keepdims=True) + EPS) * g


def _sigmoid(x):
    return 1.0 / (1.0 + jnp.exp(-x))


def _split_bf16(a):
    hi = a.astype(BF16)
    lo = (a - hi.astype(F32)).astype(BF16)
    return hi, lo


def _dot(a, b):
    return jnp.dot(a, b, preferred_element_type=F32)


def _dot3(a, b):
    ah, al = _split_bf16(a)
    bh, bl = _split_bf16(b)
    return _dot(ah, bh) + _dot(al, bh) + _dot(ah, bl)


def _inproj_body(x_ref, g_ref, wl_ref, wht_ref, lru_ref, hy_ref):
    hn = _rms(x_ref[0], g_ref[...]).astype(BF16)
    lru_ref[0] = _dot(hn, wl_ref[...])
    hyt = lax.dot_general(wht_ref[...], hn, (((1,), (1,)), ((), ())), preferred_element_type=F32)
    nrow = hyt.shape[0] // SUBLANES
    for j in range(hyt.shape[1] // LANES):
        hy_ref[0, :, j, :, :] = hyt[:, LANES * j:LANES * (j + 1)].reshape(nrow, SUBLANES, LANES)


def _inproj(x, g, wl, wht):
    bsz, seq, d = x.shape
    nl = wl.shape[1]
    nh = wht.shape[0]
    tm = TM_IN
    return pl.pallas_call(
        _inproj_body,
        grid=(bsz, seq // tm),
        in_specs=[
            pl.BlockSpec((1, tm, d), lambda b, i: (b, i, 0)),
            pl.BlockSpec((1, d), lambda b, i: (0, 0)),
            pl.BlockSpec((d, nl), lambda b, i: (0, 0)),
            pl.BlockSpec((nh, d), lambda b, i: (0, 0)),
        ],
        out_specs=[
            pl.BlockSpec((1, tm, nl), lambda b, i: (b, i, 0)),
            pl.BlockSpec((1, nh // SUBLANES, tm // LANES, SUBLANES, LANES), lambda b, i: (b, 0, i, 0, 0)),
        ],
        out_shape=[
            jax.ShapeDtypeStruct((bsz, seq, nl), F32),
            jax.ShapeDtypeStruct((bsz, nh // SUBLANES, seq // LANES, SUBLANES, LANES), F32),
        ],
        compiler_params=_params(("parallel", "parallel")),
        name="inproj",
    )(x, g, wl, wht)


def _one_minus_sq(a, log_a):
    x = 2.0 * log_a
    series = -x * (1.0 + x * (0.5 + x * (1.0 / 6.0)))
    return jnp.where(x > -4e-3, series, 1.0 - a * a)


def _group_scan(a3, b3, reverse):
    sub = lax.broadcasted_iota(I32, a3.shape, 1)
    for s in (1, 2, 4):
        if reverse:
            a_sh = pltpu.roll(a3, SUBLANES - s, axis=1)
            b_sh = pltpu.roll(b3, SUBLANES - s, axis=1)
            m = sub < SUBLANES - s
        else:
            a_sh = pltpu.roll(a3, s, axis=1)
            b_sh = pltpu.roll(b3, s, axis=1)
            m = sub >= s
        b3 = jnp.where(m, a3 * b_sh + b3, b3)
        a3 = jnp.where(m, a3 * a_sh, a3)
    return a3, b3


def _lru_body(cur_ref, prev_ref, next_ref, cw_ref, cb_ref, wg_ref, br_ref, bi_ref, lam_ref, o_ref,
              h_sc, carry_sc, *, nt, tc):
    d = pl.program_id(1)
    i = pl.program_id(2)
    c = jnp.where(d == 0, i, nt - 1 - i)
    ns = SUBLANES
    ng = tc // ns
    half = D_LRU // 2

    @pl.when(i == 0)
    def _():
        carry_sc[...] = jnp.zeros_like(carry_sc)

    xs = jnp.swapaxes(cur_ref[0].reshape(ns, ng, D_LRU), 0, 1)
    prev = jnp.where(c > 0, prev_ref[0], 0.0)
    nxt = jnp.where(c < nt - 1, next_ref[0], 0.0)
    sub = lax.broadcasted_iota(I32, (ns, D_LRU), 0)

    def prev_segment(slab, halo_row):
        return jnp.where(sub == 0, halo_row, pltpu.roll(slab, 1, axis=0))

    def next_segment(slab, halo_row):
        return jnp.where(sub == ns - 1, halo_row, pltpu.roll(slab, ns - 1, axis=0))

    xext = jnp.concatenate([prev_segment(xs[ng - 2], prev[ns - 2:ns - 1])[None],
                            prev_segment(xs[ng - 1], prev[ns - 1:ns])[None], xs,
                            next_segment(xs[0], nxt[0:1])[None]], axis=0)
    cw = cw_ref[...]
    xc = cb_ref[...] + cw[0:1] * xext[0:ng]
    for k in range(1, LRU_CONV):
        xc = xc + cw[k:k + 1] * xext[k:k + ng]
    xc = xc.reshape(tc, D_LRU)

    lam = lam_ref[0]
    nlam = -lam
    softplus = jnp.maximum(nlam, 0.0) + jnp.log1p(jnp.exp(-jnp.abs(nlam)))
    half_c = (-0.5 * LRU_C) * softplus

    def gates(hh):
        sl = slice(half * hh, half * (hh + 1))
        xh = xc[:, sl]
        logits = _dot(xh.astype(BF16), wg_ref[0, hh])
        tr_ = jnp.tanh(0.5 * (logits[:, :half] + br_ref[0][:, sl]))
        gi = 0.5 + 0.5 * jnp.tanh(0.5 * (logits[:, half:] + bi_ref[0][:, sl]))
        log_a = half_c[:, sl] * (1.0 + tr_)
        a = jnp.exp(log_a)
        b = jnp.sqrt(_one_minus_sq(a, log_a)) * (gi * xh)
        return a.reshape(ng, ns, half), b.reshape(ng, ns, half)

    def run(reverse):
        subh = lax.broadcasted_iota(I32, (ns, half), 0)
        for hh in range(2):
            sl = slice(half * hh, half * (hh + 1))
            a3, b3 = gates(hh)
            h = jnp.zeros((ns, half), F32)
            p = jnp.ones((ns, half), F32)
            hs = [None] * ng
            ps = [None] * ng
            for g in (range(ng - 1, -1, -1) if reverse else range(ng)):
                h = a3[g] * h + b3[g]
                p = a3[g] * p
                hs[g] = h
                ps[g] = p
            ac, hc = _group_scan(p[None], h[None], reverse)
            c_in = carry_sc[:, sl]
            end = ac[0] * c_in + hc[0]
            if reverse:
                seg_in = jnp.where(subh == ns - 1, c_in, pltpu.roll(end, ns - 1, axis=0))
                carry_sc[:, sl] = jnp.broadcast_to(end[0:1], end.shape)
            else:
                seg_in = jnp.where(subh == 0, c_in, pltpu.roll(end, 1, axis=0))
                carry_sc[:, sl] = jnp.broadcast_to(end[ns - 1:ns], end.shape)
            h_sc[:, :, sl] = jnp.stack(hs, axis=0) + jnp.stack(ps, axis=0) * seg_in[None]
        o_ref[0, 0] = jnp.swapaxes(h_sc[...], 0, 1).reshape(tc, D_LRU)

    @pl.when(d == 0)
    def _():
        run(False)

    @pl.when(d == 1)
    def _():
        run(True)


def _lru(lru, cw, cb, wg, br, bi, lam):
    bsz, seq, _ = lru.shape
    tc = TC_LRU
    nt = seq // tc
    r8 = tc // SUBLANES
    nrow8 = seq // SUBLANES

    def cidx(d, i):
        return jnp.where(d == 0, i, nt - 1 - i)

    return pl.pallas_call(
        functools.partial(_lru_body, nt=nt, tc=tc),
        grid=(bsz, 2, nt),
        in_specs=[
            pl.BlockSpec((1, tc, D_LRU), lambda b, d, i: (b, cidx(d, i), 0)),
            pl.BlockSpec((1, SUBLANES, D_LRU), lambda b, d, i: (b, jnp.maximum(cidx(d, i) * r8 - 1, 0), 0)),
            pl.BlockSpec((1, SUBLANES, D_LRU), lambda b, d, i: (b, jnp.minimum((cidx(d, i) + 1) * r8, nrow8 - 1), 0)),
            pl.BlockSpec((LRU_CONV, D_LRU), lambda b, d, i: (0, 0)),
            pl.BlockSpec((1, D_LRU), lambda b, d, i: (0, 0)),
            pl.BlockSpec((1, 2, D_LRU // 2, D_LRU), lambda b, d, i: (d, 0, 0, 0)),
            pl.BlockSpec((1, 1, D_LRU), lambda b, d, i: (d, 0, 0)),
            pl.BlockSpec((1, 1, D_LRU), lambda b, d, i: (d, 0, 0)),
            pl.BlockSpec((1, 1, D_LRU), lambda b, d, i: (d, 0, 0)),
        ],
        out_specs=pl.BlockSpec((1, 1, tc, D_LRU), lambda b, d, i: (d, b, cidx(d, i), 0)),
        out_shape=jax.ShapeDtypeStruct((2, bsz, seq, D_LRU), F32),
        scratch_shapes=[
            pltpu.VMEM((tc // SUBLANES, SUBLANES, D_LRU), F32),
            pltpu.VMEM((SUBLANES, D_LRU), F32),
        ],
        compiler_params=_params(("parallel", "arbitrary", "arbitrary")),
        name="rg_lru",
    )(lru, lru, lru, cw, cb, wg, br, bi, lam)


def _fft_tables():
    r = FFT_R
    n = r * r
    idx = np.arange(r, dtype=np.float64)
    ang = 2.0 * np.pi * np.outer(idx, idx) / r
    wr, wi = np.cos(ang), -np.sin(ang)
    angt = 2.0 * np.pi * np.outer(idx, idx) / n
    tr, ti = np.cos(angt), -np.sin(angt)
    h = r // 2
    f1 = np.concatenate([wr, wi], axis=1)
    f1p = np.block([[wr[:h], wi[:h]], [-wi[:h], wr[:h]]])
    f2 = np.block([[wr, wi], [-wi, wr]])
    f2c = np.block([[wr, -wi], [wi, wr]])
    f3p = np.block([[wr[:, :h], -wi[:, :h]], [wi[:, :h], wr[:, :h]]]) / n
    as_bf16 = lambda a: jnp.asarray(a, F32).astype(BF16)
    return dict(f1=as_bf16(f1), f1p=as_bf16(f1p), f2=as_bf16(f2), f2c=as_bf16(f2c), f3p=as_bf16(f3p),
                tr=jnp.asarray(tr, F32), ti=jnp.asarray(ti, F32))


def _const_specs(arrays, nargs):
    return [pl.BlockSpec(a.shape, (lambda nd: (lambda *_: (0,) * nd))(a.ndim)) for a in arrays]


def _fft_fwd(x3, f1, f2, tr, ti):
    g = x3.shape[0]
    r = FFT_R
    xt = jnp.swapaxes(x3, 1, 2).reshape(g * r, r)
    a = _dot(xt.astype(BF16), f1)
    ar = a[:, :r].reshape(g, r, r)
    ai = a[:, r:].reshape(g, r, r)
    pr = ar * tr - ai * ti
    pi = ar * ti + ai * tr
    pt = jnp.concatenate([jnp.swapaxes(pr, 1, 2), jnp.swapaxes(pi, 1, 2)], axis=-1).reshape(g * r, 2 * r)
    x = _dot(pt.astype(BF16), f2)
    return x[:, :r], x[:, r:]


def _fft_inv(yr, yi, f2c, f3, tr, ti):
    r = FFT_R
    g = yr.shape[0] // r
    b = _dot(jnp.concatenate([yr, yi], axis=-1).astype(BF16), f2c)
    br = b[:, :r].reshape(g, r, r)
    bi = b[:, r:].reshape(g, r, r)
    cr = br * tr + bi * ti
    ci = bi * tr - br * ti
    ct = jnp.concatenate([jnp.swapaxes(cr, 1, 2), jnp.swapaxes(ci, 1, 2)], axis=-1).reshape(g * r, 2 * r)
    yt = _dot(ct.astype(BF16), f3)
    return jnp.swapaxes(yt.reshape(g, r, r), 1, 2)


def _filt_mlp_body(ft_ref, w1_ref, b1_ref, f1_ref, w2_ref, b2_ref, f2_ref, o_ref):
    z = jnp.sin(f1_ref[...] * (_dot3(w1_ref[...], ft_ref[0]) + b1_ref[...]))
    o_ref[0] = jnp.sin(f2_ref[...] * (_dot3(w2_ref[...], z) + b2_ref[...]))


def _filt_mlp(featst, w1t, b1, f1, w2t, b2, f2):
    _, ke, seq = featst.shape
    col = lambda d: (0, 0)
    return pl.pallas_call(
        _filt_mlp_body,
        grid=(2,),
        in_specs=[
            pl.BlockSpec((1, ke, seq), lambda d: (d, 0, 0)),
            pl.BlockSpec((HY_FFN, ke), col),
            pl.BlockSpec((HY_FFN, 1), col),
            pl.BlockSpec((HY_FFN, 1), col),
            pl.BlockSpec((HY_FFN, HY_FFN), col),
            pl.BlockSpec((HY_FFN, 1), col),
            pl.BlockSpec((HY_FFN, 1), col),
        ],
        out_specs=pl.BlockSpec((1, HY_FFN, seq), lambda d: (d, 0, 0)),
        out_shape=jax.ShapeDtypeStruct((2, HY_FFN, seq), F32),
        compiler_params=_params(("parallel",)),
        name="hyena_filter_mlp",
    )(featst, w1t, b1, f1, w2t, b2, f2)


def _filt_time_body(z_ref, wf_ref, wb_ref, dl_ref, tf_ref, tb_ref, o_ref):
    seq = z_ref.shape[2]
    hf = _dot3(wf_ref[...], z_ref[0]) * jnp.exp(-tf_ref[...] * dl_ref[...])
    hb = _dot3(wb_ref[...], z_ref[1]) * jnp.exp(-tb_ref[...] * dl_ref[...])
    lane = lax.broadcasted_iota(I32, hb.shape, 1)
    hb = jnp.where(lane == 0, 0.0, hb)
    norm = jnp.sum(jnp.abs(hf), axis=-1, keepdims=True) + jnp.sum(jnp.abs(hb), axis=-1, keepdims=True) + EPS
    hf = hf / norm
    hb = hb / norm
    nrow = hf.shape[0] // SUBLANES
    nch = seq // LANES
    for j in range(nch):
        o_ref[:, j, :, :] = hf[:, LANES * j:LANES * (j + 1)].reshape(nrow, SUBLANES, LANES)
        o_ref[:, nch + j, :, :] = hb[:, LANES * j:LANES * (j + 1)].reshape(nrow, SUBLANES, LANES)


def _filt_time(z2t, w3f, w3b, delta_rows, t_f, t_b):
    nrows = w3f.shape[0]
    seq = z2t.shape[2]
    rb = RB_FILT
    return pl.pallas_call(
        _filt_time_body,
        grid=(nrows // rb,),
        in_specs=[
            pl.BlockSpec((2, HY_FFN, seq), lambda r: (0, 0, 0)),
            pl.BlockSpec((rb, HY_FFN), lambda r: (r, 0)),
            pl.BlockSpec((rb, HY_FFN), lambda r: (r, 0)),
            pl.BlockSpec((rb, 1), lambda r: (r, 0)),
            pl.BlockSpec((1, seq), lambda r: (0, 0)),
            pl.BlockSpec((1, seq), lambda r: (0, 0)),
        ],
        out_specs=pl.BlockSpec((rb // SUBLANES, 2 * seq // LANES, SUBLANES, LANES), lambda r: (r, 0, 0, 0)),
        out_shape=jax.ShapeDtypeStruct((nrows // SUBLANES, 2 * seq // LANES, SUBLANES, LANES), F32),
        compiler_params=_params(("parallel",)),
        name="hyena_filter_time",
    )(z2t, w3f, w3b, delta_rows, t_f, t_b)


def _filt_fft_body(k_ref, f1_ref, f2_ref, tr_ref, ti_ref, kr_ref, ki_ref):
    tr = tr_ref[...]
    ti = ti_ref[...]
    for g in range(k_ref.shape[0]):
        x3 = jnp.swapaxes(k_ref[g], 0, 1)
        xr, xi = _fft_fwd(x3, f1_ref[...], f2_ref[...], tr, ti)
        kr_ref[SUBLANES * g:SUBLANES * (g + 1)] = xr.reshape(SUBLANES, FFT_R, FFT_R)
        ki_ref[SUBLANES * g:SUBLANES * (g + 1)] = xi.reshape(SUBLANES, FFT_R, FFT_R)


def _filt_fft(ktile, tables):
    n8, r, _, _ = ktile.shape
    nrows = n8 * SUBLANES
    rf = RF_KF
    out_spec = pl.BlockSpec((rf, r, r), lambda i: (i, 0, 0))
    consts = [tables[k] for k in ("f1", "f2", "tr", "ti")]
    return pl.pallas_call(
        _filt_fft_body,
        grid=(nrows // rf,),
        in_specs=[pl.BlockSpec((rf // SUBLANES, r, SUBLANES, LANES), lambda i: (i, 0, 0, 0))] + _const_specs(consts, 1),
        out_specs=[out_spec, out_spec],
        out_shape=[jax.ShapeDtypeStruct((nrows, r, r), F32)] * 2,
        compiler_params=_params(("parallel",)),
        name="hyena_filter_fft",
    )(ktile, *consts)


def _conv3_time(x, w):
    lane = lax.broadcasted_iota(I32, x.shape, 2)
    zero = jnp.zeros_like(x[:1])
    xm = pltpu.roll(x, 1, axis=2)
    xm = jnp.where(lane == 0, jnp.concatenate([zero, xm[:-1]], axis=0), xm)
    xp = pltpu.roll(x, LANES - 1, axis=2)
    xp = jnp.where(lane == LANES - 1, jnp.concatenate([xp[1:], zero], axis=0), xp)
    return w[0] * xm + w[1] * x + w[2] * xp + w[3]


def _hyena_body(sk_ref, v_ref, x1_ref, x2_ref, wv_ref, w1_ref, w2_ref, k0r_ref, k0i_ref, k1r_ref, k1i_ref,
                f1p_ref, f2_ref, f2c_ref, f3p_ref, tr_ref, ti_ref, o_ref, *, rc):
    ct = pl.program_id(0)
    tr = tr_ref[...]
    ti = ti_ref[...]
    np_ = v_ref.shape[2]
    nb = v_ref.shape[0]

    def load(ref, w_ref, g):
        parts = [jnp.swapaxes(_conv3_time(ref[bb, g], w_ref[:, g]), 0, 1) for bb in range(nb)]
        return jnp.concatenate(parts, axis=1)

    def long_conv(z3, kr_ref, ki_ref, g):
        xr, xi = _fft_fwd(z3, f1p_ref[...], f2_ref[...], tr, ti)
        kr = kr_ref[pl.ds(g * SUBLANES, SUBLANES)].reshape(SUBLANES * FFT_R, FFT_R)
        ki = ki_ref[pl.ds(g * SUBLANES, SUBLANES)].reshape(SUBLANES * FFT_R, FFT_R)
        yr = xr * kr - xi * ki
        yi = xr * ki + xi * kr
        return _fft_inv(yr, yi, f2c_ref[...], f3p_ref[...], tr, ti)

    def group(g, carry):
        cbase = ct * rc + g * SUBLANES

        def skip(z3, order):
            return jnp.stack([sk_ref[order, cbase + ci] * z3[ci] for ci in range(SUBLANES)], axis=0)

        v3 = load(v_ref, wv_ref, g)
        z1 = load(x1_ref, w1_ref, g) * (long_conv(v3, k0r_ref, k0i_ref, g) + skip(v3, 0))
        out = load(x2_ref, w2_ref, g) * (long_conv(z1, k1r_ref, k1i_ref, g) + skip(z1, 1))
        for bb in range(nb):
            o_ref[bb, g] = jnp.swapaxes(out[:, bb * np_:(bb + 1) * np_, :], 0, 1)
        return carry

    lax.fori_loop(0, rc // SUBLANES, group, 0)


def _hyena(hy5, cwt, skip, kfr, kfi, tables):
    bsz, n8, np_, _, _ = hy5.shape
    assert bsz % 2 == 0 and 2 * np_ == FFT_R
    rc = RC_HY
    r8 = rc // SUBLANES
    nct = D_HY // rc
    r = FFT_R

    def xspec(off):
        return pl.BlockSpec((2, r8, np_, SUBLANES, LANES), lambda c, b: (b, c + off * nct, 0, 0, 0))

    def wspec(off):
        return pl.BlockSpec((HY_CONV + 1, r8, SUBLANES, LANES), lambda c, b: (0, c + off * nct, 0, 0))

    def kspec(order):
        return pl.BlockSpec((rc, r, r), lambda c, b: (c + order * nct, 0, 0))

    consts = [tables[k] for k in ("f1p", "f2", "f2c", "f3p", "tr", "ti")]
    return pl.pallas_call(
        functools.partial(_hyena_body, rc=rc),
        grid=(nct, bsz // 2),
        in_specs=[pl.BlockSpec(memory_space=pltpu.SMEM), xspec(0), xspec(1), xspec(2), wspec(0), wspec(1), wspec(2),
                  kspec(0), kspec(0), kspec(1), kspec(1)] + _const_specs(consts, 2),
        out_specs=pl.BlockSpec((2, r8, np_, SUBLANES, LANES), lambda c, b: (b, c, 0, 0, 0)),
        out_shape=jax.ShapeDtypeStruct((bsz, D_HY // SUBLANES, np_, SUBLANES, LANES), F32),
        compiler_params=_params(("parallel", "parallel")),
        name="hyena",
    )(skip, hy5, hy5, hy5, cwt, cwt, cwt, kfr, kfi, kfr, kfi, *consts)


def _gelu_tanh(x):
    return 0.5 * x * (1.0 + jnp.tanh(math.sqrt(2.0 / math.pi) * (x + 0.044715 * (x * x * x))))


def _mix_body(hf_ref, hb_ref, gate_ref, yhy_ref, x_ref, gl_ref, gh_ref, wl_ref, wh_ref, gffn_ref,
              wrh_ref, wrl_ref, rb_ref, tri_ref, xmid_ref, h_ref, route_ref, cnt_ref):
    tm = x_ref.shape[1]
    y_lru = (hf_ref[0, 0] + hb_ref[0, 0]) * _gelu_tanh(gate_ref[0])
    nl = _rms(y_lru, gl_ref[...]).astype(BF16)
    gh = gh_ref[...]
    for j in range(tm // LANES):
        yt = yhy_ref[0, :, j, :, :].reshape(D_HY, LANES)
        nt = (yt * lax.rsqrt(jnp.mean(yt * yt, axis=0, keepdims=True) + EPS) * gh).astype(BF16)
        rows = slice(LANES * j, LANES * (j + 1))
        mix = _dot(nl[rows], wl_ref[...]) + lax.dot_general(
            nt, wh_ref[...], (((0,), (0,)), ((), ())), preferred_element_type=F32)
        xmid_ref[0, rows, :] = x_ref[0, rows, :] + mix
    h = _rms(xmid_ref[0], gffn_ref[...])
    h_ref[...] = h.astype(BF16)

    hh, hl = _split_bf16(h)
    logits = _dot(hh, wrh_ref[...]) + _dot(hl, wrh_ref[...]) + _dot(hh, wrl_ref[...]) + rb_ref[...]
    lane = lax.broadcasted_iota(I32, logits.shape, 1)
    big = jnp.int32(1 << 20)
    is_g = lane < N_GROUPS
    lg = jnp.where(is_g, logits, NEG_BIG)
    mg = jnp.max(lg, axis=-1, keepdims=True)
    gidx = jnp.min(jnp.where(lg == mg, lane, big), axis=-1, keepdims=True)
    g_p = 1.0 / jnp.sum(jnp.where(is_g, jnp.exp(lg - mg), 0.0), axis=-1, keepdims=True)
    e_lo = N_GROUPS + EXPERTS_PER_GROUP * gidx
    is_e = jnp.logical_and(lane >= e_lo, lane < e_lo + EXPERTS_PER_GROUP)
    le = jnp.where(is_e, logits, NEG_BIG)
    m1 = jnp.max(le, axis=-1, keepdims=True)
    i1 = jnp.min(jnp.where(le == m1, lane, big), axis=-1, keepdims=True)
    le2 = jnp.where(lane == i1, NEG_BIG, le)
    m2 = jnp.max(le2, axis=-1, keepdims=True)
    i2 = jnp.min(jnp.where(le2 == m2, lane, big), axis=-1, keepdims=True)
    ratio = jnp.exp(m2 - m1)
    gate1 = g_p / (1.0 + ratio)
    gate2 = g_p * ratio / (1.0 + ratio)
    oh1 = lane == i1
    oh2 = lane == i2
    oh = jnp.where(oh1, 1.0, jnp.where(oh2, 1.0, 0.0))
    tot = _dot(tri_ref[...], oh.astype(BF16))
    rank1 = jnp.sum(jnp.where(oh1, tot, 0.0), axis=-1, keepdims=True)
    rank2 = jnp.sum(jnp.where(oh2, tot, 0.0), axis=-1, keepdims=True)
    cnt_ref[0] = jnp.sum(oh, axis=0, keepdims=True)
    cols = [(i1 - N_GROUPS).astype(F32), (i2 - N_GROUPS).astype(F32), rank1, rank2, gate1, gate2]
    route = jnp.zeros(logits.shape, F32)
    for k, col in enumerate(cols):
        route = jnp.where(lane == k, col, route)
    route_ref[...] = route[:, :SUBLANES]


def _mix_router(h4, lru, yhy5, x, gl, gh, wl, wh, gffn, wrh, wrl, rb, tri):
    bsz, seq, d = x.shape
    tm = TM_MIX
    nt = seq // tm
    n_tok = bsz * seq
    c2 = lambda b, i: (0, 0)
    return pl.pallas_call(
        _mix_body,
        grid=(bsz, nt),
        in_specs=[
            pl.BlockSpec((1, 1, tm, D_LRU), lambda b, i: (0, b, i, 0)),
            pl.BlockSpec((1, 1, tm, D_LRU), lambda b, i: (1, b, i, 0)),
            pl.BlockSpec((1, tm, D_LRU), lambda b, i: (b, i, 1)),
            pl.BlockSpec((1, D_HY // SUBLANES, tm // LANES, SUBLANES, LANES), lambda b, i: (b, 0, i, 0, 0)),
            pl.BlockSpec((1, tm, d), lambda b, i: (b, i, 0)),
            pl.BlockSpec((1, D_LRU), c2),
            pl.BlockSpec((D_HY, 1), c2),
            pl.BlockSpec((D_LRU, d), c2),
            pl.BlockSpec((D_HY, d), c2),
            pl.BlockSpec((1, d), c2),
            pl.BlockSpec((d, LANES), c2),
            pl.BlockSpec((d, LANES), c2),
            pl.BlockSpec((1, LANES), c2),
            pl.BlockSpec((tm, tm), c2),
        ],
        out_specs=[
            pl.BlockSpec((1, tm, d), lambda b, i: (b, i, 0)),
            pl.BlockSpec((tm, d), lambda b, i: (b * nt + i, 0)),
            pl.BlockSpec((tm, SUBLANES), lambda b, i: (b * nt + i, 0)),
            pl.BlockSpec((1, 1, LANES), lambda b, i: (b * nt + i, 0, 0)),
        ],
        out_shape=[
            jax.ShapeDtypeStruct((bsz, seq, d), F32),
            jax.ShapeDtypeStruct((n_tok, d), BF16),
            jax.ShapeDtypeStruct((n_tok, SUBLANES), F32),
            jax.ShapeDtypeStruct((n_tok // tm, 1, LANES), F32),
        ],
        compiler_params=_params(("parallel", "parallel")),
        name="mix_router",
    )(h4, h4, lru, yhy5, x, gl, gh, wl, wh, gffn, wrh, wrl, rb, tri)


def _local_slots(route, lsl):
    lane = lax.broadcasted_iota(I32, (route.shape[0], LANES), 1)
    pos = []
    for k in range(2):
        start = jnp.sum(jnp.where(lane == route[:, k:k + 1].astype(I32), lsl, 0.0), axis=-1, keepdims=True)
        pos.append((start + route[:, 2 + k:3 + k]).astype(I32))
    return pos


def _run_chunks(i, nch_ref, gs_ref, ls_ref, make_copy):
    def per_expert(e, total):
        idx = i * N_EXPERTS + e
        n = nch_ref[idx]
        gs = gs_ref[idx]
        ls = ls_ref[idx]

        def per_chunk(k, c):
            make_copy(ls + k * ROW_CHUNK, gs + k * ROW_CHUNK).start()
            return c

        lax.fori_loop(0, n, per_chunk, 0)
        return total + n

    total = lax.fori_loop(0, N_EXPERTS, per_expert, 0)

    def wait(k, c):
        make_copy(0, 0).wait()
        return c

    lax.fori_loop(0, total, wait, 0)


def _dispatch_body(nch_ref, gs_ref, ls_ref, tn_ref, ts_ref, nu_ref, h_ref, route_ref, lsl_ref, xb_ref, buf, zbuf, sem):
    i = pl.program_id(0)
    tm = h_ref.shape[0]
    lbuf = buf.shape[0]
    n_blocks = xb_ref.shape[0] // MOE_BLK

    def aligned(r):
        return pl.ds(pl.multiple_of(r, ROW_CHUNK), ROW_CHUNK)

    @pl.when(i == 0)
    def _():
        zbuf[...] = jnp.zeros_like(zbuf)

        def zero_chunk(r):
            return pltpu.make_async_copy(zbuf.at[pl.ds(0, ROW_CHUNK)], xb_ref.at[aligned(r)], sem)

        def zero_block(j):
            return pltpu.make_async_copy(zbuf, xb_ref.at[pl.ds(pl.multiple_of(j * MOE_BLK, MOE_BLK), MOE_BLK)], sem)

        def per_expert(e, total):
            def per_chunk(k, c):
                zero_chunk(ts_ref[e] + k * ROW_CHUNK).start()
                return c

            lax.fori_loop(0, tn_ref[e], per_chunk, 0)
            return total + tn_ref[e]

        total = lax.fori_loop(0, N_EXPERTS, per_expert, 0)

        def start_block(j, c):
            zero_block(j).start()
            return c

        lax.fori_loop(nu_ref[0], n_blocks, start_block, 0)

        def wait_chunk(k, c):
            zero_chunk(0).wait()
            return c

        lax.fori_loop(0, total, wait_chunk, 0)

        def wait_block(j, c):
            zero_block(0).wait()
            return c

        lax.fori_loop(nu_ref[0], n_blocks, wait_block, 0)

    l0, l1 = _local_slots(route_ref[...], lsl_ref[0])
    lane = lax.broadcasted_iota(I32, (tm, LANES), 1)
    slots_t = jnp.where(lane == 0, l0, jnp.where(lane == 1, l1, 0)).astype(F32).T
    l0t = slots_t[0:1].astype(I32)
    l1t = slots_t[1:2].astype(I32)
    row = lax.broadcasted_iota(I32, (lbuf, tm), 0)
    perm = jnp.where(row == l0t, 1.0, jnp.where(row == l1t, 1.0, 0.0)).astype(BF16)
    buf[...] = _dot(perm, h_ref[...]).astype(BF16)

    def make_copy(src_row, dst_row):
        return pltpu.make_async_copy(buf.at[aligned(src_row)], xb_ref.at[aligned(dst_row)], sem)

    _run_chunks(i, nch_ref, gs_ref, ls_ref, make_copy)


def _dispatch(nch, gstart, lstart, tail_n, tail_s, n_used, h, route, lsl, n_slots):
    n_tok, d = h.shape
    tm = TM_MIX
    grid_spec = pltpu.PrefetchScalarGridSpec(
        num_scalar_prefetch=6,
        grid=(n_tok // tm,),
        in_specs=[
            pl.BlockSpec((tm, d), lambda i, *_: (i, 0)),
            pl.BlockSpec((tm, SUBLANES), lambda i, *_: (i, 0)),
            pl.BlockSpec((1, 1, LANES), lambda i, *_: (i, 0, 0)),
        ],
        out_specs=pl.BlockSpec(memory_space=pl.ANY),
        scratch_shapes=[pltpu.VMEM((LOCAL_ROWS, d), BF16), pltpu.VMEM((MOE_BLK, d), BF16),
                        pltpu.SemaphoreType.DMA(())],
    )
    return pl.pallas_call(
        _dispatch_body,
        grid_spec=grid_spec,
        out_shape=jax.ShapeDtypeStruct((n_slots, d), BF16),
        compiler_params=_params(("arbitrary",)),
        name="moe_dispatch",
    )(nch, gstart, lstart, tail_n, tail_s, n_used, h, route, lsl)


def _expert_body(be_ref, fl_ref, nu_ref, xb_ref, w1_ref, w3_ref, w2_ref, yb_ref, w1_sc, w3_sc, w2_sc):
    del be_ref
    j = pl.program_id(0)
    used = j < nu_ref[0]

    @pl.when(jnp.logical_and(used, fl_ref[j] == 1))
    def _():
        w1_sc[...] = w1_ref[...].astype(BF16)
        w3_sc[...] = w3_ref[...].astype(BF16)
        w2_sc[...] = w2_ref[...].astype(BF16)

    @pl.when(used)
    def _():
        xb = xb_ref[...]
        a = _dot(xb, w1_sc[...])
        b = _dot(xb, w3_sc[...])
        act = (a * _sigmoid(a) * b).astype(BF16)
        yb_ref[...] = _dot(act, w2_sc[...]).astype(BF16)

    @pl.when(jnp.logical_not(used))
    def _():
        yb_ref[...] = jnp.zeros_like(yb_ref)


def _experts(block_e, first_flag, n_used, xb, w1, w3, w2):
    n_slots, d = xb.shape
    blk = MOE_BLK
    de = w1.shape[2]
    grid_spec = pltpu.PrefetchScalarGridSpec(
        num_scalar_prefetch=3,
        grid=(n_slots // blk,),
        in_specs=[
            pl.BlockSpec((blk, d), lambda j, be, fl, nu: (j, 0)),
            pl.BlockSpec((None, d, de), lambda j, be, fl, nu: (be[j], 0, 0)),
            pl.BlockSpec((None, d, de), lambda j, be, fl, nu: (be[j], 0, 0)),
            pl.BlockSpec((None, de, d), lambda j, be, fl, nu: (be[j], 0, 0)),
        ],
        out_specs=pl.BlockSpec((blk, d), lambda j, be, fl, nu: (j, 0)),
        scratch_shapes=[pltpu.VMEM((d, de), BF16), pltpu.VMEM((d, de), BF16), pltpu.VMEM((de, d), BF16)],
    )
    return pl.pallas_call(
        _expert_body,
        grid_spec=grid_spec,
        out_shape=jax.ShapeDtypeStruct((n_slots, d), BF16),
        compiler_params=_params(("arbitrary",)),
        name="moe_experts",
    )(block_e, first_flag, n_used, xb, w1, w3, w2)


def _combine_body(nch_ref, gs_ref, ls_ref, xmid_ref, route_ref, lsl_ref, g_ref, yb_ref, o_ref, buf, sem):
    i = pl.program_id(0)
    tm = xmid_ref.shape[0]
    lbuf = buf.shape[0]

    @pl.when(i == 0)
    def _():
        buf[...] = jnp.zeros_like(buf)

    def aligned(r):
        return pl.ds(pl.multiple_of(r, ROW_CHUNK), ROW_CHUNK)

    def make_copy(local_row, slot_row):
        return pltpu.make_async_copy(yb_ref.at[aligned(slot_row)], buf.at[aligned(local_row)], sem)

    _run_chunks(i, nch_ref, gs_ref, ls_ref, make_copy)
    route = route_ref[...]
    l0, l1 = _local_slots(route, lsl_ref[0])
    lane = lax.broadcasted_iota(I32, (tm, lbuf), 1)
    gmat = (jnp.where(lane == l0, route[:, 4:5], 0.0) + jnp.where(lane == l1, route[:, 5:6], 0.0)).astype(BF16)
    y = xmid_ref[...] + _dot(gmat, buf[...])
    o_ref[...] = _rms(y, g_ref[...])


def _combine(nch, gstart, lstart, xmid, route, lsl, g, yb):
    n_tok, d = xmid.shape
    tm = TM_MIX
    grid_spec = pltpu.PrefetchScalarGridSpec(
        num_scalar_prefetch=3,
        grid=(n_tok // tm,),
        in_specs=[
            pl.BlockSpec((tm, d), lambda i, *_: (i, 0)),
            pl.BlockSpec((tm, SUBLANES), lambda i, *_: (i, 0)),
            pl.BlockSpec((1, 1, LANES), lambda i, *_: (i, 0, 0)),
            pl.BlockSpec((1, d), lambda i, *_: (0, 0)),
            pl.BlockSpec(memory_space=pl.ANY),
        ],
        out_specs=pl.BlockSpec((tm, d), lambda i, *_: (i, 0)),
        scratch_shapes=[pltpu.VMEM((LOCAL_ROWS, d), BF16), pltpu.SemaphoreType.DMA(())],
    )
    return pl.pallas_call(
        _combine_body,
        grid_spec=grid_spec,
        out_shape=jax.ShapeDtypeStruct((n_tok, d), F32),
        compiler_params=_params(("arbitrary",)),
        name="moe_combine",
    )(nch, gstart, lstart, xmid, route, lsl, g, yb)


def _filter_features(seq):
    pos = jnp.arange(seq, dtype=F32)
    t = jnp.linspace(0.0, 1.0, seq, dtype=F32)
    w = (2.0 * math.pi / seq) * pos
    bands = jnp.linspace(1e-4, HY_BANDS - 1, HY_BANDS, dtype=F32)
    ang = w[:, None] * bands[None, :]
    feats = jnp.concatenate([t[:, None], jnp.cos(ang), jnp.sin(ang)], axis=-1)
    rev = lambda a: jnp.concatenate([jnp.zeros_like(a[:1]), a[1:][::-1]], axis=0)
    pad = lambda a: jnp.pad(a, ((0, 0), (0, HY_EMB_PAD - HY_EMB)))
    featst = jnp.stack([pad(feats).T, pad(rev(feats)).T], axis=0)
    return featst, t[None, :], rev(t[:, None]).T


def _gate_weights(wr, wi):
    half = D_LRU // 2
    hph = half // LRU_HEAD_DIM
    eye = jnp.eye(hph, dtype=wr.dtype)[:, None, :, None]

    def block_diag(w):
        return (eye * w[:, :, None, :]).reshape(half, half)

    out = [jnp.concatenate([block_diag(wr[hh * hph:(hh + 1) * hph]), block_diag(wi[hh * hph:(hh + 1) * hph])], axis=1)
           for hh in range(2)]
    return jnp.stack(out, axis=0).astype(BF16)


def _layer(x, l, p):
    bsz, seq, d = x.shape
    n_tok = bsz * seq
    row = lambda a: a.reshape(1, -1).astype(F32)
    col = lambda a: a.reshape(-1, 1).astype(F32)

    w_in = p["w_in"][l]
    wl = w_in[:, :2 * D_LRU].astype(BF16)
    wht = w_in[:, 2 * D_LRU:].T.astype(BF16)
    lru, hy5 = _inproj(x, row(p["norm_mix_g"][l]), wl, wht)

    wg = jnp.stack([_gate_weights(p["lru_wr_f"][l], p["lru_wi_f"][l]),
                    _gate_weights(p["lru_wr_b"][l], p["lru_wi_b"][l])], axis=0)
    stack2 = lambda a, b: jnp.stack([a.reshape(1, -1), b.reshape(1, -1)], axis=0).astype(F32)
    h4 = _lru(lru, p["lru_conv_w"][l], row(p["lru_conv_b"][l]), wg,
              stack2(p["lru_br_f"][l], p["lru_br_b"][l]), stack2(p["lru_bi_f"][l], p["lru_bi_b"][l]),
              stack2(p["lru_lambda_f"][l], p["lru_lambda_b"][l]))

    tables = _fft_tables()
    featst, t_f, t_b = _filter_features(seq)
    w1t = jnp.pad(p["hy_filt_w1"][l].T, ((0, 0), (0, HY_EMB_PAD - HY_EMB)))
    z2t = _filt_mlp(featst, w1t, col(p["hy_filt_b1"][l]), col(p["hy_filt_freq1"][l]),
                    p["hy_filt_w2"][l].T, col(p["hy_filt_b2"][l]), col(p["hy_filt_freq2"][l]))
    w3 = p["hy_filt_w3"][l]
    nfr = HY_ORDER * D_HY
    deltas = jnp.abs(jnp.linspace(math.log(HY_DECAY_TARGET) / HY_SLOW_DECAY_PCT,
                                  math.log(HY_DECAY_TARGET) / HY_FAST_DECAY_PCT, D_HY, dtype=F32))
    ktile = _filt_time(z2t, w3[:, :nfr].T, w3[:, nfr:].T, col(jnp.tile(deltas, HY_ORDER)), t_f, t_b)
    kfr, kfi = _filt_fft(ktile, tables)
    cwt = jnp.concatenate([p["hy_conv_w"][l], p["hy_conv_b"][l][None, :]], axis=0)
    cwt = jnp.broadcast_to(cwt.reshape(HY_CONV + 1, -1, SUBLANES, 1), (HY_CONV + 1, cwt.shape[1] // SUBLANES, SUBLANES, LANES))
    yhy5 = _hyena(hy5, cwt, p["hy_skip"][l], kfr, kfi, tables)

    w_out = p["w_out"][l]
    wr_cat = jnp.concatenate([p["router_group_w"][l], p["router_expert_w"][l]], axis=1)
    wr_cat = jnp.pad(wr_cat, ((0, 0), (0, LANES - wr_cat.shape[1])))
    wrh, wrl = _split_bf16(wr_cat)
    rb = jnp.concatenate([p["router_group_b"][l], p["router_expert_b"][l]])
    rb = jnp.pad(rb, (0, LANES - rb.shape[0])).reshape(1, LANES)
    tri = (jnp.arange(TM_MIX)[:, None] > jnp.arange(TM_MIX)[None, :]).astype(BF16)
    xmid, h, route, tile_cnt = _mix_router(
        h4, lru, yhy5, x, row(p["grp_norm_lru_g"][l]), col(p["grp_norm_hy_g"][l]),
        w_out[:D_LRU].astype(BF16), w_out[D_LRU:].astype(BF16), row(p["norm_ffn_g"][l]), wrh, wrl, rb, tri)

    blk = MOE_BLK
    ch = ROW_CHUNK
    n_tiles = n_tok // TM_MIX
    n_blocks = (2 * n_tok + n_tiles * N_EXPERTS * (ch - 1)) // blk + N_EXPERTS
    cnt = tile_cnt[:, 0, N_GROUPS:N_GROUPS + N_EXPERTS].astype(I32)
    run = (cnt + ch - 1) // ch * ch
    tot_e = jnp.sum(run, axis=0)
    padded_e = (tot_e + blk - 1) // blk * blk
    pad_end = jnp.cumsum(padded_e)
    pad_start = pad_end - padded_e
    gstart = pad_start[None, :] + jnp.cumsum(run, axis=0) - run
    lstart = jnp.cumsum(run, axis=1) - run
    lsl = jnp.pad(lstart.astype(F32), ((0, 0), (0, LANES - N_EXPERTS))).reshape(n_tiles, 1, LANES)
    flat = lambda a: a.reshape(-1).astype(I32)
    nch, gstart, lstart = flat(run // ch), flat(gstart), flat(lstart)
    tail_n = flat((padded_e - tot_e) // ch)
    tail_s = flat(pad_start + tot_e)
    block_start = jnp.arange(n_blocks, dtype=I32) * blk
    block_e = jnp.minimum(jnp.sum(block_start[:, None] >= pad_end[None, :], axis=1), N_EXPERTS - 1).astype(I32)
    first_flag = jnp.concatenate([jnp.ones((1,), I32), (block_e[1:] != block_e[:-1]).astype(I32)])
    n_used = (pad_end[-1:] // blk).astype(I32)

    xb = _dispatch(nch, gstart, lstart, tail_n, tail_s, n_used, h, route, lsl, n_blocks * blk)
    yb = _experts(block_e, first_flag, n_used, xb, p["exp_w1"][l], p["exp_w3"][l], p["exp_w2"][l])
    return xmid.reshape(n_tok, d), route, (nch, gstart, lstart, lsl), yb


def kernel(x, norm_mix_g, w_in, lru_conv_w, lru_conv_b, lru_wr_f, lru_br_f, lru_wi_f, lru_bi_f, lru_lambda_f, lru_wr_b, lru_br_b, lru_wi_b, lru_bi_b, lru_lambda_b, hy_conv_w, hy_conv_b, hy_filt_w1, hy_filt_b1, hy_filt_freq1, hy_filt_w2, hy_filt_b2, hy_filt_freq2, hy_filt_w3, hy_skip, grp_norm_lru_g, grp_norm_hy_g, w_out, norm_ffn_g, router_group_w, router_group_b, router_expert_w, router_expert_b, exp_w1, exp_w3, exp_w2, norm_final_g):
    p = dict(locals())
    bsz, seq, d = x.shape
    assert d == D_MODEL and 2 * seq == FFT_R * FFT_R and w_in.shape[0] == 1
    xmid, route, (nch, gstart, lstart, lsl), yb = _layer(x, 0, p)
    out = _combine(nch, gstart, lstart, xmid, route, lsl, norm_final_g.reshape(1, d), yb)
    return out.reshape(bsz, seq, d)
```

```python
import functools
import math

import numpy as np
import jax
import jax.numpy as jnp
from jax import lax
from jax.experimental import pallas as pl
from jax.experimental.pallas import tpu as pltpu

F32 = jnp.float32
BF16 = jnp.bfloat16
I32 = jnp.int32

D_MODEL = 1024
D_LRU = 512
D_HY = 512
LRU_HEADS = 8
LRU_HEAD_DIM = D_LRU // LRU_HEADS
LRU_CONV = 4
LRU_C = 8.0
HY_ORDER = 2
HY_CONV = 3
HY_BANDS = 16
HY_EMB = 2 * HY_BANDS + 1
HY_EMB_PAD = 40
HY_FFN = 64
HY_FAST_DECAY_PCT = 0.3
HY_SLOW_DECAY_PCT = 1.5
HY_DECAY_TARGET = 1e-2
N_GROUPS = 4
EXPERTS_PER_GROUP = 8
N_EXPERTS = N_GROUPS * EXPERTS_PER_GROUP
D_EXPERT = D_MODEL // 2
EPS = 1e-6

LANES = 128
SUBLANES = 8
FFT_R = 128
VMEM_LIMIT = 56 * 1024 * 1024

TM_IN = 512
TC_LRU = 512
RC_HY = 16
RF_KF = 16
RB_FILT = 64
TM_MIX = 512
MOE_BLK = 512
ROW_CHUNK = 16
LOCAL_ROWS = -(-(2 * TM_MIX + N_EXPERTS * (ROW_CHUNK - 1)) // LANES) * LANES
NEG_BIG = -1e30


def _params(sem, vmem=VMEM_LIMIT):
    return pltpu.CompilerParams(dimension_semantics=sem, vmem_limit_bytes=vmem)


def _rms(x, g):
    return x * lax.rsqrt(jnp.mean(x * x, axis=-1, keepdims=True) + EPS) * g


def _sigmoid(x):
    return 1.0 / (1.0 + jnp.exp(-x))


def _split_bf16(a):
    hi = a.astype(BF16)
    lo = (a - hi.astype(F32)).astype(BF16)
    return hi, lo


def _dot(a, b):
    return jnp.dot(a, b, preferred_element_type=F32)


def _dot3(a, b):
    ah, al = _split_bf16(a)
    bh, bl = _split_bf16(b)
    return _dot(ah, bh) + _dot(al, bh) + _dot(ah, bl)


def _inproj_body(x_ref, g_ref, wl_ref, wht_ref, lru_ref, hy_ref):
    hn = _rms(x_ref[0], g_ref[...]).astype(BF16)
    lru_ref[0] = _dot(hn, wl_ref[...])
    hyt = lax.dot_general(wht_ref[...], hn, (((1,), (1,)), ((), ())), preferred_element_type=F32)
    nrow = hyt.shape[0] // SUBLANES
    for j in range(hyt.shape[1] // LANES):
        hy_ref[0, :, j, :, :] = hyt[:, LANES * j:LANES * (j + 1)].reshape(nrow, SUBLANES, LANES)


def _inproj(x, g, wl, wht):
    bsz, seq, d = x.shape
    nl = wl.shape[1]
    nh = wht.shape[0]
    tm = TM_IN
    return pl.pallas_call(
        _inproj_body,
        grid=(bsz, seq // tm),
        in_specs=[
            pl.BlockSpec((1, tm, d), lambda b, i: (b, i, 0)),
            pl.BlockSpec((1, d), lambda b, i: (0, 0)),
            pl.BlockSpec((d, nl), lambda b, i: (0, 0)),
            pl.BlockSpec((nh, d), lambda b, i: (0, 0)),
        ],
        out_specs=[
            pl.BlockSpec((1, tm, nl), lambda b, i: (b, i, 0)),
            pl.BlockSpec((1, nh // SUBLANES, tm // LANES, SUBLANES, LANES), lambda b, i: (b, 0, i, 0, 0)),
        ],
        out_shape=[
            jax.ShapeDtypeStruct((bsz, seq, nl), F32),
            jax.ShapeDtypeStruct((bsz, nh // SUBLANES, seq // LANES, SUBLANES, LANES), F32),
        ],
        compiler_params=_params(("parallel", "parallel")),
        name="inproj",
    )(x, g, wl, wht)


def _one_minus_sq(a, log_a):
    x = 2.0 * log_a
    series = -x * (1.0 + x * (0.5 + x * (1.0 / 6.0)))
    return jnp.where(x > -4e-3, series, 1.0 - a * a)


def _group_scan(a3, b3, reverse):
    sub = lax.broadcasted_iota(I32, a3.shape, 1)
    for s in (1, 2, 4):
        if reverse:
            a_sh = pltpu.roll(a3, SUBLANES - s, axis=1)
            b_sh = pltpu.roll(b3, SUBLANES - s, axis=1)
            m = sub < SUBLANES - s
        else:
            a_sh = pltpu.roll(a3, s, axis=1)
            b_sh = pltpu.roll(b3, s, axis=1)
            m = sub >= s
        b3 = jnp.where(m, a3 * b_sh + b3, b3)
        a3 = jnp.where(m, a3 * a_sh, a3)
    return a3, b3


def _lru_body(cur_ref, prev_ref, next_ref, cw_ref, cb_ref, wg_ref, br_ref, bi_ref, lam_ref, o_ref,
              h_sc, carry_sc, *, nt, tc):
    d = pl.program_id(1)
    i = pl.program_id(2)
    c = jnp.where(d == 0, i, nt - 1 - i)
    ns = SUBLANES
    ng = tc // ns
    half = D_LRU // 2

    @pl.when(i == 0)
    def _():
        carry_sc[...] = jnp.zeros_like(carry_sc)

    xs = jnp.swapaxes(cur_ref[0].reshape(ns, ng, D_LRU), 0, 1)
    prev = jnp.where(c > 0, prev_ref[0], 0.0)
    nxt = jnp.where(c < nt - 1, next_ref[0], 0.0)
    sub = lax.broadcasted_iota(I32, (ns, D_LRU), 0)

    def prev_segment(slab, halo_row):
        return jnp.where(sub == 0, halo_row, pltpu.roll(slab, 1, axis=0))

    def next_segment(slab, halo_row):
        return jnp.where(sub == ns - 1, halo_row, pltpu.roll(slab, ns - 1, axis=0))

    xext = jnp.concatenate([prev_segment(xs[ng - 2], prev[ns - 2:ns - 1])[None],
                            prev_segment(xs[ng - 1], prev[ns - 1:ns])[None], xs,
                            next_segment(xs[0], nxt[0:1])[None]], axis=0)
    cw = cw_ref[...]
    xc = cb_ref[...] + cw[0:1] * xext[0:ng]
    for k in range(1, LRU_CONV):
        xc = xc + cw[k:k + 1] * xext[k:k + ng]
    xc = xc.reshape(tc, D_LRU)

    lam = lam_ref[0]
    nlam = -lam
    softplus = jnp.maximum(nlam, 0.0) + jnp.log1p(jnp.exp(-jnp.abs(nlam)))
    half_c = (-0.5 * LRU_C) * softplus

    def gates(hh):
        sl = slice(half * hh, half * (hh + 1))
        xh = xc[:, sl]
        logits = _dot(xh.astype(BF16), wg_ref[0, hh])
        tr_ = jnp.tanh(0.5 * (logits[:, :half] + br_ref[0][:, sl]))
        gi = 0.5 + 0.5 * jnp.tanh(0.5 * (logits[:, half:] + bi_ref[0][:, sl]))
        log_a = half_c[:, sl] * (1.0 + tr_)
        a = jnp.exp(log_a)
        b = jnp.sqrt(_one_minus_sq(a, log_a)) * (gi * xh)
        return a.reshape(ng, ns, half), b.reshape(ng, ns, half)

    def run(reverse):
        subh = lax.broadcasted_iota(I32, (ns, half), 0)
        for hh in range(2):
            sl = slice(half * hh, half * (hh + 1))
            a3, b3 = gates(hh)
            h = jnp.zeros((ns, half), F32)
            p = jnp.ones((ns, half), F32)
            hs = [None] * ng
            ps = [None] * ng
            for g in (range(ng - 1, -1, -1) if reverse else range(ng)):
                h = a3[g] * h + b3[g]
                p = a3[g] * p
                hs[g] = h
                ps[g] = p
            ac, hc = _group_scan(p[None], h[None], reverse)
            c_in = carry_sc[:, sl]
            end = ac[0] * c_in + hc[0]
            if reverse:
                seg_in = jnp.where(subh == ns - 1, c_in, pltpu.roll(end, ns - 1, axis=0))
                carry_sc[:, sl] = jnp.broadcast_to(end[0:1], end.shape)
            else:
                seg_in = jnp.where(subh == 0, c_in, pltpu.roll(end, 1, axis=0))
                carry_sc[:, sl] = jnp.broadcast_to(end[ns - 1:ns], end.shape)
            h_sc[:, :, sl] = jnp.stack(hs, axis=0) + jnp.stack(ps, axis=0) * seg_in[None]
        o_ref[0, 0] = jnp.swapaxes(h_sc[...], 0, 1).reshape(tc, D_LRU)

    @pl.when(d == 0)
    def _():
        run(False)

    @pl.when(d == 1)
    def _():
        run(True)


def _lru(lru, cw, cb, wg, br, bi, lam):
    bsz, seq, _ = lru.shape
    tc = TC_LRU
    nt = seq // tc
    r8 = tc // SUBLANES
    nrow8 = seq // SUBLANES

    def cidx(d, i):
        return jnp.where(d == 0, i, nt - 1 - i)

    return pl.pallas_call(
        functools.partial(_lru_body, nt=nt, tc=tc),
        grid=(bsz, 2, nt),
        in_specs=[
            pl.BlockSpec((1, tc, D_LRU), lambda b, d, i: (b, cidx(d, i), 0)),
            pl.BlockSpec((1, SUBLANES, D_LRU), lambda b, d, i: (b, jnp.maximum(cidx(d, i) * r8 - 1, 0), 0)),
            pl.BlockSpec((1, SUBLANES, D_LRU), lambda b, d, i: (b, jnp.minimum((cidx(d, i) + 1) * r8, nrow8 - 1), 0)),
            pl.BlockSpec((LRU_CONV, D_LRU), lambda b, d, i: (0, 0)),
            pl.BlockSpec((1, D_LRU), lambda b, d, i: (0, 0)),
            pl.BlockSpec((1, 2, D_LRU // 2, D_LRU), lambda b, d, i: (d, 0, 0, 0)),
            pl.BlockSpec((1, 1, D_LRU), lambda b, d, i: (d, 0, 0)),
            pl.BlockSpec((1, 1, D_LRU), lambda b, d, i: (d, 0, 0)),
            pl.BlockSpec((1, 1, D_LRU), lambda b, d, i: (d, 0, 0)),
        ],
        out_specs=pl.BlockSpec((1, 1, tc, D_LRU), lambda b, d, i: (d, b, cidx(d, i), 0)),
        out_shape=jax.ShapeDtypeStruct((2, bsz, seq, D_LRU), F32),
        scratch_shapes=[
            pltpu.VMEM((tc // SUBLANES, SUBLANES, D_LRU), F32),
            pltpu.VMEM((SUBLANES, D_LRU), F32),
        ],
        compiler_params=_params(("parallel", "arbitrary", "arbitrary")),
        name="rg_lru",
    )(lru, lru, lru, cw, cb, wg, br, bi, lam)


def _fft_tables():
    r = FFT_R
    n = r * r
    idx = np.arange(r, dtype=np.float64)
    ang = 2.0 * np.pi * np.outer(idx, idx) / r
    wr, wi = np.cos(ang), -np.sin(ang)
    angt = 2.0 * np.pi * np.outer(idx, idx) / n
    tr, ti = np.cos(angt), -np.sin(angt)
    h = r // 2
    f1 = np.concatenate([wr, wi], axis=1)
    f1p = np.block([[wr[:h], wi[:h]], [-wi[:h], wr[:h]]])
    f2 = np.block([[wr, wi], [-wi, wr]])
    f2c = np.block([[wr, -wi], [wi, wr]])
    f3p = np.block([[wr[:, :h], -wi[:, :h]], [wi[:, :h], wr[:, :h]]]) / n
    as_bf16 = lambda a: jnp.asarray(a, F32).astype(BF16)
    return dict(f1=as_bf16(f1), f1p=as_bf16(f1p), f2=as_bf16(f2), f2c=as_bf16(f2c), f3p=as_bf16(f3p),
                tr=jnp.asarray(tr, F32), ti=jnp.asarray(ti, F32))


def _const_specs(arrays, nargs):
    return [pl.BlockSpec(a.shape, (lambda nd: (lambda *_: (0,) * nd))(a.ndim)) for a in arrays]


def _fft_fwd(x3, f1, f2, tr, ti):
    g = x3.shape[0]
    r = FFT_R
    xt = jnp.swapaxes(x3, 1, 2).reshape(g * r, r)
    a = _dot(xt.astype(BF16), f1)
    ar = a[:, :r].reshape(g, r, r)
    ai = a[:, r:].reshape(g, r, r)
    pr = ar * tr - ai * ti
    pi = ar * ti + ai * tr
    pt = jnp.concatenate([jnp.swapaxes(pr, 1, 2), jnp.swapaxes(pi, 1, 2)], axis=-1).reshape(g * r, 2 * r)
    x = _dot(pt.astype(BF16), f2)
    return x[:, :r], x[:, r:]


def _fft_inv(yr, yi, f2c, f3, tr, ti):
    r = FFT_R
    g = yr.shape[0] // r
    b = _dot(jnp.concatenate([yr, yi], axis=-1).astype(BF16), f2c)
    br = b[:, :r].reshape(g, r, r)
    bi = b[:, r:].reshape(g, r, r)
    cr = br * tr + bi * ti
    ci = bi * tr - br * ti
    ct = jnp.concatenate([jnp.swapaxes(cr, 1, 2), jnp.swapaxes(ci, 1, 2)], axis=-1).reshape(g * r, 2 * r)
    yt = _dot(ct.astype(BF16), f3)
    return jnp.swapaxes(yt.reshape(g, r, r), 1, 2)


def _filt_mlp_body(ft_ref, w1_ref, b1_ref, f1_ref, w2_ref, b2_ref, f2_ref, o_ref):
    z = jnp.sin(f1_ref[...] * (_dot3(w1_ref[...], ft_ref[0]) + b1_ref[...]))
    o_ref[0] = jnp.sin(f2_ref[...] * (_dot3(w2_ref[...], z) + b2_ref[...]))


def _filt_mlp(featst, w1t, b1, f1, w2t, b2, f2):
    _, ke, seq = featst.shape
    col = lambda d: (0, 0)
    return pl.pallas_call(
        _filt_mlp_body,
        grid=(2,),
        in_specs=[
            pl.BlockSpec((1, ke, seq), lambda d: (d, 0, 0)),
            pl.BlockSpec((HY_FFN, ke), col),
            pl.BlockSpec((HY_FFN, 1), col),
            pl.BlockSpec((HY_FFN, 1), col),
            pl.BlockSpec((HY_FFN, HY_FFN), col),
            pl.BlockSpec((HY_FFN, 1), col),
            pl.BlockSpec((HY_FFN, 1), col),
        ],
        out_specs=pl.BlockSpec((1, HY_FFN, seq), lambda d: (d, 0, 0)),
        out_shape=jax.ShapeDtypeStruct((2, HY_FFN, seq), F32),
        compiler_params=_params(("parallel",)),
        name="hyena_filter_mlp",
    )(featst, w1t, b1, f1, w2t, b2, f2)


def _filt_time_body(z_ref, wf_ref, wb_ref, dl_ref, tf_ref, tb_ref, o_ref):
    seq = z_ref.shape[2]
    hf = _dot3(wf_ref[...], z_ref[0]) * jnp.exp(-tf_ref[...] * dl_ref[...])
    hb = _dot3(wb_ref[...], z_ref[1]) * jnp.exp(-tb_ref[...] * dl_ref[...])
    lane = lax.broadcasted_iota(I32, hb.shape, 1)
    hb = jnp.where(lane == 0, 0.0, hb)
    norm = jnp.sum(jnp.abs(hf), axis=-1, keepdims=True) + jnp.sum(jnp.abs(hb), axis=-1, keepdims=True) + EPS
    hf = hf / norm
    hb = hb / norm
    nrow = hf.shape[0] // SUBLANES
    nch = seq // LANES
    for j in range(nch):
        o_ref[:, j, :, :] = hf[:, LANES * j:LANES * (j + 1)].reshape(nrow, SUBLANES, LANES)
        o_ref[:, nch + j, :, :] = hb[:, LANES * j:LANES * (j + 1)].reshape(nrow, SUBLANES, LANES)


def _filt_time(z2t, w3f, w3b, delta_rows, t_f, t_b):
    nrows = w3f.shape[0]
    seq = z2t.shape[2]
    rb = RB_FILT
    return pl.pallas_call(
        _filt_time_body,
        grid=(nrows // rb,),
        in_specs=[
            pl.BlockSpec((2, HY_FFN, seq), lambda r: (0, 0, 0)),
            pl.BlockSpec((rb, HY_FFN), lambda r: (r, 0)),
            pl.BlockSpec((rb, HY_FFN), lambda r: (r, 0)),
            pl.BlockSpec((rb, 1), lambda r: (r, 0)),
            pl.BlockSpec((1, seq), lambda r: (0, 0)),
            pl.BlockSpec((1, seq), lambda r: (0, 0)),
        ],
        out_specs=pl.BlockSpec((rb // SUBLANES, 2 * seq // LANES, SUBLANES, LANES), lambda r: (r, 0, 0, 0)),
        out_shape=jax.ShapeDtypeStruct((nrows // SUBLANES, 2 * seq // LANES, SUBLANES, LANES), F32),
        compiler_params=_params(("parallel",)),
        name="hyena_filter_time",
    )(z2t, w3f, w3b, delta_rows, t_f, t_b)


def _filt_fft_body(k_ref, f1_ref, f2_ref, tr_ref, ti_ref, kr_ref, ki_ref):
    tr = tr_ref[...]
    ti = ti_ref[...]
    for g in range(k_ref.shape[0]):
        x3 = jnp.swapaxes(k_ref[g], 0, 1)
        xr, xi = _fft_fwd(x3, f1_ref[...], f2_ref[...], tr, ti)
        kr_ref[SUBLANES * g:SUBLANES * (g + 1)] = xr.reshape(SUBLANES, FFT_R, FFT_R)
        ki_ref[SUBLANES * g:SUBLANES * (g + 1)] = xi.reshape(SUBLANES, FFT_R, FFT_R)


def _filt_fft(ktile, tables):
    n8, r, _, _ = ktile.shape
    nrows = n8 * SUBLANES
    rf = RF_KF
    out_spec = pl.BlockSpec((rf, r, r), lambda i: (i, 0, 0))
    consts = [tables[k] for k in ("f1", "f2", "tr", "ti")]
    return pl.pallas_call(
        _filt_fft_body,
        grid=(nrows // rf,),
        in_specs=[pl.BlockSpec((rf // SUBLANES, r, SUBLANES, LANES), lambda i: (i, 0, 0, 0))] + _const_specs(consts, 1),
        out_specs=[out_spec, out_spec],
        out_shape=[jax.ShapeDtypeStruct((nrows, r, r), F32)] * 2,
        compiler_params=_params(("parallel",)),
        name="hyena_filter_fft",
    )(ktile, *consts)


def _conv3_time(x, w):
    lane = lax.broadcasted_iota(I32, x.shape, 2)
    zero = jnp.zeros_like(x[:1])
    xm = pltpu.roll(x, 1, axis=2)
    xm = jnp.where(lane == 0, jnp.concatenate([zero, xm[:-1]], axis=0), xm)
    xp = pltpu.roll(x, LANES - 1, axis=2)
    xp = jnp.where(lane == LANES - 1, jnp.concatenate([xp[1:], zero], axis=0), xp)
    return w[0] * xm + w[1] * x + w[2] * xp + w[3]


def _hyena_body(sk_ref, v_ref, x1_ref, x2_ref, wv_ref, w1_ref, w2_ref, k0r_ref, k0i_ref, k1r_ref, k1i_ref,
                f1p_ref, f2_ref, f2c_ref, f3p_ref, tr_ref, ti_ref, o_ref, *, rc):
    ct = pl.program_id(0)
    tr = tr_ref[...]
    ti = ti_ref[...]
    np_ = v_ref.shape[2]
    nb = v_ref.shape[0]

    def load(ref, w_ref, g):
        parts = [jnp.swapaxes(_conv3_time(ref[bb, g], w_ref[:, g]), 0, 1) for bb in range(nb)]
        return jnp.concatenate(parts, axis=1)

    def long_conv(z3, kr_ref, ki_ref, g):
        xr, xi = _fft_fwd(z3, f1p_ref[...], f2_ref[...], tr, ti)
        kr = kr_ref[pl.ds(g * SUBLANES, SUBLANES)].reshape(SUBLANES * FFT_R, FFT_R)
        ki = ki_ref[pl.ds(g * SUBLANES, SUBLANES)].reshape(SUBLANES * FFT_R, FFT_R)
        yr = xr * kr - xi * ki
        yi = xr * ki + xi * kr
        return _fft_inv(yr, yi, f2c_ref[...], f3p_ref[...], tr, ti)

    def group(g, carry):
        cbase = ct * rc + g * SUBLANES

        def skip(z3, order):
            return jnp.stack([sk_ref[order, cbase + ci] * z3[ci] for ci in range(SUBLANES)], axis=0)

        v3 = load(v_ref, wv_ref, g)
        z1 = load(x1_ref, w1_ref, g) * (long_conv(v3, k0r_ref, k0i_ref, g) + skip(v3, 0))
        out = load(x2_ref, w2_ref, g) * (long_conv(z1, k1r_ref, k1i_ref, g) + skip(z1, 1))
        for bb in range(nb):
            o_ref[bb, g] = jnp.swapaxes(out[:, bb * np_:(bb + 1) * np_, :], 0, 1)
        return carry

    lax.fori_loop(0, rc // SUBLANES, group, 0)


def _hyena(hy5, cwt, skip, kfr, kfi, tables):
    bsz, n8, np_, _, _ = hy5.shape
    assert bsz % 2 == 0 and 2 * np_ == FFT_R
    rc = RC_HY
    r8 = rc // SUBLANES
    nct = D_HY // rc
    r = FFT_R

    def xspec(off):
        return pl.BlockSpec((2, r8, np_, SUBLANES, LANES), lambda c, b: (b, c + off * nct, 0, 0, 0))

    def wspec(off):
        return pl.BlockSpec((HY_CONV + 1, r8, SUBLANES, LANES), lambda c, b: (0, c + off * nct, 0, 0))

    def kspec(order):
        return pl.BlockSpec((rc, r, r), lambda c, b: (c + order * nct, 0, 0))

    consts = [tables[k] for k in ("f1p", "f2", "f2c", "f3p", "tr", "ti")]
    return pl.pallas_call(
        functools.partial(_hyena_body, rc=rc),
        grid=(nct, bsz // 2),
        in_specs=[pl.BlockSpec(memory_space=pltpu.SMEM), xspec(0), xspec(1), xspec(2), wspec(0), wspec(1), wspec(2),
                  kspec(0), kspec(0), kspec(1), kspec(1)] + _const_specs(consts, 2),
        out_specs=pl.BlockSpec((2, r8, np_, SUBLANES, LANES), lambda c, b: (b, c, 0, 0, 0)),
        out_shape=jax.ShapeDtypeStruct((bsz, D_HY // SUBLANES, np_, SUBLANES, LANES), F32),
        compiler_params=_params(("parallel", "parallel")),
        name="hyena",
    )(skip, hy5, hy5, hy5, cwt, cwt, cwt, kfr, kfi, kfr, kfi, *consts)


def _gelu_tanh(x):
    return 0.5 * x * (1.0 + jnp.tanh(math.sqrt(2.0 / math.pi) * (x + 0.044715 * (x * x * x))))


def _mix_body(hf_ref, hb_ref, gate_ref, yhy_ref, x_ref, gl_ref, gh_ref, wl_ref, wh_ref, gffn_ref,
              wrh_ref, wrl_ref, rb_ref, tri_ref, xmid_ref, h_ref, route_ref, cnt_ref):
    tm = x_ref.shape[1]
    y_lru = (hf_ref[0, 0] + hb_ref[0, 0]) * _gelu_tanh(gate_ref[0])
    nl = _rms(y_lru, gl_ref[...]).astype(BF16)
    gh = gh_ref[...]
    for j in range(tm // LANES):
        yt = yhy_ref[0, :, j, :, :].reshape(D_HY, LANES)
        nt = (yt * lax.rsqrt(jnp.mean(yt * yt, axis=0, keepdims=True) + EPS) * gh).astype(BF16)
        rows = slice(LANES * j, LANES * (j + 1))
        mix = _dot(nl[rows], wl_ref[...]) + lax.dot_general(
            nt, wh_ref[...], (((0,), (0,)), ((), ())), preferred_element_type=F32)
        xmid_ref[0, rows, :] = x_ref[0, rows, :] + mix
    h = _rms(xmid_ref[0], gffn_ref[...])
    h_ref[...] = h.astype(BF16)

    hh, hl = _split_bf16(h)
    logits = _dot(hh, wrh_ref[...]) + _dot(hl, wrh_ref[...]) + _dot(hh, wrl_ref[...]) + rb_ref[...]
    lane = lax.broadcasted_iota(I32, logits.shape, 1)
    big = jnp.int32(1 << 20)
    is_g = lane < N_GROUPS
    lg = jnp.where(is_g, logits, NEG_BIG)
    mg = jnp.max(lg, axis=-1, keepdims=True)
    gidx = jnp.min(jnp.where(lg == mg, lane, big), axis=-1, keepdims=True)
    g_p = 1.0 / jnp.sum(jnp.where(is_g, jnp.exp(lg - mg), 0.0), axis=-1, keepdims=True)
    e_lo = N_GROUPS + EXPERTS_PER_GROUP * gidx
    is_e = jnp.logical_and(lane >= e_lo, lane < e_lo + EXPERTS_PER_GROUP)
    le = jnp.where(is_e, logits, NEG_BIG)
    m1 = jnp.max(le, axis=-1, keepdims=True)
    i1 = jnp.min(jnp.where(le == m1, lane, big), axis=-1, keepdims=True)
    le2 = jnp.where(lane == i1, NEG_BIG, le)
    m2 = jnp.max(le2, axis=-1, keepdims=True)
    i2 = jnp.min(jnp.where(le2 == m2, lane, big), axis=-1, keepdims=True)
    ratio = jnp.exp(m2 - m1)
    gate1 = g_p / (1.0 + ratio)
    gate2 = g_p * ratio / (1.0 + ratio)
    oh1 = lane == i1
    oh2 = lane == i2
    oh = jnp.where(oh1, 1.0, jnp.where(oh2, 1.0, 0.0))
    tot = _dot(tri_ref[...], oh.astype(BF16))
    rank1 = jnp.sum(jnp.where(oh1, tot, 0.0), axis=-1, keepdims=True)
    rank2 = jnp.sum(jnp.where(oh2, tot, 0.0), axis=-1, keepdims=True)
    cnt_ref[0] = jnp.sum(oh, axis=0, keepdims=True)
    cols = [(i1 - N_GROUPS).astype(F32), (i2 - N_GROUPS).astype(F32), rank1, rank2, gate1, gate2]
    route = jnp.zeros(logits.shape, F32)
    for k, col in enumerate(cols):
        route = jnp.where(lane == k, col, route)
    route_ref[...] = route[:, :SUBLANES]


def _mix_router(h4, lru, yhy5, x, gl, gh, wl, wh, gffn, wrh, wrl, rb, tri):
    bsz, seq, d = x.shape
    tm = TM_MIX
    nt = seq // tm
    n_tok = bsz * seq
    c2 = lambda b, i: (0, 0)
    return pl.pallas_call(
        _mix_body,
        grid=(bsz, nt),
        in_specs=[
            pl.BlockSpec((1, 1, tm, D_LRU), lambda b, i: (0, b, i, 0)),
            pl.BlockSpec((1, 1, tm, D_LRU), lambda b, i: (1, b, i, 0)),
            pl.BlockSpec((1, tm, D_LRU), lambda b, i: (b, i, 1)),
            pl.BlockSpec((1, D_HY // SUBLANES, tm // LANES, SUBLANES, LANES), lambda b, i: (b, 0, i, 0, 0)),
            pl.BlockSpec((1, tm, d), lambda b, i: (b, i, 0)),
            pl.BlockSpec((1, D_LRU), c2),
            pl.BlockSpec((D_HY, 1), c2),
            pl.BlockSpec((D_LRU, d), c2),
            pl.BlockSpec((D_HY, d), c2),
            pl.BlockSpec((1, d), c2),
            pl.BlockSpec((d, LANES), c2),
            pl.BlockSpec((d, LANES), c2),
            pl.BlockSpec((1, LANES), c2),
            pl.BlockSpec((tm, tm), c2),
        ],
        out_specs=[
            pl.BlockSpec((1, tm, d), lambda b, i: (b, i, 0)),
            pl.BlockSpec((tm, d), lambda b, i: (b * nt + i, 0)),
            pl.BlockSpec((tm, SUBLANES), lambda b, i: (b * nt + i, 0)),
            pl.BlockSpec((1, 1, LANES), lambda b, i: (b * nt + i, 0, 0)),
        ],
        out_shape=[
            jax.ShapeDtypeStruct((bsz, seq, d), F32),
            jax.ShapeDtypeStruct((n_tok, d), BF16),
            jax.ShapeDtypeStruct((n_tok, SUBLANES), F32),
            jax.ShapeDtypeStruct((n_tok // tm, 1, LANES), F32),
        ],
        compiler_params=_params(("parallel", "parallel")),
        name="mix_router",
    )(h4, h4, lru, yhy5, x, gl, gh, wl, wh, gffn, wrh, wrl, rb, tri)


def _local_slots(route, lsl):
    lane = lax.broadcasted_iota(I32, (route.shape[0], LANES), 1)
    pos = []
    for k in range(2):
        start = jnp.sum(jnp.where(lane == route[:, k:k + 1].astype(I32), lsl, 0.0), axis=-1, keepdims=True)
        pos.append((start + route[:, 2 + k:3 + k]).astype(I32))
    return pos


def _start_chunks(i, nch_ref, gs_ref, ls_ref, make_copy):
    def per_expert(e, c):
        idx = i * N_EXPERTS + e
        gs = gs_ref[idx]
        ls = ls_ref[idx]

        def per_chunk(k, c2):
            make_copy(ls + k * ROW_CHUNK, gs + k * ROW_CHUNK).start()
            return c2

        lax.fori_loop(0, nch_ref[idx], per_chunk, 0)
        return c

    lax.fori_loop(0, N_EXPERTS, per_expert, 0)


def _wait_chunks(count, make_copy):
    def wait(k, c):
        make_copy(0, 0).wait()
        return c

    lax.fori_loop(0, count, wait, 0)


def _dispatch_body(nch_ref, gs_ref, ls_ref, tch_ref, tn_ref, ts_ref, nu_ref, h_ref, route_ref, lsl_ref, xb_ref,
                   buf, zbuf, sem, zsem):
    i = pl.program_id(0)
    n_tiles = pl.num_programs(0)
    tm = h_ref.shape[0]
    lbuf = buf.shape[1]
    n_blocks = xb_ref.shape[0] // MOE_BLK
    slot = i % 2

    def aligned(r):
        return pl.ds(pl.multiple_of(r, ROW_CHUNK), ROW_CHUNK)

    @pl.when(i == 0)
    def _():
        zbuf[...] = jnp.zeros_like(zbuf)

        def zero_chunk(r):
            return pltpu.make_async_copy(zbuf.at[pl.ds(0, ROW_CHUNK)], xb_ref.at[aligned(r)], zsem)

        def zero_block(j):
            return pltpu.make_async_copy(zbuf, xb_ref.at[pl.ds(pl.multiple_of(j * MOE_BLK, MOE_BLK), MOE_BLK)], zsem)

        def per_expert(e, total):
            def per_chunk(k, c):
                zero_chunk(ts_ref[e] + k * ROW_CHUNK).start()
                return c

            lax.fori_loop(0, tn_ref[e], per_chunk, 0)
            return total + tn_ref[e]

        total = lax.fori_loop(0, N_EXPERTS, per_expert, 0)

        def start_block(j, c):
            zero_block(j).start()
            return c

        lax.fori_loop(nu_ref[0], n_blocks, start_block, 0)

        def wait_chunk(k, c):
            zero_chunk(0).wait()
            return c

        lax.fori_loop(0, total, wait_chunk, 0)

        def wait_block(j, c):
            zero_block(0).wait()
            return c

        lax.fori_loop(nu_ref[0], n_blocks, wait_block, 0)

    l0, l1 = _local_slots(route_ref[...], lsl_ref[0])
    lane = lax.broadcasted_iota(I32, (tm, LANES), 1)
    slots_t = jnp.where(lane == 0, l0, jnp.where(lane == 1, l1, 0)).astype(F32).T
    l0t = slots_t[0:1].astype(I32)
    l1t = slots_t[1:2].astype(I32)
    row = lax.broadcasted_iota(I32, (lbuf, tm), 0)
    perm = jnp.where(row == l0t, 1.0, jnp.where(row == l1t, 1.0, 0.0)).astype(BF16)
    srt = _dot(perm, h_ref[...]).astype(BF16)

    def copy_from(s):
        def make_copy(src_row, dst_row):
            return pltpu.make_async_copy(buf.at[s, aligned(src_row)], xb_ref.at[aligned(dst_row)], sem.at[s])
        return make_copy

    @pl.when(i >= 2)
    def _():
        _wait_chunks(tch_ref[i - 2], copy_from(slot))

    buf[slot] = srt
    _start_chunks(i, nch_ref, gs_ref, ls_ref, copy_from(slot))

    @pl.when(i == n_tiles - 1)
    def _():
        @pl.when(i >= 1)
        def _():
            _wait_chunks(tch_ref[i - 1], copy_from(1 - slot))

        _wait_chunks(tch_ref[i], copy_from(slot))


def _dispatch(nch, gstart, lstart, tile_nch, tail_n, tail_s, n_used, h, route, lsl, n_slots):
    n_tok, d = h.shape
    tm = TM_MIX
    grid_spec = pltpu.PrefetchScalarGridSpec(
        num_scalar_prefetch=7,
        grid=(n_tok // tm,),
        in_specs=[
            pl.BlockSpec((tm, d), lambda i, *_: (i, 0)),
            pl.BlockSpec((tm, SUBLANES), lambda i, *_: (i, 0)),
            pl.BlockSpec((1, 1, LANES), lambda i, *_: (i, 0, 0)),
        ],
        out_specs=pl.BlockSpec(memory_space=pl.ANY),
        scratch_shapes=[pltpu.VMEM((2, LOCAL_ROWS, d), BF16), pltpu.VMEM((MOE_BLK, d), BF16),
                        pltpu.SemaphoreType.DMA((2,)), pltpu.SemaphoreType.DMA(())],
    )
    return pl.pallas_call(
        _dispatch_body,
        grid_spec=grid_spec,
        out_shape=jax.ShapeDtypeStruct((n_slots, d), BF16),
        compiler_params=_params(("arbitrary",)),
        name="moe_dispatch",
    )(nch, gstart, lstart, tile_nch, tail_n, tail_s, n_used, h, route, lsl)


def _expert_body(be_ref, fl_ref, nu_ref, xb_ref, w1_ref, w3_ref, w2_ref, yb_ref, w1_sc, w3_sc, w2_sc):
    del be_ref
    j = pl.program_id(0)
    used = j < nu_ref[0]

    @pl.when(jnp.logical_and(used, fl_ref[j] == 1))
    def _():
        w1_sc[...] = w1_ref[...].astype(BF16)
        w3_sc[...] = w3_ref[...].astype(BF16)
        w2_sc[...] = w2_ref[...].astype(BF16)

    @pl.when(used)
    def _():
        xb = xb_ref[...]
        a = _dot(xb, w1_sc[...])
        b = _dot(xb, w3_sc[...])
        act = (a * _sigmoid(a) * b).astype(BF16)
        yb_ref[...] = _dot(act, w2_sc[...]).astype(BF16)

    @pl.when(jnp.logical_not(used))
    def _():
        yb_ref[...] = jnp.zeros_like(yb_ref)


def _experts(block_e, first_flag, n_used, xb, w1, w3, w2):
    n_slots, d = xb.shape
    blk = MOE_BLK
    de = w1.shape[2]
    grid_spec = pltpu.PrefetchScalarGridSpec(
        num_scalar_prefetch=3,
        grid=(n_slots // blk,),
        in_specs=[
            pl.BlockSpec((blk, d), lambda j, be, fl, nu: (j, 0)),
            pl.BlockSpec((None, d, de), lambda j, be, fl, nu: (be[j], 0, 0)),
            pl.BlockSpec((None, d, de), lambda j, be, fl, nu: (be[j], 0, 0)),
            pl.BlockSpec((None, de, d), lambda j, be, fl, nu: (be[j], 0, 0)),
        ],
        out_specs=pl.BlockSpec((blk, d), lambda j, be, fl, nu: (j, 0)),
        scratch_shapes=[pltpu.VMEM((d, de), BF16), pltpu.VMEM((d, de), BF16), pltpu.VMEM((de, d), BF16)],
    )
    return pl.pallas_call(
        _expert_body,
        grid_spec=grid_spec,
        out_shape=jax.ShapeDtypeStruct((n_slots, d), BF16),
        compiler_params=_params(("arbitrary",)),
        name="moe_experts",
    )(block_e, first_flag, n_used, xb, w1, w3, w2)


def _combine_body(nch_ref, gs_ref, ls_ref, tch_ref, xmid_ref, route_ref, lsl_ref, g_ref, yb_ref, o_ref, buf, sem):
    i = pl.program_id(0)
    n_tiles = pl.num_programs(0)
    tm = xmid_ref.shape[0]
    lbuf = buf.shape[1]
    slot = i % 2

    def aligned(r):
        return pl.ds(pl.multiple_of(r, ROW_CHUNK), ROW_CHUNK)

    def copy_into(s):
        def make_copy(local_row, slot_row):
            return pltpu.make_async_copy(yb_ref.at[aligned(slot_row)], buf.at[s, aligned(local_row)], sem.at[s])
        return make_copy

    @pl.when(i == 0)
    def _():
        buf[...] = jnp.zeros_like(buf)
        _start_chunks(i, nch_ref, gs_ref, ls_ref, copy_into(slot))

    @pl.when(i + 1 < n_tiles)
    def _():
        _start_chunks(i + 1, nch_ref, gs_ref, ls_ref, copy_into(1 - slot))

    route = route_ref[...]
    l0, l1 = _local_slots(route, lsl_ref[0])
    lane = lax.broadcasted_iota(I32, (tm, lbuf), 1)
    gmat = (jnp.where(lane == l0, route[:, 4:5], 0.0) + jnp.where(lane == l1, route[:, 5:6], 0.0)).astype(BF16)
    _wait_chunks(tch_ref[i], copy_into(slot))
    y = xmid_ref[...] + _dot(gmat, buf[slot])
    o_ref[...] = _rms(y, g_ref[...])


def _combine(nch, gstart, lstart, tile_nch, xmid, route, lsl, g, yb):
    n_tok, d = xmid.shape
    tm = TM_MIX
    grid_spec = pltpu.PrefetchScalarGridSpec(
        num_scalar_prefetch=4,
        grid=(n_tok // tm,),
        in_specs=[
            pl.BlockSpec((tm, d), lambda i, *_: (i, 0)),
            pl.BlockSpec((tm, SUBLANES), lambda i, *_: (i, 0)),
            pl.BlockSpec((1, 1, LANES), lambda i, *_: (i, 0, 0)),
            pl.BlockSpec((1, d), lambda i, *_: (0, 0)),
            pl.BlockSpec(memory_space=pl.ANY),
        ],
        out_specs=pl.BlockSpec((tm, d), lambda i, *_: (i, 0)),
        scratch_shapes=[pltpu.VMEM((2, LOCAL_ROWS, d), BF16), pltpu.SemaphoreType.DMA((2,))],
    )
    return pl.pallas_call(
        _combine_body,
        grid_spec=grid_spec,
        out_shape=jax.ShapeDtypeStruct((n_tok, d), F32),
        compiler_params=_params(("arbitrary",)),
        name="moe_combine",
    )(nch, gstart, lstart, tile_nch, xmid, route, lsl, g, yb)


def _filter_features(seq):
    pos = jnp.arange(seq, dtype=F32)
    t = jnp.linspace(0.0, 1.0, seq, dtype=F32)
    w = (2.0 * math.pi / seq) * pos
    bands = jnp.linspace(1e-4, HY_BANDS - 1, HY_BANDS, dtype=F32)
    ang = w[:, None] * bands[None, :]
    feats = jnp.concatenate([t[:, None], jnp.cos(ang), jnp.sin(ang)], axis=-1)
    rev = lambda a: jnp.concatenate([jnp.zeros_like(a[:1]), a[1:][::-1]], axis=0)
    pad = lambda a: jnp.pad(a, ((0, 0), (0, HY_EMB_PAD - HY_EMB)))
    featst = jnp.stack([pad(feats).T, pad(rev(feats)).T], axis=0)
    return featst, t[None, :], rev(t[:, None]).T


def _gate_weights(wr, wi):
    half = D_LRU // 2
    hph = half // LRU_HEAD_DIM
    eye = jnp.eye(hph, dtype=wr.dtype)[:, None, :, None]

    def block_diag(w):
        return (eye * w[:, :, None, :]).reshape(half, half)

    out = [jnp.concatenate([block_diag(wr[hh * hph:(hh + 1) * hph]), block_diag(wi[hh * hph:(hh + 1) * hph])], axis=1)
           for hh in range(2)]
    return jnp.stack(out, axis=0).astype(BF16)


def _layer(x, l, p):
    bsz, seq, d = x.shape
    n_tok = bsz * seq
    row = lambda a: a.reshape(1, -1).astype(F32)
    col = lambda a: a.reshape(-1, 1).astype(F32)

    w_in = p["w_in"][l]
    wl = w_in[:, :2 * D_LRU].astype(BF16)
    wht = w_in[:, 2 * D_LRU:].T.astype(BF16)
    lru, hy5 = _inproj(x, row(p["norm_mix_g"][l]), wl, wht)

    wg = jnp.stack([_gate_weights(p["lru_wr_f"][l], p["lru_wi_f"][l]),
                    _gate_weights(p["lru_wr_b"][l], p["lru_wi_b"][l])], axis=0)
    stack2 = lambda a, b: jnp.stack([a.reshape(1, -1), b.reshape(1, -1)], axis=0).astype(F32)
    h4 = _lru(lru, p["lru_conv_w"][l], row(p["lru_conv_b"][l]), wg,
              stack2(p["lru_br_f"][l], p["lru_br_b"][l]), stack2(p["lru_bi_f"][l], p["lru_bi_b"][l]),
              stack2(p["lru_lambda_f"][l], p["lru_lambda_b"][l]))

    tables = _fft_tables()
    featst, t_f, t_b = _filter_features(seq)
    w1t = jnp.pad(p["hy_filt_w1"][l].T, ((0, 0), (0, HY_EMB_PAD - HY_EMB)))
    z2t = _filt_mlp(featst, w1t, col(p["hy_filt_b1"][l]), col(p["hy_filt_freq1"][l]),
                    p["hy_filt_w2"][l].T, col(p["hy_filt_b2"][l]), col(p["hy_filt_freq2"][l]))
    w3 = p["hy_filt_w3"][l]
    nfr = HY_ORDER * D_HY
    deltas = jnp.abs(jnp.linspace(math.log(HY_DECAY_TARGET) / HY_SLOW_DECAY_PCT,
                                  math.log(HY_DECAY_TARGET) / HY_FAST_DECAY_PCT, D_HY, dtype=F32))
    ktile = _filt_time(z2t, w3[:, :nfr].T, w3[:, nfr:].T, col(jnp.tile(deltas, HY_ORDER)), t_f, t_b)
    kfr, kfi = _filt_fft(ktile, tables)
    cwt = jnp.concatenate([p["hy_conv_w"][l], p["hy_conv_b"][l][None, :]], axis=0)
    cwt = jnp.broadcast_to(cwt.reshape(HY_CONV + 1, -1, SUBLANES, 1), (HY_CONV + 1, cwt.shape[1] // SUBLANES, SUBLANES, LANES))
    yhy5 = _hyena(hy5, cwt, p["hy_skip"][l], kfr, kfi, tables)

    w_out = p["w_out"][l]
    wr_cat = jnp.concatenate([p["router_group_w"][l], p["router_expert_w"][l]], axis=1)
    wr_cat = jnp.pad(wr_cat, ((0, 0), (0, LANES - wr_cat.shape[1])))
    wrh, wrl = _split_bf16(wr_cat)
    rb = jnp.concatenate([p["router_group_b"][l], p["router_expert_b"][l]])
    rb = jnp.pad(rb, (0, LANES - rb.shape[0])).reshape(1, LANES)
    tri = (jnp.arange(TM_MIX)[:, None] > jnp.arange(TM_MIX)[None, :]).astype(BF16)
    xmid, h, route, tile_cnt = _mix_router(
        h4, lru, yhy5, x, row(p["grp_norm_lru_g"][l]), col(p["grp_norm_hy_g"][l]),
        w_out[:D_LRU].astype(BF16), w_out[D_LRU:].astype(BF16), row(p["norm_ffn_g"][l]), wrh, wrl, rb, tri)

    blk = MOE_BLK
    ch = ROW_CHUNK
    n_tiles = n_tok // TM_MIX
    n_blocks = (2 * n_tok + n_tiles * N_EXPERTS * (ch - 1)) // blk + N_EXPERTS
    cnt = tile_cnt[:, 0, N_GROUPS:N_GROUPS + N_EXPERTS].astype(I32)
    run = (cnt + ch - 1) // ch * ch
    tot_e = jnp.sum(run, axis=0)
    padded_e = (tot_e + blk - 1) // blk * blk
    pad_end = jnp.cumsum(padded_e)
    pad_start = pad_end - padded_e
    gstart = pad_start[None, :] + jnp.cumsum(run, axis=0) - run
    lstart = jnp.cumsum(run, axis=1) - run
    lsl = jnp.pad(lstart.astype(F32), ((0, 0), (0, LANES - N_EXPERTS))).reshape(n_tiles, 1, LANES)
    flat = lambda a: a.reshape(-1).astype(I32)
    tile_nch = flat(jnp.sum(run, axis=1) // ch)
    nch, gstart, lstart = flat(run // ch), flat(gstart), flat(lstart)
    tail_n = flat((padded_e - tot_e) // ch)
    tail_s = flat(pad_start + tot_e)
    block_start = jnp.arange(n_blocks, dtype=I32) * blk
    block_e = jnp.minimum(jnp.sum(block_start[:, None] >= pad_end[None, :], axis=1), N_EXPERTS - 1).astype(I32)
    first_flag = jnp.concatenate([jnp.ones((1,), I32), (block_e[1:] != block_e[:-1]).astype(I32)])
    n_used = (pad_end[-1:] // blk).astype(I32)

    xb = _dispatch(nch, gstart, lstart, tile_nch, tail_n, tail_s, n_used, h, route, lsl, n_blocks * blk)
    yb = _experts(block_e, first_flag, n_used, xb, p["exp_w1"][l], p["exp_w3"][l], p["exp_w2"][l])
    return xmid.reshape(n_tok, d), route, (nch, gstart, lstart, tile_nch, lsl), yb


def kernel(x, norm_mix_g, w_in, lru_conv_w, lru_conv_b, lru_wr_f, lru_br_f, lru_wi_f, lru_bi_f, lru_lambda_f, lru_wr_b, lru_br_b, lru_wi_b, lru_bi_b, lru_lambda_b, hy_conv_w, hy_conv_b, hy_filt_w1, hy_filt_b1, hy_filt_freq1, hy_filt_w2, hy_filt_b2, hy_filt_freq2, hy_filt_w3, hy_skip, grp_norm_lru_g, grp_norm_hy_g, w_out, norm_ffn_g, router_group_w, router_group_b, router_expert_w, router_expert_b, exp_w1, exp_w3, exp_w2, norm_final_g):
    p = dict(locals())
    bsz, seq, d = x.shape
    assert d == D_MODEL and 2 * seq == FFT_R * FFT_R and w_in.shape[0] == 1
    xmid, route, (nch, gstart, lstart, tile_nch, lsl), yb = _layer(x, 0, p)
    out = _combine(nch, gstart, lstart, tile_nch, xmid, route, lsl, norm_final_g.reshape(1, d), yb)
    return out.reshape(bsz, seq, d)
```

```python
import functools
import math

import numpy as np
import jax
import jax.numpy as jnp
from jax import lax
from jax.experimental import pallas as pl
from jax.experimental.pallas import tpu as pltpu

F32 = jnp.float32
BF16 = jnp.bfloat16
I32 = jnp.int32

D_MODEL = 1024
D_LRU = 512
D_HY = 512
LRU_HEADS = 8
LRU_HEAD_DIM = D_LRU // LRU_HEADS
LRU_CONV = 4
LRU_C = 8.0
HY_ORDER = 2
HY_CONV = 3
HY_BANDS = 16
HY_EMB = 2 * HY_BANDS + 1
HY_EMB_PAD = 40
HY_FFN = 64
HY_FAST_DECAY_PCT = 0.3
HY_SLOW_DECAY_PCT = 1.5
HY_DECAY_TARGET = 1e-2
N_GROUPS = 4
EXPERTS_PER_GROUP = 8
N_EXPERTS = N_GROUPS * EXPERTS_PER_GROUP
D_EXPERT = D_MODEL // 2
EPS = 1e-6

LANES = 128
SUBLANES = 8
FFT_R = 128
VMEM_LIMIT = 56 * 1024 * 1024

TM_IN = 512
TC_LRU = 512
RC_HY = 16
RF_KF = 16
RB_FILT = 64
TM_MIX = 512
MOE_BLK = 512
ROW_CHUNK = 16
LOCAL_ROWS = -(-(2 * TM_MIX + N_EXPERTS * (ROW_CHUNK - 1)) // LANES) * LANES
MAX_CHUNKS = LOCAL_ROWS // ROW_CHUNK
NEG_BIG = -1e30


def _params(sem, vmem=VMEM_LIMIT):
    return pltpu.CompilerParams(dimension_semantics=sem, vmem_limit_bytes=vmem)


def _rms(x, g):
    return x * lax.rsqrt(jnp.mean(x * x, axis=-1, keepdims=True) + EPS) * g


def _sigmoid(x):
    return 1.0 / (1.0 + jnp.exp(-x))


def _split_bf16(a):
    hi = a.astype(BF16)
    lo = (a - hi.astype(F32)).astype(BF16)
    return hi, lo


def _dot(a, b):
    return jnp.dot(a, b, preferred_element_type=F32)


def _dot3(a, b):
    ah, al = _split_bf16(a)
    bh, bl = _split_bf16(b)
    return _dot(ah, bh) + _dot(al, bh) + _dot(ah, bl)


def _inproj_body(x_ref, g_ref, wl_ref, wht_ref, lru_ref, hy_ref):
    hn = _rms(x_ref[0], g_ref[...]).astype(BF16)
    lru_ref[0] = _dot(hn, wl_ref[...])
    hyt = lax.dot_general(wht_ref[...], hn, (((1,), (1,)), ((), ())), preferred_element_type=F32)
    nrow = hyt.shape[0] // SUBLANES
    for j in range(hyt.shape[1] // LANES):
        hy_ref[0, :, j, :, :] = hyt[:, LANES * j:LANES * (j + 1)].reshape(nrow, SUBLANES, LANES)


def _inproj(x, g, wl, wht):
    bsz, seq, d = x.shape
    nl = wl.shape[1]
    nh = wht.shape[0]
    tm = TM_IN
    return pl.pallas_call(
        _inproj_body,
        grid=(bsz, seq // tm),
        in_specs=[
            pl.BlockSpec((1, tm, d), lambda b, i: (b, i, 0)),
            pl.BlockSpec((1, d), lambda b, i: (0, 0)),
            pl.BlockSpec((d, nl), lambda b, i: (0, 0)),
            pl.BlockSpec((nh, d), lambda b, i: (0, 0)),
        ],
        out_specs=[
            pl.BlockSpec((1, tm, nl), lambda b, i: (b, i, 0)),
            pl.BlockSpec((1, nh // SUBLANES, tm // LANES, SUBLANES, LANES), lambda b, i: (b, 0, i, 0, 0)),
        ],
        out_shape=[
            jax.ShapeDtypeStruct((bsz, seq, nl), F32),
            jax.ShapeDtypeStruct((bsz, nh // SUBLANES, seq // LANES, SUBLANES, LANES), F32),
        ],
        compiler_params=_params(("parallel", "parallel")),
        name="inproj",
    )(x, g, wl, wht)


def _one_minus_sq(a, log_a):
    x = 2.0 * log_a
    series = -x * (1.0 + x * (0.5 + x * (1.0 / 6.0)))
    return jnp.where(x > -4e-3, series, 1.0 - a * a)


def _group_scan(a3, b3, reverse):
    sub = lax.broadcasted_iota(I32, a3.shape, 1)
    for s in (1, 2, 4):
        if reverse:
            a_sh = pltpu.roll(a3, SUBLANES - s, axis=1)
            b_sh = pltpu.roll(b3, SUBLANES - s, axis=1)
            m = sub < SUBLANES - s
        else:
            a_sh = pltpu.roll(a3, s, axis=1)
            b_sh = pltpu.roll(b3, s, axis=1)
            m = sub >= s
        b3 = jnp.where(m, a3 * b_sh + b3, b3)
        a3 = jnp.where(m, a3 * a_sh, a3)
    return a3, b3


def _lru_body(cur_ref, prev_ref, next_ref, cw_ref, cb_ref, wg_ref, br_ref, bi_ref, lam_ref, o_ref,
              h_sc, carry_sc, *, nt, tc):
    d = pl.program_id(1)
    i = pl.program_id(2)
    c = jnp.where(d == 0, i, nt - 1 - i)
    ns = SUBLANES
    ng = tc // ns
    half = D_LRU // 2

    @pl.when(i == 0)
    def _():
        carry_sc[...] = jnp.zeros_like(carry_sc)

    xs = jnp.swapaxes(cur_ref[0].reshape(ns, ng, D_LRU), 0, 1)
    prev = jnp.where(c > 0, prev_ref[0], 0.0)
    nxt = jnp.where(c < nt - 1, next_ref[0], 0.0)
    sub = lax.broadcasted_iota(I32, (ns, D_LRU), 0)

    def prev_segment(slab, halo_row):
        return jnp.where(sub == 0, halo_row, pltpu.roll(slab, 1, axis=0))

    def next_segment(slab, halo_row):
        return jnp.where(sub == ns - 1, halo_row, pltpu.roll(slab, ns - 1, axis=0))

    xext = jnp.concatenate([prev_segment(xs[ng - 2], prev[ns - 2:ns - 1])[None],
                            prev_segment(xs[ng - 1], prev[ns - 1:ns])[None], xs,
                            next_segment(xs[0], nxt[0:1])[None]], axis=0)
    cw = cw_ref[...]
    xc = cb_ref[...] + cw[0:1] * xext[0:ng]
    for k in range(1, LRU_CONV):
        xc = xc + cw[k:k + 1] * xext[k:k + ng]
    xc = xc.reshape(tc, D_LRU)

    lam = lam_ref[0]
    nlam = -lam
    softplus = jnp.maximum(nlam, 0.0) + jnp.log1p(jnp.exp(-jnp.abs(nlam)))
    half_c = (-0.5 * LRU_C) * softplus

    def gates(hh):
        sl = slice(half * hh, half * (hh + 1))
        xh = xc[:, sl]
        logits = _dot(xh.astype(BF16), wg_ref[0, hh])
        tr_ = jnp.tanh(0.5 * (logits[:, :half] + br_ref[0][:, sl]))
        gi = 0.5 + 0.5 * jnp.tanh(0.5 * (logits[:, half:] + bi_ref[0][:, sl]))
        log_a = half_c[:, sl] * (1.0 + tr_)
        a = jnp.exp(log_a)
        b = jnp.sqrt(_one_minus_sq(a, log_a)) * (gi * xh)
        return a.reshape(ng, ns, half), b.reshape(ng, ns, half)

    def run(reverse):
        subh = lax.broadcasted_iota(I32, (ns, half), 0)
        for hh in range(2):
            sl = slice(half * hh, half * (hh + 1))
            a3, b3 = gates(hh)
            h = jnp.zeros((ns, half), F32)
            p = jnp.ones((ns, half), F32)
            hs = [None] * ng
            ps = [None] * ng
            for g in (range(ng - 1, -1, -1) if reverse else range(ng)):
                h = a3[g] * h + b3[g]
                p = a3[g] * p
                hs[g] = h
                ps[g] = p
            ac, hc = _group_scan(p[None], h[None], reverse)
            c_in = carry_sc[:, sl]
            end = ac[0] * c_in + hc[0]
            if reverse:
                seg_in = jnp.where(subh == ns - 1, c_in, pltpu.roll(end, ns - 1, axis=0))
                carry_sc[:, sl] = jnp.broadcast_to(end[0:1], end.shape)
            else:
                seg_in = jnp.where(subh == 0, c_in, pltpu.roll(end, 1, axis=0))
                carry_sc[:, sl] = jnp.broadcast_to(end[ns - 1:ns], end.shape)
            h_sc[:, :, sl] = jnp.stack(hs, axis=0) + jnp.stack(ps, axis=0) * seg_in[None]
        o_ref[0, 0] = jnp.swapaxes(h_sc[...], 0, 1).reshape(tc, D_LRU)

    @pl.when(d == 0)
    def _():
        run(False)

    @pl.when(d == 1)
    def _():
        run(True)


def _lru(lru, cw, cb, wg, br, bi, lam):
    bsz, seq, _ = lru.shape
    tc = TC_LRU
    nt = seq // tc
    r8 = tc // SUBLANES
    nrow8 = seq // SUBLANES

    def cidx(d, i):
        return jnp.where(d == 0, i, nt - 1 - i)

    return pl.pallas_call(
        functools.partial(_lru_body, nt=nt, tc=tc),
        grid=(bsz, 2, nt),
        in_specs=[
            pl.BlockSpec((1, tc, D_LRU), lambda b, d, i: (b, cidx(d, i), 0)),
            pl.BlockSpec((1, SUBLANES, D_LRU), lambda b, d, i: (b, jnp.maximum(cidx(d, i) * r8 - 1, 0), 0)),
            pl.BlockSpec((1, SUBLANES, D_LRU), lambda b, d, i: (b, jnp.minimum((cidx(d, i) + 1) * r8, nrow8 - 1), 0)),
            pl.BlockSpec((LRU_CONV, D_LRU), lambda b, d, i: (0, 0)),
            pl.BlockSpec((1, D_LRU), lambda b, d, i: (0, 0)),
            pl.BlockSpec((1, 2, D_LRU // 2, D_LRU), lambda b, d, i: (d, 0, 0, 0)),
            pl.BlockSpec((1, 1, D_LRU), lambda b, d, i: (d, 0, 0)),
            pl.BlockSpec((1, 1, D_LRU), lambda b, d, i: (d, 0, 0)),
            pl.BlockSpec((1, 1, D_LRU), lambda b, d, i: (d, 0, 0)),
        ],
        out_specs=pl.BlockSpec((1, 1, tc, D_LRU), lambda b, d, i: (d, b, cidx(d, i), 0)),
        out_shape=jax.ShapeDtypeStruct((2, bsz, seq, D_LRU), F32),
        scratch_shapes=[
            pltpu.VMEM((tc // SUBLANES, SUBLANES, D_LRU), F32),
            pltpu.VMEM((SUBLANES, D_LRU), F32),
        ],
        compiler_params=_params(("parallel", "arbitrary", "arbitrary")),
        name="rg_lru",
    )(lru, lru, lru, cw, cb, wg, br, bi, lam)


def _fft_tables():
    r = FFT_R
    n = r * r
    idx = np.arange(r, dtype=np.float64)
    ang = 2.0 * np.pi * np.outer(idx, idx) / r
    wr, wi = np.cos(ang), -np.sin(ang)
    angt = 2.0 * np.pi * np.outer(idx, idx) / n
    tr, ti = np.cos(angt), -np.sin(angt)
    h = r // 2
    f1 = np.concatenate([wr, wi], axis=1)
    f1p = np.block([[wr[:h], wi[:h]], [-wi[:h], wr[:h]]])
    f2 = np.block([[wr, wi], [-wi, wr]])
    f2c = np.block([[wr, -wi], [wi, wr]])
    f3p = np.block([[wr[:, :h], -wi[:, :h]], [wi[:, :h], wr[:, :h]]]) / n
    as_bf16 = lambda a: jnp.asarray(a, F32).astype(BF16)
    return dict(f1=as_bf16(f1), f1p=as_bf16(f1p), f2=as_bf16(f2), f2c=as_bf16(f2c), f3p=as_bf16(f3p),
                tr=jnp.asarray(tr, F32), ti=jnp.asarray(ti, F32), tr16=as_bf16(tr), ti16=as_bf16(ti))


def _const_specs(arrays, nargs):
    return [pl.BlockSpec(a.shape, (lambda nd: (lambda *_: (0,) * nd))(a.ndim)) for a in arrays]


def _fft_fwd(x3, f1, f2, tr, ti):
    g = x3.shape[0]
    r = FFT_R
    wd = tr.dtype
    xt = jnp.swapaxes(x3.astype(wd), 1, 2).reshape(g * r, r)
    a = _dot(xt.astype(BF16), f1)
    ar = a[:, :r].astype(wd).reshape(g, r, r)
    ai = a[:, r:].astype(wd).reshape(g, r, r)
    pr = ar * tr - ai * ti
    pi = ar * ti + ai * tr
    pt = jnp.concatenate([jnp.swapaxes(pr, 1, 2), jnp.swapaxes(pi, 1, 2)], axis=-1).reshape(g * r, 2 * r)
    x = _dot(pt.astype(BF16), f2)
    return x[:, :r], x[:, r:]


def _fft_inv(yr, yi, f2c, f3, tr, ti):
    r = FFT_R
    g = yr.shape[0] // r
    wd = tr.dtype
    b = _dot(jnp.concatenate([yr, yi], axis=-1).astype(BF16), f2c)
    br = b[:, :r].astype(wd).reshape(g, r, r)
    bi = b[:, r:].astype(wd).reshape(g, r, r)
    cr = br * tr + bi * ti
    ci = bi * tr - br * ti
    ct = jnp.concatenate([jnp.swapaxes(cr, 1, 2), jnp.swapaxes(ci, 1, 2)], axis=-1).reshape(g * r, 2 * r)
    yt = _dot(ct.astype(BF16), f3)
    return jnp.swapaxes(yt.reshape(g, r, r), 1, 2)


def _filt_mlp_body(ft_ref, w1_ref, b1_ref, f1_ref, w2_ref, b2_ref, f2_ref, o_ref):
    z = jnp.sin(f1_ref[...] * (_dot3(w1_ref[...], ft_ref[0]) + b1_ref[...]))
    o_ref[0] = jnp.sin(f2_ref[...] * (_dot3(w2_ref[...], z) + b2_ref[...]))


def _filt_mlp(featst, w1t, b1, f1, w2t, b2, f2):
    _, ke, seq = featst.shape
    col = lambda d: (0, 0)
    return pl.pallas_call(
        _filt_mlp_body,
        grid=(2,),
        in_specs=[
            pl.BlockSpec((1, ke, seq), lambda d: (d, 0, 0)),
            pl.BlockSpec((HY_FFN, ke), col),
            pl.BlockSpec((HY_FFN, 1), col),
            pl.BlockSpec((HY_FFN, 1), col),
            pl.BlockSpec((HY_FFN, HY_FFN), col),
            pl.BlockSpec((HY_FFN, 1), col),
            pl.BlockSpec((HY_FFN, 1), col),
        ],
        out_specs=pl.BlockSpec((1, HY_FFN, seq), lambda d: (d, 0, 0)),
        out_shape=jax.ShapeDtypeStruct((2, HY_FFN, seq), F32),
        compiler_params=_params(("parallel",)),
        name="hyena_filter_mlp",
    )(featst, w1t, b1, f1, w2t, b2, f2)


def _filt_time_body(z_ref, wf_ref, wb_ref, dl_ref, tf_ref, tb_ref, o_ref):
    seq = z_ref.shape[2]
    hf = _dot3(wf_ref[...], z_ref[0]) * jnp.exp(-tf_ref[...] * dl_ref[...])
    hb = _dot3(wb_ref[...], z_ref[1]) * jnp.exp(-tb_ref[...] * dl_ref[...])
    lane = lax.broadcasted_iota(I32, hb.shape, 1)
    hb = jnp.where(lane == 0, 0.0, hb)
    norm = jnp.sum(jnp.abs(hf), axis=-1, keepdims=True) + jnp.sum(jnp.abs(hb), axis=-1, keepdims=True) + EPS
    hf = hf / norm
    hb = hb / norm
    nrow = hf.shape[0] // SUBLANES
    nch = seq // LANES
    for j in range(nch):
        o_ref[:, j, :, :] = hf[:, LANES * j:LANES * (j + 1)].reshape(nrow, SUBLANES, LANES)
        o_ref[:, nch + j, :, :] = hb[:, LANES * j:LANES * (j + 1)].reshape(nrow, SUBLANES, LANES)


def _filt_time(z2t, w3f, w3b, delta_rows, t_f, t_b):
    nrows = w3f.shape[0]
    seq = z2t.shape[2]
    rb = RB_FILT
    return pl.pallas_call(
        _filt_time_body,
        grid=(nrows // rb,),
        in_specs=[
            pl.BlockSpec((2, HY_FFN, seq), lambda r: (0, 0, 0)),
            pl.BlockSpec((rb, HY_FFN), lambda r: (r, 0)),
            pl.BlockSpec((rb, HY_FFN), lambda r: (r, 0)),
            pl.BlockSpec((rb, 1), lambda r: (r, 0)),
            pl.BlockSpec((1, seq), lambda r: (0, 0)),
            pl.BlockSpec((1, seq), lambda r: (0, 0)),
        ],
        out_specs=pl.BlockSpec((rb // SUBLANES, 2 * seq // LANES, SUBLANES, LANES), lambda r: (r, 0, 0, 0)),
        out_shape=jax.ShapeDtypeStruct((nrows // SUBLANES, 2 * seq // LANES, SUBLANES, LANES), F32),
        compiler_params=_params(("parallel",)),
        name="hyena_filter_time",
    )(z2t, w3f, w3b, delta_rows, t_f, t_b)


def _filt_fft_body(k_ref, f1_ref, f2_ref, tr_ref, ti_ref, kr_ref, ki_ref):
    tr = tr_ref[...]
    ti = ti_ref[...]
    for g in range(k_ref.shape[0]):
        x3 = jnp.swapaxes(k_ref[g], 0, 1)
        xr, xi = _fft_fwd(x3, f1_ref[...], f2_ref[...], tr, ti)
        kr_ref[SUBLANES * g:SUBLANES * (g + 1)] = xr.reshape(SUBLANES, FFT_R, FFT_R)
        ki_ref[SUBLANES * g:SUBLANES * (g + 1)] = xi.reshape(SUBLANES, FFT_R, FFT_R)


def _filt_fft(ktile, tables):
    n8, r, _, _ = ktile.shape
    nrows = n8 * SUBLANES
    rf = RF_KF
    out_spec = pl.BlockSpec((rf, r, r), lambda i: (i, 0, 0))
    consts = [tables[k] for k in ("f1", "f2", "tr", "ti")]
    return pl.pallas_call(
        _filt_fft_body,
        grid=(nrows // rf,),
        in_specs=[pl.BlockSpec((rf // SUBLANES, r, SUBLANES, LANES), lambda i: (i, 0, 0, 0))] + _const_specs(consts, 1),
        out_specs=[out_spec, out_spec],
        out_shape=[jax.ShapeDtypeStruct((nrows, r, r), F32)] * 2,
        compiler_params=_params(("parallel",)),
        name="hyena_filter_fft",
    )(ktile, *consts)


def _conv3_time(x, w):
    lane = lax.broadcasted_iota(I32, x.shape, 2)
    zero = jnp.zeros_like(x[:1])
    xm = pltpu.roll(x, 1, axis=2)
    xm = jnp.where(lane == 0, jnp.concatenate([zero, xm[:-1]], axis=0), xm)
    xp = pltpu.roll(x, LANES - 1, axis=2)
    xp = jnp.where(lane == LANES - 1, jnp.concatenate([xp[1:], zero], axis=0), xp)
    return w[0] * xm + w[1] * x + w[2] * xp + w[3]


def _hyena_body(sk_ref, v_ref, x1_ref, x2_ref, wv_ref, w1_ref, w2_ref, k0r_ref, k0i_ref, k1r_ref, k1i_ref,
                f1p_ref, f2_ref, f2c_ref, f3p_ref, tr_ref, ti_ref, o_ref, *, rc):
    ct = pl.program_id(0)
    tr = tr_ref[...]
    ti = ti_ref[...]
    np_ = v_ref.shape[2]
    nb = v_ref.shape[0]

    def load(ref, w_ref, g):
        parts = [jnp.swapaxes(_conv3_time(ref[bb, g], w_ref[:, g]), 0, 1) for bb in range(nb)]
        return jnp.concatenate(parts, axis=1)

    def long_conv(z3, kr_ref, ki_ref, g):
        xr, xi = _fft_fwd(z3, f1p_ref[...], f2_ref[...], tr, ti)
        kr = kr_ref[pl.ds(g * SUBLANES, SUBLANES)].reshape(SUBLANES * FFT_R, FFT_R)
        ki = ki_ref[pl.ds(g * SUBLANES, SUBLANES)].reshape(SUBLANES * FFT_R, FFT_R)
        yr = xr * kr - xi * ki
        yi = xr * ki + xi * kr
        return _fft_inv(yr, yi, f2c_ref[...], f3p_ref[...], tr, ti)

    def group(g, carry):
        cbase = ct * rc + g * SUBLANES

        def skip(z3, order):
            return jnp.stack([sk_ref[order, cbase + ci] * z3[ci] for ci in range(SUBLANES)], axis=0)

        v3 = load(v_ref, wv_ref, g)
        z1 = load(x1_ref, w1_ref, g) * (long_conv(v3, k0r_ref, k0i_ref, g) + skip(v3, 0))
        out = load(x2_ref, w2_ref, g) * (long_conv(z1, k1r_ref, k1i_ref, g) + skip(z1, 1))
        for bb in range(nb):
            o_ref[bb, g] = jnp.swapaxes(out[:, bb * np_:(bb + 1) * np_, :], 0, 1)
        return carry

    lax.fori_loop(0, rc // SUBLANES, group, 0)


def _hyena(hy5, cwt, skip, kfr, kfi, tables):
    bsz, n8, np_, _, _ = hy5.shape
    assert bsz % 2 == 0 and 2 * np_ == FFT_R
    rc = RC_HY
    r8 = rc // SUBLANES
    nct = D_HY // rc
    r = FFT_R

    def xspec(off):
        return pl.BlockSpec((2, r8, np_, SUBLANES, LANES), lambda c, b: (b, c + off * nct, 0, 0, 0))

    def wspec(off):
        return pl.BlockSpec((HY_CONV + 1, r8, SUBLANES, LANES), lambda c, b: (0, c + off * nct, 0, 0))

    def kspec(order):
        return pl.BlockSpec((rc, r, r), lambda c, b: (c + order * nct, 0, 0))

    consts = [tables[k] for k in ("f1p", "f2", "f2c", "f3p", "tr16", "ti16")]
    return pl.pallas_call(
        functools.partial(_hyena_body, rc=rc),
        grid=(nct, bsz // 2),
        in_specs=[pl.BlockSpec(memory_space=pltpu.SMEM), xspec(0), xspec(1), xspec(2), wspec(0), wspec(1), wspec(2),
                  kspec(0), kspec(0), kspec(1), kspec(1)] + _const_specs(consts, 2),
        out_specs=pl.BlockSpec((2, r8, np_, SUBLANES, LANES), lambda c, b: (b, c, 0, 0, 0)),
        out_shape=jax.ShapeDtypeStruct((bsz, D_HY // SUBLANES, np_, SUBLANES, LANES), F32),
        compiler_params=_params(("parallel", "parallel")),
        name="hyena",
    )(skip, hy5, hy5, hy5, cwt, cwt, cwt, kfr, kfi, kfr, kfi, *consts)


def _gelu_tanh(x):
    return 0.5 * x * (1.0 + jnp.tanh(math.sqrt(2.0 / math.pi) * (x + 0.044715 * (x * x * x))))


def _mix_body(hf_ref, hb_ref, gate_ref, yhy_ref, x_ref, gl_ref, gh_ref, wl_ref, wh_ref, gffn_ref,
              wrh_ref, wrl_ref, rb_ref, tri_ref, xmid_ref, h_ref, route_ref, cnt_ref):
    tm = x_ref.shape[1]
    y_lru = (hf_ref[0, 0] + hb_ref[0, 0]) * _gelu_tanh(gate_ref[0])
    nl = _rms(y_lru, gl_ref[...]).astype(BF16)
    gh = gh_ref[...]
    for j in range(tm // LANES):
        yt = yhy_ref[0, :, j, :, :].reshape(D_HY, LANES)
        nt = (yt * lax.rsqrt(jnp.mean(yt * yt, axis=0, keepdims=True) + EPS) * gh).astype(BF16)
        rows = slice(LANES * j, LANES * (j + 1))
        mix = _dot(nl[rows], wl_ref[...]) + lax.dot_general(
            nt, wh_ref[...], (((0,), (0,)), ((), ())), preferred_element_type=F32)
        xmid_ref[0, rows, :] = x_ref[0, rows, :] + mix
    h = _rms(xmid_ref[0], gffn_ref[...])
    h_ref[...] = h.astype(BF16)

    hh, hl = _split_bf16(h)
    logits = _dot(hh, wrh_ref[...]) + _dot(hl, wrh_ref[...]) + _dot(hh, wrl_ref[...]) + rb_ref[...]
    lane = lax.broadcasted_iota(I32, logits.shape, 1)
    big = jnp.int32(1 << 20)
    is_g = lane < N_GROUPS
    lg = jnp.where(is_g, logits, NEG_BIG)
    mg = jnp.max(lg, axis=-1, keepdims=True)
    gidx = jnp.min(jnp.where(lg == mg, lane, big), axis=-1, keepdims=True)
    g_p = 1.0 / jnp.sum(jnp.where(is_g, jnp.exp(lg - mg), 0.0), axis=-1, keepdims=True)
    e_lo = N_GROUPS + EXPERTS_PER_GROUP * gidx
    is_e = jnp.logical_and(lane >= e_lo, lane < e_lo + EXPERTS_PER_GROUP)
    le = jnp.where(is_e, logits, NEG_BIG)
    m1 = jnp.max(le, axis=-1, keepdims=True)
    i1 = jnp.min(jnp.where(le == m1, lane, big), axis=-1, keepdims=True)
    le2 = jnp.where(lane == i1, NEG_BIG, le)
    m2 = jnp.max(le2, axis=-1, keepdims=True)
    i2 = jnp.min(jnp.where(le2 == m2, lane, big), axis=-1, keepdims=True)
    ratio = jnp.exp(m2 - m1)
    gate1 = g_p / (1.0 + ratio)
    gate2 = g_p * ratio / (1.0 + ratio)
    oh1 = lane == i1
    oh2 = lane == i2
    oh = jnp.where(oh1, 1.0, jnp.where(oh2, 1.0, 0.0))
    tot = _dot(tri_ref[...], oh.astype(BF16))
    rank1 = jnp.sum(jnp.where(oh1, tot, 0.0), axis=-1, keepdims=True)
    rank2 = jnp.sum(jnp.where(oh2, tot, 0.0), axis=-1, keepdims=True)
    cnt_ref[0] = jnp.sum(oh, axis=0, keepdims=True)
    cols = [(i1 - N_GROUPS).astype(F32), (i2 - N_GROUPS).astype(F32), rank1, rank2, gate1, gate2]
    route = jnp.zeros(logits.shape, F32)
    for k, col in enumerate(cols):
        route = jnp.where(lane == k, col, route)
    route_ref[...] = route[:, :SUBLANES]


def _mix_router(h4, lru, yhy5, x, gl, gh, wl, wh, gffn, wrh, wrl, rb, tri):
    bsz, seq, d = x.shape
    tm = TM_MIX
    nt = seq // tm
    n_tok = bsz * seq
    c2 = lambda b, i: (0, 0)
    return pl.pallas_call(
        _mix_body,
        grid=(bsz, nt),
        in_specs=[
            pl.BlockSpec((1, 1, tm, D_LRU), lambda b, i: (0, b, i, 0)),
            pl.BlockSpec((1, 1, tm, D_LRU), lambda b, i: (1, b, i, 0)),
            pl.BlockSpec((1, tm, D_LRU), lambda b, i: (b, i, 1)),
            pl.BlockSpec((1, D_HY // SUBLANES, tm // LANES, SUBLANES, LANES), lambda b, i: (b, 0, i, 0, 0)),
            pl.BlockSpec((1, tm, d), lambda b, i: (b, i, 0)),
            pl.BlockSpec((1, D_LRU), c2),
            pl.BlockSpec((D_HY, 1), c2),
            pl.BlockSpec((D_LRU, d), c2),
            pl.BlockSpec((D_HY, d), c2),
            pl.BlockSpec((1, d), c2),
            pl.BlockSpec((d, LANES), c2),
            pl.BlockSpec((d, LANES), c2),
            pl.BlockSpec((1, LANES), c2),
            pl.BlockSpec((tm, tm), c2),
        ],
        out_specs=[
            pl.BlockSpec((1, tm, d), lambda b, i: (b, i, 0)),
            pl.BlockSpec((tm, d), lambda b, i: (b * nt + i, 0)),
            pl.BlockSpec((tm, SUBLANES), lambda b, i: (b * nt + i, 0)),
            pl.BlockSpec((1, 1, LANES), lambda b, i: (b * nt + i, 0, 0)),
        ],
        out_shape=[
            jax.ShapeDtypeStruct((bsz, seq, d), F32),
            jax.ShapeDtypeStruct((n_tok, d), BF16),
            jax.ShapeDtypeStruct((n_tok, SUBLANES), F32),
            jax.ShapeDtypeStruct((n_tok // tm, 1, LANES), F32),
        ],
        compiler_params=_params(("parallel", "parallel")),
        name="mix_router",
    )(h4, h4, lru, yhy5, x, gl, gh, wl, wh, gffn, wrh, wrl, rb, tri)


def _local_slots(route, lsl):
    lane = lax.broadcasted_iota(I32, (route.shape[0], LANES), 1)
    pos = []
    for k in range(2):
        start = jnp.sum(jnp.where(lane == route[:, k:k + 1].astype(I32), lsl, 0.0), axis=-1, keepdims=True)
        pos.append((start + route[:, 2 + k:3 + k]).astype(I32))
    return pos


def _start_chunks(i, tch_ref, dst_ref, make_copy):
    def per_chunk(k, c):
        make_copy(k * ROW_CHUNK, dst_ref[i * MAX_CHUNKS + k], ROW_CHUNK).start()
        return c

    lax.fori_loop(0, tch_ref[i], per_chunk, 0)


def _wait_chunks(count, make_copy):
    nbits = MAX_CHUNKS.bit_length()
    for j in reversed(range(nbits)):
        @pl.when((count >> j) & 1 == 1)
        def _():
            make_copy(0, 0, ROW_CHUNK << j).wait()


def _dispatch_body(dst_ref, tch_ref, tn_ref, ts_ref, nu_ref, h_ref, route_ref, lsl_ref, xb_ref,
                   buf, zbuf, sem, zsem):
    i = pl.program_id(0)
    n_tiles = pl.num_programs(0)
    tm = h_ref.shape[0]
    lbuf = buf.shape[1]
    n_blocks = xb_ref.shape[0] // MOE_BLK
    slot = i % 2

    def aligned(r):
        return pl.ds(pl.multiple_of(r, ROW_CHUNK), ROW_CHUNK)

    @pl.when(i == 0)
    def _():
        zbuf[...] = jnp.zeros_like(zbuf)

        def zero_chunk(r):
            return pltpu.make_async_copy(zbuf.at[pl.ds(0, ROW_CHUNK)], xb_ref.at[aligned(r)], zsem)

        def zero_block(j):
            return pltpu.make_async_copy(zbuf, xb_ref.at[pl.ds(pl.multiple_of(j * MOE_BLK, MOE_BLK), MOE_BLK)], zsem)

        def per_expert(e, total):
            def per_chunk(k, c):
                zero_chunk(ts_ref[e] + k * ROW_CHUNK).start()
                return c

            lax.fori_loop(0, tn_ref[e], per_chunk, 0)
            return total + tn_ref[e]

        total = lax.fori_loop(0, N_EXPERTS, per_expert, 0)

        def start_block(j, c):
            zero_block(j).start()
            return c

        lax.fori_loop(nu_ref[0], n_blocks, start_block, 0)

        def wait_chunk(k, c):
            zero_chunk(0).wait()
            return c

        lax.fori_loop(0, total, wait_chunk, 0)

        def wait_block(j, c):
            zero_block(0).wait()
            return c

        lax.fori_loop(nu_ref[0], n_blocks, wait_block, 0)

    l0, l1 = _local_slots(route_ref[...], lsl_ref[0])
    lane = lax.broadcasted_iota(I32, (tm, LANES), 1)
    slots_t = jnp.where(lane == 0, l0, jnp.where(lane == 1, l1, 0)).astype(F32).T
    l0t = slots_t[0:1].astype(I32)
    l1t = slots_t[1:2].astype(I32)
    row = lax.broadcasted_iota(I32, (lbuf, tm), 0)
    perm = jnp.where(row == l0t, 1.0, jnp.where(row == l1t, 1.0, 0.0)).astype(BF16)
    srt = _dot(perm, h_ref[...]).astype(BF16)

    def copy_from(s):
        def make_copy(src_row, dst_row, rows):
            return pltpu.make_async_copy(buf.at[s, pl.ds(pl.multiple_of(src_row, ROW_CHUNK), rows)],
                                         xb_ref.at[pl.ds(pl.multiple_of(dst_row, ROW_CHUNK), rows)], sem.at[s])
        return make_copy

    @pl.when(i >= 2)
    def _():
        _wait_chunks(tch_ref[i - 2], copy_from(slot))

    buf[slot] = srt
    _start_chunks(i, tch_ref, dst_ref, copy_from(slot))

    @pl.when(i == n_tiles - 1)
    def _():
        @pl.when(i >= 1)
        def _():
            _wait_chunks(tch_ref[i - 1], copy_from(1 - slot))

        _wait_chunks(tch_ref[i], copy_from(slot))


def _dispatch(chunk_dst, tile_nch, tail_n, tail_s, n_used, h, route, lsl, n_slots):
    n_tok, d = h.shape
    tm = TM_MIX
    grid_spec = pltpu.PrefetchScalarGridSpec(
        num_scalar_prefetch=5,
        grid=(n_tok // tm,),
        in_specs=[
            pl.BlockSpec((tm, d), lambda i, *_: (i, 0)),
            pl.BlockSpec((tm, SUBLANES), lambda i, *_: (i, 0)),
            pl.BlockSpec((1, 1, LANES), lambda i, *_: (i, 0, 0)),
        ],
        out_specs=pl.BlockSpec(memory_space=pl.ANY),
        scratch_shapes=[pltpu.VMEM((2, LOCAL_ROWS, d), BF16), pltpu.VMEM((MOE_BLK, d), BF16),
                        pltpu.SemaphoreType.DMA((2,)), pltpu.SemaphoreType.DMA(())],
    )
    return pl.pallas_call(
        _dispatch_body,
        grid_spec=grid_spec,
        out_shape=jax.ShapeDtypeStruct((n_slots, d), BF16),
        compiler_params=_params(("arbitrary",)),
        name="moe_dispatch",
    )(chunk_dst, tile_nch, tail_n, tail_s, n_used, h, route, lsl)


def _expert_body(be_ref, fl_ref, nu_ref, xb_ref, w1_ref, w3_ref, w2_ref, yb_ref, w1_sc, w3_sc, w2_sc):
    del be_ref
    j = pl.program_id(0)
    used = j < nu_ref[0]

    @pl.when(jnp.logical_and(used, fl_ref[j] == 1))
    def _():
        w1_sc[...] = w1_ref[...].astype(BF16)
        w3_sc[...] = w3_ref[...].astype(BF16)
        w2_sc[...] = w2_ref[...].astype(BF16)

    @pl.when(used)
    def _():
        xb = xb_ref[...]
        a = _dot(xb, w1_sc[...])
        b = _dot(xb, w3_sc[...])
        act = (a * _sigmoid(a) * b).astype(BF16)
        yb_ref[...] = _dot(act, w2_sc[...]).astype(BF16)

    @pl.when(jnp.logical_not(used))
    def _():
        yb_ref[...] = jnp.zeros_like(yb_ref)


def _experts(block_e, first_flag, n_used, xb, w1, w3, w2):
    n_slots, d = xb.shape
    blk = MOE_BLK
    de = w1.shape[2]
    grid_spec = pltpu.PrefetchScalarGridSpec(
        num_scalar_prefetch=3,
        grid=(n_slots // blk,),
        in_specs=[
            pl.BlockSpec((blk, d), lambda j, be, fl, nu: (j, 0)),
            pl.BlockSpec((None, d, de), lambda j, be, fl, nu: (be[j], 0, 0)),
            pl.BlockSpec((None, d, de), lambda j, be, fl, nu: (be[j], 0, 0)),
            pl.BlockSpec((None, de, d), lambda j, be, fl, nu: (be[j], 0, 0)),
        ],
        out_specs=pl.BlockSpec((blk, d), lambda j, be, fl, nu: (j, 0)),
        scratch_shapes=[pltpu.VMEM((d, de), BF16), pltpu.VMEM((d, de), BF16), pltpu.VMEM((de, d), BF16)],
    )
    return pl.pallas_call(
        _expert_body,
        grid_spec=grid_spec,
        out_shape=jax.ShapeDtypeStruct((n_slots, d), BF16),
        compiler_params=_params(("arbitrary",)),
        name="moe_experts",
    )(block_e, first_flag, n_used, xb, w1, w3, w2)


def _combine_body(dst_ref, tch_ref, xmid_ref, route_ref, lsl_ref, g_ref, yb_ref, o_ref, buf, sem):
    i = pl.program_id(0)
    n_tiles = pl.num_programs(0)
    tm = xmid_ref.shape[0]
    lbuf = buf.shape[1]
    slot = i % 2

    def copy_into(s):
        def make_copy(local_row, slot_row, rows):
            return pltpu.make_async_copy(yb_ref.at[pl.ds(pl.multiple_of(slot_row, ROW_CHUNK), rows)],
                                         buf.at[s, pl.ds(pl.multiple_of(local_row, ROW_CHUNK), rows)], sem.at[s])
        return make_copy

    @pl.when(i == 0)
    def _():
        buf[...] = jnp.zeros_like(buf)
        _start_chunks(i, tch_ref, dst_ref, copy_into(slot))

    @pl.when(i + 1 < n_tiles)
    def _():
        _start_chunks(i + 1, tch_ref, dst_ref, copy_into(1 - slot))

    route = route_ref[...]
    l0, l1 = _local_slots(route, lsl_ref[0])
    lane = lax.broadcasted_iota(I32, (tm, lbuf), 1)
    gmat = (jnp.where(lane == l0, route[:, 4:5], 0.0) + jnp.where(lane == l1, route[:, 5:6], 0.0)).astype(BF16)
    _wait_chunks(tch_ref[i], copy_into(slot))
    y = xmid_ref[...] + _dot(gmat, buf[slot])
    o_ref[...] = _rms(y, g_ref[...])


def _combine(chunk_dst, tile_nch, xmid, route, lsl, g, yb):
    n_tok, d = xmid.shape
    tm = TM_MIX
    grid_spec = pltpu.PrefetchScalarGridSpec(
        num_scalar_prefetch=2,
        grid=(n_tok // tm,),
        in_specs=[
            pl.BlockSpec((tm, d), lambda i, *_: (i, 0)),
            pl.BlockSpec((tm, SUBLANES), lambda i, *_: (i, 0)),
            pl.BlockSpec((1, 1, LANES), lambda i, *_: (i, 0, 0)),
            pl.BlockSpec((1, d), lambda i, *_: (0, 0)),
            pl.BlockSpec(memory_space=pl.ANY),
        ],
        out_specs=pl.BlockSpec((tm, d), lambda i, *_: (i, 0)),
        scratch_shapes=[pltpu.VMEM((2, LOCAL_ROWS, d), BF16), pltpu.SemaphoreType.DMA((2,))],
    )
    return pl.pallas_call(
        _combine_body,
        grid_spec=grid_spec,
        out_shape=jax.ShapeDtypeStruct((n_tok, d), F32),
        compiler_params=_params(("arbitrary",)),
        name="moe_combine",
    )(chunk_dst, tile_nch, xmid, route, lsl, g, yb)


def _filter_features(seq):
    pos = jnp.arange(seq, dtype=F32)
    t = jnp.linspace(0.0, 1.0, seq, dtype=F32)
    w = (2.0 * math.pi / seq) * pos
    bands = jnp.linspace(1e-4, HY_BANDS - 1, HY_BANDS, dtype=F32)
    ang = w[:, None] * bands[None, :]
    feats = jnp.concatenate([t[:, None], jnp.cos(ang), jnp.sin(ang)], axis=-1)
    rev = lambda a: jnp.concatenate([jnp.zeros_like(a[:1]), a[1:][::-1]], axis=0)
    pad = lambda a: jnp.pad(a, ((0, 0), (0, HY_EMB_PAD - HY_EMB)))
    featst = jnp.stack([pad(feats).T, pad(rev(feats)).T], axis=0)
    return featst, t[None, :], rev(t[:, None]).T


def _gate_weights(wr, wi):
    half = D_LRU // 2
    hph = half // LRU_HEAD_DIM
    eye = jnp.eye(hph, dtype=wr.dtype)[:, None, :, None]

    def block_diag(w):
        return (eye * w[:, :, None, :]).reshape(half, half)

    out = [jnp.concatenate([block_diag(wr[hh * hph:(hh + 1) * hph]), block_diag(wi[hh * hph:(hh + 1) * hph])], axis=1)
           for hh in range(2)]
    return jnp.stack(out, axis=0).astype(BF16)


def _layer(x, l, p):
    bsz, seq, d = x.shape
    n_tok = bsz * seq
    row = lambda a: a.reshape(1, -1).astype(F32)
    col = lambda a: a.reshape(-1, 1).astype(F32)

    w_in = p["w_in"][l]
    wl = w_in[:, :2 * D_LRU].astype(BF16)
    wht = w_in[:, 2 * D_LRU:].T.astype(BF16)
    lru, hy5 = _inproj(x, row(p["norm_mix_g"][l]), wl, wht)

    wg = jnp.stack([_gate_weights(p["lru_wr_f"][l], p["lru_wi_f"][l]),
                    _gate_weights(p["lru_wr_b"][l], p["lru_wi_b"][l])], axis=0)
    stack2 = lambda a, b: jnp.stack([a.reshape(1, -1), b.reshape(1, -1)], axis=0).astype(F32)
    h4 = _lru(lru, p["lru_conv_w"][l], row(p["lru_conv_b"][l]), wg,
              stack2(p["lru_br_f"][l], p["lru_br_b"][l]), stack2(p["lru_bi_f"][l], p["lru_bi_b"][l]),
              stack2(p["lru_lambda_f"][l], p["lru_lambda_b"][l]))

    tables = _fft_tables()
    featst, t_f, t_b = _filter_features(seq)
    w1t = jnp.pad(p["hy_filt_w1"][l].T, ((0, 0), (0, HY_EMB_PAD - HY_EMB)))
    z2t = _filt_mlp(featst, w1t, col(p["hy_filt_b1"][l]), col(p["hy_filt_freq1"][l]),
                    p["hy_filt_w2"][l].T, col(p["hy_filt_b2"][l]), col(p["hy_filt_freq2"][l]))
    w3 = p["hy_filt_w3"][l]
    nfr = HY_ORDER * D_HY
    deltas = jnp.abs(jnp.linspace(math.log(HY_DECAY_TARGET) / HY_SLOW_DECAY_PCT,
                                  math.log(HY_DECAY_TARGET) / HY_FAST_DECAY_PCT, D_HY, dtype=F32))
    ktile = _filt_time(z2t, w3[:, :nfr].T, w3[:, nfr:].T, col(jnp.tile(deltas, HY_ORDER)), t_f, t_b)
    kfr, kfi = _filt_fft(ktile, tables)
    cwt = jnp.concatenate([p["hy_conv_w"][l], p["hy_conv_b"][l][None, :]], axis=0)
    cwt = jnp.broadcast_to(cwt.reshape(HY_CONV + 1, -1, SUBLANES, 1), (HY_CONV + 1, cwt.shape[1] // SUBLANES, SUBLANES, LANES))
    yhy5 = _hyena(hy5, cwt, p["hy_skip"][l], kfr, kfi, tables)

    w_out = p["w_out"][l]
    wr_cat = jnp.concatenate([p["router_group_w"][l], p["router_expert_w"][l]], axis=1)
    wr_cat = jnp.pad(wr_cat, ((0, 0), (0, LANES - wr_cat.shape[1])))
    wrh, wrl = _split_bf16(wr_cat)
    rb = jnp.concatenate([p["router_group_b"][l], p["router_expert_b"][l]])
    rb = jnp.pad(rb, (0, LANES - rb.shape[0])).reshape(1, LANES)
    tri = (jnp.arange(TM_MIX)[:, None] > jnp.arange(TM_MIX)[None, :]).astype(BF16)
    xmid, h, route, tile_cnt = _mix_router(
        h4, lru, yhy5, x, row(p["grp_norm_lru_g"][l]), col(p["grp_norm_hy_g"][l]),
        w_out[:D_LRU].astype(BF16), w_out[D_LRU:].astype(BF16), row(p["norm_ffn_g"][l]), wrh, wrl, rb, tri)

    blk = MOE_BLK
    ch = ROW_CHUNK
    n_tiles = n_tok // TM_MIX
    n_blocks = (2 * n_tok + n_tiles * N_EXPERTS * (ch - 1)) // blk + N_EXPERTS
    cnt = tile_cnt[:, 0, N_GROUPS:N_GROUPS + N_EXPERTS].astype(I32)
    run = (cnt + ch - 1) // ch * ch
    tot_e = jnp.sum(run, axis=0)
    padded_e = (tot_e + blk - 1) // blk * blk
    pad_end = jnp.cumsum(padded_e)
    pad_start = pad_end - padded_e
    gstart = pad_start[None, :] + jnp.cumsum(run, axis=0) - run
    lstart = jnp.cumsum(run, axis=1) - run
    lsl = jnp.pad(lstart.astype(F32), ((0, 0), (0, LANES - N_EXPERTS))).reshape(n_tiles, 1, LANES)
    flat = lambda a: a.reshape(-1).astype(I32)
    tile_nch = flat(jnp.sum(run, axis=1) // ch)
    nch_end = jnp.cumsum(run // ch, axis=1)
    k = jnp.arange(MAX_CHUNKS, dtype=I32)
    owner = jnp.minimum(jnp.sum(k[None, :, None] >= nch_end[:, None, :], axis=2), N_EXPERTS - 1)
    run_first = jnp.take_along_axis(gstart - lstart, owner, axis=1)
    chunk_dst = flat(run_first + k[None, :] * ch)
    tail_n = flat((padded_e - tot_e) // ch)
    tail_s = flat(pad_start + tot_e)
    block_start = jnp.arange(n_blocks, dtype=I32) * blk
    block_e = jnp.minimum(jnp.sum(block_start[:, None] >= pad_end[None, :], axis=1), N_EXPERTS - 1).astype(I32)
    first_flag = jnp.concatenate([jnp.ones((1,), I32), (block_e[1:] != block_e[:-1]).astype(I32)])
    n_used = (pad_end[-1:] // blk).astype(I32)

    xb = _dispatch(chunk_dst, tile_nch, tail_n, tail_s, n_used, h, route, lsl, n_blocks * blk)
    yb = _experts(block_e, first_flag, n_used, xb, p["exp_w1"][l], p["exp_w3"][l], p["exp_w2"][l])
    return xmid.reshape(n_tok, d), route, (chunk_dst, tile_nch, lsl), yb


def kernel(x, norm_mix_g, w_in, lru_conv_w, lru_conv_b, lru_wr_f, lru_br_f, lru_wi_f, lru_bi_f, lru_lambda_f, lru_wr_b, lru_br_b, lru_wi_b, lru_bi_b, lru_lambda_b, hy_conv_w, hy_conv_b, hy_filt_w1, hy_filt_b1, hy_filt_freq1, hy_filt_w2, hy_filt_b2, hy_filt_freq2, hy_filt_w3, hy_skip, grp_norm_lru_g, grp_norm_hy_g, w_out, norm_ffn_g, router_group_w, router_group_b, router_expert_w, router_expert_b, exp_w1, exp_w3, exp_w2, norm_final_g):
    p = dict(locals())
    bsz, seq, d = x.shape
    assert d == D_MODEL and 2 * seq == FFT_R * FFT_R and w_in.shape[0] == 1
    xmid, route, (chunk_dst, tile_nch, lsl), yb = _layer(x, 0, p)
    out = _combine(chunk_dst, tile_nch, xmid, route, lsl, norm_final_g.reshape(1, d), yb)
    return out.reshape(bsz, seq, d)
```

```python
import functools
import math

import numpy as np
import jax
import jax.numpy as jnp
from jax import lax
from jax.experimental import pallas as pl
from jax.experimental.pallas import tpu as pltpu

F32 = jnp.float32
BF16 = jnp.bfloat16
I32 = jnp.int32

D_MODEL = 1024
D_LRU = 512
D_HY = 512
LRU_HEADS = 8
LRU_HEAD_DIM = D_LRU // LRU_HEADS
LRU_CONV = 4
LRU_C = 8.0
HY_ORDER = 2
HY_CONV = 3
HY_BANDS = 16
HY_EMB = 2 * HY_BANDS + 1
HY_EMB_PAD = 40
HY_FFN = 64
HY_FAST_DECAY_PCT = 0.3
HY_SLOW_DECAY_PCT = 1.5
HY_DECAY_TARGET = 1e-2
N_GROUPS = 4
EXPERTS_PER_GROUP = 8
N_EXPERTS = N_GROUPS * EXPERTS_PER_GROUP
D_EXPERT = D_MODEL // 2
EPS = 1e-6

LANES = 128
SUBLANES = 8
FFT_R = 128
VMEM_LIMIT = 56 * 1024 * 1024

TM_IN = 512
TC_LRU = 512
RC_HY = 32
RF_KF = 16
HY_SPLIT = 2
RB_FILT = 64
TM_MIX = 512
MOE_BLK = 512
ROW_CHUNK = 16
LOCAL_ROWS = -(-(2 * TM_MIX + N_EXPERTS * (ROW_CHUNK - 1)) // LANES) * LANES
MAX_CHUNKS = LOCAL_ROWS // ROW_CHUNK
NEG_BIG = -1e30


def _params(sem, vmem=VMEM_LIMIT):
    return pltpu.CompilerParams(dimension_semantics=sem, vmem_limit_bytes=vmem)


def _rms(x, g):
    return x * lax.rsqrt(jnp.mean(x * x, axis=-1, keepdims=True) + EPS) * g


def _sigmoid(x):
    return 1.0 / (1.0 + jnp.exp(-x))


def _split_bf16(a):
    hi = a.astype(BF16)
    lo = (a - hi.astype(F32)).astype(BF16)
    return hi, lo


def _dot(a, b):
    return jnp.dot(a, b, preferred_element_type=F32)


def _dot3(a, b):
    ah, al = _split_bf16(a)
    bh, bl = _split_bf16(b)
    return _dot(ah, bh) + _dot(al, bh) + _dot(ah, bl)


def _inproj_body(x_ref, g_ref, wl_ref, wht_ref, lru_ref, hy_ref):
    hn = _rms(x_ref[0], g_ref[...]).astype(BF16)
    lru_ref[0] = _dot(hn, wl_ref[...])
    hyt = lax.dot_general(wht_ref[...], hn, (((1,), (1,)), ((), ())), preferred_element_type=F32)
    nrow = hyt.shape[0] // SUBLANES
    for j in range(hyt.shape[1] // LANES):
        hy_ref[0, :, j, :, :] = hyt[:, LANES * j:LANES * (j + 1)].reshape(nrow, SUBLANES, LANES)


def _inproj(x, g, wl, wht):
    bsz, seq, d = x.shape
    nl = wl.shape[1]
    nh = wht.shape[0]
    tm = TM_IN
    return pl.pallas_call(
        _inproj_body,
        grid=(bsz, seq // tm),
        in_specs=[
            pl.BlockSpec((1, tm, d), lambda b, i: (b, i, 0)),
            pl.BlockSpec((1, d), lambda b, i: (0, 0)),
            pl.BlockSpec((d, nl), lambda b, i: (0, 0)),
            pl.BlockSpec((nh, d), lambda b, i: (0, 0)),
        ],
        out_specs=[
            pl.BlockSpec((1, tm, nl), lambda b, i: (b, i, 0)),
            pl.BlockSpec((1, nh // SUBLANES, tm // LANES, SUBLANES, LANES), lambda b, i: (b, 0, i, 0, 0)),
        ],
        out_shape=[
            jax.ShapeDtypeStruct((bsz, seq, nl), F32),
            jax.ShapeDtypeStruct((bsz, nh // SUBLANES, seq // LANES, SUBLANES, LANES), F32),
        ],
        compiler_params=_params(("parallel", "parallel")),
        name="inproj",
    )(x, g, wl, wht)


def _sqrt_one_minus_sq(a, neg_log_a):
    om = jnp.maximum(jnp.tanh(neg_log_a) * (1.0 + a * a), 1e-30)
    return om * lax.rsqrt(om)


def _group_scan(a3, b3, reverse):
    sub = lax.broadcasted_iota(I32, a3.shape, 1)
    for s in (1, 2, 4):
        if reverse:
            a_sh = pltpu.roll(a3, SUBLANES - s, axis=1)
            b_sh = pltpu.roll(b3, SUBLANES - s, axis=1)
            m = sub < SUBLANES - s
        else:
            a_sh = pltpu.roll(a3, s, axis=1)
            b_sh = pltpu.roll(b3, s, axis=1)
            m = sub >= s
        b3 = jnp.where(m, a3 * b_sh + b3, b3)
        a3 = jnp.where(m, a3 * a_sh, a3)
    return a3, b3


def _lru_body(cur_ref, prev_ref, next_ref, cw_ref, cb_ref, wg_ref, br_ref, bi_ref, lam_ref, o_ref,
              carry_sc, *, nt, tc):
    d = pl.program_id(1)
    i = pl.program_id(2)
    c = jnp.where(d == 0, i, nt - 1 - i)
    ns = SUBLANES
    ng = tc // ns
    half = D_LRU // 2

    @pl.when(i == 0)
    def _():
        carry_sc[...] = jnp.zeros_like(carry_sc)

    xs = jnp.swapaxes(cur_ref[0].reshape(ns, ng, D_LRU), 0, 1)
    prev = jnp.where(c > 0, prev_ref[0], 0.0)
    nxt = jnp.where(c < nt - 1, next_ref[0], 0.0)
    sub = lax.broadcasted_iota(I32, (ns, D_LRU), 0)

    def prev_segment(slab, halo_row):
        return jnp.where(sub == 0, halo_row, pltpu.roll(slab, 1, axis=0))

    def next_segment(slab, halo_row):
        return jnp.where(sub == ns - 1, halo_row, pltpu.roll(slab, ns - 1, axis=0))

    xext = jnp.concatenate([prev_segment(xs[ng - 2], prev[ns - 2:ns - 1])[None],
                            prev_segment(xs[ng - 1], prev[ns - 1:ns])[None], xs,
                            next_segment(xs[0], nxt[0:1])[None]], axis=0)
    cw = cw_ref[...]
    xc = cb_ref[...] + cw[0:1] * xext[0:ng]
    for k in range(1, LRU_CONV):
        xc = xc + cw[k:k + 1] * xext[k:k + ng]
    xc = xc.reshape(tc, D_LRU)

    lam = lam_ref[0]
    nlam = -lam
    softplus = jnp.maximum(nlam, 0.0) + jnp.log1p(jnp.exp(-jnp.abs(nlam)))
    half_c = (0.5 * LRU_C) * softplus

    def gates(hh):
        sl = slice(half * hh, half * (hh + 1))
        xh = xc[:, sl]
        logits = _dot(xh.astype(BF16), wg_ref[0, hh])
        tr_ = jnp.tanh(0.5 * (logits[:, :half] + br_ref[0][:, sl]))
        gi = 0.5 + 0.5 * jnp.tanh(0.5 * (logits[:, half:] + bi_ref[0][:, sl]))
        neg_log_a = half_c[:, sl] * (1.0 + tr_)
        a = jnp.exp(-neg_log_a)
        b = _sqrt_one_minus_sq(a, neg_log_a) * (gi * xh)
        return a.reshape(ng, ns, half), b.reshape(ng, ns, half)

    def run(reverse):
        subh = lax.broadcasted_iota(I32, (ns, half), 0)
        for hh in range(2):
            sl = slice(half * hh, half * (hh + 1))
            a3, b3 = gates(hh)
            h = jnp.zeros((ns, half), F32)
            p = jnp.ones((ns, half), F32)
            hs = [None] * ng
            ps = [None] * ng
            for g in (range(ng - 1, -1, -1) if reverse else range(ng)):
                h = a3[g] * h + b3[g]
                p = a3[g] * p
                hs[g] = h
                ps[g] = p
            ac, hc = _group_scan(p[None], h[None], reverse)
            c_in = carry_sc[:, sl]
            end = ac[0] * c_in + hc[0]
            if reverse:
                seg_in = jnp.where(subh == ns - 1, c_in, pltpu.roll(end, ns - 1, axis=0))
                carry_sc[:, sl] = jnp.broadcast_to(end[0:1], end.shape)
            else:
                seg_in = jnp.where(subh == 0, c_in, pltpu.roll(end, 1, axis=0))
                carry_sc[:, sl] = jnp.broadcast_to(end[ns - 1:ns], end.shape)
            h3 = jnp.stack(hs, axis=0) + jnp.stack(ps, axis=0) * seg_in[None]
            o_ref[0, 0, :, sl] = jnp.swapaxes(h3, 0, 1).reshape(tc, half)

    @pl.when(d == 0)
    def _():
        run(False)

    @pl.when(d == 1)
    def _():
        run(True)


def _lru(lru, cw, cb, wg, br, bi, lam):
    bsz, seq, _ = lru.shape
    tc = TC_LRU
    nt = seq // tc
    r8 = tc // SUBLANES
    nrow8 = seq // SUBLANES

    def cidx(d, i):
        return jnp.where(d == 0, i, nt - 1 - i)

    return pl.pallas_call(
        functools.partial(_lru_body, nt=nt, tc=tc),
        grid=(bsz, 2, nt),
        in_specs=[
            pl.BlockSpec((1, tc, D_LRU), lambda b, d, i: (b, cidx(d, i), 0)),
            pl.BlockSpec((1, SUBLANES, D_LRU), lambda b, d, i: (b, jnp.maximum(cidx(d, i) * r8 - 1, 0), 0)),
            pl.BlockSpec((1, SUBLANES, D_LRU), lambda b, d, i: (b, jnp.minimum((cidx(d, i) + 1) * r8, nrow8 - 1), 0)),
            pl.BlockSpec((LRU_CONV, D_LRU), lambda b, d, i: (0, 0)),
            pl.BlockSpec((1, D_LRU), lambda b, d, i: (0, 0)),
            pl.BlockSpec((1, 2, D_LRU // 2, D_LRU), lambda b, d, i: (d, 0, 0, 0)),
            pl.BlockSpec((1, 1, D_LRU), lambda b, d, i: (d, 0, 0)),
            pl.BlockSpec((1, 1, D_LRU), lambda b, d, i: (d, 0, 0)),
            pl.BlockSpec((1, 1, D_LRU), lambda b, d, i: (d, 0, 0)),
        ],
        out_specs=pl.BlockSpec((1, 1, tc, D_LRU), lambda b, d, i: (d, b, cidx(d, i), 0)),
        out_shape=jax.ShapeDtypeStruct((2, bsz, seq, D_LRU), F32),
        scratch_shapes=[pltpu.VMEM((SUBLANES, D_LRU), F32)],
        compiler_params=_params(("parallel", "arbitrary", "arbitrary")),
        name="rg_lru",
    )(lru, lru, lru, cw, cb, wg, br, bi, lam)


def _fft_tables():
    r = FFT_R
    n = r * r
    idx = np.arange(r, dtype=np.float64)
    ang = 2.0 * np.pi * np.outer(idx, idx) / r
    wr, wi = np.cos(ang), -np.sin(ang)
    angt = 2.0 * np.pi * np.outer(idx, idx) / n
    tr, ti = np.cos(angt), -np.sin(angt)
    h = r // 2
    f1 = np.concatenate([wr, wi], axis=1)
    f1p = np.block([[wr[:h], wi[:h]], [-wi[:h], wr[:h]]])
    f2 = np.block([[wr, wi], [-wi, wr]])
    f2c = np.block([[wr, -wi], [wi, wr]])
    f3p = np.block([[wr[:, :h], -wi[:, :h]], [wi[:, :h], wr[:, :h]]]) / n
    as_bf16 = lambda a: jnp.asarray(a, F32).astype(BF16)
    return dict(f1=as_bf16(f1), f1p=as_bf16(f1p), f2=as_bf16(f2), f2c=as_bf16(f2c), f3p=as_bf16(f3p),
                tr=jnp.asarray(tr, F32), ti=jnp.asarray(ti, F32), tr16=as_bf16(tr), ti16=as_bf16(ti))


def _const_specs(arrays, nargs):
    return [pl.BlockSpec(a.shape, (lambda nd: (lambda *_: (0,) * nd))(a.ndim)) for a in arrays]


def _fft_fwd(x3, f1, f2, tr, ti):
    g = x3.shape[0]
    r = FFT_R
    wd = tr.dtype
    xt = jnp.swapaxes(x3.astype(wd), 1, 2).reshape(g * r, r)
    a = _dot(xt.astype(BF16), f1)
    ar = a[:, :r].astype(wd).reshape(g, r, r)
    ai = a[:, r:].astype(wd).reshape(g, r, r)
    pr = ar * tr - ai * ti
    pi = ar * ti + ai * tr
    pt = jnp.concatenate([jnp.swapaxes(pr, 1, 2), jnp.swapaxes(pi, 1, 2)], axis=-1).reshape(g * r, 2 * r)
    x = _dot(pt.astype(BF16), f2)
    return x[:, :r], x[:, r:]


def _fft_inv(yr, yi, f2c, f3, tr, ti):
    r = FFT_R
    g = yr.shape[0] // r
    wd = tr.dtype
    b = _dot(jnp.concatenate([yr, yi], axis=-1).astype(BF16), f2c)
    br = b[:, :r].astype(wd).reshape(g, r, r)
    bi = b[:, r:].astype(wd).reshape(g, r, r)
    cr = br * tr + bi * ti
    ci = bi * tr - br * ti
    ct = jnp.concatenate([jnp.swapaxes(cr, 1, 2), jnp.swapaxes(ci, 1, 2)], axis=-1).reshape(g * r, 2 * r)
    yt = _dot(ct.astype(BF16), f3)
    return jnp.swapaxes(yt.reshape(g, r, r), 1, 2)


def _filt_mlp_body(ft_ref, w1_ref, b1_ref, f1_ref, w2_ref, b2_ref, f2_ref, o_ref):
    z = jnp.sin(f1_ref[...] * (_dot3(w1_ref[...], ft_ref[0]) + b1_ref[...]))
    o_ref[0] = jnp.sin(f2_ref[...] * (_dot3(w2_ref[...], z) + b2_ref[...]))


def _filt_mlp(featst, w1t, b1, f1, w2t, b2, f2):
    _, ke, seq = featst.shape
    col = lambda d: (0, 0)
    return pl.pallas_call(
        _filt_mlp_body,
        grid=(2,),
        in_specs=[
            pl.BlockSpec((1, ke, seq), lambda d: (d, 0, 0)),
            pl.BlockSpec((HY_FFN, ke), col),
            pl.BlockSpec((HY_FFN, 1), col),
            pl.BlockSpec((HY_FFN, 1), col),
            pl.BlockSpec((HY_FFN, HY_FFN), col),
            pl.BlockSpec((HY_FFN, 1), col),
            pl.BlockSpec((HY_FFN, 1), col),
        ],
        out_specs=pl.BlockSpec((1, HY_FFN, seq), lambda d: (d, 0, 0)),
        out_shape=jax.ShapeDtypeStruct((2, HY_FFN, seq), F32),
        compiler_params=_params(("parallel",)),
        name="hyena_filter_mlp",
    )(featst, w1t, b1, f1, w2t, b2, f2)


def _filt_time_body(z_ref, wf_ref, wb_ref, dl_ref, tf_ref, tb_ref, o_ref):
    seq = z_ref.shape[2]
    hf = _dot3(wf_ref[...], z_ref[0]) * jnp.exp(-tf_ref[...] * dl_ref[...])
    hb = _dot3(wb_ref[...], z_ref[1]) * jnp.exp(-tb_ref[...] * dl_ref[...])
    lane = lax.broadcasted_iota(I32, hb.shape, 1)
    hb = jnp.where(lane == 0, 0.0, hb)
    norm = jnp.sum(jnp.abs(hf), axis=-1, keepdims=True) + jnp.sum(jnp.abs(hb), axis=-1, keepdims=True) + EPS
    hf = hf / norm
    hb = hb / norm
    nrow = hf.shape[0] // SUBLANES
    nch = seq // LANES
    for j in range(nch):
        o_ref[:, j, :, :] = hf[:, LANES * j:LANES * (j + 1)].reshape(nrow, SUBLANES, LANES)
        o_ref[:, nch + j, :, :] = hb[:, LANES * j:LANES * (j + 1)].reshape(nrow, SUBLANES, LANES)


def _filt_time(z2t, w3f, w3b, delta_rows, t_f, t_b):
    nrows = w3f.shape[0]
    seq = z2t.shape[2]
    rb = RB_FILT
    return pl.pallas_call(
        _filt_time_body,
        grid=(nrows // rb,),
        in_specs=[
            pl.BlockSpec((2, HY_FFN, seq), lambda r: (0, 0, 0)),
            pl.BlockSpec((rb, HY_FFN), lambda r: (r, 0)),
            pl.BlockSpec((rb, HY_FFN), lambda r: (r, 0)),
            pl.BlockSpec((rb, 1), lambda r: (r, 0)),
            pl.BlockSpec((1, seq), lambda r: (0, 0)),
            pl.BlockSpec((1, seq), lambda r: (0, 0)),
        ],
        out_specs=pl.BlockSpec((rb // SUBLANES, 2 * seq // LANES, SUBLANES, LANES), lambda r: (r, 0, 0, 0)),
        out_shape=jax.ShapeDtypeStruct((nrows // SUBLANES, 2 * seq // LANES, SUBLANES, LANES), F32),
        compiler_params=_params(("parallel",)),
        name="hyena_filter_time",
    )(z2t, w3f, w3b, delta_rows, t_f, t_b)


def _filt_fft_body(k_ref, f1_ref, f2_ref, tr_ref, ti_ref, kr_ref, ki_ref):
    tr = tr_ref[...]
    ti = ti_ref[...]
    for g in range(k_ref.shape[0]):
        x3 = jnp.swapaxes(k_ref[g], 0, 1)
        xr, xi = _fft_fwd(x3, f1_ref[...], f2_ref[...], tr, ti)
        kr_ref[SUBLANES * g:SUBLANES * (g + 1)] = xr.reshape(SUBLANES, FFT_R, FFT_R)
        ki_ref[SUBLANES * g:SUBLANES * (g + 1)] = xi.reshape(SUBLANES, FFT_R, FFT_R)


def _filt_fft(ktile, tables):
    n8, r, _, _ = ktile.shape
    nrows = n8 * SUBLANES
    rf = RF_KF
    out_spec = pl.BlockSpec((rf, r, r), lambda i: (i, 0, 0))
    consts = [tables[k] for k in ("f1", "f2", "tr16", "ti16")]
    return pl.pallas_call(
        _filt_fft_body,
        grid=(nrows // rf,),
        in_specs=[pl.BlockSpec((rf // SUBLANES, r, SUBLANES, LANES), lambda i: (i, 0, 0, 0))] + _const_specs(consts, 1),
        out_specs=[out_spec, out_spec],
        out_shape=[jax.ShapeDtypeStruct((nrows, r, r), F32)] * 2,
        compiler_params=_params(("parallel",)),
        name="hyena_filter_fft",
    )(ktile, *consts)


def _conv3_time(x, w):
    lane = lax.broadcasted_iota(I32, x.shape, 2)
    zero = jnp.zeros_like(x[:1])
    xm = pltpu.roll(x, 1, axis=2)
    xm = jnp.where(lane == 0, jnp.concatenate([zero, xm[:-1]], axis=0), xm)
    xp = pltpu.roll(x, LANES - 1, axis=2)
    xp = jnp.where(lane == LANES - 1, jnp.concatenate([xp[1:], zero], axis=0), xp)
    return w[0] * xm + w[1] * x + w[2] * xp + w[3]


def _hyena_body(sk_ref, v_ref, x1_ref, x2_ref, wv_ref, w1_ref, w2_ref, k0r_ref, k0i_ref, k1r_ref, k1i_ref,
                f1p_ref, f2_ref, f2c_ref, f3p_ref, tr_ref, ti_ref, o_ref, *, rc):
    ct = pl.program_id(0)
    tr = tr_ref[...]
    ti = ti_ref[...]
    np_ = v_ref.shape[2]
    nb = v_ref.shape[0]

    def load(ref, w_ref, g):
        parts = [jnp.swapaxes(_conv3_time(ref[bb, g], w_ref[:, g]), 0, 1) for bb in range(nb)]
        return jnp.concatenate(parts, axis=1)

    def long_conv(z3, kr_ref, ki_ref, g):
        r = FFT_R
        gsub = SUBLANES // HY_SPLIT
        wd = tr.dtype

        def split(m):
            return m[:, :r].astype(wd).reshape(gsub, r, r), m[:, r:].astype(wd).reshape(gsub, r, r)

        def join_t(re, im):
            return jnp.concatenate([jnp.swapaxes(re, 1, 2), jnp.swapaxes(im, 1, 2)], axis=-1).reshape(
                gsub * r, 2 * r).astype(BF16)

        subs = range(HY_SPLIT)
        zs = [z3[s * gsub:(s + 1) * gsub] for s in subs]
        a = [_dot(jnp.swapaxes(z.astype(wd), 1, 2).reshape(gsub * r, r).astype(BF16), f1p_ref[...]) for z in zs]
        p = []
        for m in a:
            ar, ai = split(m)
            p.append(join_t(ar * tr - ai * ti, ar * ti + ai * tr))
        x = [_dot(m, f2_ref[...]) for m in p]
        y = []
        for s, m in zip(subs, x):
            rows = pl.ds(g * SUBLANES + s * gsub, gsub)
            kr = kr_ref[rows].reshape(gsub * r, r)
            ki = ki_ref[rows].reshape(gsub * r, r)
            xr, xi = m[:, :r], m[:, r:]
            y.append(jnp.concatenate([xr * kr - xi * ki, xr * ki + xi * kr], axis=-1).astype(BF16))
        b = [_dot(m, f2c_ref[...]) for m in y]
        c = []
        for m in b:
            br, bi = split(m)
            c.append(join_t(br * tr + bi * ti, bi * tr - br * ti))
        yt = [_dot(m, f3p_ref[...]) for m in c]
        return jnp.concatenate([jnp.swapaxes(m.reshape(gsub, r, r), 1, 2) for m in yt], axis=0)

    def group(g, carry):
        cbase = ct * rc + g * SUBLANES

        def skip(z3, order):
            return jnp.stack([sk_ref[order, cbase + ci] * z3[ci] for ci in range(SUBLANES)], axis=0)

        v3 = load(v_ref, wv_ref, g)
        z1 = load(x1_ref, w1_ref, g) * (long_conv(v3, k0r_ref, k0i_ref, g) + skip(v3, 0))
        out = load(x2_ref, w2_ref, g) * (long_conv(z1, k1r_ref, k1i_ref, g) + skip(z1, 1))
        for bb in range(nb):
            o_ref[bb, g] = jnp.swapaxes(out[:, bb * np_:(bb + 1) * np_, :], 0, 1)
        return carry

    lax.fori_loop(0, rc // SUBLANES, group, 0)


def _hyena(hy5, cwt, skip, kfr, kfi, tables):
    bsz, n8, np_, _, _ = hy5.shape
    assert bsz % 2 == 0 and 2 * np_ == FFT_R
    rc = RC_HY
    r8 = rc // SUBLANES
    nct = D_HY // rc
    r = FFT_R

    def xspec(off):
        return pl.BlockSpec((2, r8, np_, SUBLANES, LANES), lambda c, b: (b, c + off * nct, 0, 0, 0))

    def wspec(off):
        return pl.BlockSpec((HY_CONV + 1, r8, SUBLANES, LANES), lambda c, b: (0, c + off * nct, 0, 0))

    def kspec(order):
        return pl.BlockSpec((rc, r, r), lambda c, b: (c + order * nct, 0, 0))

    consts = [tables[k] for k in ("f1p", "f2", "f2c", "f3p", "tr16", "ti16")]
    return pl.pallas_call(
        functools.partial(_hyena_body, rc=rc),
        grid=(nct, bsz // 2),
        in_specs=[pl.BlockSpec(memory_space=pltpu.SMEM), xspec(0), xspec(1), xspec(2), wspec(0), wspec(1), wspec(2),
                  kspec(0), kspec(0), kspec(1), kspec(1)] + _const_specs(consts, 2),
        out_specs=pl.BlockSpec((2, r8, np_, SUBLANES, LANES), lambda c, b: (b, c, 0, 0, 0)),
        out_shape=jax.ShapeDtypeStruct((bsz, D_HY // SUBLANES, np_, SUBLANES, LANES), F32),
        compiler_params=_params(("parallel", "parallel")),
        name="hyena",
    )(skip, hy5, hy5, hy5, cwt, cwt, cwt, kfr, kfi, kfr, kfi, *consts)


def _gelu_tanh(x):
    return 0.5 * x * (1.0 + jnp.tanh(math.sqrt(2.0 / math.pi) * (x + 0.044715 * (x * x * x))))


def _mix_body(hf_ref, hb_ref, gate_ref, yhy_ref, x_ref, gl_ref, gh_ref, wl_ref, wh_ref, gffn_ref,
              wrh_ref, wrl_ref, rb_ref, tri_ref, xmid_ref, h_ref, route_ref, cnt_ref):
    tm = x_ref.shape[1]
    y_lru = (hf_ref[0, 0] + hb_ref[0, 0]) * _gelu_tanh(gate_ref[0])
    nl = _rms(y_lru, gl_ref[...]).astype(BF16)
    gh = gh_ref[...]
    for j in range(tm // LANES):
        yt = yhy_ref[0, :, j, :, :].reshape(D_HY, LANES)
        nt = (yt * lax.rsqrt(jnp.mean(yt * yt, axis=0, keepdims=True) + EPS) * gh).astype(BF16)
        rows = slice(LANES * j, LANES * (j + 1))
        mix = _dot(nl[rows], wl_ref[...]) + lax.dot_general(
            nt, wh_ref[...], (((0,), (0,)), ((), ())), preferred_element_type=F32)
        xmid_ref[0, rows, :] = x_ref[0, rows, :] + mix
    h = _rms(xmid_ref[0], gffn_ref[...])
    h_ref[...] = h.astype(BF16)

    hh, hl = _split_bf16(h)
    logits = _dot(hh, wrh_ref[...]) + _dot(hl, wrh_ref[...]) + _dot(hh, wrl_ref[...]) + rb_ref[...]
    lane = lax.broadcasted_iota(I32, logits.shape, 1)
    big = jnp.int32(1 << 20)
    is_g = lane < N_GROUPS
    lg = jnp.where(is_g, logits, NEG_BIG)
    mg = jnp.max(lg, axis=-1, keepdims=True)
    gidx = jnp.min(jnp.where(lg == mg, lane, big), axis=-1, keepdims=True)
    g_p = 1.0 / jnp.sum(jnp.where(is_g, jnp.exp(lg - mg), 0.0), axis=-1, keepdims=True)
    e_lo = N_GROUPS + EXPERTS_PER_GROUP * gidx
    is_e = jnp.logical_and(lane >= e_lo, lane < e_lo + EXPERTS_PER_GROUP)
    le = jnp.where(is_e, logits, NEG_BIG)
    m1 = jnp.max(le, axis=-1, keepdims=True)
    i1 = jnp.min(jnp.where(le == m1, lane, big), axis=-1, keepdims=True)
    le2 = jnp.where(lane == i1, NEG_BIG, le)
    m2 = jnp.max(le2, axis=-1, keepdims=True)
    i2 = jnp.min(jnp.where(le2 == m2, lane, big), axis=-1, keepdims=True)
    ratio = jnp.exp(m2 - m1)
    gate1 = g_p / (1.0 + ratio)
    gate2 = g_p * ratio / (1.0 + ratio)
    oh1 = lane == i1
    oh2 = lane == i2
    oh = jnp.where(oh1, 1.0, jnp.where(oh2, 1.0, 0.0))
    tot = _dot(tri_ref[...], oh.astype(BF16))
    rank1 = jnp.sum(jnp.where(oh1, tot, 0.0), axis=-1, keepdims=True)
    rank2 = jnp.sum(jnp.where(oh2, tot, 0.0), axis=-1, keepdims=True)
    cnt_ref[0] = jnp.sum(oh, axis=0, keepdims=True)
    cols = [(i1 - N_GROUPS).astype(F32), (i2 - N_GROUPS).astype(F32), rank1, rank2, gate1, gate2]
    route = jnp.zeros(logits.shape, F32)
    for k, col in enumerate(cols):
        route = jnp.where(lane == k, col, route)
    route_ref[...] = route[:, :SUBLANES]


def _mix_router(h4, lru, yhy5, x, gl, gh, wl, wh, gffn, wrh, wrl, rb, tri):
    bsz, seq, d = x.shape
    tm = TM_MIX
    nt = seq // tm
    n_tok = bsz * seq
    c2 = lambda b, i: (0, 0)
    return pl.pallas_call(
        _mix_body,
        grid=(bsz, nt),
        in_specs=[
            pl.BlockSpec((1, 1, tm, D_LRU), lambda b, i: (0, b, i, 0)),
            pl.BlockSpec((1, 1, tm, D_LRU), lambda b, i: (1, b, i, 0)),
            pl.BlockSpec((1, tm, D_LRU), lambda b, i: (b, i, 1)),
            pl.BlockSpec((1, D_HY // SUBLANES, tm // LANES, SUBLANES, LANES), lambda b, i: (b, 0, i, 0, 0)),
            pl.BlockSpec((1, tm, d), lambda b, i: (b, i, 0)),
            pl.BlockSpec((1, D_LRU), c2),
            pl.BlockSpec((D_HY, 1), c2),
            pl.BlockSpec((D_LRU, d), c2),
            pl.BlockSpec((D_HY, d), c2),
            pl.BlockSpec((1, d), c2),
            pl.BlockSpec((d, LANES), c2),
            pl.BlockSpec((d, LANES), c2),
            pl.BlockSpec((1, LANES), c2),
            pl.BlockSpec((tm, tm), c2),
        ],
        out_specs=[
            pl.BlockSpec((1, tm, d), lambda b, i: (b, i, 0)),
            pl.BlockSpec((tm, d), lambda b, i: (b * nt + i, 0)),
            pl.BlockSpec((tm, SUBLANES), lambda b, i: (b * nt + i, 0)),
            pl.BlockSpec((1, 1, LANES), lambda b, i: (b * nt + i, 0, 0)),
        ],
        out_shape=[
            jax.ShapeDtypeStruct((bsz, seq, d), F32),
            jax.ShapeDtypeStruct((n_tok, d), BF16),
            jax.ShapeDtypeStruct((n_tok, SUBLANES), F32),
            jax.ShapeDtypeStruct((n_tok // tm, 1, LANES), F32),
        ],
        compiler_params=_params(("parallel", "parallel")),
        name="mix_router",
    )(h4, h4, lru, yhy5, x, gl, gh, wl, wh, gffn, wrh, wrl, rb, tri)


def _local_slots(route, lsl):
    lane = lax.broadcasted_iota(I32, (route.shape[0], LANES), 1)
    pos = []
    for k in range(2):
        start = jnp.sum(jnp.where(lane == route[:, k:k + 1].astype(I32), lsl, 0.0), axis=-1, keepdims=True)
        pos.append((start + route[:, 2 + k:3 + k]).astype(I32))
    return pos


def _start_chunks(i, tch_ref, dst_ref, make_copy):
    def per_chunk(k, c):
        make_copy(k * ROW_CHUNK, dst_ref[i * MAX_CHUNKS + k], ROW_CHUNK).start()
        return c

    lax.fori_loop(0, tch_ref[i], per_chunk, 0)


def _wait_chunks(count, make_copy):
    nbits = MAX_CHUNKS.bit_length()
    for j in reversed(range(nbits)):
        @pl.when((count >> j) & 1 == 1)
        def _():
            make_copy(0, 0, ROW_CHUNK << j).wait()


def _dispatch_body(dst_ref, tch_ref, tn_ref, ts_ref, nu_ref, h_ref, route_ref, lsl_ref, xb_ref,
                   buf, zbuf, sem, zsem):
    i = pl.program_id(0)
    n_tiles = pl.num_programs(0)
    tm = h_ref.shape[0]
    lbuf = buf.shape[1]
    n_blocks = xb_ref.shape[0] // MOE_BLK
    slot = i % 2

    def aligned(r):
        return pl.ds(pl.multiple_of(r, ROW_CHUNK), ROW_CHUNK)

    @pl.when(i == 0)
    def _():
        zbuf[...] = jnp.zeros_like(zbuf)

        def zero_chunk(r):
            return pltpu.make_async_copy(zbuf.at[pl.ds(0, ROW_CHUNK)], xb_ref.at[aligned(r)], zsem)

        def zero_block(j):
            return pltpu.make_async_copy(zbuf, xb_ref.at[pl.ds(pl.multiple_of(j * MOE_BLK, MOE_BLK), MOE_BLK)], zsem)

        def per_expert(e, total):
            def per_chunk(k, c):
                zero_chunk(ts_ref[e] + k * ROW_CHUNK).start()
                return c

            lax.fori_loop(0, tn_ref[e], per_chunk, 0)
            return total + tn_ref[e]

        total = lax.fori_loop(0, N_EXPERTS, per_expert, 0)

        def start_block(j, c):
            zero_block(j).start()
            return c

        lax.fori_loop(nu_ref[0], n_blocks, start_block, 0)

        def wait_chunk(k, c):
            zero_chunk(0).wait()
            return c

        lax.fori_loop(0, total, wait_chunk, 0)

        def wait_block(j, c):
            zero_block(0).wait()
            return c

        lax.fori_loop(nu_ref[0], n_blocks, wait_block, 0)

    l0, l1 = _local_slots(route_ref[...], lsl_ref[0])
    lane = lax.broadcasted_iota(I32, (tm, LANES), 1)
    slots_t = jnp.where(lane == 0, l0, jnp.where(lane == 1, l1, 0)).astype(F32).T
    l0t = slots_t[0:1].astype(I32)
    l1t = slots_t[1:2].astype(I32)
    row = lax.broadcasted_iota(I32, (lbuf, tm), 0)
    perm = jnp.where(row == l0t, 1.0, jnp.where(row == l1t, 1.0, 0.0)).astype(BF16)
    srt = _dot(perm, h_ref[...]).astype(BF16)

    def copy_from(s):
        def make_copy(src_row, dst_row, rows):
            return pltpu.make_async_copy(buf.at[s, pl.ds(pl.multiple_of(src_row, ROW_CHUNK), rows)],
                                         xb_ref.at[pl.ds(pl.multiple_of(dst_row, ROW_CHUNK), rows)], sem.at[s])
        return make_copy

    @pl.when(i >= 2)
    def _():
        _wait_chunks(tch_ref[i - 2], copy_from(slot))

    buf[slot] = srt
    _start_chunks(i, tch_ref, dst_ref, copy_from(slot))

    @pl.when(i == n_tiles - 1)
    def _():
        @pl.when(i >= 1)
        def _():
            _wait_chunks(tch_ref[i - 1], copy_from(1 - slot))

        _wait_chunks(tch_ref[i], copy_from(slot))


def _dispatch(chunk_dst, tile_nch, tail_n, tail_s, n_used, h, route, lsl, n_slots):
    n_tok, d = h.shape
    tm = TM_MIX
    grid_spec = pltpu.PrefetchScalarGridSpec(
        num_scalar_prefetch=5,
        grid=(n_tok // tm,),
        in_specs=[
            pl.BlockSpec((tm, d), lambda i, *_: (i, 0)),
            pl.BlockSpec((tm, SUBLANES), lambda i, *_: (i, 0)),
            pl.BlockSpec((1, 1, LANES), lambda i, *_: (i, 0, 0)),
        ],
        out_specs=pl.BlockSpec(memory_space=pl.ANY),
        scratch_shapes=[pltpu.VMEM((2, LOCAL_ROWS, d), BF16), pltpu.VMEM((MOE_BLK, d), BF16),
                        pltpu.SemaphoreType.DMA((2,)), pltpu.SemaphoreType.DMA(())],
    )
    return pl.pallas_call(
        _dispatch_body,
        grid_spec=grid_spec,
        out_shape=jax.ShapeDtypeStruct((n_slots, d), BF16),
        compiler_params=_params(("arbitrary",)),
        name="moe_dispatch",
    )(chunk_dst, tile_nch, tail_n, tail_s, n_used, h, route, lsl)


def _expert_body(be_ref, fl_ref, nu_ref, xb_ref, w1_ref, w3_ref, w2_ref, yb_ref, w1_sc, w3_sc, w2_sc):
    del be_ref
    j = pl.program_id(0)
    used = j < nu_ref[0]

    @pl.when(jnp.logical_and(used, fl_ref[j] == 1))
    def _():
        w1_sc[...] = w1_ref[...].astype(BF16)
        w3_sc[...] = w3_ref[...].astype(BF16)
        w2_sc[...] = w2_ref[...].astype(BF16)

    @pl.when(used)
    def _():
        xb = xb_ref[...]
        a = _dot(xb, w1_sc[...])
        b = _dot(xb, w3_sc[...])
        act = (a * _sigmoid(a) * b).astype(BF16)
        yb_ref[...] = _dot(act, w2_sc[...]).astype(BF16)

    @pl.when(jnp.logical_not(used))
    def _():
        yb_ref[...] = jnp.zeros_like(yb_ref)


def _experts(block_e, first_flag, n_used, xb, w1, w3, w2):
    n_slots, d = xb.shape
    blk = MOE_BLK
    de = w1.shape[2]
    grid_spec = pltpu.PrefetchScalarGridSpec(
        num_scalar_prefetch=3,
        grid=(n_slots // blk,),
        in_specs=[
            pl.BlockSpec((blk, d), lambda j, be, fl, nu: (j, 0)),
            pl.BlockSpec((None, d, de), lambda j, be, fl, nu: (be[j], 0, 0)),
            pl.BlockSpec((None, d, de), lambda j, be, fl, nu: (be[j], 0, 0)),
            pl.BlockSpec((None, de, d), lambda j, be, fl, nu: (be[j], 0, 0)),
        ],
        out_specs=pl.BlockSpec((blk, d), lambda j, be, fl, nu: (j, 0)),
        scratch_shapes=[pltpu.VMEM((d, de), BF16), pltpu.VMEM((d, de), BF16), pltpu.VMEM((de, d), BF16)],
    )
    return pl.pallas_call(
        _expert_body,
        grid_spec=grid_spec,
        out_shape=jax.ShapeDtypeStruct((n_slots, d), BF16),
        compiler_params=_params(("arbitrary",)),
        name="moe_experts",
    )(block_e, first_flag, n_used, xb, w1, w3, w2)


def _combine_body(dst_ref, tch_ref, xmid_ref, route_ref, lsl_ref, g_ref, yb_ref, o_ref, buf, sem):
    i = pl.program_id(0)
    n_tiles = pl.num_programs(0)
    tm = xmid_ref.shape[0]
    lbuf = buf.shape[1]
    slot = i % 2

    def copy_into(s):
        def make_copy(local_row, slot_row, rows):
            return pltpu.make_async_copy(yb_ref.at[pl.ds(pl.multiple_of(slot_row, ROW_CHUNK), rows)],
                                         buf.at[s, pl.ds(pl.multiple_of(local_row, ROW_CHUNK), rows)], sem.at[s])
        return make_copy

    @pl.when(i == 0)
    def _():
        buf[...] = jnp.zeros_like(buf)
        _start_chunks(i, tch_ref, dst_ref, copy_into(slot))

    @pl.when(i + 1 < n_tiles)
    def _():
        _start_chunks(i + 1, tch_ref, dst_ref, copy_into(1 - slot))

    route = route_ref[...]
    l0, l1 = _local_slots(route, lsl_ref[0])
    lane = lax.broadcasted_iota(I32, (tm, lbuf), 1)
    gmat = (jnp.where(lane == l0, route[:, 4:5], 0.0) + jnp.where(lane == l1, route[:, 5:6], 0.0)).astype(BF16)
    _wait_chunks(tch_ref[i], copy_into(slot))
    y = xmid_ref[...] + _dot(gmat, buf[slot])
    o_ref[...] = _rms(y, g_ref[...])


def _combine(chunk_dst, tile_nch, xmid, route, lsl, g, yb):
    n_tok, d = xmid.shape
    tm = TM_MIX
    grid_spec = pltpu.PrefetchScalarGridSpec(
        num_scalar_prefetch=2,
        grid=(n_tok // tm,),
        in_specs=[
            pl.BlockSpec((tm, d), lambda i, *_: (i, 0)),
            pl.BlockSpec((tm, SUBLANES), lambda i, *_: (i, 0)),
            pl.BlockSpec((1, 1, LANES), lambda i, *_: (i, 0, 0)),
            pl.BlockSpec((1, d), lambda i, *_: (0, 0)),
            pl.BlockSpec(memory_space=pl.ANY),
        ],
        out_specs=pl.BlockSpec((tm, d), lambda i, *_: (i, 0)),
        scratch_shapes=[pltpu.VMEM((2, LOCAL_ROWS, d), BF16), pltpu.SemaphoreType.DMA((2,))],
    )
    return pl.pallas_call(
        _combine_body,
        grid_spec=grid_spec,
        out_shape=jax.ShapeDtypeStruct((n_tok, d), F32),
        compiler_params=_params(("arbitrary",)),
        name="moe_combine",
    )(chunk_dst, tile_nch, xmid, route, lsl, g, yb)


def _filter_features(seq):
    pos = jnp.arange(seq, dtype=F32)
    t = jnp.linspace(0.0, 1.0, seq, dtype=F32)
    w = (2.0 * math.pi / seq) * pos
    bands = jnp.linspace(1e-4, HY_BANDS - 1, HY_BANDS, dtype=F32)
    ang = w[:, None] * bands[None, :]
    feats = jnp.concatenate([t[:, None], jnp.cos(ang), jnp.sin(ang)], axis=-1)
    rev = lambda a: jnp.concatenate([jnp.zeros_like(a[:1]), a[1:][::-1]], axis=0)
    pad = lambda a: jnp.pad(a, ((0, 0), (0, HY_EMB_PAD - HY_EMB)))
    featst = jnp.stack([pad(feats).T, pad(rev(feats)).T], axis=0)
    return featst, t[None, :], rev(t[:, None]).T


def _gate_weights(wr, wi):
    half = D_LRU // 2
    hph = half // LRU_HEAD_DIM
    eye = jnp.eye(hph, dtype=wr.dtype)[:, None, :, None]

    def block_diag(w):
        return (eye * w[:, :, None, :]).reshape(half, half)

    out = [jnp.concatenate([block_diag(wr[hh * hph:(hh + 1) * hph]), block_diag(wi[hh * hph:(hh + 1) * hph])], axis=1)
           for hh in range(2)]
    return jnp.stack(out, axis=0).astype(BF16)


def _layer(x, l, p):
    bsz, seq, d = x.shape
    n_tok = bsz * seq
    row = lambda a: a.reshape(1, -1).astype(F32)
    col = lambda a: a.reshape(-1, 1).astype(F32)

    w_in = p["w_in"][l]
    wl = w_in[:, :2 * D_LRU].astype(BF16)
    wht = w_in[:, 2 * D_LRU:].T.astype(BF16)
    lru, hy5 = _inproj(x, row(p["norm_mix_g"][l]), wl, wht)

    wg = jnp.stack([_gate_weights(p["lru_wr_f"][l], p["lru_wi_f"][l]),
                    _gate_weights(p["lru_wr_b"][l], p["lru_wi_b"][l])], axis=0)
    stack2 = lambda a, b: jnp.stack([a.reshape(1, -1), b.reshape(1, -1)], axis=0).astype(F32)
    h4 = _lru(lru, p["lru_conv_w"][l], row(p["lru_conv_b"][l]), wg,
              stack2(p["lru_br_f"][l], p["lru_br_b"][l]), stack2(p["lru_bi_f"][l], p["lru_bi_b"][l]),
              stack2(p["lru_lambda_f"][l], p["lru_lambda_b"][l]))

    tables = _fft_tables()
    featst, t_f, t_b = _filter_features(seq)
    w1t = jnp.pad(p["hy_filt_w1"][l].T, ((0, 0), (0, HY_EMB_PAD - HY_EMB)))
    z2t = _filt_mlp(featst, w1t, col(p["hy_filt_b1"][l]), col(p["hy_filt_freq1"][l]),
                    p["hy_filt_w2"][l].T, col(p["hy_filt_b2"][l]), col(p["hy_filt_freq2"][l]))
    w3 = p["hy_filt_w3"][l]
    nfr = HY_ORDER * D_HY
    deltas = jnp.abs(jnp.linspace(math.log(HY_DECAY_TARGET) / HY_SLOW_DECAY_PCT,
                                  math.log(HY_DECAY_TARGET) / HY_FAST_DECAY_PCT, D_HY, dtype=F32))
    ktile = _filt_time(z2t, w3[:, :nfr].T, w3[:, nfr:].T, col(jnp.tile(deltas, HY_ORDER)), t_f, t_b)
    kfr, kfi = _filt_fft(ktile, tables)
    cwt = jnp.concatenate([p["hy_conv_w"][l], p["hy_conv_b"][l][None, :]], axis=0)
    cwt = jnp.broadcast_to(cwt.reshape(HY_CONV + 1, -1, SUBLANES, 1), (HY_CONV + 1, cwt.shape[1] // SUBLANES, SUBLANES, LANES))
    yhy5 = _hyena(hy5, cwt, p["hy_skip"][l], kfr, kfi, tables)

    w_out = p["w_out"][l]
    wr_cat = jnp.concatenate([p["router_group_w"][l], p["router_expert_w"][l]], axis=1)
    wr_cat = jnp.pad(wr_cat, ((0, 0), (0, LANES - wr_cat.shape[1])))
    wrh, wrl = _split_bf16(wr_cat)
    rb = jnp.concatenate([p["router_group_b"][l], p["router_expert_b"][l]])
    rb = jnp.pad(rb, (0, LANES - rb.shape[0])).reshape(1, LANES)
    tri = (jnp.arange(TM_MIX)[:, None] > jnp.arange(TM_MIX)[None, :]).astype(BF16)
    xmid, h, route, tile_cnt = _mix_router(
        h4, lru, yhy5, x, row(p["grp_norm_lru_g"][l]), col(p["grp_norm_hy_g"][l]),
        w_out[:D_LRU].astype(BF16), w_out[D_LRU:].astype(BF16), row(p["norm_ffn_g"][l]), wrh, wrl, rb, tri)

    blk = MOE_BLK
    ch = ROW_CHUNK
    n_tiles = n_tok // TM_MIX
    n_blocks = (2 * n_tok + n_tiles * N_EXPERTS * (ch - 1)) // blk + N_EXPERTS
    cnt = tile_cnt[:, 0, N_GROUPS:N_GROUPS + N_EXPERTS].astype(I32)
    run = (cnt + ch - 1) // ch * ch
    tot_e = jnp.sum(run, axis=0)
    padded_e = (tot_e + blk - 1) // blk * blk
    pad_end = jnp.cumsum(padded_e)
    pad_start = pad_end - padded_e
    gstart = pad_start[None, :] + jnp.cumsum(run, axis=0) - run
    lstart = jnp.cumsum(run, axis=1) - run
    lsl = jnp.pad(lstart.astype(F32), ((0, 0), (0, LANES - N_EXPERTS))).reshape(n_tiles, 1, LANES)
    flat = lambda a: a.reshape(-1).astype(I32)
    tile_nch = flat(jnp.sum(run, axis=1) // ch)
    nch_end = jnp.cumsum(run // ch, axis=1)
    k = jnp.arange(MAX_CHUNKS, dtype=I32)
    base = gstart - lstart
    step = base[:, 1:] - base[:, :-1]
    passed = (k[None, :, None] >= nch_end[:, None, :-1]).astype(I32)
    chunk_dst = flat(base[:, :1] + jnp.sum(passed * step[:, None, :], axis=2) + k[None, :] * ch)
    tail_n = flat((padded_e - tot_e) // ch)
    tail_s = flat(pad_start + tot_e)
    block_start = jnp.arange(n_blocks, dtype=I32) * blk
    block_e = jnp.minimum(jnp.sum(block_start[:, None] >= pad_end[None, :], axis=1), N_EXPERTS - 1).astype(I32)
    first_flag = jnp.concatenate([jnp.ones((1,), I32), (block_e[1:] != block_e[:-1]).astype(I32)])
    n_used = (pad_end[-1:] // blk).astype(I32)

    xb = _dispatch(chunk_dst, tile_nch, tail_n, tail_s, n_used, h, route, lsl, n_blocks * blk)
    yb = _experts(block_e, first_flag, n_used, xb, p["exp_w1"][l], p["exp_w3"][l], p["exp_w2"][l])
    return xmid.reshape(n_tok, d), route, (chunk_dst, tile_nch, lsl), yb


def kernel(x, norm_mix_g, w_in, lru_conv_w, lru_conv_b, lru_wr_f, lru_br_f, lru_wi_f, lru_bi_f, lru_lambda_f, lru_wr_b, lru_br_b, lru_wi_b, lru_bi_b, lru_lambda_b, hy_conv_w, hy_conv_b, hy_filt_w1, hy_filt_b1, hy_filt_freq1, hy_filt_w2, hy_filt_b2, hy_filt_freq2, hy_filt_w3, hy_skip, grp_norm_lru_g, grp_norm_hy_g, w_out, norm_ffn_g, router_group_w, router_group_b, router_expert_w, router_expert_b, exp_w1, exp_w3, exp_w2, norm_final_g):
    p = dict(locals())
    bsz, seq, d = x.shape
    assert d == D_MODEL and 2 * seq == FFT_R * FFT_R and w_in.shape[0] == 1
    xmid, route, (chunk_dst, tile_nch, lsl), yb = _layer(x, 0, p)
    out = _combine(chunk_dst, tile_nch, xmid, route, lsl, norm_final_g.reshape(1, d), yb)
    return out.reshape(bsz, seq, d)
```

```python
import functools
import math

import numpy as np
import jax
import jax.numpy as jnp
from jax import lax
from jax.experimental import pallas as pl
from jax.experimental.pallas import tpu as pltpu

F32 = jnp.float32
BF16 = jnp.bfloat16
I32 = jnp.int32

D_MODEL = 1024
D_LRU = 512
D_HY = 512
LRU_HEADS = 8
LRU_HEAD_DIM = D_LRU // LRU_HEADS
LRU_CONV = 4
LRU_C = 8.0
HY_ORDER = 2
HY_CONV = 3
HY_BANDS = 16
HY_EMB = 2 * HY_BANDS + 1
HY_EMB_PAD = 40
HY_FFN = 64
HY_FAST_DECAY_PCT = 0.3
HY_SLOW_DECAY_PCT = 1.5
HY_DECAY_TARGET = 1e-2
N_GROUPS = 4
EXPERTS_PER_GROUP = 8
N_EXPERTS = N_GROUPS * EXPERTS_PER_GROUP
D_EXPERT = D_MODEL // 2
EPS = 1e-6

LANES = 128
SUBLANES = 8
FFT_R = 128
VMEM_LIMIT = 56 * 1024 * 1024

TM_IN = 512
TC_LRU = 512
RC_HY = 32
RF_KF = 16
HY_SPLIT = 2
RB_FILT = 64
TM_MIX = 512
MOE_BLK = 512
ROW_CHUNK = 16
LOCAL_ROWS = -(-(2 * TM_MIX + N_EXPERTS * (ROW_CHUNK - 1)) // LANES) * LANES
MAX_CHUNKS = LOCAL_ROWS // ROW_CHUNK
ROUTE_ROWS = -(-(N_GROUPS + N_EXPERTS) // SUBLANES) * SUBLANES
NEG_BIG = -1e30


def _params(sem, vmem=VMEM_LIMIT):
    return pltpu.CompilerParams(dimension_semantics=sem, vmem_limit_bytes=vmem)


def _rms(x, g):
    return x * lax.rsqrt(jnp.mean(x * x, axis=-1, keepdims=True) + EPS) * g


def _sigmoid(x):
    return 1.0 / (1.0 + jnp.exp(-x))


def _split_bf16(a):
    hi = a.astype(BF16)
    lo = (a - hi.astype(F32)).astype(BF16)
    return hi, lo


def _dot(a, b):
    return jnp.dot(a, b, preferred_element_type=F32)


def _dot3(a, b):
    ah, al = _split_bf16(a)
    bh, bl = _split_bf16(b)
    return _dot(ah, bh) + _dot(al, bh) + _dot(ah, bl)


def _inproj_body(x_ref, g_ref, wl_ref, wht_ref, lru_ref, hy_ref):
    hn = _rms(x_ref[0], g_ref[...]).astype(BF16)
    lru_ref[0] = _dot(hn, wl_ref[...])
    hyt = lax.dot_general(wht_ref[...], hn, (((1,), (1,)), ((), ())), preferred_element_type=F32)
    nrow = hyt.shape[0] // SUBLANES
    for j in range(hyt.shape[1] // LANES):
        hy_ref[0, :, j, :, :] = hyt[:, LANES * j:LANES * (j + 1)].reshape(nrow, SUBLANES, LANES)


def _inproj(x, g, wl, wht):
    bsz, seq, d = x.shape
    nl = wl.shape[1]
    nh = wht.shape[0]
    tm = TM_IN
    return pl.pallas_call(
        _inproj_body,
        grid=(bsz, seq // tm),
        in_specs=[
            pl.BlockSpec((1, tm, d), lambda b, i: (b, i, 0)),
            pl.BlockSpec((1, d), lambda b, i: (0, 0)),
            pl.BlockSpec((d, nl), lambda b, i: (0, 0)),
            pl.BlockSpec((nh, d), lambda b, i: (0, 0)),
        ],
        out_specs=[
            pl.BlockSpec((1, tm, nl), lambda b, i: (b, i, 0)),
            pl.BlockSpec((1, nh // SUBLANES, tm // LANES, SUBLANES, LANES), lambda b, i: (b, 0, i, 0, 0)),
        ],
        out_shape=[
            jax.ShapeDtypeStruct((bsz, seq, nl), F32),
            jax.ShapeDtypeStruct((bsz, nh // SUBLANES, seq // LANES, SUBLANES, LANES), F32),
        ],
        compiler_params=_params(("parallel", "parallel")),
        name="inproj",
    )(x, g, wl, wht)


def _sqrt_one_minus_sq(a, neg_log_a):
    om = jnp.maximum(jnp.tanh(neg_log_a) * (1.0 + a * a), 1e-30)
    return om * lax.rsqrt(om)


def _group_scan(a3, b3, reverse):
    sub = lax.broadcasted_iota(I32, a3.shape, 1)
    for s in (1, 2, 4):
        if reverse:
            a_sh = pltpu.roll(a3, SUBLANES - s, axis=1)
            b_sh = pltpu.roll(b3, SUBLANES - s, axis=1)
            m = sub < SUBLANES - s
        else:
            a_sh = pltpu.roll(a3, s, axis=1)
            b_sh = pltpu.roll(b3, s, axis=1)
            m = sub >= s
        b3 = jnp.where(m, a3 * b_sh + b3, b3)
        a3 = jnp.where(m, a3 * a_sh, a3)
    return a3, b3


def _lru_body(cur_ref, prev_ref, next_ref, cw_ref, cb_ref, wg_ref, br_ref, bi_ref, lam_ref, o_ref,
              carry_sc, *, nt, tc):
    d = pl.program_id(1)
    i = pl.program_id(2)
    c = jnp.where(d == 0, i, nt - 1 - i)
    ns = SUBLANES
    ng = tc // ns
    half = D_LRU // 2

    @pl.when(i == 0)
    def _():
        carry_sc[...] = jnp.zeros_like(carry_sc)

    xs = jnp.swapaxes(cur_ref[0].reshape(ns, ng, D_LRU), 0, 1)
    prev = jnp.where(c > 0, prev_ref[0], 0.0)
    nxt = jnp.where(c < nt - 1, next_ref[0], 0.0)
    sub = lax.broadcasted_iota(I32, (ns, D_LRU), 0)

    def prev_segment(slab, halo_row):
        return jnp.where(sub == 0, halo_row, pltpu.roll(slab, 1, axis=0))

    def next_segment(slab, halo_row):
        return jnp.where(sub == ns - 1, halo_row, pltpu.roll(slab, ns - 1, axis=0))

    xext = jnp.concatenate([prev_segment(xs[ng - 2], prev[ns - 2:ns - 1])[None],
                            prev_segment(xs[ng - 1], prev[ns - 1:ns])[None], xs,
                            next_segment(xs[0], nxt[0:1])[None]], axis=0)
    cw = cw_ref[...]
    xc = cb_ref[...] + cw[0:1] * xext[0:ng]
    for k in range(1, LRU_CONV):
        xc = xc + cw[k:k + 1] * xext[k:k + ng]
    xc = xc.reshape(tc, D_LRU)

    lam = lam_ref[0]
    nlam = -lam
    softplus = jnp.maximum(nlam, 0.0) + jnp.log1p(jnp.exp(-jnp.abs(nlam)))
    half_c = (0.5 * LRU_C) * softplus

    def gates(hh):
        sl = slice(half * hh, half * (hh + 1))
        xh = xc[:, sl]
        logits = _dot(xh.astype(BF16), wg_ref[0, hh])
        tr_ = jnp.tanh(0.5 * (logits[:, :half] + br_ref[0][:, sl]))
        gi = 0.5 + 0.5 * jnp.tanh(0.5 * (logits[:, half:] + bi_ref[0][:, sl]))
        neg_log_a = half_c[:, sl] * (1.0 + tr_)
        a = jnp.exp(-neg_log_a)
        b = _sqrt_one_minus_sq(a, neg_log_a) * (gi * xh)
        return a.reshape(ng, ns, half), b.reshape(ng, ns, half)

    def run(reverse):
        subh = lax.broadcasted_iota(I32, (ns, half), 0)
        for hh in range(2):
            sl = slice(half * hh, half * (hh + 1))
            a3, b3 = gates(hh)
            h = jnp.zeros((ns, half), F32)
            p = jnp.ones((ns, half), F32)
            hs = [None] * ng
            ps = [None] * ng
            for g in (range(ng - 1, -1, -1) if reverse else range(ng)):
                h = a3[g] * h + b3[g]
                p = a3[g] * p
                hs[g] = h
                ps[g] = p
            ac, hc = _group_scan(p[None], h[None], reverse)
            c_in = carry_sc[:, sl]
            end = ac[0] * c_in + hc[0]
            if reverse:
                seg_in = jnp.where(subh == ns - 1, c_in, pltpu.roll(end, ns - 1, axis=0))
                carry_sc[:, sl] = jnp.broadcast_to(end[0:1], end.shape)
            else:
                seg_in = jnp.where(subh == 0, c_in, pltpu.roll(end, 1, axis=0))
                carry_sc[:, sl] = jnp.broadcast_to(end[ns - 1:ns], end.shape)
            h3 = jnp.stack(hs, axis=0) + jnp.stack(ps, axis=0) * seg_in[None]
            o_ref[0, 0, :, sl] = jnp.swapaxes(h3, 0, 1).reshape(tc, half)

    @pl.when(d == 0)
    def _():
        run(False)

    @pl.when(d == 1)
    def _():
        run(True)


def _lru(lru, cw, cb, wg, br, bi, lam):
    bsz, seq, _ = lru.shape
    tc = TC_LRU
    nt = seq // tc
    r8 = tc // SUBLANES
    nrow8 = seq // SUBLANES

    def cidx(d, i):
        return jnp.where(d == 0, i, nt - 1 - i)

    return pl.pallas_call(
        functools.partial(_lru_body, nt=nt, tc=tc),
        grid=(bsz, 2, nt),
        in_specs=[
            pl.BlockSpec((1, tc, D_LRU), lambda b, d, i: (b, cidx(d, i), 0)),
            pl.BlockSpec((1, SUBLANES, D_LRU), lambda b, d, i: (b, jnp.maximum(cidx(d, i) * r8 - 1, 0), 0)),
            pl.BlockSpec((1, SUBLANES, D_LRU), lambda b, d, i: (b, jnp.minimum((cidx(d, i) + 1) * r8, nrow8 - 1), 0)),
            pl.BlockSpec((LRU_CONV, D_LRU), lambda b, d, i: (0, 0)),
            pl.BlockSpec((1, D_LRU), lambda b, d, i: (0, 0)),
            pl.BlockSpec((1, 2, D_LRU // 2, D_LRU), lambda b, d, i: (d, 0, 0, 0)),
            pl.BlockSpec((1, 1, D_LRU), lambda b, d, i: (d, 0, 0)),
            pl.BlockSpec((1, 1, D_LRU), lambda b, d, i: (d, 0, 0)),
            pl.BlockSpec((1, 1, D_LRU), lambda b, d, i: (d, 0, 0)),
        ],
        out_specs=pl.BlockSpec((1, 1, tc, D_LRU), lambda b, d, i: (d, b, cidx(d, i), 0)),
        out_shape=jax.ShapeDtypeStruct((2, bsz, seq, D_LRU), F32),
        scratch_shapes=[pltpu.VMEM((SUBLANES, D_LRU), F32)],
        compiler_params=_params(("parallel", "arbitrary", "arbitrary")),
        name="rg_lru",
    )(lru, lru, lru, cw, cb, wg, br, bi, lam)


def _fft_tables():
    r = FFT_R
    n = r * r
    idx = np.arange(r, dtype=np.float64)
    ang = 2.0 * np.pi * np.outer(idx, idx) / r
    wr, wi = np.cos(ang), -np.sin(ang)
    angt = 2.0 * np.pi * np.outer(idx, idx) / n
    tr, ti = np.cos(angt), -np.sin(angt)
    h = r // 2
    f1 = np.concatenate([wr, wi], axis=1)
    f1p = np.block([[wr[:h], wi[:h]], [-wi[:h], wr[:h]]])
    f2 = np.block([[wr, wi], [-wi, wr]])
    f2c = np.block([[wr, -wi], [wi, wr]])
    f3p = np.block([[wr[:, :h], -wi[:, :h]], [wi[:, :h], wr[:, :h]]]) / n
    as_bf16 = lambda a: jnp.asarray(a, F32).astype(BF16)
    return dict(f1=as_bf16(f1), f1p=as_bf16(f1p), f2=as_bf16(f2), f2c=as_bf16(f2c), f3p=as_bf16(f3p),
                tr=jnp.asarray(tr, F32), ti=jnp.asarray(ti, F32), tr16=as_bf16(tr), ti16=as_bf16(ti))


def _const_specs(arrays, nargs):
    return [pl.BlockSpec(a.shape, (lambda nd: (lambda *_: (0,) * nd))(a.ndim)) for a in arrays]


def _fft_fwd(x3, f1, f2, tr, ti):
    g = x3.shape[0]
    r = FFT_R
    wd = tr.dtype
    xt = jnp.swapaxes(x3.astype(wd), 1, 2).reshape(g * r, r)
    a = _dot(xt.astype(BF16), f1)
    ar = a[:, :r].astype(wd).reshape(g, r, r)
    ai = a[:, r:].astype(wd).reshape(g, r, r)
    pr = ar * tr - ai * ti
    pi = ar * ti + ai * tr
    pt = jnp.concatenate([jnp.swapaxes(pr, 1, 2), jnp.swapaxes(pi, 1, 2)], axis=-1).reshape(g * r, 2 * r)
    x = _dot(pt.astype(BF16), f2)
    return x[:, :r], x[:, r:]


def _fft_inv(yr, yi, f2c, f3, tr, ti):
    r = FFT_R
    g = yr.shape[0] // r
    wd = tr.dtype
    b = _dot(jnp.concatenate([yr, yi], axis=-1).astype(BF16), f2c)
    br = b[:, :r].astype(wd).reshape(g, r, r)
    bi = b[:, r:].astype(wd).reshape(g, r, r)
    cr = br * tr + bi * ti
    ci = bi * tr - br * ti
    ct = jnp.concatenate([jnp.swapaxes(cr, 1, 2), jnp.swapaxes(ci, 1, 2)], axis=-1).reshape(g * r, 2 * r)
    yt = _dot(ct.astype(BF16), f3)
    return jnp.swapaxes(yt.reshape(g, r, r), 1, 2)


def _filt_mlp_body(ft_ref, w1_ref, b1_ref, f1_ref, w2_ref, b2_ref, f2_ref, o_ref):
    z = jnp.sin(f1_ref[...] * (_dot3(w1_ref[...], ft_ref[0]) + b1_ref[...]))
    o_ref[0] = jnp.sin(f2_ref[...] * (_dot3(w2_ref[...], z) + b2_ref[...]))


def _filt_mlp(featst, w1t, b1, f1, w2t, b2, f2):
    _, ke, seq = featst.shape
    col = lambda d: (0, 0)
    return pl.pallas_call(
        _filt_mlp_body,
        grid=(2,),
        in_specs=[
            pl.BlockSpec((1, ke, seq), lambda d: (d, 0, 0)),
            pl.BlockSpec((HY_FFN, ke), col),
            pl.BlockSpec((HY_FFN, 1), col),
            pl.BlockSpec((HY_FFN, 1), col),
            pl.BlockSpec((HY_FFN, HY_FFN), col),
            pl.BlockSpec((HY_FFN, 1), col),
            pl.BlockSpec((HY_FFN, 1), col),
        ],
        out_specs=pl.BlockSpec((1, HY_FFN, seq), lambda d: (d, 0, 0)),
        out_shape=jax.ShapeDtypeStruct((2, HY_FFN, seq), F32),
        compiler_params=_params(("parallel",)),
        name="hyena_filter_mlp",
    )(featst, w1t, b1, f1, w2t, b2, f2)


def _filt_time_body(z_ref, wf_ref, wb_ref, dl_ref, tf_ref, tb_ref, o_ref):
    seq = z_ref.shape[2]
    hf = _dot3(wf_ref[...], z_ref[0]) * jnp.exp(-tf_ref[...] * dl_ref[...])
    hb = _dot3(wb_ref[...], z_ref[1]) * jnp.exp(-tb_ref[...] * dl_ref[...])
    lane = lax.broadcasted_iota(I32, hb.shape, 1)
    hb = jnp.where(lane == 0, 0.0, hb)
    norm = jnp.sum(jnp.abs(hf), axis=-1, keepdims=True) + jnp.sum(jnp.abs(hb), axis=-1, keepdims=True) + EPS
    hf = hf / norm
    hb = hb / norm
    nrow = hf.shape[0] // SUBLANES
    nch = seq // LANES
    for j in range(nch):
        o_ref[:, j, :, :] = hf[:, LANES * j:LANES * (j + 1)].reshape(nrow, SUBLANES, LANES)
        o_ref[:, nch + j, :, :] = hb[:, LANES * j:LANES * (j + 1)].reshape(nrow, SUBLANES, LANES)


def _filt_time(z2t, w3f, w3b, delta_rows, t_f, t_b):
    nrows = w3f.shape[0]
    seq = z2t.shape[2]
    rb = RB_FILT
    return pl.pallas_call(
        _filt_time_body,
        grid=(nrows // rb,),
        in_specs=[
            pl.BlockSpec((2, HY_FFN, seq), lambda r: (0, 0, 0)),
            pl.BlockSpec((rb, HY_FFN), lambda r: (r, 0)),
            pl.BlockSpec((rb, HY_FFN), lambda r: (r, 0)),
            pl.BlockSpec((rb, 1), lambda r: (r, 0)),
            pl.BlockSpec((1, seq), lambda r: (0, 0)),
            pl.BlockSpec((1, seq), lambda r: (0, 0)),
        ],
        out_specs=pl.BlockSpec((rb // SUBLANES, 2 * seq // LANES, SUBLANES, LANES), lambda r: (r, 0, 0, 0)),
        out_shape=jax.ShapeDtypeStruct((nrows // SUBLANES, 2 * seq // LANES, SUBLANES, LANES), F32),
        compiler_params=_params(("parallel",)),
        name="hyena_filter_time",
    )(z2t, w3f, w3b, delta_rows, t_f, t_b)


def _filt_fft_body(k_ref, f1_ref, f2_ref, tr_ref, ti_ref, kr_ref, ki_ref):
    tr = tr_ref[...]
    ti = ti_ref[...]
    for g in range(k_ref.shape[0]):
        x3 = jnp.swapaxes(k_ref[g], 0, 1)
        xr, xi = _fft_fwd(x3, f1_ref[...], f2_ref[...], tr, ti)
        kr_ref[SUBLANES * g:SUBLANES * (g + 1)] = xr.reshape(SUBLANES, FFT_R, FFT_R)
        ki_ref[SUBLANES * g:SUBLANES * (g + 1)] = xi.reshape(SUBLANES, FFT_R, FFT_R)


def _filt_fft(ktile, tables):
    n8, r, _, _ = ktile.shape
    nrows = n8 * SUBLANES
    rf = RF_KF
    out_spec = pl.BlockSpec((rf, r, r), lambda i: (i, 0, 0))
    consts = [tables[k] for k in ("f1", "f2", "tr16", "ti16")]
    return pl.pallas_call(
        _filt_fft_body,
        grid=(nrows // rf,),
        in_specs=[pl.BlockSpec((rf // SUBLANES, r, SUBLANES, LANES), lambda i: (i, 0, 0, 0))] + _const_specs(consts, 1),
        out_specs=[out_spec, out_spec],
        out_shape=[jax.ShapeDtypeStruct((nrows, r, r), F32)] * 2,
        compiler_params=_params(("parallel",)),
        name="hyena_filter_fft",
    )(ktile, *consts)


def _conv3_time(x, w):
    lane = lax.broadcasted_iota(I32, x.shape, 2)
    zero = jnp.zeros_like(x[:1])
    xm = pltpu.roll(x, 1, axis=2)
    xm = jnp.where(lane == 0, jnp.concatenate([zero, xm[:-1]], axis=0), xm)
    xp = pltpu.roll(x, LANES - 1, axis=2)
    xp = jnp.where(lane == LANES - 1, jnp.concatenate([xp[1:], zero], axis=0), xp)
    return w[0] * xm + w[1] * x + w[2] * xp + w[3]


def _hyena_body(sk_ref, v_ref, x1_ref, x2_ref, wv_ref, w1_ref, w2_ref, k0r_ref, k0i_ref, k1r_ref, k1i_ref,
                f1p_ref, f2_ref, f2c_ref, f3p_ref, tr_ref, ti_ref, o_ref, *, rc):
    ct = pl.program_id(0)
    tr = tr_ref[...]
    ti = ti_ref[...]
    np_ = v_ref.shape[2]
    nb = v_ref.shape[0]

    def load(ref, w_ref, g):
        parts = [jnp.swapaxes(_conv3_time(ref[bb, g], w_ref[:, g]), 0, 1) for bb in range(nb)]
        return jnp.concatenate(parts, axis=1)

    def long_conv(z3, kr_ref, ki_ref, g):
        r = FFT_R
        gsub = SUBLANES // HY_SPLIT
        wd = tr.dtype

        def split(m):
            return m[:, :r].astype(wd).reshape(gsub, r, r), m[:, r:].astype(wd).reshape(gsub, r, r)

        def join_t(re, im):
            return jnp.concatenate([jnp.swapaxes(re, 1, 2), jnp.swapaxes(im, 1, 2)], axis=-1).reshape(
                gsub * r, 2 * r).astype(BF16)

        subs = range(HY_SPLIT)
        zs = [z3[s * gsub:(s + 1) * gsub] for s in subs]
        a = [_dot(jnp.swapaxes(z.astype(wd), 1, 2).reshape(gsub * r, r).astype(BF16), f1p_ref[...]) for z in zs]
        p = []
        for m in a:
            ar, ai = split(m)
            p.append(join_t(ar * tr - ai * ti, ar * ti + ai * tr))
        x = [_dot(m, f2_ref[...]) for m in p]
        y = []
        for s, m in zip(subs, x):
            rows = pl.ds(g * SUBLANES + s * gsub, gsub)
            kr = kr_ref[rows].reshape(gsub * r, r)
            ki = ki_ref[rows].reshape(gsub * r, r)
            xr, xi = m[:, :r], m[:, r:]
            y.append(jnp.concatenate([xr * kr - xi * ki, xr * ki + xi * kr], axis=-1).astype(BF16))
        b = [_dot(m, f2c_ref[...]) for m in y]
        c = []
        for m in b:
            br, bi = split(m)
            c.append(join_t(br * tr + bi * ti, bi * tr - br * ti))
        yt = [_dot(m, f3p_ref[...]) for m in c]
        return jnp.concatenate([jnp.swapaxes(m.reshape(gsub, r, r), 1, 2) for m in yt], axis=0)

    def group(g, carry):
        cbase = ct * rc + g * SUBLANES

        def skip(z3, order):
            return jnp.stack([sk_ref[order, cbase + ci] * z3[ci] for ci in range(SUBLANES)], axis=0)

        v3 = load(v_ref, wv_ref, g)
        z1 = load(x1_ref, w1_ref, g) * (long_conv(v3, k0r_ref, k0i_ref, g) + skip(v3, 0))
        out = load(x2_ref, w2_ref, g) * (long_conv(z1, k1r_ref, k1i_ref, g) + skip(z1, 1))
        for bb in range(nb):
            o_ref[bb, g] = jnp.swapaxes(out[:, bb * np_:(bb + 1) * np_, :], 0, 1)
        return carry

    lax.fori_loop(0, rc // SUBLANES, group, 0)


def _hyena(hy5, cwt, skip, kfr, kfi, tables):
    bsz, n8, np_, _, _ = hy5.shape
    assert bsz % 2 == 0 and 2 * np_ == FFT_R
    rc = RC_HY
    r8 = rc // SUBLANES
    nct = D_HY // rc
    r = FFT_R

    def xspec(off):
        return pl.BlockSpec((2, r8, np_, SUBLANES, LANES), lambda c, b: (b, c + off * nct, 0, 0, 0))

    def wspec(off):
        return pl.BlockSpec((HY_CONV + 1, r8, SUBLANES, LANES), lambda c, b: (0, c + off * nct, 0, 0))

    def kspec(order):
        return pl.BlockSpec((rc, r, r), lambda c, b: (c + order * nct, 0, 0))

    consts = [tables[k] for k in ("f1p", "f2", "f2c", "f3p", "tr16", "ti16")]
    return pl.pallas_call(
        functools.partial(_hyena_body, rc=rc),
        grid=(nct, bsz // 2),
        in_specs=[pl.BlockSpec(memory_space=pltpu.SMEM), xspec(0), xspec(1), xspec(2), wspec(0), wspec(1), wspec(2),
                  kspec(0), kspec(0), kspec(1), kspec(1)] + _const_specs(consts, 2),
        out_specs=pl.BlockSpec((2, r8, np_, SUBLANES, LANES), lambda c, b: (b, c, 0, 0, 0)),
        out_shape=jax.ShapeDtypeStruct((bsz, D_HY // SUBLANES, np_, SUBLANES, LANES), F32),
        compiler_params=_params(("parallel", "parallel")),
        name="hyena",
    )(skip, hy5, hy5, hy5, cwt, cwt, cwt, kfr, kfi, kfr, kfi, *consts)


def _gelu_tanh(x):
    return 0.5 * x * (1.0 + jnp.tanh(math.sqrt(2.0 / math.pi) * (x + 0.044715 * (x * x * x))))


def _mix_body(hf_ref, hb_ref, gate_ref, yhy_ref, x_ref, gl_ref, gh_ref, wl_ref, wh_ref, gffn_ref,
              wrh_ref, wrl_ref, rb_ref, tri_ref, ltri_ref, xmid_ref, h_ref, routec_ref, routet_ref, cnt_ref):
    tm = x_ref.shape[1]
    y_lru = (hf_ref[0, 0] + hb_ref[0, 0]) * _gelu_tanh(gate_ref[0])
    nl = _rms(y_lru, gl_ref[...]).astype(BF16)
    gh = gh_ref[...]
    for j in range(tm // LANES):
        yt = yhy_ref[0, :, j, :, :].reshape(D_HY, LANES)
        nt = (yt * lax.rsqrt(jnp.mean(yt * yt, axis=0, keepdims=True) + EPS) * gh).astype(BF16)
        rows = slice(LANES * j, LANES * (j + 1))
        mix = _dot(nl[rows], wl_ref[...]) + lax.dot_general(
            nt, wh_ref[...], (((0,), (0,)), ((), ())), preferred_element_type=F32)
        xmid_ref[0, rows, :] = x_ref[0, rows, :] + mix
    h = _rms(xmid_ref[0], gffn_ref[...])
    h_ref[...] = h.astype(BF16)

    hh, hl = _split_bf16(h)
    logits = _dot(hh, wrh_ref[...]) + _dot(hl, wrh_ref[...]) + _dot(hh, wrl_ref[...]) + rb_ref[...]
    lt = logits.T[:ROUTE_ROWS]
    row = lax.broadcasted_iota(I32, lt.shape, 0)
    big = jnp.int32(1 << 20)
    is_g = row < N_GROUPS
    lg = jnp.where(is_g, lt, NEG_BIG)
    mg = jnp.max(lg, axis=0, keepdims=True)
    gidx = jnp.min(jnp.where(lg == mg, row, big), axis=0, keepdims=True)
    g_p = 1.0 / jnp.sum(jnp.where(is_g, jnp.exp(lg - mg), 0.0), axis=0, keepdims=True)
    e_lo = N_GROUPS + EXPERTS_PER_GROUP * gidx
    is_e = jnp.logical_and(row >= e_lo, row < e_lo + EXPERTS_PER_GROUP)
    le = jnp.where(is_e, lt, NEG_BIG)
    m1 = jnp.max(le, axis=0, keepdims=True)
    i1 = jnp.min(jnp.where(le == m1, row, big), axis=0, keepdims=True)
    le2 = jnp.where(row == i1, NEG_BIG, le)
    m2 = jnp.max(le2, axis=0, keepdims=True)
    i2 = jnp.min(jnp.where(le2 == m2, row, big), axis=0, keepdims=True)
    ratio = jnp.exp(m2 - m1)
    gate1 = g_p / (1.0 + ratio)
    gate2 = g_p * ratio / (1.0 + ratio)
    oh1 = row == i1
    oh2 = row == i2
    oh = jnp.where(oh1, 1.0, jnp.where(oh2, 1.0, 0.0))
    before = _dot(oh.astype(BF16), tri_ref[...])
    cnt = jnp.sum(oh, axis=1, keepdims=True)
    nchunk = jnp.floor((cnt + (ROW_CHUNK - 1)) * (1.0 / ROW_CHUNK))
    run_start = ROW_CHUNK * _dot(ltri_ref[...], jnp.broadcast_to(nchunk, (ROUTE_ROWS, LANES)).astype(BF16))[:, 0:1]
    pos = run_start + before
    l0 = jnp.sum(jnp.where(oh1, pos, 0.0), axis=0, keepdims=True)
    l1 = jnp.sum(jnp.where(oh2, pos, 0.0), axis=0, keepdims=True)
    cnt_ref[0] = cnt
    route_t = jnp.concatenate([l0, l1, gate1, gate2, jnp.zeros((SUBLANES - 4, tm), F32)], axis=0)
    routet_ref[0] = route_t
    routec_ref[...] = jnp.concatenate([route_t, jnp.zeros((LANES - SUBLANES, tm), F32)], axis=0).T[:, :SUBLANES]


def _mix_router(h4, lru, yhy5, x, gl, gh, wl, wh, gffn, wrh, wrl, rb, tri, ltri):
    bsz, seq, d = x.shape
    tm = TM_MIX
    nt = seq // tm
    n_tok = bsz * seq
    c2 = lambda b, i: (0, 0)
    return pl.pallas_call(
        _mix_body,
        grid=(bsz, nt),
        in_specs=[
            pl.BlockSpec((1, 1, tm, D_LRU), lambda b, i: (0, b, i, 0)),
            pl.BlockSpec((1, 1, tm, D_LRU), lambda b, i: (1, b, i, 0)),
            pl.BlockSpec((1, tm, D_LRU), lambda b, i: (b, i, 1)),
            pl.BlockSpec((1, D_HY // SUBLANES, tm // LANES, SUBLANES, LANES), lambda b, i: (b, 0, i, 0, 0)),
            pl.BlockSpec((1, tm, d), lambda b, i: (b, i, 0)),
            pl.BlockSpec((1, D_LRU), c2),
            pl.BlockSpec((D_HY, 1), c2),
            pl.BlockSpec((D_LRU, d), c2),
            pl.BlockSpec((D_HY, d), c2),
            pl.BlockSpec((1, d), c2),
            pl.BlockSpec((d, LANES), c2),
            pl.BlockSpec((d, LANES), c2),
            pl.BlockSpec((1, LANES), c2),
            pl.BlockSpec((tm, tm), c2),
            pl.BlockSpec((ROUTE_ROWS, ROUTE_ROWS), c2),
        ],
        out_specs=[
            pl.BlockSpec((1, tm, d), lambda b, i: (b, i, 0)),
            pl.BlockSpec((tm, d), lambda b, i: (b * nt + i, 0)),
            pl.BlockSpec((tm, SUBLANES), lambda b, i: (b * nt + i, 0)),
            pl.BlockSpec((1, SUBLANES, tm), lambda b, i: (b * nt + i, 0, 0)),
            pl.BlockSpec((1, ROUTE_ROWS, 1), lambda b, i: (b * nt + i, 0, 0)),
        ],
        out_shape=[
            jax.ShapeDtypeStruct((bsz, seq, d), F32),
            jax.ShapeDtypeStruct((n_tok, d), BF16),
            jax.ShapeDtypeStruct((n_tok, SUBLANES), F32),
            jax.ShapeDtypeStruct((n_tok // tm, SUBLANES, tm), F32),
            jax.ShapeDtypeStruct((n_tok // tm, ROUTE_ROWS, 1), F32),
        ],
        compiler_params=_params(("parallel", "parallel")),
        name="mix_router",
    )(h4, h4, lru, yhy5, x, gl, gh, wl, wh, gffn, wrh, wrl, rb, tri, ltri)


def _start_chunks(i, tch_ref, dst_ref, make_copy):
    def per_chunk(k, c):
        make_copy(k * ROW_CHUNK, dst_ref[i * MAX_CHUNKS + k], ROW_CHUNK).start()
        return c

    lax.fori_loop(0, tch_ref[i], per_chunk, 0)


def _wait_chunks(count, make_copy):
    nbits = MAX_CHUNKS.bit_length()
    for j in reversed(range(nbits)):
        @pl.when((count >> j) & 1 == 1)
        def _():
            make_copy(0, 0, ROW_CHUNK << j).wait()


def _dispatch_body(dst_ref, tch_ref, tn_ref, ts_ref, nu_ref, h_ref, routet_ref, xb_ref,
                   buf, zbuf, sem, zsem):
    i = pl.program_id(0)
    n_tiles = pl.num_programs(0)
    tm = h_ref.shape[0]
    lbuf = buf.shape[1]
    n_blocks = xb_ref.shape[0] // MOE_BLK
    slot = i % 2

    def aligned(r):
        return pl.ds(pl.multiple_of(r, ROW_CHUNK), ROW_CHUNK)

    @pl.when(i == 0)
    def _():
        zbuf[...] = jnp.zeros_like(zbuf)

        def zero_chunk(r):
            return pltpu.make_async_copy(zbuf.at[pl.ds(0, ROW_CHUNK)], xb_ref.at[aligned(r)], zsem)

        def zero_block(j):
            return pltpu.make_async_copy(zbuf, xb_ref.at[pl.ds(pl.multiple_of(j * MOE_BLK, MOE_BLK), MOE_BLK)], zsem)

        def per_expert(e, total):
            def per_chunk(k, c):
                zero_chunk(ts_ref[e] + k * ROW_CHUNK).start()
                return c

            lax.fori_loop(0, tn_ref[e], per_chunk, 0)
            return total + tn_ref[e]

        total = lax.fori_loop(0, N_EXPERTS, per_expert, 0)

        def start_block(j, c):
            zero_block(j).start()
            return c

        lax.fori_loop(nu_ref[0], n_blocks, start_block, 0)

        def wait_chunk(k, c):
            zero_chunk(0).wait()
            return c

        lax.fori_loop(0, total, wait_chunk, 0)

        def wait_block(j, c):
            zero_block(0).wait()
            return c

        lax.fori_loop(nu_ref[0], n_blocks, wait_block, 0)

    l0t = routet_ref[0, 0:1, :].astype(I32)
    l1t = routet_ref[0, 1:2, :].astype(I32)
    row = lax.broadcasted_iota(I32, (lbuf, tm), 0)
    perm = jnp.where(row == l0t, 1.0, jnp.where(row == l1t, 1.0, 0.0)).astype(BF16)
    srt = _dot(perm, h_ref[...]).astype(BF16)

    def copy_from(s):
        def make_copy(src_row, dst_row, rows):
            return pltpu.make_async_copy(buf.at[s, pl.ds(pl.multiple_of(src_row, ROW_CHUNK), rows)],
                                         xb_ref.at[pl.ds(pl.multiple_of(dst_row, ROW_CHUNK), rows)], sem.at[s])
        return make_copy

    @pl.when(i >= 2)
    def _():
        _wait_chunks(tch_ref[i - 2], copy_from(slot))

    buf[slot] = srt
    _start_chunks(i, tch_ref, dst_ref, copy_from(slot))

    @pl.when(i == n_tiles - 1)
    def _():
        @pl.when(i >= 1)
        def _():
            _wait_chunks(tch_ref[i - 1], copy_from(1 - slot))

        _wait_chunks(tch_ref[i], copy_from(slot))


def _dispatch(chunk_dst, tile_nch, tail_n, tail_s, n_used, h, route_t, n_slots):
    n_tok, d = h.shape
    tm = TM_MIX
    grid_spec = pltpu.PrefetchScalarGridSpec(
        num_scalar_prefetch=5,
        grid=(n_tok // tm,),
        in_specs=[
            pl.BlockSpec((tm, d), lambda i, *_: (i, 0)),
            pl.BlockSpec((1, SUBLANES, tm), lambda i, *_: (i, 0, 0)),
        ],
        out_specs=pl.BlockSpec(memory_space=pl.ANY),
        scratch_shapes=[pltpu.VMEM((2, LOCAL_ROWS, d), BF16), pltpu.VMEM((MOE_BLK, d), BF16),
                        pltpu.SemaphoreType.DMA((2,)), pltpu.SemaphoreType.DMA(())],
    )
    return pl.pallas_call(
        _dispatch_body,
        grid_spec=grid_spec,
        out_shape=jax.ShapeDtypeStruct((n_slots, d), BF16),
        compiler_params=_params(("arbitrary",)),
        name="moe_dispatch",
    )(chunk_dst, tile_nch, tail_n, tail_s, n_used, h, route_t)


def _expert_body(be_ref, fl_ref, nu_ref, xb_ref, w1_ref, w3_ref, w2_ref, yb_ref, w1_sc, w3_sc, w2_sc):
    del be_ref
    j = pl.program_id(0)
    used = j < nu_ref[0]

    @pl.when(jnp.logical_and(used, fl_ref[j] == 1))
    def _():
        w1_sc[...] = w1_ref[...].astype(BF16)
        w3_sc[...] = w3_ref[...].astype(BF16)
        w2_sc[...] = w2_ref[...].astype(BF16)

    @pl.when(used)
    def _():
        xb = xb_ref[...]
        a = _dot(xb, w1_sc[...])
        b = _dot(xb, w3_sc[...])
        act = (a * _sigmoid(a) * b).astype(BF16)
        yb_ref[...] = _dot(act, w2_sc[...]).astype(BF16)

    @pl.when(jnp.logical_not(used))
    def _():
        yb_ref[...] = jnp.zeros_like(yb_ref)


def _experts(block_e, first_flag, n_used, xb, w1, w3, w2):
    n_slots, d = xb.shape
    blk = MOE_BLK
    de = w1.shape[2]
    grid_spec = pltpu.PrefetchScalarGridSpec(
        num_scalar_prefetch=3,
        grid=(n_slots // blk,),
        in_specs=[
            pl.BlockSpec((blk, d), lambda j, be, fl, nu: (j, 0)),
            pl.BlockSpec((None, d, de), lambda j, be, fl, nu: (be[j], 0, 0)),
            pl.BlockSpec((None, d, de), lambda j, be, fl, nu: (be[j], 0, 0)),
            pl.BlockSpec((None, de, d), lambda j, be, fl, nu: (be[j], 0, 0)),
        ],
        out_specs=pl.BlockSpec((blk, d), lambda j, be, fl, nu: (j, 0)),
        scratch_shapes=[pltpu.VMEM((d, de), BF16), pltpu.VMEM((d, de), BF16), pltpu.VMEM((de, d), BF16)],
    )
    return pl.pallas_call(
        _expert_body,
        grid_spec=grid_spec,
        out_shape=jax.ShapeDtypeStruct((n_slots, d), BF16),
        compiler_params=_params(("arbitrary",)),
        name="moe_experts",
    )(block_e, first_flag, n_used, xb, w1, w3, w2)


def _combine_body(dst_ref, tch_ref, xmid_ref, route_ref, g_ref, yb_ref, o_ref, buf, sem):
    i = pl.program_id(0)
    n_tiles = pl.num_programs(0)
    tm = xmid_ref.shape[0]
    lbuf = buf.shape[1]
    slot = i % 2

    def copy_into(s):
        def make_copy(local_row, slot_row, rows):
            return pltpu.make_async_copy(yb_ref.at[pl.ds(pl.multiple_of(slot_row, ROW_CHUNK), rows)],
                                         buf.at[s, pl.ds(pl.multiple_of(local_row, ROW_CHUNK), rows)], sem.at[s])
        return make_copy

    @pl.when(i == 0)
    def _():
        buf[...] = jnp.zeros_like(buf)
        _start_chunks(i, tch_ref, dst_ref, copy_into(slot))

    @pl.when(i + 1 < n_tiles)
    def _():
        _start_chunks(i + 1, tch_ref, dst_ref, copy_into(1 - slot))

    route = route_ref[...]
    l0 = route[:, 0:1].astype(I32)
    l1 = route[:, 1:2].astype(I32)
    lane = lax.broadcasted_iota(I32, (tm, lbuf), 1)
    gmat = (jnp.where(lane == l0, route[:, 2:3], 0.0) + jnp.where(lane == l1, route[:, 3:4], 0.0)).astype(BF16)
    _wait_chunks(tch_ref[i], copy_into(slot))
    y = xmid_ref[...] + _dot(gmat, buf[slot])
    o_ref[...] = _rms(y, g_ref[...])


def _combine(chunk_dst, tile_nch, xmid, route, g, yb):
    n_tok, d = xmid.shape
    tm = TM_MIX
    grid_spec = pltpu.PrefetchScalarGridSpec(
        num_scalar_prefetch=2,
        grid=(n_tok // tm,),
        in_specs=[
            pl.BlockSpec((tm, d), lambda i, *_: (i, 0)),
            pl.BlockSpec((tm, SUBLANES), lambda i, *_: (i, 0)),
            pl.BlockSpec((1, d), lambda i, *_: (0, 0)),
            pl.BlockSpec(memory_space=pl.ANY),
        ],
        out_specs=pl.BlockSpec((tm, d), lambda i, *_: (i, 0)),
        scratch_shapes=[pltpu.VMEM((2, LOCAL_ROWS, d), BF16), pltpu.SemaphoreType.DMA((2,))],
    )
    return pl.pallas_call(
        _combine_body,
        grid_spec=grid_spec,
        out_shape=jax.ShapeDtypeStruct((n_tok, d), F32),
        compiler_params=_params(("arbitrary",)),
        name="moe_combine",
    )(chunk_dst, tile_nch, xmid, route, g, yb)


def _filter_features(seq):
    pos = jnp.arange(seq, dtype=F32)
    t = jnp.linspace(0.0, 1.0, seq, dtype=F32)
    w = (2.0 * math.pi / seq) * pos
    bands = jnp.linspace(1e-4, HY_BANDS - 1, HY_BANDS, dtype=F32)
    ang = w[:, None] * bands[None, :]
    feats = jnp.concatenate([t[:, None], jnp.cos(ang), jnp.sin(ang)], axis=-1)
    rev = lambda a: jnp.concatenate([jnp.zeros_like(a[:1]), a[1:][::-1]], axis=0)
    pad = lambda a: jnp.pad(a, ((0, 0), (0, HY_EMB_PAD - HY_EMB)))
    featst = jnp.stack([pad(feats).T, pad(rev(feats)).T], axis=0)
    return featst, t[None, :], rev(t[:, None]).T


def _gate_weights(wr, wi):
    half = D_LRU // 2
    hph = half // LRU_HEAD_DIM
    eye = jnp.eye(hph, dtype=wr.dtype)[:, None, :, None]

    def block_diag(w):
        return (eye * w[:, :, None, :]).reshape(half, half)

    out = [jnp.concatenate([block_diag(wr[hh * hph:(hh + 1) * hph]), block_diag(wi[hh * hph:(hh + 1) * hph])], axis=1)
           for hh in range(2)]
    return jnp.stack(out, axis=0).astype(BF16)


def _layer(x, l, p):
    bsz, seq, d = x.shape
    n_tok = bsz * seq
    row = lambda a: a.reshape(1, -1).astype(F32)
    col = lambda a: a.reshape(-1, 1).astype(F32)

    w_in = p["w_in"][l]
    wl = w_in[:, :2 * D_LRU].astype(BF16)
    wht = w_in[:, 2 * D_LRU:].T.astype(BF16)
    lru, hy5 = _inproj(x, row(p["norm_mix_g"][l]), wl, wht)

    wg = jnp.stack([_gate_weights(p["lru_wr_f"][l], p["lru_wi_f"][l]),
                    _gate_weights(p["lru_wr_b"][l], p["lru_wi_b"][l])], axis=0)
    stack2 = lambda a, b: jnp.stack([a.reshape(1, -1), b.reshape(1, -1)], axis=0).astype(F32)
    h4 = _lru(lru, p["lru_conv_w"][l], row(p["lru_conv_b"][l]), wg,
              stack2(p["lru_br_f"][l], p["lru_br_b"][l]), stack2(p["lru_bi_f"][l], p["lru_bi_b"][l]),
              stack2(p["lru_lambda_f"][l], p["lru_lambda_b"][l]))

    tables = _fft_tables()
    featst, t_f, t_b = _filter_features(seq)
    w1t = jnp.pad(p["hy_filt_w1"][l].T, ((0, 0), (0, HY_EMB_PAD - HY_EMB)))
    z2t = _filt_mlp(featst, w1t, col(p["hy_filt_b1"][l]), col(p["hy_filt_freq1"][l]),
                    p["hy_filt_w2"][l].T, col(p["hy_filt_b2"][l]), col(p["hy_filt_freq2"][l]))
    w3 = p["hy_filt_w3"][l]
    nfr = HY_ORDER * D_HY
    deltas = jnp.abs(jnp.linspace(math.log(HY_DECAY_TARGET) / HY_SLOW_DECAY_PCT,
                                  math.log(HY_DECAY_TARGET) / HY_FAST_DECAY_PCT, D_HY, dtype=F32))
    ktile = _filt_time(z2t, w3[:, :nfr].T, w3[:, nfr:].T, col(jnp.tile(deltas, HY_ORDER)), t_f, t_b)
    kfr, kfi = _filt_fft(ktile, tables)
    cwt = jnp.concatenate([p["hy_conv_w"][l], p["hy_conv_b"][l][None, :]], axis=0)
    cwt = jnp.broadcast_to(cwt.reshape(HY_CONV + 1, -1, SUBLANES, 1), (HY_CONV + 1, cwt.shape[1] // SUBLANES, SUBLANES, LANES))
    yhy5 = _hyena(hy5, cwt, p["hy_skip"][l], kfr, kfi, tables)

    w_out = p["w_out"][l]
    wr_cat = jnp.concatenate([p["router_group_w"][l], p["router_expert_w"][l]], axis=1)
    wr_cat = jnp.pad(wr_cat, ((0, 0), (0, LANES - wr_cat.shape[1])))
    wrh, wrl = _split_bf16(wr_cat)
    rb = jnp.concatenate([p["router_group_b"][l], p["router_expert_b"][l]])
    rb = jnp.pad(rb, (0, LANES - rb.shape[0])).reshape(1, LANES)
    earlier = lambda n: (jnp.arange(n)[:, None] < jnp.arange(n)[None, :]).astype(BF16)
    xmid, h, route, route_t, tile_cnt = _mix_router(
        h4, lru, yhy5, x, row(p["grp_norm_lru_g"][l]), col(p["grp_norm_hy_g"][l]),
        w_out[:D_LRU].astype(BF16), w_out[D_LRU:].astype(BF16), row(p["norm_ffn_g"][l]), wrh, wrl, rb,
        earlier(TM_MIX), earlier(ROUTE_ROWS).T)

    blk = MOE_BLK
    ch = ROW_CHUNK
    n_tiles = n_tok // TM_MIX
    n_blocks = (2 * n_tok + n_tiles * N_EXPERTS * (ch - 1)) // blk + N_EXPERTS
    cnt = tile_cnt[:, N_GROUPS:N_GROUPS + N_EXPERTS, 0].astype(I32)
    run = (cnt + ch - 1) // ch * ch
    tot_e = jnp.sum(run, axis=0)
    padded_e = (tot_e + blk - 1) // blk * blk
    pad_end = jnp.cumsum(padded_e)
    pad_start = pad_end - padded_e
    gstart = pad_start[None, :] + jnp.cumsum(run, axis=0) - run
    lstart = jnp.cumsum(run, axis=1) - run
    flat = lambda a: a.reshape(-1).astype(I32)
    tile_nch = flat(jnp.sum(run, axis=1) // ch)
    nch_end = jnp.cumsum(run // ch, axis=1)
    k = jnp.arange(MAX_CHUNKS, dtype=I32)
    base = gstart - lstart
    step = base[:, 1:] - base[:, :-1]
    passed = (k[None, :, None] >= nch_end[:, None, :-1]).astype(I32)
    chunk_dst = flat(base[:, :1] + jnp.sum(passed * step[:, None, :], axis=2) + k[None, :] * ch)
    tail_n = flat((padded_e - tot_e) // ch)
    tail_s = flat(pad_start + tot_e)
    block_start = jnp.arange(n_blocks, dtype=I32) * blk
    block_e = jnp.minimum(jnp.sum(block_start[:, None] >= pad_end[None, :], axis=1), N_EXPERTS - 1).astype(I32)
    first_flag = jnp.concatenate([jnp.ones((1,), I32), (block_e[1:] != block_e[:-1]).astype(I32)])
    n_used = (pad_end[-1:] // blk).astype(I32)

    xb = _dispatch(chunk_dst, tile_nch, tail_n, tail_s, n_used, h, route_t, n_blocks * blk)
    yb = _experts(block_e, first_flag, n_used, xb, p["exp_w1"][l], p["exp_w3"][l], p["exp_w2"][l])
    return xmid.reshape(n_tok, d), route, (chunk_dst, tile_nch), yb


def kernel(x, norm_mix_g, w_in, lru_conv_w, lru_conv_b, lru_wr_f, lru_br_f, lru_wi_f, lru_bi_f, lru_lambda_f, lru_wr_b, lru_br_b, lru_wi_b, lru_bi_b, lru_lambda_b, hy_conv_w, hy_conv_b, hy_filt_w1, hy_filt_b1, hy_filt_freq1, hy_filt_w2, hy_filt_b2, hy_filt_freq2, hy_filt_w3, hy_skip, grp_norm_lru_g, grp_norm_hy_g, w_out, norm_ffn_g, router_group_w, router_group_b, router_expert_w, router_expert_b, exp_w1, exp_w3, exp_w2, norm_final_g):
    p = dict(locals())
    bsz, seq, d = x.shape
    assert d == D_MODEL and 2 * seq == FFT_R * FFT_R and w_in.shape[0] == 1
    xmid, route, (chunk_dst, tile_nch), yb = _layer(x, 0, p)
    out = _combine(chunk_dst, tile_nch, xmid, route, norm_final_g.reshape(1, d), yb)
    return out.reshape(bsz, seq, d)
```

```python
import functools
import math

import numpy as np
import jax
import jax.numpy as jnp
from jax import lax
from jax.experimental import pallas as pl
from jax.experimental.pallas import tpu as pltpu

F32 = jnp.float32
BF16 = jnp.bfloat16
I32 = jnp.int32

D_MODEL = 1024
D_LRU = 512
D_HY = 512
LRU_HEADS = 8
LRU_HEAD_DIM = D_LRU // LRU_HEADS
LRU_CONV = 4
LRU_C = 8.0
HY_ORDER = 2
HY_CONV = 3
HY_BANDS = 16
HY_EMB = 2 * HY_BANDS + 1
HY_EMB_PAD = 40
HY_FFN = 64
HY_FAST_DECAY_PCT = 0.3
HY_SLOW_DECAY_PCT = 1.5
HY_DECAY_TARGET = 1e-2
N_GROUPS = 4
EXPERTS_PER_GROUP = 8
N_EXPERTS = N_GROUPS * EXPERTS_PER_GROUP
D_EXPERT = D_MODEL // 2
EPS = 1e-6

LANES = 128
SUBLANES = 8
FFT_R = 128
VMEM_LIMIT = 56 * 1024 * 1024

TM_IN = 1024
TC_LRU = 1024
RC_HY = 32
RF_KF = 16
HY_SPLIT = 2
RB_FILT = 64
TM_MIX = 512
MOE_BLK = 512
ROW_CHUNK = 16
LOCAL_ROWS = -(-(2 * TM_MIX + N_EXPERTS * (ROW_CHUNK - 1)) // LANES) * LANES
MAX_CHUNKS = LOCAL_ROWS // ROW_CHUNK
ROUTE_ROWS = -(-(N_GROUPS + N_EXPERTS) // SUBLANES) * SUBLANES
NEG_BIG = -1e30


def _params(sem, vmem=VMEM_LIMIT):
    return pltpu.CompilerParams(dimension_semantics=sem, vmem_limit_bytes=vmem)


def _rms(x, g):
    return x * lax.rsqrt(jnp.mean(x * x, axis=-1, keepdims=True) + EPS) * g


def _sigmoid(x):
    return 1.0 / (1.0 + jnp.exp(-x))


def _split_bf16(a):
    hi = a.astype(BF16)
    lo = (a - hi.astype(F32)).astype(BF16)
    return hi, lo


def _dot(a, b):
    return jnp.dot(a, b, preferred_element_type=F32)


def _dot3(a, b):
    ah, al = _split_bf16(a)
    bh, bl = _split_bf16(b)
    return _dot(ah, bh) + _dot(al, bh) + _dot(ah, bl)


def _inproj_body(x_ref, g_ref, wl_ref, wht_ref, lru_ref, hy_ref):
    hn = _rms(x_ref[0], g_ref[...]).astype(BF16)
    lru_ref[0] = _dot(hn, wl_ref[...])
    hyt = lax.dot_general(wht_ref[...], hn, (((1,), (1,)), ((), ())), preferred_element_type=F32)
    nrow = hyt.shape[0] // SUBLANES
    for j in range(hyt.shape[1] // LANES):
        hy_ref[0, :, j, :, :] = hyt[:, LANES * j:LANES * (j + 1)].reshape(nrow, SUBLANES, LANES)


def _inproj(x, g, wl, wht):
    bsz, seq, d = x.shape
    nl = wl.shape[1]
    nh = wht.shape[0]
    tm = TM_IN
    return pl.pallas_call(
        _inproj_body,
        grid=(bsz, seq // tm),
        in_specs=[
            pl.BlockSpec((1, tm, d), lambda b, i: (b, i, 0)),
            pl.BlockSpec((1, d), lambda b, i: (0, 0)),
            pl.BlockSpec((d, nl), lambda b, i: (0, 0)),
            pl.BlockSpec((nh, d), lambda b, i: (0, 0)),
        ],
        out_specs=[
            pl.BlockSpec((1, tm, nl), lambda b, i: (b, i, 0)),
            pl.BlockSpec((1, nh // SUBLANES, tm // LANES, SUBLANES, LANES), lambda b, i: (b, 0, i, 0, 0)),
        ],
        out_shape=[
            jax.ShapeDtypeStruct((bsz, seq, nl), F32),
            jax.ShapeDtypeStruct((bsz, nh // SUBLANES, seq // LANES, SUBLANES, LANES), F32),
        ],
        compiler_params=_params(("parallel", "parallel")),
        name="inproj",
    )(x, g, wl, wht)


def _sqrt_one_minus_sq(a, neg_log_a):
    om = jnp.maximum(jnp.tanh(neg_log_a) * (1.0 + a * a), 1e-30)
    return om * lax.rsqrt(om)


def _group_scan(a3, b3, reverse):
    sub = lax.broadcasted_iota(I32, a3.shape, 1)
    for s in (1, 2, 4):
        if reverse:
            a_sh = pltpu.roll(a3, SUBLANES - s, axis=1)
            b_sh = pltpu.roll(b3, SUBLANES - s, axis=1)
            m = sub < SUBLANES - s
        else:
            a_sh = pltpu.roll(a3, s, axis=1)
            b_sh = pltpu.roll(b3, s, axis=1)
            m = sub >= s
        b3 = jnp.where(m, a3 * b_sh + b3, b3)
        a3 = jnp.where(m, a3 * a_sh, a3)
    return a3, b3


def _lru_body(cur_ref, prev_ref, next_ref, cw_ref, cb_ref, wg_ref, br_ref, bi_ref, lam_ref, o_ref,
              carry_sc, *, nt, tc):
    d = pl.program_id(1)
    i = pl.program_id(2)
    c = jnp.where(d == 0, i, nt - 1 - i)
    ns = SUBLANES
    ng = tc // ns
    half = D_LRU // 2

    @pl.when(i == 0)
    def _():
        carry_sc[...] = jnp.zeros_like(carry_sc)

    xs = jnp.swapaxes(cur_ref[0].reshape(ns, ng, D_LRU), 0, 1)
    prev = jnp.where(c > 0, prev_ref[0], 0.0)
    nxt = jnp.where(c < nt - 1, next_ref[0], 0.0)
    sub = lax.broadcasted_iota(I32, (ns, D_LRU), 0)

    def prev_segment(slab, halo_row):
        return jnp.where(sub == 0, halo_row, pltpu.roll(slab, 1, axis=0))

    def next_segment(slab, halo_row):
        return jnp.where(sub == ns - 1, halo_row, pltpu.roll(slab, ns - 1, axis=0))

    xext = jnp.concatenate([prev_segment(xs[ng - 2], prev[ns - 2:ns - 1])[None],
                            prev_segment(xs[ng - 1], prev[ns - 1:ns])[None], xs,
                            next_segment(xs[0], nxt[0:1])[None]], axis=0)
    cw = cw_ref[...]
    xc = cb_ref[...] + cw[0:1] * xext[0:ng]
    for k in range(1, LRU_CONV):
        xc = xc + cw[k:k + 1] * xext[k:k + ng]
    xc = xc.reshape(tc, D_LRU)

    lam = lam_ref[0]
    nlam = -lam
    softplus = jnp.maximum(nlam, 0.0) + jnp.log1p(jnp.exp(-jnp.abs(nlam)))
    half_c = (0.5 * LRU_C) * softplus

    def gates(hh):
        sl = slice(half * hh, half * (hh + 1))
        xh = xc[:, sl]
        logits = _dot(xh.astype(BF16), wg_ref[0, hh])
        tr_ = jnp.tanh(0.5 * (logits[:, :half] + br_ref[0][:, sl]))
        gi = 0.5 + 0.5 * jnp.tanh(0.5 * (logits[:, half:] + bi_ref[0][:, sl]))
        neg_log_a = half_c[:, sl] * (1.0 + tr_)
        a = jnp.exp(-neg_log_a)
        b = _sqrt_one_minus_sq(a, neg_log_a) * (gi * xh)
        return a.reshape(ng, ns, half), b.reshape(ng, ns, half)

    def run(reverse):
        subh = lax.broadcasted_iota(I32, (ns, half), 0)
        for hh in range(2):
            sl = slice(half * hh, half * (hh + 1))
            a3, b3 = gates(hh)
            h = jnp.zeros((ns, half), F32)
            p = jnp.ones((ns, half), F32)
            hs = [None] * ng
            ps = [None] * ng
            for g in (range(ng - 1, -1, -1) if reverse else range(ng)):
                h = a3[g] * h + b3[g]
                p = a3[g] * p
                hs[g] = h
                ps[g] = p
            ac, hc = _group_scan(p[None], h[None], reverse)
            c_in = carry_sc[:, sl]
            end = ac[0] * c_in + hc[0]
            if reverse:
                seg_in = jnp.where(subh == ns - 1, c_in, pltpu.roll(end, ns - 1, axis=0))
                carry_sc[:, sl] = jnp.broadcast_to(end[0:1], end.shape)
            else:
                seg_in = jnp.where(subh == 0, c_in, pltpu.roll(end, 1, axis=0))
                carry_sc[:, sl] = jnp.broadcast_to(end[ns - 1:ns], end.shape)
            h3 = jnp.stack(hs, axis=0) + jnp.stack(ps, axis=0) * seg_in[None]
            o_ref[0, 0, :, sl] = jnp.swapaxes(h3, 0, 1).reshape(tc, half)

    @pl.when(d == 0)
    def _():
        run(False)

    @pl.when(d == 1)
    def _():
        run(True)


def _lru(lru, cw, cb, wg, br, bi, lam):
    bsz, seq, _ = lru.shape
    tc = TC_LRU
    nt = seq // tc
    r8 = tc // SUBLANES
    nrow8 = seq // SUBLANES

    def cidx(d, i):
        return jnp.where(d == 0, i, nt - 1 - i)

    return pl.pallas_call(
        functools.partial(_lru_body, nt=nt, tc=tc),
        grid=(bsz, 2, nt),
        in_specs=[
            pl.BlockSpec((1, tc, D_LRU), lambda b, d, i: (b, cidx(d, i), 0)),
            pl.BlockSpec((1, SUBLANES, D_LRU), lambda b, d, i: (b, jnp.maximum(cidx(d, i) * r8 - 1, 0), 0)),
            pl.BlockSpec((1, SUBLANES, D_LRU), lambda b, d, i: (b, jnp.minimum((cidx(d, i) + 1) * r8, nrow8 - 1), 0)),
            pl.BlockSpec((LRU_CONV, D_LRU), lambda b, d, i: (0, 0)),
            pl.BlockSpec((1, D_LRU), lambda b, d, i: (0, 0)),
            pl.BlockSpec((1, 2, D_LRU // 2, D_LRU), lambda b, d, i: (d, 0, 0, 0)),
            pl.BlockSpec((1, 1, D_LRU), lambda b, d, i: (d, 0, 0)),
            pl.BlockSpec((1, 1, D_LRU), lambda b, d, i: (d, 0, 0)),
            pl.BlockSpec((1, 1, D_LRU), lambda b, d, i: (d, 0, 0)),
        ],
        out_specs=pl.BlockSpec((1, 1, tc, D_LRU), lambda b, d, i: (d, b, cidx(d, i), 0)),
        out_shape=jax.ShapeDtypeStruct((2, bsz, seq, D_LRU), F32),
        scratch_shapes=[pltpu.VMEM((SUBLANES, D_LRU), F32)],
        compiler_params=_params(("parallel", "arbitrary", "arbitrary")),
        name="rg_lru",
    )(lru, lru, lru, cw, cb, wg, br, bi, lam)


def _fft_tables():
    r = FFT_R
    n = r * r
    idx = np.arange(r, dtype=np.float64)
    ang = 2.0 * np.pi * np.outer(idx, idx) / r
    wr, wi = np.cos(ang), -np.sin(ang)
    angt = 2.0 * np.pi * np.outer(idx, idx) / n
    tr, ti = np.cos(angt), -np.sin(angt)
    h = r // 2
    f1 = np.concatenate([wr, wi], axis=1)
    f1p = np.block([[wr[:h], wi[:h]], [-wi[:h], wr[:h]]])
    f2 = np.block([[wr, wi], [-wi, wr]])
    f2c = np.block([[wr, -wi], [wi, wr]])
    f3p = np.block([[wr[:, :h], -wi[:, :h]], [wi[:, :h], wr[:, :h]]]) / n
    as_bf16 = lambda a: jnp.asarray(a, F32).astype(BF16)
    return dict(f1=as_bf16(f1), f1p=as_bf16(f1p), f2=as_bf16(f2), f2c=as_bf16(f2c), f3p=as_bf16(f3p),
                tr=jnp.asarray(tr, F32), ti=jnp.asarray(ti, F32), tr16=as_bf16(tr), ti16=as_bf16(ti))


def _const_specs(arrays, nargs):
    return [pl.BlockSpec(a.shape, (lambda nd: (lambda *_: (0,) * nd))(a.ndim)) for a in arrays]


def _fft_fwd(x3, f1, f2, tr, ti):
    g = x3.shape[0]
    r = FFT_R
    wd = tr.dtype
    xt = jnp.swapaxes(x3.astype(wd), 1, 2).reshape(g * r, r)
    a = _dot(xt.astype(BF16), f1)
    ar = a[:, :r].astype(wd).reshape(g, r, r)
    ai = a[:, r:].astype(wd).reshape(g, r, r)
    pr = ar * tr - ai * ti
    pi = ar * ti + ai * tr
    pt = jnp.concatenate([jnp.swapaxes(pr, 1, 2), jnp.swapaxes(pi, 1, 2)], axis=-1).reshape(g * r, 2 * r)
    x = _dot(pt.astype(BF16), f2)
    return x[:, :r], x[:, r:]


def _fft_inv(yr, yi, f2c, f3, tr, ti):
    r = FFT_R
    g = yr.shape[0] // r
    wd = tr.dtype
    b = _dot(jnp.concatenate([yr, yi], axis=-1).astype(BF16), f2c)
    br = b[:, :r].astype(wd).reshape(g, r, r)
    bi = b[:, r:].astype(wd).reshape(g, r, r)
    cr = br * tr + bi * ti
    ci = bi * tr - br * ti
    ct = jnp.concatenate([jnp.swapaxes(cr, 1, 2), jnp.swapaxes(ci, 1, 2)], axis=-1).reshape(g * r, 2 * r)
    yt = _dot(ct.astype(BF16), f3)
    return jnp.swapaxes(yt.reshape(g, r, r), 1, 2)


def _filt_mlp_body(ft_ref, w1_ref, b1_ref, f1_ref, w2_ref, b2_ref, f2_ref, o_ref):
    z = jnp.sin(f1_ref[...] * (_dot3(w1_ref[...], ft_ref[0]) + b1_ref[...]))
    o_ref[0] = jnp.sin(f2_ref[...] * (_dot3(w2_ref[...], z) + b2_ref[...]))


def _filt_mlp(featst, w1t, b1, f1, w2t, b2, f2):
    nd, ke, seq = featst.shape
    col = lambda d: (0, 0)
    return pl.pallas_call(
        _filt_mlp_body,
        grid=(nd,),
        in_specs=[
            pl.BlockSpec((1, ke, seq), lambda d: (d, 0, 0)),
            pl.BlockSpec((HY_FFN, ke), col),
            pl.BlockSpec((HY_FFN, 1), col),
            pl.BlockSpec((HY_FFN, 1), col),
            pl.BlockSpec((HY_FFN, HY_FFN), col),
            pl.BlockSpec((HY_FFN, 1), col),
            pl.BlockSpec((HY_FFN, 1), col),
        ],
        out_specs=pl.BlockSpec((1, HY_FFN, seq), lambda d: (d, 0, 0)),
        out_shape=jax.ShapeDtypeStruct((nd, HY_FFN, seq), F32),
        compiler_params=_params(("parallel",)),
        name="hyena_filter_mlp",
    )(featst, w1t, b1, f1, w2t, b2, f2)


def _filt_time_body(z_ref, wf_ref, wb_ref, dl_ref, tf_ref, tb_ref, o_ref):
    seq = z_ref.shape[2]
    hf = _dot3(wf_ref[...], z_ref[0]) * jnp.exp(-tf_ref[...] * dl_ref[...])
    hb = _dot3(wb_ref[...], z_ref[1]) * jnp.exp(-tb_ref[...] * dl_ref[...])
    lane = lax.broadcasted_iota(I32, hb.shape, 1)
    hb = jnp.where(lane == 0, 0.0, hb)
    norm = jnp.sum(jnp.abs(hf), axis=-1, keepdims=True) + jnp.sum(jnp.abs(hb), axis=-1, keepdims=True) + EPS
    hf = hf / norm
    hb = hb / norm
    nrow = hf.shape[0] // SUBLANES
    nch = seq // LANES
    for j in range(nch):
        o_ref[:, j, :, :] = hf[:, LANES * j:LANES * (j + 1)].reshape(nrow, SUBLANES, LANES)
        o_ref[:, nch + j, :, :] = hb[:, LANES * j:LANES * (j + 1)].reshape(nrow, SUBLANES, LANES)


def _filt_time(z2t, w3f, w3b, delta_rows, t_f, t_b):
    nrows = w3f.shape[0]
    seq = z2t.shape[2]
    rb = RB_FILT
    return pl.pallas_call(
        _filt_time_body,
        grid=(nrows // rb,),
        in_specs=[
            pl.BlockSpec((2, HY_FFN, seq), lambda r: (0, 0, 0)),
            pl.BlockSpec((rb, HY_FFN), lambda r: (r, 0)),
            pl.BlockSpec((rb, HY_FFN), lambda r: (r, 0)),
            pl.BlockSpec((rb, 1), lambda r: (r, 0)),
            pl.BlockSpec((1, seq), lambda r: (0, 0)),
            pl.BlockSpec((1, seq), lambda r: (0, 0)),
        ],
        out_specs=pl.BlockSpec((rb // SUBLANES, 2 * seq // LANES, SUBLANES, LANES), lambda r: (r, 0, 0, 0)),
        out_shape=jax.ShapeDtypeStruct((nrows // SUBLANES, 2 * seq // LANES, SUBLANES, LANES), F32),
        compiler_params=_params(("parallel",)),
        name="hyena_filter_time",
    )(z2t, w3f, w3b, delta_rows, t_f, t_b)


def _filt_fft_body(k_ref, f1_ref, f2_ref, tr_ref, ti_ref, kr_ref, ki_ref):
    tr = tr_ref[...]
    ti = ti_ref[...]
    for g in range(k_ref.shape[0]):
        x3 = jnp.swapaxes(k_ref[g], 0, 1)
        xr, xi = _fft_fwd(x3, f1_ref[...], f2_ref[...], tr, ti)
        kr_ref[SUBLANES * g:SUBLANES * (g + 1)] = xr.astype(BF16).reshape(SUBLANES, FFT_R, FFT_R)
        ki_ref[SUBLANES * g:SUBLANES * (g + 1)] = xi.astype(BF16).reshape(SUBLANES, FFT_R, FFT_R)


def _filt_fft(ktile, tables):
    n8, r, _, _ = ktile.shape
    nrows = n8 * SUBLANES
    rf = RF_KF
    out_spec = pl.BlockSpec((rf, r, r), lambda i: (i, 0, 0))
    consts = [tables[k] for k in ("f1", "f2", "tr16", "ti16")]
    return pl.pallas_call(
        _filt_fft_body,
        grid=(nrows // rf,),
        in_specs=[pl.BlockSpec((rf // SUBLANES, r, SUBLANES, LANES), lambda i: (i, 0, 0, 0))] + _const_specs(consts, 1),
        out_specs=[out_spec, out_spec],
        out_shape=[jax.ShapeDtypeStruct((nrows, r, r), BF16)] * 2,
        compiler_params=_params(("parallel",)),
        name="hyena_filter_fft",
    )(ktile, *consts)


def _conv3_time(x, w):
    lane = lax.broadcasted_iota(I32, x.shape, 2)
    zero = jnp.zeros_like(x[:1])
    xm = pltpu.roll(x, 1, axis=2)
    xm = jnp.where(lane == 0, jnp.concatenate([zero, xm[:-1]], axis=0), xm)
    xp = pltpu.roll(x, LANES - 1, axis=2)
    xp = jnp.where(lane == LANES - 1, jnp.concatenate([xp[1:], zero], axis=0), xp)
    return w[0] * xm + w[1] * x + w[2] * xp + w[3]


def _hyena_body(sk_ref, v_ref, x1_ref, x2_ref, wv_ref, w1_ref, w2_ref, k0r_ref, k0i_ref, k1r_ref, k1i_ref,
                f1p_ref, f2_ref, f2c_ref, f3p_ref, tr_ref, ti_ref, o_ref, *, rc):
    ct = pl.program_id(0)
    tr = tr_ref[...]
    ti = ti_ref[...]
    np_ = v_ref.shape[2]
    nb = v_ref.shape[0]

    def load(ref, w_ref, g):
        parts = [jnp.swapaxes(_conv3_time(ref[bb, g], w_ref[:, g]), 0, 1) for bb in range(nb)]
        return jnp.concatenate(parts, axis=1)

    def long_conv(z3, kr_ref, ki_ref, g):
        r = FFT_R
        gsub = SUBLANES // HY_SPLIT
        wd = tr.dtype

        def split(m):
            return m[:, :r].astype(wd).reshape(gsub, r, r), m[:, r:].astype(wd).reshape(gsub, r, r)

        def join_t(re, im):
            return jnp.concatenate([jnp.swapaxes(re, 1, 2), jnp.swapaxes(im, 1, 2)], axis=-1).reshape(
                gsub * r, 2 * r).astype(BF16)

        subs = range(HY_SPLIT)
        zs = [z3[s * gsub:(s + 1) * gsub] for s in subs]
        a = [_dot(jnp.swapaxes(z.astype(wd), 1, 2).reshape(gsub * r, r).astype(BF16), f1p_ref[...]) for z in zs]
        p = []
        for m in a:
            ar, ai = split(m)
            p.append(join_t(ar * tr - ai * ti, ar * ti + ai * tr))
        x = [_dot(m, f2_ref[...]) for m in p]
        y = []
        for s, m in zip(subs, x):
            rows = pl.ds(g * SUBLANES + s * gsub, gsub)
            kr = kr_ref[rows].reshape(gsub * r, r)
            ki = ki_ref[rows].reshape(gsub * r, r)
            xr, xi = m[:, :r].astype(wd), m[:, r:].astype(wd)
            y.append(jnp.concatenate([xr * kr - xi * ki, xr * ki + xi * kr], axis=-1).astype(BF16))
        b = [_dot(m, f2c_ref[...]) for m in y]
        c = []
        for m in b:
            br, bi = split(m)
            c.append(join_t(br * tr + bi * ti, bi * tr - br * ti))
        yt = [_dot(m, f3p_ref[...]) for m in c]
        return jnp.concatenate([jnp.swapaxes(m.reshape(gsub, r, r), 1, 2) for m in yt], axis=0)

    def group(g, carry):
        cbase = ct * rc + g * SUBLANES

        def skip(z3, order):
            return jnp.stack([sk_ref[order, cbase + ci] * z3[ci] for ci in range(SUBLANES)], axis=0)

        v3 = load(v_ref, wv_ref, g)
        z1 = load(x1_ref, w1_ref, g) * (long_conv(v3, k0r_ref, k0i_ref, g) + skip(v3, 0))
        out = load(x2_ref, w2_ref, g) * (long_conv(z1, k1r_ref, k1i_ref, g) + skip(z1, 1))
        for bb in range(nb):
            o_ref[bb, g] = jnp.swapaxes(out[:, bb * np_:(bb + 1) * np_, :], 0, 1)
        return carry

    lax.fori_loop(0, rc // SUBLANES, group, 0)


def _hyena(hy5, cwt, skip, kfr, kfi, tables):
    bsz, n8, np_, _, _ = hy5.shape
    assert bsz % 2 == 0 and 2 * np_ == FFT_R
    rc = RC_HY
    r8 = rc // SUBLANES
    nct = D_HY // rc
    r = FFT_R

    def xspec(off):
        return pl.BlockSpec((2, r8, np_, SUBLANES, LANES), lambda c, b: (b, c + off * nct, 0, 0, 0))

    def wspec(off):
        return pl.BlockSpec((HY_CONV + 1, r8, SUBLANES, LANES), lambda c, b: (0, c + off * nct, 0, 0))

    def kspec(order):
        return pl.BlockSpec((rc, r, r), lambda c, b: (c + order * nct, 0, 0))

    consts = [tables[k] for k in ("f1p", "f2", "f2c", "f3p", "tr16", "ti16")]
    return pl.pallas_call(
        functools.partial(_hyena_body, rc=rc),
        grid=(nct, bsz // 2),
        in_specs=[pl.BlockSpec(memory_space=pltpu.SMEM), xspec(0), xspec(1), xspec(2), wspec(0), wspec(1), wspec(2),
                  kspec(0), kspec(0), kspec(1), kspec(1)] + _const_specs(consts, 2),
        out_specs=pl.BlockSpec((2, r8, np_, SUBLANES, LANES), lambda c, b: (b, c, 0, 0, 0)),
        out_shape=jax.ShapeDtypeStruct((bsz, D_HY // SUBLANES, np_, SUBLANES, LANES), F32),
        compiler_params=_params(("parallel", "parallel")),
        name="hyena",
    )(skip, hy5, hy5, hy5, cwt, cwt, cwt, kfr, kfi, kfr, kfi, *consts)


def _gelu_tanh(x):
    return 0.5 * x * (1.0 + jnp.tanh(math.sqrt(2.0 / math.pi) * (x + 0.044715 * (x * x * x))))


def _mix_body(hf_ref, hb_ref, gate_ref, yhy_ref, x_ref, gl_ref, gh_ref, wl_ref, wh_ref, gffn_ref,
              wrh_ref, wrl_ref, rb_ref, tri_ref, ltri_ref, xmid_ref, h_ref, routec_ref, routet_ref, cnt_ref):
    tm = x_ref.shape[1]
    y_lru = (hf_ref[0, 0] + hb_ref[0, 0]) * _gelu_tanh(gate_ref[0])
    nl = _rms(y_lru, gl_ref[...]).astype(BF16)
    gh = gh_ref[...]
    for j in range(tm // LANES):
        yt = yhy_ref[0, :, j, :, :].reshape(D_HY, LANES)
        nt = (yt * lax.rsqrt(jnp.mean(yt * yt, axis=0, keepdims=True) + EPS) * gh).astype(BF16)
        rows = slice(LANES * j, LANES * (j + 1))
        mix = _dot(nl[rows], wl_ref[...]) + lax.dot_general(
            nt, wh_ref[...], (((0,), (0,)), ((), ())), preferred_element_type=F32)
        xmid_ref[0, rows, :] = x_ref[0, rows, :] + mix
    h = _rms(xmid_ref[0], gffn_ref[...])
    h_ref[...] = h.astype(BF16)

    hh, hl = _split_bf16(h)
    logits = _dot(hh, wrh_ref[...]) + _dot(hl, wrh_ref[...]) + _dot(hh, wrl_ref[...]) + rb_ref[...]
    lt = logits.T[:ROUTE_ROWS]
    row = lax.broadcasted_iota(I32, lt.shape, 0)
    big = jnp.int32(1 << 20)
    is_g = row < N_GROUPS
    lg = jnp.where(is_g, lt, NEG_BIG)
    mg = jnp.max(lg, axis=0, keepdims=True)
    gidx = jnp.min(jnp.where(lg == mg, row, big), axis=0, keepdims=True)
    g_p = 1.0 / jnp.sum(jnp.where(is_g, jnp.exp(lg - mg), 0.0), axis=0, keepdims=True)
    e_lo = N_GROUPS + EXPERTS_PER_GROUP * gidx
    is_e = jnp.logical_and(row >= e_lo, row < e_lo + EXPERTS_PER_GROUP)
    le = jnp.where(is_e, lt, NEG_BIG)
    m1 = jnp.max(le, axis=0, keepdims=True)
    i1 = jnp.min(jnp.where(le == m1, row, big), axis=0, keepdims=True)
    le2 = jnp.where(row == i1, NEG_BIG, le)
    m2 = jnp.max(le2, axis=0, keepdims=True)
    i2 = jnp.min(jnp.where(le2 == m2, row, big), axis=0, keepdims=True)
    ratio = jnp.exp(m2 - m1)
    gate1 = g_p / (1.0 + ratio)
    gate2 = g_p * ratio / (1.0 + ratio)
    oh1 = row == i1
    oh2 = row == i2
    oh = jnp.where(oh1, 1.0, jnp.where(oh2, 1.0, 0.0))
    before = _dot(oh.astype(BF16), tri_ref[...])
    cnt = jnp.sum(oh, axis=1, keepdims=True)
    nchunk = jnp.floor((cnt + (ROW_CHUNK - 1)) * (1.0 / ROW_CHUNK))
    run_start = ROW_CHUNK * _dot(ltri_ref[...], jnp.broadcast_to(nchunk, (ROUTE_ROWS, LANES)).astype(BF16))[:, 0:1]
    pos = run_start + before
    l0 = jnp.sum(jnp.where(oh1, pos, 0.0), axis=0, keepdims=True)
    l1 = jnp.sum(jnp.where(oh2, pos, 0.0), axis=0, keepdims=True)
    cnt_ref[0] = cnt
    route_t = jnp.concatenate([l0, l1, gate1, gate2, jnp.zeros((SUBLANES - 4, tm), F32)], axis=0)
    routet_ref[0] = route_t
    routec_ref[...] = jnp.concatenate([route_t, jnp.zeros((LANES - SUBLANES, tm), F32)], axis=0).T[:, :SUBLANES]


def _mix_router(h4, lru, yhy5, x, gl, gh, wl, wh, gffn, wrh, wrl, rb, tri, ltri):
    bsz, seq, d = x.shape
    tm = TM_MIX
    nt = seq // tm
    n_tok = bsz * seq
    c2 = lambda b, i: (0, 0)
    return pl.pallas_call(
        _mix_body,
        grid=(bsz, nt),
        in_specs=[
            pl.BlockSpec((1, 1, tm, D_LRU), lambda b, i: (0, b, i, 0)),
            pl.BlockSpec((1, 1, tm, D_LRU), lambda b, i: (1, b, i, 0)),
            pl.BlockSpec((1, tm, D_LRU), lambda b, i: (b, i, 1)),
            pl.BlockSpec((1, D_HY // SUBLANES, tm // LANES, SUBLANES, LANES), lambda b, i: (b, 0, i, 0, 0)),
            pl.BlockSpec((1, tm, d), lambda b, i: (b, i, 0)),
            pl.BlockSpec((1, D_LRU), c2),
            pl.BlockSpec((D_HY, 1), c2),
            pl.BlockSpec((D_LRU, d), c2),
            pl.BlockSpec((D_HY, d), c2),
            pl.BlockSpec((1, d), c2),
            pl.BlockSpec((d, LANES), c2),
            pl.BlockSpec((d, LANES), c2),
            pl.BlockSpec((1, LANES), c2),
            pl.BlockSpec((tm, tm), c2),
            pl.BlockSpec((ROUTE_ROWS, ROUTE_ROWS), c2),
        ],
        out_specs=[
            pl.BlockSpec((1, tm, d), lambda b, i: (b, i, 0)),
            pl.BlockSpec((tm, d), lambda b, i: (b * nt + i, 0)),
            pl.BlockSpec((tm, SUBLANES), lambda b, i: (b * nt + i, 0)),
            pl.BlockSpec((1, SUBLANES, tm), lambda b, i: (b * nt + i, 0, 0)),
            pl.BlockSpec((1, ROUTE_ROWS, 1), lambda b, i: (b * nt + i, 0, 0)),
        ],
        out_shape=[
            jax.ShapeDtypeStruct((bsz, seq, d), F32),
            jax.ShapeDtypeStruct((n_tok, d), BF16),
            jax.ShapeDtypeStruct((n_tok, SUBLANES), F32),
            jax.ShapeDtypeStruct((n_tok // tm, SUBLANES, tm), F32),
            jax.ShapeDtypeStruct((n_tok // tm, ROUTE_ROWS, 1), F32),
        ],
        compiler_params=_params(("parallel", "parallel")),
        name="mix_router",
    )(h4, h4, lru, yhy5, x, gl, gh, wl, wh, gffn, wrh, wrl, rb, tri, ltri)


def _start_chunks(i, tch_ref, dst_ref, make_copy):
    def per_chunk(k, c):
        make_copy(k * ROW_CHUNK, dst_ref[i * MAX_CHUNKS + k], ROW_CHUNK).start()
        return c

    lax.fori_loop(0, tch_ref[i], per_chunk, 0)


def _wait_chunks(count, make_copy):
    nbits = MAX_CHUNKS.bit_length()
    for j in reversed(range(nbits)):
        @pl.when((count >> j) & 1 == 1)
        def _():
            make_copy(0, 0, ROW_CHUNK << j).wait()


def _dispatch_body(dst_ref, tch_ref, tn_ref, ts_ref, nu_ref, h_ref, routet_ref, xb_ref,
                   buf, zbuf, sem, zsem):
    i = pl.program_id(0)
    n_tiles = pl.num_programs(0)
    tm = h_ref.shape[0]
    lbuf = buf.shape[1]
    n_blocks = xb_ref.shape[0] // MOE_BLK
    slot = i % 2

    def aligned(r):
        return pl.ds(pl.multiple_of(r, ROW_CHUNK), ROW_CHUNK)

    @pl.when(i == 0)
    def _():
        zbuf[...] = jnp.zeros_like(zbuf)

        def zero_chunk(r):
            return pltpu.make_async_copy(zbuf.at[pl.ds(0, ROW_CHUNK)], xb_ref.at[aligned(r)], zsem)

        def zero_block(j):
            return pltpu.make_async_copy(zbuf, xb_ref.at[pl.ds(pl.multiple_of(j * MOE_BLK, MOE_BLK), MOE_BLK)], zsem)

        def per_expert(e, total):
            def per_chunk(k, c):
                zero_chunk(ts_ref[e] + k * ROW_CHUNK).start()
                return c

            lax.fori_loop(0, tn_ref[e], per_chunk, 0)
            return total + tn_ref[e]

        total = lax.fori_loop(0, N_EXPERTS, per_expert, 0)

        def start_block(j, c):
            zero_block(j).start()
            return c

        lax.fori_loop(nu_ref[0], n_blocks, start_block, 0)

        def wait_chunk(k, c):
            zero_chunk(0).wait()
            return c

        lax.fori_loop(0, total, wait_chunk, 0)

        def wait_block(j, c):
            zero_block(0).wait()
            return c

        lax.fori_loop(nu_ref[0], n_blocks, wait_block, 0)

    l0t = routet_ref[0, 0:1, :].astype(I32)
    l1t = routet_ref[0, 1:2, :].astype(I32)
    row = lax.broadcasted_iota(I32, (lbuf, tm), 0)
    perm = jnp.where(row == l0t, 1.0, jnp.where(row == l1t, 1.0, 0.0)).astype(BF16)
    srt = _dot(perm, h_ref[...]).astype(BF16)

    def copy_from(s):
        def make_copy(src_row, dst_row, rows):
            return pltpu.make_async_copy(buf.at[s, pl.ds(pl.multiple_of(src_row, ROW_CHUNK), rows)],
                                         xb_ref.at[pl.ds(pl.multiple_of(dst_row, ROW_CHUNK), rows)], sem.at[s])
        return make_copy

    @pl.when(i >= 2)
    def _():
        _wait_chunks(tch_ref[i - 2], copy_from(slot))

    buf[slot] = srt
    _start_chunks(i, tch_ref, dst_ref, copy_from(slot))

    @pl.when(i == n_tiles - 1)
    def _():
        @pl.when(i >= 1)
        def _():
            _wait_chunks(tch_ref[i - 1], copy_from(1 - slot))

        _wait_chunks(tch_ref[i], copy_from(slot))


def _dispatch(chunk_dst, tile_nch, tail_n, tail_s, n_used, h, route_t, n_slots):
    n_tok, d = h.shape
    tm = TM_MIX
    grid_spec = pltpu.PrefetchScalarGridSpec(
        num_scalar_prefetch=5,
        grid=(n_tok // tm,),
        in_specs=[
            pl.BlockSpec((tm, d), lambda i, *_: (i, 0)),
            pl.BlockSpec((1, SUBLANES, tm), lambda i, *_: (i, 0, 0)),
        ],
        out_specs=pl.BlockSpec(memory_space=pl.ANY),
        scratch_shapes=[pltpu.VMEM((2, LOCAL_ROWS, d), BF16), pltpu.VMEM((MOE_BLK, d), BF16),
                        pltpu.SemaphoreType.DMA((2,)), pltpu.SemaphoreType.DMA(())],
    )
    return pl.pallas_call(
        _dispatch_body,
        grid_spec=grid_spec,
        out_shape=jax.ShapeDtypeStruct((n_slots, d), BF16),
        compiler_params=_params(("arbitrary",)),
        name="moe_dispatch",
    )(chunk_dst, tile_nch, tail_n, tail_s, n_used, h, route_t)


def _expert_body(be_ref, fl_ref, nu_ref, xb_ref, w1_ref, w3_ref, w2_ref, yb_ref, w1_sc, w3_sc, w2_sc):
    del be_ref
    j = pl.program_id(0)
    used = j < nu_ref[0]

    @pl.when(jnp.logical_and(used, fl_ref[j] == 1))
    def _():
        w1_sc[...] = w1_ref[...].astype(BF16)
        w3_sc[...] = w3_ref[...].astype(BF16)
        w2_sc[...] = w2_ref[...].astype(BF16)

    @pl.when(used)
    def _():
        xb = xb_ref[...]
        a = _dot(xb, w1_sc[...])
        b = _dot(xb, w3_sc[...])
        act = (a * _sigmoid(a) * b).astype(BF16)
        yb_ref[...] = _dot(act, w2_sc[...]).astype(BF16)

    @pl.when(jnp.logical_not(used))
    def _():
        yb_ref[...] = jnp.zeros_like(yb_ref)


def _experts(block_e, first_flag, n_used, xb, w1, w3, w2):
    n_slots, d = xb.shape
    blk = MOE_BLK
    de = w1.shape[2]
    grid_spec = pltpu.PrefetchScalarGridSpec(
        num_scalar_prefetch=3,
        grid=(n_slots // blk,),
        in_specs=[
            pl.BlockSpec((blk, d), lambda j, be, fl, nu: (j, 0)),
            pl.BlockSpec((None, d, de), lambda j, be, fl, nu: (be[j], 0, 0)),
            pl.BlockSpec((None, d, de), lambda j, be, fl, nu: (be[j], 0, 0)),
            pl.BlockSpec((None, de, d), lambda j, be, fl, nu: (be[j], 0, 0)),
        ],
        out_specs=pl.BlockSpec((blk, d), lambda j, be, fl, nu: (j, 0)),
        scratch_shapes=[pltpu.VMEM((d, de), BF16), pltpu.VMEM((d, de), BF16), pltpu.VMEM((de, d), BF16)],
    )
    return pl.pallas_call(
        _expert_body,
        grid_spec=grid_spec,
        out_shape=jax.ShapeDtypeStruct((n_slots, d), BF16),
        compiler_params=_params(("arbitrary",)),
        name="moe_experts",
    )(block_e, first_flag, n_used, xb, w1, w3, w2)


def _combine_body(dst_ref, tch_ref, xmid_ref, route_ref, g_ref, yb_ref, o_ref, buf, sem):
    i = pl.program_id(0)
    n_tiles = pl.num_programs(0)
    tm = xmid_ref.shape[0]
    lbuf = buf.shape[1]
    slot = i % 2

    def copy_into(s):
        def make_copy(local_row, slot_row, rows):
            return pltpu.make_async_copy(yb_ref.at[pl.ds(pl.multiple_of(slot_row, ROW_CHUNK), rows)],
                                         buf.at[s, pl.ds(pl.multiple_of(local_row, ROW_CHUNK), rows)], sem.at[s])
        return make_copy

    @pl.when(i == 0)
    def _():
        buf[...] = jnp.zeros_like(buf)
        _start_chunks(i, tch_ref, dst_ref, copy_into(slot))

    @pl.when(i + 1 < n_tiles)
    def _():
        _start_chunks(i + 1, tch_ref, dst_ref, copy_into(1 - slot))

    route = route_ref[...]
    l0 = route[:, 0:1].astype(I32)
    l1 = route[:, 1:2].astype(I32)
    lane = lax.broadcasted_iota(I32, (tm, lbuf), 1)
    gmat = (jnp.where(lane == l0, route[:, 2:3], 0.0) + jnp.where(lane == l1, route[:, 3:4], 0.0)).astype(BF16)
    _wait_chunks(tch_ref[i], copy_into(slot))
    y = xmid_ref[...] + _dot(gmat, buf[slot])
    o_ref[...] = _rms(y, g_ref[...])


def _combine(chunk_dst, tile_nch, xmid, route, g, yb):
    n_tok, d = xmid.shape
    tm = TM_MIX
    grid_spec = pltpu.PrefetchScalarGridSpec(
        num_scalar_prefetch=2,
        grid=(n_tok // tm,),
        in_specs=[
            pl.BlockSpec((tm, d), lambda i, *_: (i, 0)),
            pl.BlockSpec((tm, SUBLANES), lambda i, *_: (i, 0)),
            pl.BlockSpec((1, d), lambda i, *_: (0, 0)),
            pl.BlockSpec(memory_space=pl.ANY),
        ],
        out_specs=pl.BlockSpec((tm, d), lambda i, *_: (i, 0)),
        scratch_shapes=[pltpu.VMEM((2, LOCAL_ROWS, d), BF16), pltpu.SemaphoreType.DMA((2,))],
    )
    return pl.pallas_call(
        _combine_body,
        grid_spec=grid_spec,
        out_shape=jax.ShapeDtypeStruct((n_tok, d), F32),
        compiler_params=_params(("arbitrary",)),
        name="moe_combine",
    )(chunk_dst, tile_nch, xmid, route, g, yb)


def _lag_reversed(a):
    return jnp.concatenate([jnp.zeros_like(a[..., :1]), a[..., 1:][..., ::-1]], axis=-1)


def _filter_features(seq):
    pos = jnp.arange(seq, dtype=F32)
    t = jnp.linspace(0.0, 1.0, seq, dtype=F32)
    w = (2.0 * math.pi / seq) * pos
    bands = jnp.linspace(1e-4, HY_BANDS - 1, HY_BANDS, dtype=F32)
    ang = w[:, None] * bands[None, :]
    feats = jnp.concatenate([t[:, None], jnp.cos(ang), jnp.sin(ang)], axis=-1)
    featst = jnp.pad(feats, ((0, 0), (0, HY_EMB_PAD - HY_EMB))).T[None]
    return featst, t[None, :]


def _gate_weights(wr, wi):
    half = D_LRU // 2
    hph = half // LRU_HEAD_DIM
    eye = jnp.eye(hph, dtype=wr.dtype)[:, None, :, None]

    def block_diag(w):
        return (eye * w[:, :, None, :]).reshape(half, half)

    out = [jnp.concatenate([block_diag(wr[hh * hph:(hh + 1) * hph]), block_diag(wi[hh * hph:(hh + 1) * hph])], axis=1)
           for hh in range(2)]
    return jnp.stack(out, axis=0).astype(BF16)


def _layer(x, l, p):
    bsz, seq, d = x.shape
    n_tok = bsz * seq
    row = lambda a: a.reshape(1, -1).astype(F32)
    col = lambda a: a.reshape(-1, 1).astype(F32)

    w_in = p["w_in"][l]
    wl = w_in[:, :2 * D_LRU].astype(BF16)
    wht = w_in[:, 2 * D_LRU:].T.astype(BF16)
    lru, hy5 = _inproj(x, row(p["norm_mix_g"][l]), wl, wht)

    wg = jnp.stack([_gate_weights(p["lru_wr_f"][l], p["lru_wi_f"][l]),
                    _gate_weights(p["lru_wr_b"][l], p["lru_wi_b"][l])], axis=0)
    stack2 = lambda a, b: jnp.stack([a.reshape(1, -1), b.reshape(1, -1)], axis=0).astype(F32)
    h4 = _lru(lru, p["lru_conv_w"][l], row(p["lru_conv_b"][l]), wg,
              stack2(p["lru_br_f"][l], p["lru_br_b"][l]), stack2(p["lru_bi_f"][l], p["lru_bi_b"][l]),
              stack2(p["lru_lambda_f"][l], p["lru_lambda_b"][l]))

    tables = _fft_tables()
    featst, t_f = _filter_features(seq)
    t_b = _lag_reversed(t_f)
    w1t = jnp.pad(p["hy_filt_w1"][l].T, ((0, 0), (0, HY_EMB_PAD - HY_EMB)))
    z2t = _filt_mlp(featst, w1t, col(p["hy_filt_b1"][l]), col(p["hy_filt_freq1"][l]),
                    p["hy_filt_w2"][l].T, col(p["hy_filt_b2"][l]), col(p["hy_filt_freq2"][l]))
    z2t = jnp.concatenate([z2t, _lag_reversed(z2t)], axis=0)
    w3 = p["hy_filt_w3"][l]
    nfr = HY_ORDER * D_HY
    deltas = jnp.abs(jnp.linspace(math.log(HY_DECAY_TARGET) / HY_SLOW_DECAY_PCT,
                                  math.log(HY_DECAY_TARGET) / HY_FAST_DECAY_PCT, D_HY, dtype=F32))
    ktile = _filt_time(z2t, w3[:, :nfr].T, w3[:, nfr:].T, col(jnp.tile(deltas, HY_ORDER)), t_f, t_b)
    kfr, kfi = _filt_fft(ktile, tables)
    cwt = jnp.concatenate([p["hy_conv_w"][l], p["hy_conv_b"][l][None, :]], axis=0)
    cwt = jnp.broadcast_to(cwt.reshape(HY_CONV + 1, -1, SUBLANES, 1), (HY_CONV + 1, cwt.shape[1] // SUBLANES, SUBLANES, LANES))
    yhy5 = _hyena(hy5, cwt, p["hy_skip"][l], kfr, kfi, tables)

    w_out = p["w_out"][l]
    wr_cat = jnp.concatenate([p["router_group_w"][l], p["router_expert_w"][l]], axis=1)
    wr_cat = jnp.pad(wr_cat, ((0, 0), (0, LANES - wr_cat.shape[1])))
    wrh, wrl = _split_bf16(wr_cat)
    rb = jnp.concatenate([p["router_group_b"][l], p["router_expert_b"][l]])
    rb = jnp.pad(rb, (0, LANES - rb.shape[0])).reshape(1, LANES)
    earlier = lambda n: (jnp.arange(n)[:, None] < jnp.arange(n)[None, :]).astype(BF16)
    xmid, h, route, route_t, tile_cnt = _mix_router(
        h4, lru, yhy5, x, row(p["grp_norm_lru_g"][l]), col(p["grp_norm_hy_g"][l]),
        w_out[:D_LRU].astype(BF16), w_out[D_LRU:].astype(BF16), row(p["norm_ffn_g"][l]), wrh, wrl, rb,
        earlier(TM_MIX), earlier(ROUTE_ROWS).T)

    blk = MOE_BLK
    ch = ROW_CHUNK
    n_tiles = n_tok // TM_MIX
    n_blocks = (2 * n_tok + n_tiles * N_EXPERTS * (ch - 1)) // blk + N_EXPERTS
    cnt = tile_cnt[:, N_GROUPS:N_GROUPS + N_EXPERTS, 0].astype(I32)
    run = (cnt + ch - 1) // ch * ch
    tot_e = jnp.sum(run, axis=0)
    padded_e = (tot_e + blk - 1) // blk * blk
    pad_end = jnp.cumsum(padded_e)
    pad_start = pad_end - padded_e
    gstart = pad_start[None, :] + jnp.cumsum(run, axis=0) - run
    lstart = jnp.cumsum(run, axis=1) - run
    flat = lambda a: a.reshape(-1).astype(I32)
    tile_nch = flat(jnp.sum(run, axis=1) // ch)
    nch_end = jnp.cumsum(run // ch, axis=1)
    k = jnp.arange(MAX_CHUNKS, dtype=I32)
    base = gstart - lstart
    step = base[:, 1:] - base[:, :-1]
    passed = (k[None, :, None] >= nch_end[:, None, :-1]).astype(I32)
    chunk_dst = flat(base[:, :1] + jnp.sum(passed * step[:, None, :], axis=2) + k[None, :] * ch)
    tail_n = flat((padded_e - tot_e) // ch)
    tail_s = flat(pad_start + tot_e)
    block_start = jnp.arange(n_blocks, dtype=I32) * blk
    block_e = jnp.minimum(jnp.sum(block_start[:, None] >= pad_end[None, :], axis=1), N_EXPERTS - 1).astype(I32)
    first_flag = jnp.concatenate([jnp.ones((1,), I32), (block_e[1:] != block_e[:-1]).astype(I32)])
    n_used = (pad_end[-1:] // blk).astype(I32)

    xb = _dispatch(chunk_dst, tile_nch, tail_n, tail_s, n_used, h, route_t, n_blocks * blk)
    yb = _experts(block_e, first_flag, n_used, xb, p["exp_w1"][l], p["exp_w3"][l], p["exp_w2"][l])
    return xmid.reshape(n_tok, d), route, (chunk_dst, tile_nch), yb


def kernel(x, norm_mix_g, w_in, lru_conv_w, lru_conv_b, lru_wr_f, lru_br_f, lru_wi_f, lru_bi_f, lru_lambda_f, lru_wr_b, lru_br_b, lru_wi_b, lru_bi_b, lru_lambda_b, hy_conv_w, hy_conv_b, hy_filt_w1, hy_filt_b1, hy_filt_freq1, hy_filt_w2, hy_filt_b2, hy_filt_freq2, hy_filt_w3, hy_skip, grp_norm_lru_g, grp_norm_hy_g, w_out, norm_ffn_g, router_group_w, router_group_b, router_expert_w, router_expert_b, exp_w1, exp_w3, exp_w2, norm_final_g):
    p = dict(locals())
    bsz, seq, d = x.shape
    assert d == D_MODEL and 2 * seq == FFT_R * FFT_R and w_in.shape[0] == 1
    xmid, route, (chunk_dst, tile_nch), yb = _layer(x, 0, p)
    out = _combine(chunk_dst, tile_nch, xmid, route, norm_final_g.reshape(1, d), yb)
    return out.reshape(bsz, seq, d)
```

```python
import functools
import math

import numpy as np
import jax
import jax.numpy as jnp
from jax import lax
from jax.experimental import pallas as pl
from jax.experimental.pallas import tpu as pltpu

F32 = jnp.float32
BF16 = jnp.bfloat16
I32 = jnp.int32

D_MODEL = 1024
D_LRU = 512
D_HY = 512
LRU_HEADS = 8
LRU_HEAD_DIM = D_LRU // LRU_HEADS
LRU_CONV = 4
LRU_C = 8.0
HY_ORDER = 2
HY_CONV = 3
HY_BANDS = 16
HY_EMB = 2 * HY_BANDS + 1
HY_EMB_PAD = 40
HY_FFN = 64
HY_FAST_DECAY_PCT = 0.3
HY_SLOW_DECAY_PCT = 1.5
HY_DECAY_TARGET = 1e-2
N_GROUPS = 4
EXPERTS_PER_GROUP = 8
N_EXPERTS = N_GROUPS * EXPERTS_PER_GROUP
D_EXPERT = D_MODEL // 2
EPS = 1e-6

LANES = 128
SUBLANES = 8
FFT_R = 128
VMEM_LIMIT = 56 * 1024 * 1024

TM_IN = 1024
TC_LRU = 1024
RC_HY = 32
RF_KF = 16
HY_SPLIT = 2
RB_FILT = 64
TM_MIX = 512
MOE_BLK = 512
WEIGHT_LOOKAHEAD = 2
WEIGHT_SLOTS = WEIGHT_LOOKAHEAD + 1
ROW_CHUNK = 16
LOCAL_ROWS = -(-(2 * TM_MIX + N_EXPERTS * (ROW_CHUNK - 1)) // LANES) * LANES
MAX_CHUNKS = LOCAL_ROWS // ROW_CHUNK
ROUTE_ROWS = -(-(N_GROUPS + N_EXPERTS) // SUBLANES) * SUBLANES
NEG_BIG = -1e30


def _params(sem, vmem=VMEM_LIMIT):
    return pltpu.CompilerParams(dimension_semantics=sem, vmem_limit_bytes=vmem)


def _rms(x, g):
    return x * lax.rsqrt(jnp.mean(x * x, axis=-1, keepdims=True) + EPS) * g


def _sigmoid(x):
    return 1.0 / (1.0 + jnp.exp(-x))


def _split_bf16(a):
    hi = a.astype(BF16)
    lo = (a - hi.astype(F32)).astype(BF16)
    return hi, lo


def _dot(a, b):
    return jnp.dot(a, b, preferred_element_type=F32)


def _dot3(a, b):
    ah, al = _split_bf16(a)
    bh, bl = _split_bf16(b)
    return _dot(ah, bh) + _dot(al, bh) + _dot(ah, bl)


def _inproj_body(x_ref, g_ref, wl_ref, wht_ref, lru_ref, hy_ref):
    hn = _rms(x_ref[0], g_ref[...]).astype(BF16)
    lru_ref[0] = _dot(hn, wl_ref[...])
    hyt = lax.dot_general(wht_ref[...], hn, (((1,), (1,)), ((), ())), preferred_element_type=F32)
    nrow = hyt.shape[0] // SUBLANES
    for j in range(hyt.shape[1] // LANES):
        hy_ref[0, :, j, :, :] = hyt[:, LANES * j:LANES * (j + 1)].reshape(nrow, SUBLANES, LANES)


def _inproj(x, g, wl, wht):
    bsz, seq, d = x.shape
    nl = wl.shape[1]
    nh = wht.shape[0]
    tm = TM_IN
    return pl.pallas_call(
        _inproj_body,
        grid=(bsz, seq // tm),
        in_specs=[
            pl.BlockSpec((1, tm, d), lambda b, i: (b, i, 0)),
            pl.BlockSpec((1, d), lambda b, i: (0, 0)),
            pl.BlockSpec((d, nl), lambda b, i: (0, 0)),
            pl.BlockSpec((nh, d), lambda b, i: (0, 0)),
        ],
        out_specs=[
            pl.BlockSpec((1, tm, nl), lambda b, i: (b, i, 0)),
            pl.BlockSpec((1, nh // SUBLANES, tm // LANES, SUBLANES, LANES), lambda b, i: (b, 0, i, 0, 0)),
        ],
        out_shape=[
            jax.ShapeDtypeStruct((bsz, seq, nl), F32),
            jax.ShapeDtypeStruct((bsz, nh // SUBLANES, seq // LANES, SUBLANES, LANES), F32),
        ],
        compiler_params=_params(("parallel", "parallel")),
        name="inproj",
    )(x, g, wl, wht)


def _sqrt_one_minus_sq(a, neg_log_a):
    om = jnp.maximum(jnp.tanh(neg_log_a) * (1.0 + a * a), 1e-30)
    return om * lax.rsqrt(om)


def _group_scan(a3, b3, reverse):
    sub = lax.broadcasted_iota(I32, a3.shape, 1)
    for s in (1, 2, 4):
        if reverse:
            a_sh = pltpu.roll(a3, SUBLANES - s, axis=1)
            b_sh = pltpu.roll(b3, SUBLANES - s, axis=1)
            m = sub < SUBLANES - s
        else:
            a_sh = pltpu.roll(a3, s, axis=1)
            b_sh = pltpu.roll(b3, s, axis=1)
            m = sub >= s
        b3 = jnp.where(m, a3 * b_sh + b3, b3)
        a3 = jnp.where(m, a3 * a_sh, a3)
    return a3, b3


def _lru_body(cur_ref, prev_ref, next_ref, cw_ref, cb_ref, wg_ref, br_ref, bi_ref, lam_ref, o_ref,
              carry_sc, *, nt, tc):
    d = pl.program_id(1)
    i = pl.program_id(2)
    c = jnp.where(d == 0, i, nt - 1 - i)
    ns = SUBLANES
    ng = tc // ns
    half = D_LRU // 2

    @pl.when(i == 0)
    def _():
        carry_sc[...] = jnp.zeros_like(carry_sc)

    xs = jnp.swapaxes(cur_ref[0].reshape(ns, ng, D_LRU), 0, 1)
    prev = jnp.where(c > 0, prev_ref[0], 0.0)
    nxt = jnp.where(c < nt - 1, next_ref[0], 0.0)
    sub = lax.broadcasted_iota(I32, (ns, D_LRU), 0)

    def prev_segment(slab, halo_row):
        return jnp.where(sub == 0, halo_row, pltpu.roll(slab, 1, axis=0))

    def next_segment(slab, halo_row):
        return jnp.where(sub == ns - 1, halo_row, pltpu.roll(slab, ns - 1, axis=0))

    xext = jnp.concatenate([prev_segment(xs[ng - 2], prev[ns - 2:ns - 1])[None],
                            prev_segment(xs[ng - 1], prev[ns - 1:ns])[None], xs,
                            next_segment(xs[0], nxt[0:1])[None]], axis=0)
    cw = cw_ref[...]
    xc = cb_ref[...] + cw[0:1] * xext[0:ng]
    for k in range(1, LRU_CONV):
        xc = xc + cw[k:k + 1] * xext[k:k + ng]
    xc = xc.reshape(tc, D_LRU)

    lam = lam_ref[0]
    nlam = -lam
    softplus = jnp.maximum(nlam, 0.0) + jnp.log1p(jnp.exp(-jnp.abs(nlam)))
    half_c = (0.5 * LRU_C) * softplus

    def gates(hh):
        sl = slice(half * hh, half * (hh + 1))
        xh = xc[:, sl]
        logits = _dot(xh.astype(BF16), wg_ref[0, hh])
        tr_ = jnp.tanh(0.5 * (logits[:, :half] + br_ref[0][:, sl]))
        gi = 0.5 + 0.5 * jnp.tanh(0.5 * (logits[:, half:] + bi_ref[0][:, sl]))
        neg_log_a = half_c[:, sl] * (1.0 + tr_)
        a = jnp.exp(-neg_log_a)
        b = _sqrt_one_minus_sq(a, neg_log_a) * (gi * xh)
        return a.reshape(ng, ns, half), b.reshape(ng, ns, half)

    def run(reverse):
        subh = lax.broadcasted_iota(I32, (ns, half), 0)
        for hh in range(2):
            sl = slice(half * hh, half * (hh + 1))
            a3, b3 = gates(hh)
            h = jnp.zeros((ns, half), F32)
            p = jnp.ones((ns, half), F32)
            hs = [None] * ng
            ps = [None] * ng
            for g in (range(ng - 1, -1, -1) if reverse else range(ng)):
                h = a3[g] * h + b3[g]
                p = a3[g] * p
                hs[g] = h
                ps[g] = p
            ac, hc = _group_scan(p[None], h[None], reverse)
            c_in = carry_sc[:, sl]
            end = ac[0] * c_in + hc[0]
            if reverse:
                seg_in = jnp.where(subh == ns - 1, c_in, pltpu.roll(end, ns - 1, axis=0))
                carry_sc[:, sl] = jnp.broadcast_to(end[0:1], end.shape)
            else:
                seg_in = jnp.where(subh == 0, c_in, pltpu.roll(end, 1, axis=0))
                carry_sc[:, sl] = jnp.broadcast_to(end[ns - 1:ns], end.shape)
            h3 = jnp.stack(hs, axis=0) + jnp.stack(ps, axis=0) * seg_in[None]
            o_ref[0, 0, :, sl] = jnp.swapaxes(h3, 0, 1).reshape(tc, half)

    @pl.when(d == 0)
    def _():
        run(False)

    @pl.when(d == 1)
    def _():
        run(True)


def _lru(lru, cw, cb, wg, br, bi, lam):
    bsz, seq, _ = lru.shape
    tc = TC_LRU
    nt = seq // tc
    r8 = tc // SUBLANES
    nrow8 = seq // SUBLANES

    def cidx(d, i):
        return jnp.where(d == 0, i, nt - 1 - i)

    return pl.pallas_call(
        functools.partial(_lru_body, nt=nt, tc=tc),
        grid=(bsz, 2, nt),
        in_specs=[
            pl.BlockSpec((1, tc, D_LRU), lambda b, d, i: (b, cidx(d, i), 0)),
            pl.BlockSpec((1, SUBLANES, D_LRU), lambda b, d, i: (b, jnp.maximum(cidx(d, i) * r8 - 1, 0), 0)),
            pl.BlockSpec((1, SUBLANES, D_LRU), lambda b, d, i: (b, jnp.minimum((cidx(d, i) + 1) * r8, nrow8 - 1), 0)),
            pl.BlockSpec((LRU_CONV, D_LRU), lambda b, d, i: (0, 0)),
            pl.BlockSpec((1, D_LRU), lambda b, d, i: (0, 0)),
            pl.BlockSpec((1, 2, D_LRU // 2, D_LRU), lambda b, d, i: (d, 0, 0, 0)),
            pl.BlockSpec((1, 1, D_LRU), lambda b, d, i: (d, 0, 0)),
            pl.BlockSpec((1, 1, D_LRU), lambda b, d, i: (d, 0, 0)),
            pl.BlockSpec((1, 1, D_LRU), lambda b, d, i: (d, 0, 0)),
        ],
        out_specs=pl.BlockSpec((1, 1, tc, D_LRU), lambda b, d, i: (d, b, cidx(d, i), 0)),
        out_shape=jax.ShapeDtypeStruct((2, bsz, seq, D_LRU), F32),
        scratch_shapes=[pltpu.VMEM((SUBLANES, D_LRU), F32)],
        compiler_params=_params(("parallel", "arbitrary", "arbitrary")),
        name="rg_lru",
    )(lru, lru, lru, cw, cb, wg, br, bi, lam)


def _fft_tables():
    r = FFT_R
    n = r * r
    idx = np.arange(r, dtype=np.float64)
    ang = 2.0 * np.pi * np.outer(idx, idx) / r
    wr, wi = np.cos(ang), -np.sin(ang)
    angt = 2.0 * np.pi * np.outer(idx, idx) / n
    tr, ti = np.cos(angt), -np.sin(angt)
    h = r // 2
    f1 = np.concatenate([wr, wi], axis=1)
    f1p = np.block([[wr[:h], wi[:h]], [-wi[:h], wr[:h]]])
    f2 = np.block([[wr, wi], [-wi, wr]])
    f2c = np.block([[wr, -wi], [wi, wr]])
    f3p = np.block([[wr[:, :h], -wi[:, :h]], [wi[:, :h], wr[:, :h]]]) / n
    as_bf16 = lambda a: jnp.asarray(a, F32).astype(BF16)
    return dict(f1=as_bf16(f1), f1p=as_bf16(f1p), f2=as_bf16(f2), f2c=as_bf16(f2c), f3p=as_bf16(f3p),
                tr=jnp.asarray(tr, F32), ti=jnp.asarray(ti, F32), tr16=as_bf16(tr), ti16=as_bf16(ti))


def _const_specs(arrays, nargs):
    return [pl.BlockSpec(a.shape, (lambda nd: (lambda *_: (0,) * nd))(a.ndim)) for a in arrays]


def _fft_fwd(x3, f1, f2, tr, ti):
    g = x3.shape[0]
    r = FFT_R
    wd = tr.dtype
    xt = jnp.swapaxes(x3.astype(wd), 1, 2).reshape(g * r, r)
    a = _dot(xt.astype(BF16), f1)
    ar = a[:, :r].astype(wd).reshape(g, r, r)
    ai = a[:, r:].astype(wd).reshape(g, r, r)
    pr = ar * tr - ai * ti
    pi = ar * ti + ai * tr
    pt = jnp.concatenate([jnp.swapaxes(pr, 1, 2), jnp.swapaxes(pi, 1, 2)], axis=-1).reshape(g * r, 2 * r)
    x = _dot(pt.astype(BF16), f2)
    return x[:, :r], x[:, r:]


def _fft_inv(yr, yi, f2c, f3, tr, ti):
    r = FFT_R
    g = yr.shape[0] // r
    wd = tr.dtype
    b = _dot(jnp.concatenate([yr, yi], axis=-1).astype(BF16), f2c)
    br = b[:, :r].astype(wd).reshape(g, r, r)
    bi = b[:, r:].astype(wd).reshape(g, r, r)
    cr = br * tr + bi * ti
    ci = bi * tr - br * ti
    ct = jnp.concatenate([jnp.swapaxes(cr, 1, 2), jnp.swapaxes(ci, 1, 2)], axis=-1).reshape(g * r, 2 * r)
    yt = _dot(ct.astype(BF16), f3)
    return jnp.swapaxes(yt.reshape(g, r, r), 1, 2)


def _filt_mlp_body(ft_ref, w1_ref, b1_ref, f1_ref, w2_ref, b2_ref, f2_ref, o_ref):
    z = jnp.sin(f1_ref[...] * (_dot3(w1_ref[...], ft_ref[0]) + b1_ref[...]))
    o_ref[0] = jnp.sin(f2_ref[...] * (_dot3(w2_ref[...], z) + b2_ref[...]))


def _filt_mlp(featst, w1t, b1, f1, w2t, b2, f2):
    nd, ke, seq = featst.shape
    col = lambda d: (0, 0)
    return pl.pallas_call(
        _filt_mlp_body,
        grid=(nd,),
        in_specs=[
            pl.BlockSpec((1, ke, seq), lambda d: (d, 0, 0)),
            pl.BlockSpec((HY_FFN, ke), col),
            pl.BlockSpec((HY_FFN, 1), col),
            pl.BlockSpec((HY_FFN, 1), col),
            pl.BlockSpec((HY_FFN, HY_FFN), col),
            pl.BlockSpec((HY_FFN, 1), col),
            pl.BlockSpec((HY_FFN, 1), col),
        ],
        out_specs=pl.BlockSpec((1, HY_FFN, seq), lambda d: (d, 0, 0)),
        out_shape=jax.ShapeDtypeStruct((nd, HY_FFN, seq), F32),
        compiler_params=_params(("parallel",)),
        name="hyena_filter_mlp",
    )(featst, w1t, b1, f1, w2t, b2, f2)


def _filt_time_body(z_ref, wf_ref, wb_ref, dl_ref, tf_ref, o_ref):
    seq = z_ref.shape[2]
    decay = jnp.exp(-tf_ref[...] * dl_ref[...])
    hf = _dot3(wf_ref[...], z_ref[0]) * decay
    hb = _dot3(wb_ref[...], z_ref[0]) * decay
    lane = lax.broadcasted_iota(I32, hb.shape, 1)
    hb = jnp.where(lane == 0, 0.0, hb)
    norm = jnp.sum(jnp.abs(hf), axis=-1, keepdims=True) + jnp.sum(jnp.abs(hb), axis=-1, keepdims=True) + EPS
    hf = hf / norm
    hb = hb / norm
    nrow = hf.shape[0] // SUBLANES
    nch = seq // LANES
    lane1 = lax.broadcasted_iota(I32, (hf.shape[0], LANES), 1)
    mirror = (LANES - lane1) % LANES
    for j in range(nch):
        o_ref[:, j, :, :] = hf[:, LANES * j:LANES * (j + 1)].reshape(nrow, SUBLANES, LANES)
        src = nch - 1 - j
        back = jnp.take_along_axis(hb[:, LANES * src:LANES * (src + 1)], mirror, axis=1)
        first = hb[:, LANES * (src + 1):LANES * (src + 1) + 1] if j > 0 else 0.0
        o_ref[:, nch + j, :, :] = jnp.where(lane1 == 0, first, back).reshape(nrow, SUBLANES, LANES)


def _filt_time(z2t, w3f, w3b, delta_rows, t_f):
    nrows = w3f.shape[0]
    seq = z2t.shape[2]
    rb = RB_FILT
    return pl.pallas_call(
        _filt_time_body,
        grid=(nrows // rb,),
        in_specs=[
            pl.BlockSpec((1, HY_FFN, seq), lambda r: (0, 0, 0)),
            pl.BlockSpec((rb, HY_FFN), lambda r: (r, 0)),
            pl.BlockSpec((rb, HY_FFN), lambda r: (r, 0)),
            pl.BlockSpec((rb, 1), lambda r: (r, 0)),
            pl.BlockSpec((1, seq), lambda r: (0, 0)),
        ],
        out_specs=pl.BlockSpec((rb // SUBLANES, 2 * seq // LANES, SUBLANES, LANES), lambda r: (r, 0, 0, 0)),
        out_shape=jax.ShapeDtypeStruct((nrows // SUBLANES, 2 * seq // LANES, SUBLANES, LANES), F32),
        compiler_params=_params(("parallel",)),
        name="hyena_filter_time",
    )(z2t, w3f, w3b, delta_rows, t_f)


def _filt_fft_body(k_ref, f1_ref, f2_ref, tr_ref, ti_ref, kr_ref, ki_ref):
    tr = tr_ref[...]
    ti = ti_ref[...]
    for g in range(k_ref.shape[0]):
        x3 = jnp.swapaxes(k_ref[g], 0, 1)
        xr, xi = _fft_fwd(x3, f1_ref[...], f2_ref[...], tr, ti)
        kr_ref[SUBLANES * g:SUBLANES * (g + 1)] = xr.astype(BF16).reshape(SUBLANES, FFT_R, FFT_R)
        ki_ref[SUBLANES * g:SUBLANES * (g + 1)] = xi.astype(BF16).reshape(SUBLANES, FFT_R, FFT_R)


def _filt_fft(ktile, tables):
    n8, r, _, _ = ktile.shape
    nrows = n8 * SUBLANES
    rf = RF_KF
    out_spec = pl.BlockSpec((rf, r, r), lambda i: (i, 0, 0))
    consts = [tables[k] for k in ("f1", "f2", "tr16", "ti16")]
    return pl.pallas_call(
        _filt_fft_body,
        grid=(nrows // rf,),
        in_specs=[pl.BlockSpec((rf // SUBLANES, r, SUBLANES, LANES), lambda i: (i, 0, 0, 0))] + _const_specs(consts, 1),
        out_specs=[out_spec, out_spec],
        out_shape=[jax.ShapeDtypeStruct((nrows, r, r), BF16)] * 2,
        compiler_params=_params(("parallel",)),
        name="hyena_filter_fft",
    )(ktile, *consts)


def _conv3_time(x, w):
    lane = lax.broadcasted_iota(I32, x.shape, 2)
    zero = jnp.zeros_like(x[:1])
    xm = pltpu.roll(x, 1, axis=2)
    xm = jnp.where(lane == 0, jnp.concatenate([zero, xm[:-1]], axis=0), xm)
    xp = pltpu.roll(x, LANES - 1, axis=2)
    xp = jnp.where(lane == LANES - 1, jnp.concatenate([xp[1:], zero], axis=0), xp)
    return w[0] * xm + w[1] * x + w[2] * xp + w[3]


def _hyena_body(sk_ref, v_ref, x1_ref, x2_ref, wv_ref, w1_ref, w2_ref, k0r_ref, k0i_ref, k1r_ref, k1i_ref,
                f1p_ref, f2_ref, f2c_ref, f3p_ref, tr_ref, ti_ref, o_ref, *, rc):
    ct = pl.program_id(0)
    tr = tr_ref[...]
    ti = ti_ref[...]
    np_ = v_ref.shape[2]
    nb = v_ref.shape[0]

    def load(ref, w_ref, g):
        parts = [jnp.swapaxes(_conv3_time(ref[bb, g], w_ref[:, g]), 0, 1) for bb in range(nb)]
        return jnp.concatenate(parts, axis=1)

    def long_conv(z3, kr_ref, ki_ref, g):
        r = FFT_R
        gsub = SUBLANES // HY_SPLIT
        wd = tr.dtype

        def split(m):
            return m[:, :r].astype(wd).reshape(gsub, r, r), m[:, r:].astype(wd).reshape(gsub, r, r)

        def join_t(re, im):
            return jnp.concatenate([jnp.swapaxes(re, 1, 2), jnp.swapaxes(im, 1, 2)], axis=-1).reshape(
                gsub * r, 2 * r).astype(BF16)

        subs = range(HY_SPLIT)
        zs = [z3[s * gsub:(s + 1) * gsub] for s in subs]
        a = [_dot(jnp.swapaxes(z.astype(wd), 1, 2).reshape(gsub * r, r).astype(BF16), f1p_ref[...]) for z in zs]
        p = []
        for m in a:
            ar, ai = split(m)
            p.append(join_t(ar * tr - ai * ti, ar * ti + ai * tr))
        x = [_dot(m, f2_ref[...]) for m in p]
        y = []
        for s, m in zip(subs, x):
            rows = pl.ds(g * SUBLANES + s * gsub, gsub)
            kr = kr_ref[rows].reshape(gsub * r, r)
            ki = ki_ref[rows].reshape(gsub * r, r)
            xr, xi = m[:, :r].astype(wd), m[:, r:].astype(wd)
            y.append(jnp.concatenate([xr * kr - xi * ki, xr * ki + xi * kr], axis=-1).astype(BF16))
        b = [_dot(m, f2c_ref[...]) for m in y]
        c = []
        for m in b:
            br, bi = split(m)
            c.append(join_t(br * tr + bi * ti, bi * tr - br * ti))
        yt = [_dot(m, f3p_ref[...]) for m in c]
        return jnp.concatenate([jnp.swapaxes(m.reshape(gsub, r, r), 1, 2) for m in yt], axis=0)

    def group(g, carry):
        cbase = ct * rc + g * SUBLANES

        def skip(z3, order):
            return jnp.stack([sk_ref[order, cbase + ci] * z3[ci] for ci in range(SUBLANES)], axis=0)

        v3 = load(v_ref, wv_ref, g)
        z1 = load(x1_ref, w1_ref, g) * (long_conv(v3, k0r_ref, k0i_ref, g) + skip(v3, 0))
        out = load(x2_ref, w2_ref, g) * (long_conv(z1, k1r_ref, k1i_ref, g) + skip(z1, 1))
        for bb in range(nb):
            o_ref[bb, g] = jnp.swapaxes(out[:, bb * np_:(bb + 1) * np_, :], 0, 1)
        return carry

    lax.fori_loop(0, rc // SUBLANES, group, 0)


def _hyena(hy5, cwt, skip, kfr, kfi, tables):
    bsz, n8, np_, _, _ = hy5.shape
    assert bsz % 2 == 0 and 2 * np_ == FFT_R
    rc = RC_HY
    r8 = rc // SUBLANES
    nct = D_HY // rc
    r = FFT_R

    def xspec(off):
        return pl.BlockSpec((2, r8, np_, SUBLANES, LANES), lambda c, b: (b, c + off * nct, 0, 0, 0))

    def wspec(off):
        return pl.BlockSpec((HY_CONV + 1, r8, SUBLANES, LANES), lambda c, b: (0, c + off * nct, 0, 0))

    def kspec(order):
        return pl.BlockSpec((rc, r, r), lambda c, b: (c + order * nct, 0, 0))

    consts = [tables[k] for k in ("f1p", "f2", "f2c", "f3p", "tr16", "ti16")]
    return pl.pallas_call(
        functools.partial(_hyena_body, rc=rc),
        grid=(nct, bsz // 2),
        in_specs=[pl.BlockSpec(memory_space=pltpu.SMEM), xspec(0), xspec(1), xspec(2), wspec(0), wspec(1), wspec(2),
                  kspec(0), kspec(0), kspec(1), kspec(1)] + _const_specs(consts, 2),
        out_specs=pl.BlockSpec((2, r8, np_, SUBLANES, LANES), lambda c, b: (b, c, 0, 0, 0)),
        out_shape=jax.ShapeDtypeStruct((bsz, D_HY // SUBLANES, np_, SUBLANES, LANES), F32),
        compiler_params=_params(("parallel", "parallel")),
        name="hyena",
    )(skip, hy5, hy5, hy5, cwt, cwt, cwt, kfr, kfi, kfr, kfi, *consts)


def _gelu_tanh(x):
    return 0.5 * x * (1.0 + jnp.tanh(math.sqrt(2.0 / math.pi) * (x + 0.044715 * (x * x * x))))


def _mix_body(hf_ref, hb_ref, gate_ref, yhy_ref, x_ref, gl_ref, gh_ref, wl_ref, wh_ref, gffn_ref,
              wrh_ref, wrl_ref, rb_ref, tri_ref, ltri_ref, xmid_ref, h_ref, routec_ref, routet_ref, cnt_ref):
    tm = x_ref.shape[1]
    y_lru = (hf_ref[0, 0] + hb_ref[0, 0]) * _gelu_tanh(gate_ref[0])
    nl = _rms(y_lru, gl_ref[...]).astype(BF16)
    gh = gh_ref[...]
    for j in range(tm // LANES):
        yt = yhy_ref[0, :, j, :, :].reshape(D_HY, LANES)
        nt = (yt * lax.rsqrt(jnp.mean(yt * yt, axis=0, keepdims=True) + EPS) * gh).astype(BF16)
        rows = slice(LANES * j, LANES * (j + 1))
        mix = _dot(nl[rows], wl_ref[...]) + lax.dot_general(
            nt, wh_ref[...], (((0,), (0,)), ((), ())), preferred_element_type=F32)
        xmid_ref[0, rows, :] = x_ref[0, rows, :] + mix
    h = _rms(xmid_ref[0], gffn_ref[...])
    h_ref[...] = h.astype(BF16)

    hh, hl = _split_bf16(h)
    logits = _dot(hh, wrh_ref[...]) + _dot(hl, wrh_ref[...]) + _dot(hh, wrl_ref[...]) + rb_ref[...]
    lt = logits.T[:ROUTE_ROWS]
    row = lax.broadcasted_iota(I32, lt.shape, 0)
    big = jnp.int32(1 << 20)
    is_g = row < N_GROUPS
    lg = jnp.where(is_g, lt, NEG_BIG)
    mg = jnp.max(lg, axis=0, keepdims=True)
    gidx = jnp.min(jnp.where(lg == mg, row, big), axis=0, keepdims=True)
    g_p = 1.0 / jnp.sum(jnp.where(is_g, jnp.exp(lg - mg), 0.0), axis=0, keepdims=True)
    e_lo = N_GROUPS + EXPERTS_PER_GROUP * gidx
    is_e = jnp.logical_and(row >= e_lo, row < e_lo + EXPERTS_PER_GROUP)
    le = jnp.where(is_e, lt, NEG_BIG)
    m1 = jnp.max(le, axis=0, keepdims=True)
    i1 = jnp.min(jnp.where(le == m1, row, big), axis=0, keepdims=True)
    le2 = jnp.where(row == i1, NEG_BIG, le)
    m2 = jnp.max(le2, axis=0, keepdims=True)
    i2 = jnp.min(jnp.where(le2 == m2, row, big), axis=0, keepdims=True)
    ratio = jnp.exp(m2 - m1)
    gate1 = g_p / (1.0 + ratio)
    gate2 = g_p * ratio / (1.0 + ratio)
    oh1 = row == i1
    oh2 = row == i2
    oh = jnp.where(oh1, 1.0, jnp.where(oh2, 1.0, 0.0))
    before = _dot(oh.astype(BF16), tri_ref[...])
    cnt = jnp.sum(oh, axis=1, keepdims=True)
    nchunk = jnp.floor((cnt + (ROW_CHUNK - 1)) * (1.0 / ROW_CHUNK))
    run_start = ROW_CHUNK * _dot(ltri_ref[...], jnp.broadcast_to(nchunk, (ROUTE_ROWS, LANES)).astype(BF16))[:, 0:1]
    pos = run_start + before
    l0 = jnp.sum(jnp.where(oh1, pos, 0.0), axis=0, keepdims=True)
    l1 = jnp.sum(jnp.where(oh2, pos, 0.0), axis=0, keepdims=True)
    cnt_ref[0] = cnt
    route_t = jnp.concatenate([l0, l1, gate1, gate2, jnp.zeros((SUBLANES - 4, tm), F32)], axis=0)
    routet_ref[0] = route_t
    routec_ref[...] = jnp.concatenate([route_t, jnp.zeros((LANES - SUBLANES, tm), F32)], axis=0).T[:, :SUBLANES]


def _mix_router(h4, lru, yhy5, x, gl, gh, wl, wh, gffn, wrh, wrl, rb, tri, ltri):
    bsz, seq, d = x.shape
    tm = TM_MIX
    nt = seq // tm
    n_tok = bsz * seq
    c2 = lambda b, i: (0, 0)
    return pl.pallas_call(
        _mix_body,
        grid=(bsz, nt),
        in_specs=[
            pl.BlockSpec((1, 1, tm, D_LRU), lambda b, i: (0, b, i, 0)),
            pl.BlockSpec((1, 1, tm, D_LRU), lambda b, i: (1, b, i, 0)),
            pl.BlockSpec((1, tm, D_LRU), lambda b, i: (b, i, 1)),
            pl.BlockSpec((1, D_HY // SUBLANES, tm // LANES, SUBLANES, LANES), lambda b, i: (b, 0, i, 0, 0)),
            pl.BlockSpec((1, tm, d), lambda b, i: (b, i, 0)),
            pl.BlockSpec((1, D_LRU), c2),
            pl.BlockSpec((D_HY, 1), c2),
            pl.BlockSpec((D_LRU, d), c2),
            pl.BlockSpec((D_HY, d), c2),
            pl.BlockSpec((1, d), c2),
            pl.BlockSpec((d, LANES), c2),
            pl.BlockSpec((d, LANES), c2),
            pl.BlockSpec((1, LANES), c2),
            pl.BlockSpec((tm, tm), c2),
            pl.BlockSpec((ROUTE_ROWS, ROUTE_ROWS), c2),
        ],
        out_specs=[
            pl.BlockSpec((1, tm, d), lambda b, i: (b, i, 0)),
            pl.BlockSpec((tm, d), lambda b, i: (b * nt + i, 0)),
            pl.BlockSpec((tm, SUBLANES), lambda b, i: (b * nt + i, 0)),
            pl.BlockSpec((1, SUBLANES, tm), lambda b, i: (b * nt + i, 0, 0)),
            pl.BlockSpec((1, ROUTE_ROWS, 1), lambda b, i: (b * nt + i, 0, 0)),
        ],
        out_shape=[
            jax.ShapeDtypeStruct((bsz, seq, d), F32),
            jax.ShapeDtypeStruct((n_tok, d), BF16),
            jax.ShapeDtypeStruct((n_tok, SUBLANES), F32),
            jax.ShapeDtypeStruct((n_tok // tm, SUBLANES, tm), F32),
            jax.ShapeDtypeStruct((n_tok // tm, ROUTE_ROWS, 1), F32),
        ],
        compiler_params=_params(("parallel", "parallel")),
        name="mix_router",
    )(h4, h4, lru, yhy5, x, gl, gh, wl, wh, gffn, wrh, wrl, rb, tri, ltri)


def _start_chunks(i, tch_ref, dst_ref, make_copy):
    n = tch_ref[i]

    def start(k, priority):
        make_copy(k * ROW_CHUNK, dst_ref[i * MAX_CHUNKS + k], ROW_CHUNK).start(priority=priority)

    def per_pair(k2, c):
        start(2 * k2, 0)
        start(2 * k2 + 1, 1)
        return c

    lax.fori_loop(0, n // 2, per_pair, 0)

    @pl.when(n % 2 == 1)
    def _():
        start(n - 1, 0)


def _wait_chunks(count, make_copy):
    nbits = MAX_CHUNKS.bit_length()
    for j in reversed(range(nbits)):
        @pl.when((count >> j) & 1 == 1)
        def _():
            make_copy(0, 0, ROW_CHUNK << j).wait()


def _dispatch_body(dst_ref, tch_ref, tn_ref, ts_ref, nu_ref, h_ref, routet_ref, xb_ref,
                   buf, zbuf, sem, zsem):
    i = pl.program_id(0)
    n_tiles = pl.num_programs(0)
    tm = h_ref.shape[0]
    lbuf = buf.shape[1]
    n_blocks = xb_ref.shape[0] // MOE_BLK
    slot = i % 2

    def aligned(r):
        return pl.ds(pl.multiple_of(r, ROW_CHUNK), ROW_CHUNK)

    @pl.when(i == 0)
    def _():
        zbuf[...] = jnp.zeros_like(zbuf)

        def zero_chunk(r):
            return pltpu.make_async_copy(zbuf.at[pl.ds(0, ROW_CHUNK)], xb_ref.at[aligned(r)], zsem)

        def zero_block(j):
            return pltpu.make_async_copy(zbuf, xb_ref.at[pl.ds(pl.multiple_of(j * MOE_BLK, MOE_BLK), MOE_BLK)], zsem)

        def per_expert(e, total):
            def per_chunk(k, c):
                zero_chunk(ts_ref[e] + k * ROW_CHUNK).start()
                return c

            lax.fori_loop(0, tn_ref[e], per_chunk, 0)
            return total + tn_ref[e]

        total = lax.fori_loop(0, N_EXPERTS, per_expert, 0)

        def start_block(j, c):
            zero_block(j).start()
            return c

        lax.fori_loop(nu_ref[0], n_blocks, start_block, 0)

        def wait_chunk(k, c):
            zero_chunk(0).wait()
            return c

        lax.fori_loop(0, total, wait_chunk, 0)

        def wait_block(j, c):
            zero_block(0).wait()
            return c

        lax.fori_loop(nu_ref[0], n_blocks, wait_block, 0)

    l0t = routet_ref[0, 0:1, :].astype(I32)
    l1t = routet_ref[0, 1:2, :].astype(I32)
    row = lax.broadcasted_iota(I32, (lbuf, tm), 0)
    perm = jnp.where(row == l0t, 1.0, jnp.where(row == l1t, 1.0, 0.0)).astype(BF16)
    srt = _dot(perm, h_ref[...]).astype(BF16)

    def copy_from(s):
        def make_copy(src_row, dst_row, rows):
            return pltpu.make_async_copy(buf.at[s, pl.ds(pl.multiple_of(src_row, ROW_CHUNK), rows)],
                                         xb_ref.at[pl.ds(pl.multiple_of(dst_row, ROW_CHUNK), rows)], sem.at[s])
        return make_copy

    @pl.when(i >= 2)
    def _():
        _wait_chunks(tch_ref[i - 2], copy_from(slot))

    buf[slot] = srt
    _start_chunks(i, tch_ref, dst_ref, copy_from(slot))

    @pl.when(i == n_tiles - 1)
    def _():
        @pl.when(i >= 1)
        def _():
            _wait_chunks(tch_ref[i - 1], copy_from(1 - slot))

        _wait_chunks(tch_ref[i], copy_from(slot))


def _dispatch(chunk_dst, tile_nch, tail_n, tail_s, n_used, h, route_t, n_slots):
    n_tok, d = h.shape
    tm = TM_MIX
    grid_spec = pltpu.PrefetchScalarGridSpec(
        num_scalar_prefetch=5,
        grid=(n_tok // tm,),
        in_specs=[
            pl.BlockSpec((tm, d), lambda i, *_: (i, 0)),
            pl.BlockSpec((1, SUBLANES, tm), lambda i, *_: (i, 0, 0)),
        ],
        out_specs=pl.BlockSpec(memory_space=pl.ANY),
        scratch_shapes=[pltpu.VMEM((2, LOCAL_ROWS, d), BF16), pltpu.VMEM((MOE_BLK, d), BF16),
                        pltpu.SemaphoreType.DMA((2,)), pltpu.SemaphoreType.DMA(())],
    )
    return pl.pallas_call(
        _dispatch_body,
        grid_spec=grid_spec,
        out_shape=jax.ShapeDtypeStruct((n_slots, d), BF16),
        compiler_params=_params(("arbitrary",)),
        name="moe_dispatch",
    )(chunk_dst, tile_nch, tail_n, tail_s, n_used, h, route_t)


def _expert_body(be_ref, fl_ref, sw_ref, nu_ref, xb_ref, w1_ref, w3_ref, w2_ref, yb_ref,
                 w1_sc, w3_sc, w2_sc, w1_ring, w3_ring, w2_ring, sem):
    j = pl.program_id(0)
    n_used = nu_ref[0]
    used = j < n_used

    def weight_copies(jj):
        e = be_ref[jj]
        s = sw_ref[jj] % WEIGHT_SLOTS
        return [pltpu.make_async_copy(w_ref.at[e], ring.at[s], sem.at[s])
                for w_ref, ring in ((w1_ref, w1_ring), (w3_ref, w3_ring), (w2_ref, w2_ring))]

    def fetch_if_first(jj):
        @pl.when(jnp.logical_and(jj < n_used, fl_ref[jnp.minimum(jj, pl.num_programs(0) - 1)] == 1))
        def _():
            for cp in weight_copies(jj):
                cp.start()

    @pl.when(j == 0)
    def _():
        for jj in range(WEIGHT_LOOKAHEAD):
            fetch_if_first(jnp.int32(jj))

    fetch_if_first(j + WEIGHT_LOOKAHEAD)

    @pl.when(jnp.logical_and(used, fl_ref[j] == 1))
    def _():
        for cp in weight_copies(j):
            cp.wait()
        s = sw_ref[j] % WEIGHT_SLOTS
        w1_sc[...] = w1_ring[s].astype(BF16)
        w3_sc[...] = w3_ring[s].astype(BF16)
        w2_sc[...] = w2_ring[s].astype(BF16)

    @pl.when(used)
    def _():
        xb = xb_ref[...]
        a = _dot(xb, w1_sc[...])
        b = _dot(xb, w3_sc[...])
        act = (a * _sigmoid(a) * b).astype(BF16)
        yb_ref[...] = _dot(act, w2_sc[...]).astype(BF16)

    @pl.when(jnp.logical_not(used))
    def _():
        yb_ref[...] = jnp.zeros_like(yb_ref)


def _experts(block_e, first_flag, n_used, xb, w1, w3, w2):
    n_slots, d = xb.shape
    blk = MOE_BLK
    de = w1.shape[2]
    switch_no = jnp.cumsum(first_flag) - 1
    hbm = pl.BlockSpec(memory_space=pl.ANY)
    grid_spec = pltpu.PrefetchScalarGridSpec(
        num_scalar_prefetch=4,
        grid=(n_slots // blk,),
        in_specs=[pl.BlockSpec((blk, d), lambda j, *_: (j, 0)), hbm, hbm, hbm],
        out_specs=pl.BlockSpec((blk, d), lambda j, *_: (j, 0)),
        scratch_shapes=[pltpu.VMEM((d, de), BF16), pltpu.VMEM((d, de), BF16), pltpu.VMEM((de, d), BF16),
                        pltpu.VMEM((WEIGHT_SLOTS, d, de), F32), pltpu.VMEM((WEIGHT_SLOTS, d, de), F32),
                        pltpu.VMEM((WEIGHT_SLOTS, de, d), F32), pltpu.SemaphoreType.DMA((WEIGHT_SLOTS,))],
    )
    return pl.pallas_call(
        _expert_body,
        grid_spec=grid_spec,
        out_shape=jax.ShapeDtypeStruct((n_slots, d), BF16),
        compiler_params=_params(("arbitrary",)),
        name="moe_experts",
    )(block_e, first_flag, switch_no.astype(I32), n_used, xb, w1, w3, w2)


def _combine_body(dst_ref, tch_ref, xmid_ref, route_ref, g_ref, yb_ref, o_ref, buf, sem):
    i = pl.program_id(0)
    n_tiles = pl.num_programs(0)
    tm = xmid_ref.shape[0]
    lbuf = buf.shape[1]
    slot = i % 2

    def copy_into(s):
        def make_copy(local_row, slot_row, rows):
            return pltpu.make_async_copy(yb_ref.at[pl.ds(pl.multiple_of(slot_row, ROW_CHUNK), rows)],
                                         buf.at[s, pl.ds(pl.multiple_of(local_row, ROW_CHUNK), rows)], sem.at[s])
        return make_copy

    @pl.when(i == 0)
    def _():
        buf[...] = jnp.zeros_like(buf)
        _start_chunks(i, tch_ref, dst_ref, copy_into(slot))

    @pl.when(i + 1 < n_tiles)
    def _():
        _start_chunks(i + 1, tch_ref, dst_ref, copy_into(1 - slot))

    route = route_ref[...]
    l0 = route[:, 0:1].astype(I32)
    l1 = route[:, 1:2].astype(I32)
    lane = lax.broadcasted_iota(I32, (tm, lbuf), 1)
    gmat = (jnp.where(lane == l0, route[:, 2:3], 0.0) + jnp.where(lane == l1, route[:, 3:4], 0.0)).astype(BF16)
    _wait_chunks(tch_ref[i], copy_into(slot))
    y = xmid_ref[...] + _dot(gmat, buf[slot])
    o_ref[...] = _rms(y, g_ref[...])


def _combine(chunk_dst, tile_nch, xmid, route, g, yb):
    n_tok, d = xmid.shape
    tm = TM_MIX
    grid_spec = pltpu.PrefetchScalarGridSpec(
        num_scalar_prefetch=2,
        grid=(n_tok // tm,),
        in_specs=[
            pl.BlockSpec((tm, d), lambda i, *_: (i, 0)),
            pl.BlockSpec((tm, SUBLANES), lambda i, *_: (i, 0)),
            pl.BlockSpec((1, d), lambda i, *_: (0, 0)),
            pl.BlockSpec(memory_space=pl.ANY),
        ],
        out_specs=pl.BlockSpec((tm, d), lambda i, *_: (i, 0)),
        scratch_shapes=[pltpu.VMEM((2, LOCAL_ROWS, d), BF16), pltpu.SemaphoreType.DMA((2,))],
    )
    return pl.pallas_call(
        _combine_body,
        grid_spec=grid_spec,
        out_shape=jax.ShapeDtypeStruct((n_tok, d), F32),
        compiler_params=_params(("arbitrary",)),
        name="moe_combine",
    )(chunk_dst, tile_nch, xmid, route, g, yb)


def _filter_features(seq):
    pos = jnp.arange(seq, dtype=F32)
    t = jnp.linspace(0.0, 1.0, seq, dtype=F32)
    w = (2.0 * math.pi / seq) * pos
    bands = jnp.linspace(1e-4, HY_BANDS - 1, HY_BANDS, dtype=F32)
    ang = w[:, None] * bands[None, :]
    feats = jnp.concatenate([t[:, None], jnp.cos(ang), jnp.sin(ang)], axis=-1)
    featst = jnp.pad(feats, ((0, 0), (0, HY_EMB_PAD - HY_EMB))).T[None]
    return featst, t[None, :]


def _gate_weights(wr, wi):
    half = D_LRU // 2
    hph = half // LRU_HEAD_DIM
    eye = jnp.eye(hph, dtype=wr.dtype)[:, None, :, None]

    def block_diag(w):
        return (eye * w[:, :, None, :]).reshape(half, half)

    out = [jnp.concatenate([block_diag(wr[hh * hph:(hh + 1) * hph]), block_diag(wi[hh * hph:(hh + 1) * hph])], axis=1)
           for hh in range(2)]
    return jnp.stack(out, axis=0).astype(BF16)


def _layer(x, l, p):
    bsz, seq, d = x.shape
    n_tok = bsz * seq
    row = lambda a: a.reshape(1, -1).astype(F32)
    col = lambda a: a.reshape(-1, 1).astype(F32)

    w_in = p["w_in"][l]
    wl = w_in[:, :2 * D_LRU].astype(BF16)
    wht = w_in[:, 2 * D_LRU:].T.astype(BF16)
    lru, hy5 = _inproj(x, row(p["norm_mix_g"][l]), wl, wht)

    wg = jnp.stack([_gate_weights(p["lru_wr_f"][l], p["lru_wi_f"][l]),
                    _gate_weights(p["lru_wr_b"][l], p["lru_wi_b"][l])], axis=0)
    stack2 = lambda a, b: jnp.stack([a.reshape(1, -1), b.reshape(1, -1)], axis=0).astype(F32)
    h4 = _lru(lru, p["lru_conv_w"][l], row(p["lru_conv_b"][l]), wg,
              stack2(p["lru_br_f"][l], p["lru_br_b"][l]), stack2(p["lru_bi_f"][l], p["lru_bi_b"][l]),
              stack2(p["lru_lambda_f"][l], p["lru_lambda_b"][l]))

    tables = _fft_tables()
    featst, t_f = _filter_features(seq)
    w1t = jnp.pad(p["hy_filt_w1"][l].T, ((0, 0), (0, HY_EMB_PAD - HY_EMB)))
    z2t = _filt_mlp(featst, w1t, col(p["hy_filt_b1"][l]), col(p["hy_filt_freq1"][l]),
                    p["hy_filt_w2"][l].T, col(p["hy_filt_b2"][l]), col(p["hy_filt_freq2"][l]))
    w3 = p["hy_filt_w3"][l]
    nfr = HY_ORDER * D_HY
    deltas = jnp.abs(jnp.linspace(math.log(HY_DECAY_TARGET) / HY_SLOW_DECAY_PCT,
                                  math.log(HY_DECAY_TARGET) / HY_FAST_DECAY_PCT, D_HY, dtype=F32))
    ktile = _filt_time(z2t, w3[:, :nfr].T, w3[:, nfr:].T, col(jnp.tile(deltas, HY_ORDER)), t_f)
    kfr, kfi = _filt_fft(ktile, tables)
    cwt = jnp.concatenate([p["hy_conv_w"][l], p["hy_conv_b"][l][None, :]], axis=0)
    cwt = jnp.broadcast_to(cwt.reshape(HY_CONV + 1, -1, SUBLANES, 1), (HY_CONV + 1, cwt.shape[1] // SUBLANES, SUBLANES, LANES))
    yhy5 = _hyena(hy5, cwt, p["hy_skip"][l], kfr, kfi, tables)

    w_out = p["w_out"][l]
    wr_cat = jnp.concatenate([p["router_group_w"][l], p["router_expert_w"][l]], axis=1)
    wr_cat = jnp.pad(wr_cat, ((0, 0), (0, LANES - wr_cat.shape[1])))
    wrh, wrl = _split_bf16(wr_cat)
    rb = jnp.concatenate([p["router_group_b"][l], p["router_expert_b"][l]])
    rb = jnp.pad(rb, (0, LANES - rb.shape[0])).reshape(1, LANES)
    earlier = lambda n: (jnp.arange(n)[:, None] < jnp.arange(n)[None, :]).astype(BF16)
    xmid, h, route, route_t, tile_cnt = _mix_router(
        h4, lru, yhy5, x, row(p["grp_norm_lru_g"][l]), col(p["grp_norm_hy_g"][l]),
        w_out[:D_LRU].astype(BF16), w_out[D_LRU:].astype(BF16), row(p["norm_ffn_g"][l]), wrh, wrl, rb,
        earlier(TM_MIX), earlier(ROUTE_ROWS).T)

    blk = MOE_BLK
    ch = ROW_CHUNK
    n_tiles = n_tok // TM_MIX
    n_blocks = (2 * n_tok + n_tiles * N_EXPERTS * (ch - 1)) // blk + N_EXPERTS
    cnt = tile_cnt[:, N_GROUPS:N_GROUPS + N_EXPERTS, 0].astype(I32)
    run = (cnt + ch - 1) // ch * ch
    tot_e = jnp.sum(run, axis=0)
    padded_e = (tot_e + blk - 1) // blk * blk
    pad_end = jnp.cumsum(padded_e)
    pad_start = pad_end - padded_e
    gstart = pad_start[None, :] + jnp.cumsum(run, axis=0) - run
    lstart = jnp.cumsum(run, axis=1) - run
    flat = lambda a: a.reshape(-1).astype(I32)
    tile_nch = flat(jnp.sum(run, axis=1) // ch)
    nch_end = jnp.cumsum(run // ch, axis=1)
    k = jnp.arange(MAX_CHUNKS, dtype=I32)
    base = gstart - lstart
    step = base[:, 1:] - base[:, :-1]
    passed = (k[None, :, None] >= nch_end[:, None, :-1]).astype(I32)
    chunk_dst = flat(base[:, :1] + jnp.sum(passed * step[:, None, :], axis=2) + k[None, :] * ch)
    tail_n = flat((padded_e - tot_e) // ch)
    tail_s = flat(pad_start + tot_e)
    block_start = jnp.arange(n_blocks, dtype=I32) * blk
    block_e = jnp.minimum(jnp.sum(block_start[:, None] >= pad_end[None, :], axis=1), N_EXPERTS - 1).astype(I32)
    first_flag = jnp.concatenate([jnp.ones((1,), I32), (block_e[1:] != block_e[:-1]).astype(I32)])
    n_used = (pad_end[-1:] // blk).astype(I32)

    xb = _dispatch(chunk_dst, tile_nch, tail_n, tail_s, n_used, h, route_t, n_blocks * blk)
    yb = _experts(block_e, first_flag, n_used, xb, p["exp_w1"][l], p["exp_w3"][l], p["exp_w2"][l])
    return xmid.reshape(n_tok, d), route, (chunk_dst, tile_nch), yb


def kernel(x, norm_mix_g, w_in, lru_conv_w, lru_conv_b, lru_wr_f, lru_br_f, lru_wi_f, lru_bi_f, lru_lambda_f, lru_wr_b, lru_br_b, lru_wi_b, lru_bi_b, lru_lambda_b, hy_conv_w, hy_conv_b, hy_filt_w1, hy_filt_b1, hy_filt_freq1, hy_filt_w2, hy_filt_b2, hy_filt_freq2, hy_filt_w3, hy_skip, grp_norm_lru_g, grp_norm_hy_g, w_out, norm_ffn_g, router_group_w, router_group_b, router_expert_w, router_expert_b, exp_w1, exp_w3, exp_w2, norm_final_g):
    p = dict(locals())
    bsz, seq, d = x.shape
    assert d == D_MODEL and 2 * seq == FFT_R * FFT_R and w_in.shape[0] == 1
    xmid, route, (chunk_dst, tile_nch), yb = _layer(x, 0, p)
    out = _combine(chunk_dst, tile_nch, xmid, route, norm_final_g.reshape(1, d), yb)
    return out.reshape(bsz, seq, d)
```

```python
import functools
import math

import numpy as np
import jax
import jax.numpy as jnp
from jax import lax
from jax.experimental import pallas as pl
from jax.experimental.pallas import tpu as pltpu

F32 = jnp.float32
BF16 = jnp.bfloat16
I32 = jnp.int32

D_MODEL = 1024
D_LRU = 512
D_HY = 512
LRU_HEADS = 8
LRU_HEAD_DIM = D_LRU // LRU_HEADS
LRU_CONV = 4
LRU_C = 8.0
HY_ORDER = 2
HY_CONV = 3
HY_BANDS = 16
HY_EMB = 2 * HY_BANDS + 1
HY_EMB_PAD = 40
HY_FFN = 64
HY_FAST_DECAY_PCT = 0.3
HY_SLOW_DECAY_PCT = 1.5
HY_DECAY_TARGET = 1e-2
N_GROUPS = 4
EXPERTS_PER_GROUP = 8
N_EXPERTS = N_GROUPS * EXPERTS_PER_GROUP
D_EXPERT = D_MODEL // 2
EPS = 1e-6

LANES = 128
SUBLANES = 8
FFT_R = 128
VMEM_LIMIT = 56 * 1024 * 1024

TM_IN = 1024
TC_LRU = 1024
RC_HY = 32
RF_KF = 32
HY_SPLIT = 2
RB_FILT = 64
TM_MIX = 512
MOE_BLK = 512
WEIGHT_LOOKAHEAD = 2
WEIGHT_SLOTS = WEIGHT_LOOKAHEAD + 1
ROW_CHUNK = 16
LOCAL_ROWS = -(-(2 * TM_MIX + N_EXPERTS * (ROW_CHUNK - 1)) // LANES) * LANES
MAX_CHUNKS = LOCAL_ROWS // ROW_CHUNK
ROUTE_ROWS = -(-(N_GROUPS + N_EXPERTS) // SUBLANES) * SUBLANES
NEG_BIG = -1e30


def _params(sem, vmem=VMEM_LIMIT):
    return pltpu.CompilerParams(dimension_semantics=sem, vmem_limit_bytes=vmem)


def _rms(x, g):
    return x * lax.rsqrt(jnp.mean(x * x, axis=-1, keepdims=True) + EPS) * g


def _sigmoid(x):
    return 1.0 / (1.0 + jnp.exp(-x))


def _split_bf16(a):
    hi = a.astype(BF16)
    lo = (a - hi.astype(F32)).astype(BF16)
    return hi, lo


def _dot(a, b):
    return jnp.dot(a, b, preferred_element_type=F32)


def _dot3(a, b):
    ah, al = _split_bf16(a)
    bh, bl = _split_bf16(b)
    return _dot(ah, bh) + _dot(al, bh) + _dot(ah, bl)


def _inproj_body(x_ref, g_ref, wl_ref, wht_ref, lru_ref, hy_ref):
    hn = _rms(x_ref[0], g_ref[...]).astype(BF16)
    lru_ref[0] = _dot(hn, wl_ref[...])
    hyt = lax.dot_general(wht_ref[...], hn, (((1,), (1,)), ((), ())), preferred_element_type=F32)
    nrow = hyt.shape[0] // SUBLANES
    for j in range(hyt.shape[1] // LANES):
        hy_ref[0, :, j, :, :] = hyt[:, LANES * j:LANES * (j + 1)].reshape(nrow, SUBLANES, LANES)


def _inproj(x, g, wl, wht):
    bsz, seq, d = x.shape
    nl = wl.shape[1]
    nh = wht.shape[0]
    tm = TM_IN
    return pl.pallas_call(
        _inproj_body,
        grid=(bsz, seq // tm),
        in_specs=[
            pl.BlockSpec((1, tm, d), lambda b, i: (b, i, 0)),
            pl.BlockSpec((1, d), lambda b, i: (0, 0)),
            pl.BlockSpec((d, nl), lambda b, i: (0, 0)),
            pl.BlockSpec((nh, d), lambda b, i: (0, 0)),
        ],
        out_specs=[
            pl.BlockSpec((1, tm, nl), lambda b, i: (b, i, 0)),
            pl.BlockSpec((1, nh // SUBLANES, tm // LANES, SUBLANES, LANES), lambda b, i: (b, 0, i, 0, 0)),
        ],
        out_shape=[
            jax.ShapeDtypeStruct((bsz, seq, nl), F32),
            jax.ShapeDtypeStruct((bsz, nh // SUBLANES, seq // LANES, SUBLANES, LANES), F32),
        ],
        compiler_params=_params(("parallel", "parallel")),
        name="inproj",
    )(x, g, wl, wht)


def _sqrt_one_minus_sq(a, neg_log_a):
    om = jnp.maximum(jnp.tanh(neg_log_a) * (1.0 + a * a), 1e-30)
    return om * lax.rsqrt(om)


def _group_scan(a3, b3, reverse):
    sub = lax.broadcasted_iota(I32, a3.shape, 1)
    for s in (1, 2, 4):
        if reverse:
            a_sh = pltpu.roll(a3, SUBLANES - s, axis=1)
            b_sh = pltpu.roll(b3, SUBLANES - s, axis=1)
            m = sub < SUBLANES - s
        else:
            a_sh = pltpu.roll(a3, s, axis=1)
            b_sh = pltpu.roll(b3, s, axis=1)
            m = sub >= s
        b3 = jnp.where(m, a3 * b_sh + b3, b3)
        a3 = jnp.where(m, a3 * a_sh, a3)
    return a3, b3


def _lru_body(cur_ref, prev_ref, next_ref, cw_ref, cb_ref, wg_ref, br_ref, bi_ref, lam_ref, o_ref,
              carry_sc, *, nt, tc):
    d = pl.program_id(1)
    i = pl.program_id(2)
    c = jnp.where(d == 0, i, nt - 1 - i)
    ns = SUBLANES
    ng = tc // ns
    half = D_LRU // 2

    @pl.when(i == 0)
    def _():
        carry_sc[...] = jnp.zeros_like(carry_sc)

    xs = jnp.swapaxes(cur_ref[0].reshape(ns, ng, D_LRU), 0, 1)
    prev = jnp.where(c > 0, prev_ref[0], 0.0)
    nxt = jnp.where(c < nt - 1, next_ref[0], 0.0)
    sub = lax.broadcasted_iota(I32, (ns, D_LRU), 0)

    def prev_segment(slab, halo_row):
        return jnp.where(sub == 0, halo_row, pltpu.roll(slab, 1, axis=0))

    def next_segment(slab, halo_row):
        return jnp.where(sub == ns - 1, halo_row, pltpu.roll(slab, ns - 1, axis=0))

    xext = jnp.concatenate([prev_segment(xs[ng - 2], prev[ns - 2:ns - 1])[None],
                            prev_segment(xs[ng - 1], prev[ns - 1:ns])[None], xs,
                            next_segment(xs[0], nxt[0:1])[None]], axis=0)
    cw = cw_ref[...]
    xc = cb_ref[...] + cw[0:1] * xext[0:ng]
    for k in range(1, LRU_CONV):
        xc = xc + cw[k:k + 1] * xext[k:k + ng]
    xc = xc.reshape(tc, D_LRU)

    lam = lam_ref[0]
    nlam = -lam
    softplus = jnp.maximum(nlam, 0.0) + jnp.log1p(jnp.exp(-jnp.abs(nlam)))
    half_c = (0.5 * LRU_C) * softplus

    def gates(hh):
        sl = slice(half * hh, half * (hh + 1))
        xh = xc[:, sl]
        logits = _dot(xh.astype(BF16), wg_ref[0, hh])
        tr_ = jnp.tanh(0.5 * (logits[:, :half] + br_ref[0][:, sl]))
        gi = 0.5 + 0.5 * jnp.tanh(0.5 * (logits[:, half:] + bi_ref[0][:, sl]))
        neg_log_a = half_c[:, sl] * (1.0 + tr_)
        a = jnp.exp(-neg_log_a)
        b = _sqrt_one_minus_sq(a, neg_log_a) * (gi * xh)
        return a.reshape(ng, ns, half), b.reshape(ng, ns, half)

    def run(reverse):
        subh = lax.broadcasted_iota(I32, (ns, half), 0)
        for hh in range(2):
            sl = slice(half * hh, half * (hh + 1))
            a3, b3 = gates(hh)
            h = jnp.zeros((ns, half), F32)
            p = jnp.ones((ns, half), F32)
            hs = [None] * ng
            ps = [None] * ng
            for g in (range(ng - 1, -1, -1) if reverse else range(ng)):
                h = a3[g] * h + b3[g]
                p = a3[g] * p
                hs[g] = h
                ps[g] = p
            ac, hc = _group_scan(p[None], h[None], reverse)
            c_in = carry_sc[:, sl]
            end = ac[0] * c_in + hc[0]
            if reverse:
                seg_in = jnp.where(subh == ns - 1, c_in, pltpu.roll(end, ns - 1, axis=0))
                carry_sc[:, sl] = jnp.broadcast_to(end[0:1], end.shape)
            else:
                seg_in = jnp.where(subh == 0, c_in, pltpu.roll(end, 1, axis=0))
                carry_sc[:, sl] = jnp.broadcast_to(end[ns - 1:ns], end.shape)
            h3 = jnp.stack(hs, axis=0) + jnp.stack(ps, axis=0) * seg_in[None]
            o_ref[0, 0, :, sl] = jnp.swapaxes(h3, 0, 1).reshape(tc, half)

    @pl.when(d == 0)
    def _():
        run(False)

    @pl.when(d == 1)
    def _():
        run(True)


def _lru(lru, cw, cb, wg, br, bi, lam):
    bsz, seq, _ = lru.shape
    tc = TC_LRU
    nt = seq // tc
    r8 = tc // SUBLANES
    nrow8 = seq // SUBLANES

    def cidx(d, i):
        return jnp.where(d == 0, i, nt - 1 - i)

    return pl.pallas_call(
        functools.partial(_lru_body, nt=nt, tc=tc),
        grid=(bsz, 2, nt),
        in_specs=[
            pl.BlockSpec((1, tc, D_LRU), lambda b, d, i: (b, cidx(d, i), 0)),
            pl.BlockSpec((1, SUBLANES, D_LRU), lambda b, d, i: (b, jnp.maximum(cidx(d, i) * r8 - 1, 0), 0)),
            pl.BlockSpec((1, SUBLANES, D_LRU), lambda b, d, i: (b, jnp.minimum((cidx(d, i) + 1) * r8, nrow8 - 1), 0)),
            pl.BlockSpec((LRU_CONV, D_LRU), lambda b, d, i: (0, 0)),
            pl.BlockSpec((1, D_LRU), lambda b, d, i: (0, 0)),
            pl.BlockSpec((1, 2, D_LRU // 2, D_LRU), lambda b, d, i: (d, 0, 0, 0)),
            pl.BlockSpec((1, 1, D_LRU), lambda b, d, i: (d, 0, 0)),
            pl.BlockSpec((1, 1, D_LRU), lambda b, d, i: (d, 0, 0)),
            pl.BlockSpec((1, 1, D_LRU), lambda b, d, i: (d, 0, 0)),
        ],
        out_specs=pl.BlockSpec((1, 1, tc, D_LRU), lambda b, d, i: (d, b, cidx(d, i), 0)),
        out_shape=jax.ShapeDtypeStruct((2, bsz, seq, D_LRU), F32),
        scratch_shapes=[pltpu.VMEM((SUBLANES, D_LRU), F32)],
        compiler_params=_params(("parallel", "arbitrary", "arbitrary")),
        name="rg_lru",
    )(lru, lru, lru, cw, cb, wg, br, bi, lam)


def _fft_tables():
    r = FFT_R
    n = r * r
    idx = np.arange(r, dtype=np.float64)
    ang = 2.0 * np.pi * np.outer(idx, idx) / r
    wr, wi = np.cos(ang), -np.sin(ang)
    angt = 2.0 * np.pi * np.outer(idx, idx) / n
    tr, ti = np.cos(angt), -np.sin(angt)
    h = r // 2
    f1 = np.concatenate([wr, wi], axis=1)
    f1p = np.block([[wr[:h], wi[:h]], [-wi[:h], wr[:h]]])
    f2 = np.block([[wr, wi], [-wi, wr]])
    f2c = np.block([[wr, -wi], [wi, wr]])
    f3p = np.block([[wr[:, :h], -wi[:, :h]], [wi[:, :h], wr[:, :h]]]) / n
    as_bf16 = lambda a: jnp.asarray(a, F32).astype(BF16)
    return dict(f1=as_bf16(f1), f1p=as_bf16(f1p), f2=as_bf16(f2), f2c=as_bf16(f2c), f3p=as_bf16(f3p),
                tr16=as_bf16(tr), ti16=as_bf16(ti))


def _const_specs(arrays, nargs):
    return [pl.BlockSpec(a.shape, (lambda nd: (lambda *_: (0,) * nd))(a.ndim)) for a in arrays]


def _fft_fwd(x3, f1, f2, tr, ti):
    g = x3.shape[0]
    r = FFT_R
    wd = tr.dtype
    xt = jnp.swapaxes(x3.astype(wd), 1, 2).reshape(g * r, r)
    a = _dot(xt.astype(BF16), f1)
    ar = a[:, :r].astype(wd).reshape(g, r, r)
    ai = a[:, r:].astype(wd).reshape(g, r, r)
    pr = ar * tr - ai * ti
    pi = ar * ti + ai * tr
    pt = jnp.concatenate([jnp.swapaxes(pr, 1, 2), jnp.swapaxes(pi, 1, 2)], axis=-1).reshape(g * r, 2 * r)
    x = _dot(pt.astype(BF16), f2)
    return x[:, :r], x[:, r:]


def _filt_mlp_body(ft_ref, w1_ref, b1_ref, f1_ref, w2_ref, b2_ref, f2_ref, o_ref):
    z = jnp.sin(f1_ref[...] * (_dot3(w1_ref[...], ft_ref[0]) + b1_ref[...]))
    o_ref[0] = jnp.sin(f2_ref[...] * (_dot3(w2_ref[...], z) + b2_ref[...]))


def _filt_mlp(featst, w1t, b1, f1, w2t, b2, f2):
    nd, ke, seq = featst.shape
    col = lambda d: (0, 0)
    return pl.pallas_call(
        _filt_mlp_body,
        grid=(nd,),
        in_specs=[
            pl.BlockSpec((1, ke, seq), lambda d: (d, 0, 0)),
            pl.BlockSpec((HY_FFN, ke), col),
            pl.BlockSpec((HY_FFN, 1), col),
            pl.BlockSpec((HY_FFN, 1), col),
            pl.BlockSpec((HY_FFN, HY_FFN), col),
            pl.BlockSpec((HY_FFN, 1), col),
            pl.BlockSpec((HY_FFN, 1), col),
        ],
        out_specs=pl.BlockSpec((1, HY_FFN, seq), lambda d: (d, 0, 0)),
        out_shape=jax.ShapeDtypeStruct((nd, HY_FFN, seq), F32),
        compiler_params=_params(("parallel",)),
        name="hyena_filter_mlp",
    )(featst, w1t, b1, f1, w2t, b2, f2)


def _filt_time_body(z_ref, wf_ref, wb_ref, dl_ref, tf_ref, o_ref):
    seq = z_ref.shape[2]
    decay = jnp.exp(-tf_ref[...] * dl_ref[...])
    hf = _dot3(wf_ref[...], z_ref[0]) * decay
    hb = _dot3(wb_ref[...], z_ref[0]) * decay
    lane = lax.broadcasted_iota(I32, hb.shape, 1)
    hb = jnp.where(lane == 0, 0.0, hb)
    norm = jnp.sum(jnp.abs(hf), axis=-1, keepdims=True) + jnp.sum(jnp.abs(hb), axis=-1, keepdims=True) + EPS
    hf = hf / norm
    hb = hb / norm
    nrow = hf.shape[0] // SUBLANES
    nch = seq // LANES
    lane1 = lax.broadcasted_iota(I32, (hf.shape[0], LANES), 1)
    mirror = (LANES - lane1) % LANES
    for j in range(nch):
        o_ref[:, j, :, :] = hf[:, LANES * j:LANES * (j + 1)].reshape(nrow, SUBLANES, LANES)
        src = nch - 1 - j
        back = jnp.take_along_axis(hb[:, LANES * src:LANES * (src + 1)], mirror, axis=1)
        first = hb[:, LANES * (src + 1):LANES * (src + 1) + 1] if j > 0 else 0.0
        o_ref[:, nch + j, :, :] = jnp.where(lane1 == 0, first, back).reshape(nrow, SUBLANES, LANES)


def _filt_time(z2t, w3f, w3b, delta_rows, t_f):
    nrows = w3f.shape[0]
    seq = z2t.shape[2]
    rb = RB_FILT
    return pl.pallas_call(
        _filt_time_body,
        grid=(nrows // rb,),
        in_specs=[
            pl.BlockSpec((1, HY_FFN, seq), lambda r: (0, 0, 0)),
            pl.BlockSpec((rb, HY_FFN), lambda r: (r, 0)),
            pl.BlockSpec((rb, HY_FFN), lambda r: (r, 0)),
            pl.BlockSpec((rb, 1), lambda r: (r, 0)),
            pl.BlockSpec((1, seq), lambda r: (0, 0)),
        ],
        out_specs=pl.BlockSpec((rb // SUBLANES, 2 * seq // LANES, SUBLANES, LANES), lambda r: (r, 0, 0, 0)),
        out_shape=jax.ShapeDtypeStruct((nrows // SUBLANES, 2 * seq // LANES, SUBLANES, LANES), F32),
        compiler_params=_params(("parallel",)),
        name="hyena_filter_time",
    )(z2t, w3f, w3b, delta_rows, t_f)


def _filt_fft_body(k_ref, f1_ref, f2_ref, tr_ref, ti_ref, kr_ref, ki_ref):
    tr = tr_ref[...]
    ti = ti_ref[...]
    for g in range(k_ref.shape[0]):
        x3 = jnp.swapaxes(k_ref[g], 0, 1)
        xr, xi = _fft_fwd(x3, f1_ref[...], f2_ref[...], tr, ti)
        kr_ref[SUBLANES * g:SUBLANES * (g + 1)] = xr.astype(BF16).reshape(SUBLANES, FFT_R, FFT_R)
        ki_ref[SUBLANES * g:SUBLANES * (g + 1)] = xi.astype(BF16).reshape(SUBLANES, FFT_R, FFT_R)


def _filt_fft(ktile, tables):
    n8, r, _, _ = ktile.shape
    nrows = n8 * SUBLANES
    rf = RF_KF
    out_spec = pl.BlockSpec((rf, r, r), lambda i: (i, 0, 0))
    consts = [tables[k] for k in ("f1", "f2", "tr16", "ti16")]
    return pl.pallas_call(
        _filt_fft_body,
        grid=(nrows // rf,),
        in_specs=[pl.BlockSpec((rf // SUBLANES, r, SUBLANES, LANES), lambda i: (i, 0, 0, 0))] + _const_specs(consts, 1),
        out_specs=[out_spec, out_spec],
        out_shape=[jax.ShapeDtypeStruct((nrows, r, r), BF16)] * 2,
        compiler_params=_params(("parallel",)),
        name="hyena_filter_fft",
    )(ktile, *consts)


def _conv3_time(x, w):
    lane = lax.broadcasted_iota(I32, x.shape, 2)
    zero = jnp.zeros_like(x[:1])
    xm = pltpu.roll(x, 1, axis=2)
    xm = jnp.where(lane == 0, jnp.concatenate([zero, xm[:-1]], axis=0), xm)
    xp = pltpu.roll(x, LANES - 1, axis=2)
    xp = jnp.where(lane == LANES - 1, jnp.concatenate([xp[1:], zero], axis=0), xp)
    return w[0] * xm + w[1] * x + w[2] * xp + w[3]


def _hyena_body(sk_ref, v_ref, x1_ref, x2_ref, wv_ref, w1_ref, w2_ref, k0r_ref, k0i_ref, k1r_ref, k1i_ref,
                f1p_ref, f2_ref, f2c_ref, f3p_ref, tr_ref, ti_ref, o_ref, *, rc):
    ct = pl.program_id(0)
    tr = tr_ref[...]
    ti = ti_ref[...]
    np_ = v_ref.shape[2]
    nb = v_ref.shape[0]

    def load(ref, w_ref, g):
        parts = [jnp.swapaxes(_conv3_time(ref[bb, g], w_ref[:, g]), 0, 1) for bb in range(nb)]
        return jnp.concatenate(parts, axis=1)

    def long_conv(z3, kr_ref, ki_ref, g):
        r = FFT_R
        gsub = SUBLANES // HY_SPLIT
        wd = tr.dtype

        def split(m):
            return m[:, :r].astype(wd).reshape(gsub, r, r), m[:, r:].astype(wd).reshape(gsub, r, r)

        def join_t(re, im):
            return jnp.concatenate([jnp.swapaxes(re, 1, 2), jnp.swapaxes(im, 1, 2)], axis=-1).reshape(
                gsub * r, 2 * r).astype(BF16)

        subs = range(HY_SPLIT)
        zs = [z3[s * gsub:(s + 1) * gsub] for s in subs]
        a = [_dot(jnp.swapaxes(z.astype(wd), 1, 2).reshape(gsub * r, r).astype(BF16), f1p_ref[...]) for z in zs]
        p = []
        for m in a:
            ar, ai = split(m)
            p.append(join_t(ar * tr - ai * ti, ar * ti + ai * tr))
        x = [_dot(m, f2_ref[...]) for m in p]
        y = []
        for s, m in zip(subs, x):
            rows = pl.ds(g * SUBLANES + s * gsub, gsub)
            kr = kr_ref[rows].reshape(gsub * r, r)
            ki = ki_ref[rows].reshape(gsub * r, r)
            xr, xi = m[:, :r].astype(wd), m[:, r:].astype(wd)
            y.append(jnp.concatenate([xr * kr - xi * ki, xr * ki + xi * kr], axis=-1).astype(BF16))
        b = [_dot(m, f2c_ref[...]) for m in y]
        c = []
        for m in b:
            br, bi = split(m)
            c.append(join_t(br * tr + bi * ti, bi * tr - br * ti))
        yt = [_dot(m, f3p_ref[...]) for m in c]
        return jnp.concatenate([jnp.swapaxes(m.reshape(gsub, r, r), 1, 2) for m in yt], axis=0)

    def group(g, carry):
        cbase = ct * rc + g * SUBLANES

        def skip(z3, order):
            return jnp.stack([sk_ref[order, cbase + ci] * z3[ci] for ci in range(SUBLANES)], axis=0)

        v3 = load(v_ref, wv_ref, g)
        z1 = load(x1_ref, w1_ref, g) * (long_conv(v3, k0r_ref, k0i_ref, g) + skip(v3, 0))
        out = load(x2_ref, w2_ref, g) * (long_conv(z1, k1r_ref, k1i_ref, g) + skip(z1, 1))
        for bb in range(nb):
            o_ref[bb, g] = jnp.swapaxes(out[:, bb * np_:(bb + 1) * np_, :], 0, 1)
        return carry

    lax.fori_loop(0, rc // SUBLANES, group, 0)


def _hyena(hy5, cwt, skip, kfr, kfi, tables):
    bsz, n8, np_, _, _ = hy5.shape
    assert bsz % 2 == 0 and 2 * np_ == FFT_R
    rc = RC_HY
    r8 = rc // SUBLANES
    nct = D_HY // rc
    r = FFT_R

    def xspec(off):
        return pl.BlockSpec((2, r8, np_, SUBLANES, LANES), lambda c, b: (b, c + off * nct, 0, 0, 0))

    def wspec(off):
        return pl.BlockSpec((HY_CONV + 1, r8, SUBLANES, LANES), lambda c, b: (0, c + off * nct, 0, 0))

    def kspec(order):
        return pl.BlockSpec((rc, r, r), lambda c, b: (c + order * nct, 0, 0))

    consts = [tables[k] for k in ("f1p", "f2", "f2c", "f3p", "tr16", "ti16")]
    return pl.pallas_call(
        functools.partial(_hyena_body, rc=rc),
        grid=(nct, bsz // 2),
        in_specs=[pl.BlockSpec(memory_space=pltpu.SMEM), xspec(0), xspec(1), xspec(2), wspec(0), wspec(1), wspec(2),
                  kspec(0), kspec(0), kspec(1), kspec(1)] + _const_specs(consts, 2),
        out_specs=pl.BlockSpec((2, r8, np_, SUBLANES, LANES), lambda c, b: (b, c, 0, 0, 0)),
        out_shape=jax.ShapeDtypeStruct((bsz, D_HY // SUBLANES, np_, SUBLANES, LANES), F32),
        compiler_params=_params(("parallel", "parallel")),
        name="hyena",
    )(skip, hy5, hy5, hy5, cwt, cwt, cwt, kfr, kfi, kfr, kfi, *consts)


def _gelu_tanh(x):
    return 0.5 * x * (1.0 + jnp.tanh(math.sqrt(2.0 / math.pi) * (x + 0.044715 * (x * x * x))))


def _mix_body(hf_ref, hb_ref, gate_ref, yhy_ref, x_ref, gl_ref, gh_ref, wl_ref, wh_ref, gffn_ref,
              wrh_ref, wrl_ref, rb_ref, tri_ref, ltri_ref, xmid_ref, h_ref, routec_ref, routet_ref, cnt_ref):
    tm = x_ref.shape[1]
    y_lru = (hf_ref[0, 0] + hb_ref[0, 0]) * _gelu_tanh(gate_ref[0])
    nl = _rms(y_lru, gl_ref[...]).astype(BF16)
    gh = gh_ref[...]
    nts = []
    for j in range(tm // LANES):
        yt = yhy_ref[0, :, j, :, :].reshape(D_HY, LANES)
        nts.append((yt * lax.rsqrt(jnp.mean(yt * yt, axis=0, keepdims=True) + EPS) * gh).astype(BF16))
    nt = jnp.concatenate(nts, axis=1)
    mix = _dot(nl, wl_ref[...]) + lax.dot_general(nt, wh_ref[...], (((0,), (0,)), ((), ())),
                                                   preferred_element_type=F32)
    xmid_ref[0] = x_ref[0] + mix
    h = _rms(xmid_ref[0], gffn_ref[...])
    h_ref[...] = h.astype(BF16)

    hh, hl = _split_bf16(h)
    logits = _dot(hh, wrh_ref[...]) + _dot(hl, wrh_ref[...]) + _dot(hh, wrl_ref[...]) + rb_ref[...]
    lt = logits.T[:ROUTE_ROWS]
    row = lax.broadcasted_iota(I32, lt.shape, 0)
    big = jnp.int32(1 << 20)
    is_g = row < N_GROUPS
    lg = jnp.where(is_g, lt, NEG_BIG)
    mg = jnp.max(lg, axis=0, keepdims=True)
    gidx = jnp.min(jnp.where(lg == mg, row, big), axis=0, keepdims=True)
    g_p = 1.0 / jnp.sum(jnp.where(is_g, jnp.exp(lg - mg), 0.0), axis=0, keepdims=True)
    e_lo = N_GROUPS + EXPERTS_PER_GROUP * gidx
    is_e = jnp.logical_and(row >= e_lo, row < e_lo + EXPERTS_PER_GROUP)
    le = jnp.where(is_e, lt, NEG_BIG)
    m1 = jnp.max(le, axis=0, keepdims=True)
    i1 = jnp.min(jnp.where(le == m1, row, big), axis=0, keepdims=True)
    le2 = jnp.where(row == i1, NEG_BIG, le)
    m2 = jnp.max(le2, axis=0, keepdims=True)
    i2 = jnp.min(jnp.where(le2 == m2, row, big), axis=0, keepdims=True)
    ratio = jnp.exp(m2 - m1)
    gate1 = g_p / (1.0 + ratio)
    gate2 = g_p * ratio / (1.0 + ratio)
    oh1 = row == i1
    oh2 = row == i2
    oh = jnp.where(oh1, 1.0, jnp.where(oh2, 1.0, 0.0))
    before = _dot(oh.astype(BF16), tri_ref[...])
    cnt = jnp.sum(oh, axis=1, keepdims=True)
    nchunk = jnp.floor((cnt + (ROW_CHUNK - 1)) * (1.0 / ROW_CHUNK))
    run_start = ROW_CHUNK * _dot(ltri_ref[...], jnp.broadcast_to(nchunk, (ROUTE_ROWS, LANES)).astype(BF16))[:, 0:1]
    pos = run_start + before
    l0 = jnp.sum(jnp.where(oh1, pos, 0.0), axis=0, keepdims=True)
    l1 = jnp.sum(jnp.where(oh2, pos, 0.0), axis=0, keepdims=True)
    cnt_ref[0] = cnt
    route_t = jnp.concatenate([l0, l1, gate1, gate2, jnp.zeros((SUBLANES - 4, tm), F32)], axis=0)
    routet_ref[0] = route_t
    routec_ref[...] = jnp.concatenate([route_t, jnp.zeros((LANES - SUBLANES, tm), F32)], axis=0).T[:, :SUBLANES]


def _mix_router(h4, lru, yhy5, x, gl, gh, wl, wh, gffn, wrh, wrl, rb, tri, ltri):
    bsz, seq, d = x.shape
    tm = TM_MIX
    nt = seq // tm
    n_tok = bsz * seq
    c2 = lambda b, i: (0, 0)
    return pl.pallas_call(
        _mix_body,
        grid=(bsz, nt),
        in_specs=[
            pl.BlockSpec((1, 1, tm, D_LRU), lambda b, i: (0, b, i, 0)),
            pl.BlockSpec((1, 1, tm, D_LRU), lambda b, i: (1, b, i, 0)),
            pl.BlockSpec((1, tm, D_LRU), lambda b, i: (b, i, 1)),
            pl.BlockSpec((1, D_HY // SUBLANES, tm // LANES, SUBLANES, LANES), lambda b, i: (b, 0, i, 0, 0)),
            pl.BlockSpec((1, tm, d), lambda b, i: (b, i, 0)),
            pl.BlockSpec((1, D_LRU), c2),
            pl.BlockSpec((D_HY, 1), c2),
            pl.BlockSpec((D_LRU, d), c2),
            pl.BlockSpec((D_HY, d), c2),
            pl.BlockSpec((1, d), c2),
            pl.BlockSpec((d, LANES), c2),
            pl.BlockSpec((d, LANES), c2),
            pl.BlockSpec((1, LANES), c2),
            pl.BlockSpec((tm, tm), c2),
            pl.BlockSpec((ROUTE_ROWS, ROUTE_ROWS), c2),
        ],
        out_specs=[
            pl.BlockSpec((1, tm, d), lambda b, i: (b, i, 0)),
            pl.BlockSpec((tm, d), lambda b, i: (b * nt + i, 0)),
            pl.BlockSpec((tm, SUBLANES), lambda b, i: (b * nt + i, 0)),
            pl.BlockSpec((1, SUBLANES, tm), lambda b, i: (b * nt + i, 0, 0)),
            pl.BlockSpec((1, ROUTE_ROWS, 1), lambda b, i: (b * nt + i, 0, 0)),
        ],
        out_shape=[
            jax.ShapeDtypeStruct((bsz, seq, d), F32),
            jax.ShapeDtypeStruct((n_tok, d), BF16),
            jax.ShapeDtypeStruct((n_tok, SUBLANES), F32),
            jax.ShapeDtypeStruct((n_tok // tm, SUBLANES, tm), F32),
            jax.ShapeDtypeStruct((n_tok // tm, ROUTE_ROWS, 1), F32),
        ],
        compiler_params=_params(("parallel", "parallel")),
        name="mix_router",
    )(h4, h4, lru, yhy5, x, gl, gh, wl, wh, gffn, wrh, wrl, rb, tri, ltri)


def _start_chunks(i, tch_ref, dst_ref, make_copy):
    n = tch_ref[i]

    def start(k, priority):
        make_copy(k * ROW_CHUNK, dst_ref[i * MAX_CHUNKS + k], ROW_CHUNK).start(priority=priority)

    def per_pair(k2, c):
        start(2 * k2, 0)
        start(2 * k2 + 1, 1)
        return c

    lax.fori_loop(0, n // 2, per_pair, 0)

    @pl.when(n % 2 == 1)
    def _():
        start(n - 1, 0)


def _wait_chunks(count, make_copy):
    nbits = MAX_CHUNKS.bit_length()
    for j in reversed(range(nbits)):
        @pl.when((count >> j) & 1 == 1)
        def _():
            make_copy(0, 0, ROW_CHUNK << j).wait()


def _dispatch_body(dst_ref, tch_ref, tn_ref, ts_ref, nu_ref, h_ref, routet_ref, xb_ref,
                   buf, zbuf, sem, zsem):
    i = pl.program_id(0)
    n_tiles = pl.num_programs(0)
    tm = h_ref.shape[0]
    lbuf = buf.shape[1]
    n_blocks = xb_ref.shape[0] // MOE_BLK
    slot = i % 2

    def aligned(r):
        return pl.ds(pl.multiple_of(r, ROW_CHUNK), ROW_CHUNK)

    @pl.when(i == 0)
    def _():
        zbuf[...] = jnp.zeros_like(zbuf)

        def zero_chunk(r):
            return pltpu.make_async_copy(zbuf.at[pl.ds(0, ROW_CHUNK)], xb_ref.at[aligned(r)], zsem)

        def zero_block(j):
            return pltpu.make_async_copy(zbuf, xb_ref.at[pl.ds(pl.multiple_of(j * MOE_BLK, MOE_BLK), MOE_BLK)], zsem)

        def per_expert(e, total):
            def per_chunk(k, c):
                zero_chunk(ts_ref[e] + k * ROW_CHUNK).start()
                return c

            lax.fori_loop(0, tn_ref[e], per_chunk, 0)
            return total + tn_ref[e]

        total = lax.fori_loop(0, N_EXPERTS, per_expert, 0)

        def start_block(j, c):
            zero_block(j).start()
            return c

        lax.fori_loop(nu_ref[0], n_blocks, start_block, 0)

        def wait_chunk(k, c):
            zero_chunk(0).wait()
            return c

        lax.fori_loop(0, total, wait_chunk, 0)

        def wait_block(j, c):
            zero_block(0).wait()
            return c

        lax.fori_loop(nu_ref[0], n_blocks, wait_block, 0)

    l0t = routet_ref[0, 0:1, :].astype(I32)
    l1t = routet_ref[0, 1:2, :].astype(I32)
    row = lax.broadcasted_iota(I32, (lbuf, tm), 0)
    perm = jnp.where(row == l0t, 1.0, jnp.where(row == l1t, 1.0, 0.0)).astype(BF16)
    srt = _dot(perm, h_ref[...]).astype(BF16)

    def copy_from(s):
        def make_copy(src_row, dst_row, rows):
            return pltpu.make_async_copy(buf.at[s, pl.ds(pl.multiple_of(src_row, ROW_CHUNK), rows)],
                                         xb_ref.at[pl.ds(pl.multiple_of(dst_row, ROW_CHUNK), rows)], sem.at[s])
        return make_copy

    @pl.when(i >= 2)
    def _():
        _wait_chunks(tch_ref[i - 2], copy_from(slot))

    buf[slot] = srt
    _start_chunks(i, tch_ref, dst_ref, copy_from(slot))

    @pl.when(i == n_tiles - 1)
    def _():
        @pl.when(i >= 1)
        def _():
            _wait_chunks(tch_ref[i - 1], copy_from(1 - slot))

        _wait_chunks(tch_ref[i], copy_from(slot))


def _dispatch(chunk_dst, tile_nch, tail_n, tail_s, n_used, h, route_t, n_slots):
    n_tok, d = h.shape
    tm = TM_MIX
    grid_spec = pltpu.PrefetchScalarGridSpec(
        num_scalar_prefetch=5,
        grid=(n_tok // tm,),
        in_specs=[
            pl.BlockSpec((tm, d), lambda i, *_: (i, 0)),
            pl.BlockSpec((1, SUBLANES, tm), lambda i, *_: (i, 0, 0)),
        ],
        out_specs=pl.BlockSpec(memory_space=pl.ANY),
        scratch_shapes=[pltpu.VMEM((2, LOCAL_ROWS, d), BF16), pltpu.VMEM((MOE_BLK, d), BF16),
                        pltpu.SemaphoreType.DMA((2,)), pltpu.SemaphoreType.DMA(())],
    )
    return pl.pallas_call(
        _dispatch_body,
        grid_spec=grid_spec,
        out_shape=jax.ShapeDtypeStruct((n_slots, d), BF16),
        compiler_params=_params(("arbitrary",)),
        name="moe_dispatch",
    )(chunk_dst, tile_nch, tail_n, tail_s, n_used, h, route_t)


def _expert_body(be_ref, fl_ref, sw_ref, nu_ref, xb_ref, w1_ref, w3_ref, w2_ref, yb_ref,
                 w1_sc, w3_sc, w2_sc, w1_ring, w3_ring, w2_ring, sem):
    j = pl.program_id(0)
    n_used = nu_ref[0]
    used = j < n_used

    def weight_copies(jj):
        e = be_ref[jj]
        s = sw_ref[jj] % WEIGHT_SLOTS
        return [pltpu.make_async_copy(w_ref.at[e], ring.at[s], sem.at[s])
                for w_ref, ring in ((w1_ref, w1_ring), (w3_ref, w3_ring), (w2_ref, w2_ring))]

    def fetch_if_first(jj):
        @pl.when(jnp.logical_and(jj < n_used, fl_ref[jnp.minimum(jj, pl.num_programs(0) - 1)] == 1))
        def _():
            for cp in weight_copies(jj):
                cp.start()

    @pl.when(j == 0)
    def _():
        for jj in range(WEIGHT_LOOKAHEAD):
            fetch_if_first(jnp.int32(jj))

    fetch_if_first(j + WEIGHT_LOOKAHEAD)

    @pl.when(jnp.logical_and(used, fl_ref[j] == 1))
    def _():
        for cp in weight_copies(j):
            cp.wait()
        s = sw_ref[j] % WEIGHT_SLOTS
        w1_sc[...] = w1_ring[s].astype(BF16)
        w3_sc[...] = w3_ring[s].astype(BF16)
        w2_sc[...] = w2_ring[s].astype(BF16)

    @pl.when(used)
    def _():
        xb = xb_ref[...]
        a = _dot(xb, w1_sc[...])
        b = _dot(xb, w3_sc[...])
        act = (a * _sigmoid(a) * b).astype(BF16)
        yb_ref[...] = _dot(act, w2_sc[...]).astype(BF16)

    @pl.when(jnp.logical_not(used))
    def _():
        yb_ref[...] = jnp.zeros_like(yb_ref)


def _experts(block_e, first_flag, n_used, xb, w1, w3, w2):
    n_slots, d = xb.shape
    blk = MOE_BLK
    de = w1.shape[2]
    switch_no = jnp.cumsum(first_flag) - 1
    hbm = pl.BlockSpec(memory_space=pl.ANY)
    grid_spec = pltpu.PrefetchScalarGridSpec(
        num_scalar_prefetch=4,
        grid=(n_slots // blk,),
        in_specs=[pl.BlockSpec((blk, d), lambda j, *_: (j, 0)), hbm, hbm, hbm],
        out_specs=pl.BlockSpec((blk, d), lambda j, *_: (j, 0)),
        scratch_shapes=[pltpu.VMEM((d, de), BF16), pltpu.VMEM((d, de), BF16), pltpu.VMEM((de, d), BF16),
                        pltpu.VMEM((WEIGHT_SLOTS, d, de), F32), pltpu.VMEM((WEIGHT_SLOTS, d, de), F32),
                        pltpu.VMEM((WEIGHT_SLOTS, de, d), F32), pltpu.SemaphoreType.DMA((WEIGHT_SLOTS,))],
    )
    return pl.pallas_call(
        _expert_body,
        grid_spec=grid_spec,
        out_shape=jax.ShapeDtypeStruct((n_slots, d), BF16),
        compiler_params=_params(("arbitrary",)),
        name="moe_experts",
    )(block_e, first_flag, switch_no.astype(I32), n_used, xb, w1, w3, w2)


def _combine_body(dst_ref, tch_ref, xmid_ref, route_ref, g_ref, yb_ref, o_ref, buf, sem):
    i = pl.program_id(0)
    n_tiles = pl.num_programs(0)
    tm = xmid_ref.shape[0]
    lbuf = buf.shape[1]
    slot = i % 2

    def copy_into(s):
        def make_copy(local_row, slot_row, rows):
            return pltpu.make_async_copy(yb_ref.at[pl.ds(pl.multiple_of(slot_row, ROW_CHUNK), rows)],
                                         buf.at[s, pl.ds(pl.multiple_of(local_row, ROW_CHUNK), rows)], sem.at[s])
        return make_copy

    @pl.when(i == 0)
    def _():
        buf[...] = jnp.zeros_like(buf)
        _start_chunks(i, tch_ref, dst_ref, copy_into(slot))

    @pl.when(i + 1 < n_tiles)
    def _():
        _start_chunks(i + 1, tch_ref, dst_ref, copy_into(1 - slot))

    route = route_ref[...]
    l0 = route[:, 0:1].astype(I32)
    l1 = route[:, 1:2].astype(I32)
    lane = lax.broadcasted_iota(I32, (tm, lbuf), 1)
    gmat = jnp.where(lane == l0, route[:, 2:3], jnp.where(lane == l1, route[:, 3:4], 0.0)).astype(BF16)
    _wait_chunks(tch_ref[i], copy_into(slot))
    y = xmid_ref[...] + _dot(gmat, buf[slot])
    o_ref[...] = _rms(y, g_ref[...])


def _combine(chunk_dst, tile_nch, xmid, route, g, yb):
    n_tok, d = xmid.shape
    tm = TM_MIX
    grid_spec = pltpu.PrefetchScalarGridSpec(
        num_scalar_prefetch=2,
        grid=(n_tok // tm,),
        in_specs=[
            pl.BlockSpec((tm, d), lambda i, *_: (i, 0)),
            pl.BlockSpec((tm, SUBLANES), lambda i, *_: (i, 0)),
            pl.BlockSpec((1, d), lambda i, *_: (0, 0)),
            pl.BlockSpec(memory_space=pl.ANY),
        ],
        out_specs=pl.BlockSpec((tm, d), lambda i, *_: (i, 0)),
        scratch_shapes=[pltpu.VMEM((2, LOCAL_ROWS, d), BF16), pltpu.SemaphoreType.DMA((2,))],
    )
    return pl.pallas_call(
        _combine_body,
        grid_spec=grid_spec,
        out_shape=jax.ShapeDtypeStruct((n_tok, d), F32),
        compiler_params=_params(("arbitrary",)),
        name="moe_combine",
    )(chunk_dst, tile_nch, xmid, route, g, yb)


def _filter_features(seq):
    pos = jnp.arange(seq, dtype=F32)
    t = jnp.linspace(0.0, 1.0, seq, dtype=F32)
    w = (2.0 * math.pi / seq) * pos
    bands = jnp.linspace(1e-4, HY_BANDS - 1, HY_BANDS, dtype=F32)
    ang = w[:, None] * bands[None, :]
    feats = jnp.concatenate([t[:, None], jnp.cos(ang), jnp.sin(ang)], axis=-1)
    featst = jnp.pad(feats, ((0, 0), (0, HY_EMB_PAD - HY_EMB))).T[None]
    return featst, t[None, :]


def _gate_weights(wr, wi):
    half = D_LRU // 2
    hph = half // LRU_HEAD_DIM
    eye = jnp.eye(hph, dtype=wr.dtype)[:, None, :, None]

    def block_diag(w):
        return (eye * w[:, :, None, :]).reshape(half, half)

    out = [jnp.concatenate([block_diag(wr[hh * hph:(hh + 1) * hph]), block_diag(wi[hh * hph:(hh + 1) * hph])], axis=1)
           for hh in range(2)]
    return jnp.stack(out, axis=0).astype(BF16)


def _layer(x, l, p):
    bsz, seq, d = x.shape
    n_tok = bsz * seq
    row = lambda a: a.reshape(1, -1).astype(F32)
    col = lambda a: a.reshape(-1, 1).astype(F32)

    w_in = p["w_in"][l]
    wl = w_in[:, :2 * D_LRU].astype(BF16)
    wht = w_in[:, 2 * D_LRU:].T.astype(BF16)
    lru, hy5 = _inproj(x, row(p["norm_mix_g"][l]), wl, wht)

    wg = jnp.stack([_gate_weights(p["lru_wr_f"][l], p["lru_wi_f"][l]),
                    _gate_weights(p["lru_wr_b"][l], p["lru_wi_b"][l])], axis=0)
    stack2 = lambda a, b: jnp.stack([a.reshape(1, -1), b.reshape(1, -1)], axis=0).astype(F32)
    h4 = _lru(lru, p["lru_conv_w"][l], row(p["lru_conv_b"][l]), wg,
              stack2(p["lru_br_f"][l], p["lru_br_b"][l]), stack2(p["lru_bi_f"][l], p["lru_bi_b"][l]),
              stack2(p["lru_lambda_f"][l], p["lru_lambda_b"][l]))

    tables = _fft_tables()
    featst, t_f = _filter_features(seq)
    w1t = jnp.pad(p["hy_filt_w1"][l].T, ((0, 0), (0, HY_EMB_PAD - HY_EMB)))
    z2t = _filt_mlp(featst, w1t, col(p["hy_filt_b1"][l]), col(p["hy_filt_freq1"][l]),
                    p["hy_filt_w2"][l].T, col(p["hy_filt_b2"][l]), col(p["hy_filt_freq2"][l]))
    w3 = p["hy_filt_w3"][l]
    nfr = HY_ORDER * D_HY
    deltas = jnp.abs(jnp.linspace(math.log(HY_DECAY_TARGET) / HY_SLOW_DECAY_PCT,
                                  math.log(HY_DECAY_TARGET) / HY_FAST_DECAY_PCT, D_HY, dtype=F32))
    ktile = _filt_time(z2t, w3[:, :nfr].T, w3[:, nfr:].T, col(jnp.tile(deltas, HY_ORDER)), t_f)
    kfr, kfi = _filt_fft(ktile, tables)
    cwt = jnp.concatenate([p["hy_conv_w"][l], p["hy_conv_b"][l][None, :]], axis=0)
    cwt = jnp.broadcast_to(cwt.reshape(HY_CONV + 1, -1, SUBLANES, 1), (HY_CONV + 1, cwt.shape[1] // SUBLANES, SUBLANES, LANES))
    yhy5 = _hyena(hy5, cwt, p["hy_skip"][l], kfr, kfi, tables)

    w_out = p["w_out"][l]
    wr_cat = jnp.concatenate([p["router_group_w"][l], p["router_expert_w"][l]], axis=1)
    wr_cat = jnp.pad(wr_cat, ((0, 0), (0, LANES - wr_cat.shape[1])))
    wrh, wrl = _split_bf16(wr_cat)
    rb = jnp.concatenate([p["router_group_b"][l], p["router_expert_b"][l]])
    rb = jnp.pad(rb, (0, LANES - rb.shape[0])).reshape(1, LANES)
    earlier = lambda n: (jnp.arange(n)[:, None] < jnp.arange(n)[None, :]).astype(BF16)
    xmid, h, route, route_t, tile_cnt = _mix_router(
        h4, lru, yhy5, x, row(p["grp_norm_lru_g"][l]), col(p["grp_norm_hy_g"][l]),
        w_out[:D_LRU].astype(BF16), w_out[D_LRU:].astype(BF16), row(p["norm_ffn_g"][l]), wrh, wrl, rb,
        earlier(TM_MIX), earlier(ROUTE_ROWS).T)

    blk = MOE_BLK
    ch = ROW_CHUNK
    n_tiles = n_tok // TM_MIX
    n_blocks = (2 * n_tok + n_tiles * N_EXPERTS * (ch - 1)) // blk + N_EXPERTS
    cnt = tile_cnt[:, N_GROUPS:N_GROUPS + N_EXPERTS, 0].astype(I32)
    run = (cnt + ch - 1) // ch * ch
    tot_e = jnp.sum(run, axis=0)
    padded_e = (tot_e + blk - 1) // blk * blk
    pad_end = jnp.cumsum(padded_e)
    pad_start = pad_end - padded_e
    gstart = pad_start[None, :] + jnp.cumsum(run, axis=0) - run
    lstart = jnp.cumsum(run, axis=1) - run
    flat = lambda a: a.reshape(-1).astype(I32)
    tile_nch = flat(jnp.sum(run, axis=1) // ch)
    nch_end = jnp.cumsum(run // ch, axis=1)
    k = jnp.arange(MAX_CHUNKS, dtype=I32)
    base = gstart - lstart
    step = base[:, 1:] - base[:, :-1]
    passed = (k[None, :, None] >= nch_end[:, None, :-1]).astype(I32)
    chunk_dst = flat(base[:, :1] + jnp.sum(passed * step[:, None, :], axis=2) + k[None, :] * ch)
    tail_n = flat((padded_e - tot_e) // ch)
    tail_s = flat(pad_start + tot_e)
    block_start = jnp.arange(n_blocks, dtype=I32) * blk
    block_e = jnp.minimum(jnp.sum(block_start[:, None] >= pad_end[None, :], axis=1), N_EXPERTS - 1).astype(I32)
    first_flag = jnp.concatenate([jnp.ones((1,), I32), (block_e[1:] != block_e[:-1]).astype(I32)])
    n_used = (pad_end[-1:] // blk).astype(I32)

    xb = _dispatch(chunk_dst, tile_nch, tail_n, tail_s, n_used, h, route_t, n_blocks * blk)
    yb = _experts(block_e, first_flag, n_used, xb, p["exp_w1"][l], p["exp_w3"][l], p["exp_w2"][l])
    return xmid.reshape(n_tok, d), route, (chunk_dst, tile_nch), yb


def kernel(x, norm_mix_g, w_in, lru_conv_w, lru_conv_b, lru_wr_f, lru_br_f, lru_wi_f, lru_bi_f, lru_lambda_f, lru_wr_b, lru_br_b, lru_wi_b, lru_bi_b, lru_lambda_b, hy_conv_w, hy_conv_b, hy_filt_w1, hy_filt_b1, hy_filt_freq1, hy_filt_w2, hy_filt_b2, hy_filt_freq2, hy_filt_w3, hy_skip, grp_norm_lru_g, grp_norm_hy_g, w_out, norm_ffn_g, router_group_w, router_group_b, router_expert_w, router_expert_b, exp_w1, exp_w3, exp_w2, norm_final_g):
    p = dict(locals())
    bsz, seq, d = x.shape
    assert d == D_MODEL and 2 * seq == FFT_R * FFT_R and w_in.shape[0] == 1
    xmid, route, (chunk_dst, tile_nch), yb = _layer(x, 0, p)
    out = _combine(chunk_dst, tile_nch, xmid, route, norm_final_g.reshape(1, d), yb)
    return out.reshape(bsz, seq, d)
```

```python
import functools
import math

import numpy as np
import jax
import jax.numpy as jnp
from jax import lax
from jax.experimental import pallas as pl
from jax.experimental.pallas import tpu as pltpu

F32 = jnp.float32
BF16 = jnp.bfloat16
I32 = jnp.int32

D_MODEL = 1024
D_LRU = 512
D_HY = 512
LRU_HEADS = 8
LRU_HEAD_DIM = D_LRU // LRU_HEADS
LRU_CONV = 4
LRU_C = 8.0
HY_ORDER = 2
HY_CONV = 3
HY_BANDS = 16
HY_EMB = 2 * HY_BANDS + 1
HY_EMB_PAD = 40
HY_FFN = 64
HY_FAST_DECAY_PCT = 0.3
HY_SLOW_DECAY_PCT = 1.5
HY_DECAY_TARGET = 1e-2
N_GROUPS = 4
EXPERTS_PER_GROUP = 8
N_EXPERTS = N_GROUPS * EXPERTS_PER_GROUP
D_EXPERT = D_MODEL // 2
EPS = 1e-6

LANES = 128
SUBLANES = 8
FFT_R = 128
VMEM_LIMIT = 56 * 1024 * 1024

TM_IN = 1024
TC_LRU = 1024
RC_HY = 32
RF_KF = 64
HY_SPLIT = 2
RB_FILT = 128
TM_MIX = 512
MOE_BLK = 1024
WEIGHT_LOOKAHEAD = 2
WEIGHT_SLOTS = WEIGHT_LOOKAHEAD + 1
ROW_CHUNK = 16
LOCAL_ROWS = -(-(2 * TM_MIX + N_EXPERTS * (ROW_CHUNK - 1)) // LANES) * LANES
MAX_CHUNKS = LOCAL_ROWS // ROW_CHUNK
ROUTE_ROWS = -(-(N_GROUPS + N_EXPERTS) // SUBLANES) * SUBLANES
NEG_BIG = -1e30


def _params(sem, vmem=VMEM_LIMIT):
    return pltpu.CompilerParams(dimension_semantics=sem, vmem_limit_bytes=vmem)


def _rms(x, g):
    return x * lax.rsqrt(jnp.mean(x * x, axis=-1, keepdims=True) + EPS) * g


def _sigmoid(x):
    return 1.0 / (1.0 + jnp.exp(-x))


def _split_bf16(a):
    hi = a.astype(BF16)
    lo = (a - hi.astype(F32)).astype(BF16)
    return hi, lo


def _dot(a, b):
    return jnp.dot(a, b, preferred_element_type=F32)


def _dot3(a, b):
    ah, al = _split_bf16(a)
    bh, bl = _split_bf16(b)
    return _dot(ah, bh) + _dot(al, bh) + _dot(ah, bl)


def _inproj_body(x_ref, g_ref, wl_ref, wht_ref, lru_ref, hy_ref):
    hn = _rms(x_ref[0], g_ref[...]).astype(BF16)
    lru_ref[0] = _dot(hn, wl_ref[...])
    hyt = lax.dot_general(wht_ref[...], hn, (((1,), (1,)), ((), ())), preferred_element_type=F32)
    nrow = hyt.shape[0] // SUBLANES
    for j in range(hyt.shape[1] // LANES):
        hy_ref[0, :, j, :, :] = hyt[:, LANES * j:LANES * (j + 1)].reshape(nrow, SUBLANES, LANES)


def _inproj(x, g, wl, wht):
    bsz, seq, d = x.shape
    nl = wl.shape[1]
    nh = wht.shape[0]
    tm = TM_IN
    return pl.pallas_call(
        _inproj_body,
        grid=(bsz, seq // tm),
        in_specs=[
            pl.BlockSpec((1, tm, d), lambda b, i: (b, i, 0)),
            pl.BlockSpec((1, d), lambda b, i: (0, 0)),
            pl.BlockSpec((d, nl), lambda b, i: (0, 0)),
            pl.BlockSpec((nh, d), lambda b, i: (0, 0)),
        ],
        out_specs=[
            pl.BlockSpec((1, tm, nl), lambda b, i: (b, i, 0)),
            pl.BlockSpec((1, nh // SUBLANES, tm // LANES, SUBLANES, LANES), lambda b, i: (b, 0, i, 0, 0)),
        ],
        out_shape=[
            jax.ShapeDtypeStruct((bsz, seq, nl), F32),
            jax.ShapeDtypeStruct((bsz, nh // SUBLANES, seq // LANES, SUBLANES, LANES), F32),
        ],
        compiler_params=_params(("parallel", "parallel")),
        name="inproj",
    )(x, g, wl, wht)


def _sqrt_one_minus_sq(a, neg_log_a):
    om = jnp.maximum(jnp.tanh(neg_log_a) * (1.0 + a * a), 1e-30)
    return om * lax.rsqrt(om)


def _group_scan(a3, b3, reverse):
    sub = lax.broadcasted_iota(I32, a3.shape, 1)
    for s in (1, 2, 4):
        if reverse:
            a_sh = pltpu.roll(a3, SUBLANES - s, axis=1)
            b_sh = pltpu.roll(b3, SUBLANES - s, axis=1)
            m = sub < SUBLANES - s
        else:
            a_sh = pltpu.roll(a3, s, axis=1)
            b_sh = pltpu.roll(b3, s, axis=1)
            m = sub >= s
        b3 = jnp.where(m, a3 * b_sh + b3, b3)
        a3 = jnp.where(m, a3 * a_sh, a3)
    return a3, b3


def _lru_body(cur_ref, prev_ref, next_ref, cw_ref, cb_ref, wg_ref, br_ref, bi_ref, lam_ref, o_ref,
              carry_sc, *, nt, tc):
    d = pl.program_id(1)
    i = pl.program_id(2)
    c = jnp.where(d == 0, i, nt - 1 - i)
    ns = SUBLANES
    ng = tc // ns
    half = D_LRU // 2

    @pl.when(i == 0)
    def _():
        carry_sc[...] = jnp.zeros_like(carry_sc)

    xs = jnp.swapaxes(cur_ref[0].reshape(ns, ng, D_LRU), 0, 1)
    prev = jnp.where(c > 0, prev_ref[0], 0.0)
    nxt = jnp.where(c < nt - 1, next_ref[0], 0.0)
    sub = lax.broadcasted_iota(I32, (ns, D_LRU), 0)

    def prev_segment(slab, halo_row):
        return jnp.where(sub == 0, halo_row, pltpu.roll(slab, 1, axis=0))

    def next_segment(slab, halo_row):
        return jnp.where(sub == ns - 1, halo_row, pltpu.roll(slab, ns - 1, axis=0))

    xext = jnp.concatenate([prev_segment(xs[ng - 2], prev[ns - 2:ns - 1])[None],
                            prev_segment(xs[ng - 1], prev[ns - 1:ns])[None], xs,
                            next_segment(xs[0], nxt[0:1])[None]], axis=0)
    cw = cw_ref[...]
    xc = cb_ref[...] + cw[0:1] * xext[0:ng]
    for k in range(1, LRU_CONV):
        xc = xc + cw[k:k + 1] * xext[k:k + ng]
    xc = xc.reshape(tc, D_LRU)

    lam = lam_ref[0]
    nlam = -lam
    softplus = jnp.maximum(nlam, 0.0) + jnp.log1p(jnp.exp(-jnp.abs(nlam)))
    half_c = (0.5 * LRU_C) * softplus

    def gates(hh):
        sl = slice(half * hh, half * (hh + 1))
        xh = xc[:, sl]
        logits = _dot(xh.astype(BF16), wg_ref[0, hh])
        tr_ = jnp.tanh(0.5 * (logits[:, :half] + br_ref[0][:, sl]))
        gi = 0.5 + 0.5 * jnp.tanh(0.5 * (logits[:, half:] + bi_ref[0][:, sl]))
        neg_log_a = half_c[:, sl] * (1.0 + tr_)
        a = jnp.exp(-neg_log_a)
        b = _sqrt_one_minus_sq(a, neg_log_a) * (gi * xh)
        return a.reshape(ng, ns, half), b.reshape(ng, ns, half)

    def run(reverse):
        subh = lax.broadcasted_iota(I32, (ns, half), 0)
        for hh in range(2):
            sl = slice(half * hh, half * (hh + 1))
            a3, b3 = gates(hh)
            h = jnp.zeros((ns, half), F32)
            p = jnp.ones((ns, half), F32)
            hs = [None] * ng
            ps = [None] * ng
            for g in (range(ng - 1, -1, -1) if reverse else range(ng)):
                h = a3[g] * h + b3[g]
                p = a3[g] * p
                hs[g] = h
                ps[g] = p
            ac, hc = _group_scan(p[None], h[None], reverse)
            c_in = carry_sc[:, sl]
            end = ac[0] * c_in + hc[0]
            if reverse:
                seg_in = jnp.where(subh == ns - 1, c_in, pltpu.roll(end, ns - 1, axis=0))
                carry_sc[:, sl] = jnp.broadcast_to(end[0:1], end.shape)
            else:
                seg_in = jnp.where(subh == 0, c_in, pltpu.roll(end, 1, axis=0))
                carry_sc[:, sl] = jnp.broadcast_to(end[ns - 1:ns], end.shape)
            h3 = jnp.stack(hs, axis=0) + jnp.stack(ps, axis=0) * seg_in[None]
            o_ref[0, 0, :, sl] = jnp.swapaxes(h3, 0, 1).reshape(tc, half)

    @pl.when(d == 0)
    def _():
        run(False)

    @pl.when(d == 1)
    def _():
        run(True)


def _lru(lru, cw, cb, wg, br, bi, lam):
    bsz, seq, _ = lru.shape
    tc = TC_LRU
    nt = seq // tc
    r8 = tc // SUBLANES
    nrow8 = seq // SUBLANES

    def cidx(d, i):
        return jnp.where(d == 0, i, nt - 1 - i)

    return pl.pallas_call(
        functools.partial(_lru_body, nt=nt, tc=tc),
        grid=(bsz, 2, nt),
        in_specs=[
            pl.BlockSpec((1, tc, D_LRU), lambda b, d, i: (b, cidx(d, i), 0)),
            pl.BlockSpec((1, SUBLANES, D_LRU), lambda b, d, i: (b, jnp.maximum(cidx(d, i) * r8 - 1, 0), 0)),
            pl.BlockSpec((1, SUBLANES, D_LRU), lambda b, d, i: (b, jnp.minimum((cidx(d, i) + 1) * r8, nrow8 - 1), 0)),
            pl.BlockSpec((LRU_CONV, D_LRU), lambda b, d, i: (0, 0)),
            pl.BlockSpec((1, D_LRU), lambda b, d, i: (0, 0)),
            pl.BlockSpec((1, 2, D_LRU // 2, D_LRU), lambda b, d, i: (d, 0, 0, 0)),
            pl.BlockSpec((1, 1, D_LRU), lambda b, d, i: (d, 0, 0)),
            pl.BlockSpec((1, 1, D_LRU), lambda b, d, i: (d, 0, 0)),
            pl.BlockSpec((1, 1, D_LRU), lambda b, d, i: (d, 0, 0)),
        ],
        out_specs=pl.BlockSpec((1, 1, tc, D_LRU), lambda b, d, i: (d, b, cidx(d, i), 0)),
        out_shape=jax.ShapeDtypeStruct((2, bsz, seq, D_LRU), F32),
        scratch_shapes=[pltpu.VMEM((SUBLANES, D_LRU), F32)],
        compiler_params=_params(("parallel", "arbitrary", "arbitrary")),
        name="rg_lru",
    )(lru, lru, lru, cw, cb, wg, br, bi, lam)


def _fft_tables():
    r = FFT_R
    n = r * r
    idx = np.arange(r, dtype=np.float64)
    ang = 2.0 * np.pi * np.outer(idx, idx) / r
    wr, wi = np.cos(ang), -np.sin(ang)
    angt = 2.0 * np.pi * np.outer(idx, idx) / n
    tr, ti = np.cos(angt), -np.sin(angt)
    h = r // 2
    f1 = np.concatenate([wr, wi], axis=1)
    f1p = np.block([[wr[:h], wi[:h]], [-wi[:h], wr[:h]]])
    f2 = np.block([[wr, wi], [-wi, wr]])
    f2c = np.block([[wr, -wi], [wi, wr]])
    f3p = np.block([[wr[:, :h], -wi[:, :h]], [wi[:, :h], wr[:, :h]]]) / n
    as_bf16 = lambda a: jnp.asarray(a, F32).astype(BF16)
    return dict(f1=as_bf16(f1), f1p=as_bf16(f1p), f2=as_bf16(f2), f2c=as_bf16(f2c), f3p=as_bf16(f3p),
                tr16=as_bf16(tr), ti16=as_bf16(ti))


def _const_specs(arrays, nargs):
    return [pl.BlockSpec(a.shape, (lambda nd: (lambda *_: (0,) * nd))(a.ndim)) for a in arrays]


def _fft_fwd(x3, f1, f2, tr, ti):
    g = x3.shape[0]
    r = FFT_R
    wd = tr.dtype
    xt = jnp.swapaxes(x3.astype(wd), 1, 2).reshape(g * r, r)
    a = _dot(xt.astype(BF16), f1)
    ar = a[:, :r].astype(wd).reshape(g, r, r)
    ai = a[:, r:].astype(wd).reshape(g, r, r)
    pr = ar * tr - ai * ti
    pi = ar * ti + ai * tr
    pt = jnp.concatenate([jnp.swapaxes(pr, 1, 2), jnp.swapaxes(pi, 1, 2)], axis=-1).reshape(g * r, 2 * r)
    x = _dot(pt.astype(BF16), f2)
    return x[:, :r], x[:, r:]


def _filt_mlp_body(ft_ref, w1_ref, b1_ref, f1_ref, w2_ref, b2_ref, f2_ref, o_ref):
    z = jnp.sin(f1_ref[...] * (_dot3(w1_ref[...], ft_ref[0]) + b1_ref[...]))
    o_ref[0] = jnp.sin(f2_ref[...] * (_dot3(w2_ref[...], z) + b2_ref[...]))


def _filt_mlp(featst, w1t, b1, f1, w2t, b2, f2):
    nd, ke, seq = featst.shape
    col = lambda d: (0, 0)
    return pl.pallas_call(
        _filt_mlp_body,
        grid=(nd,),
        in_specs=[
            pl.BlockSpec((1, ke, seq), lambda d: (d, 0, 0)),
            pl.BlockSpec((HY_FFN, ke), col),
            pl.BlockSpec((HY_FFN, 1), col),
            pl.BlockSpec((HY_FFN, 1), col),
            pl.BlockSpec((HY_FFN, HY_FFN), col),
            pl.BlockSpec((HY_FFN, 1), col),
            pl.BlockSpec((HY_FFN, 1), col),
        ],
        out_specs=pl.BlockSpec((1, HY_FFN, seq), lambda d: (d, 0, 0)),
        out_shape=jax.ShapeDtypeStruct((nd, HY_FFN, seq), F32),
        compiler_params=_params(("parallel",)),
        name="hyena_filter_mlp",
    )(featst, w1t, b1, f1, w2t, b2, f2)


def _filt_time_body(z_ref, wf_ref, wb_ref, dl_ref, tf_ref, o_ref):
    seq = z_ref.shape[2]
    decay = jnp.exp(-tf_ref[...] * dl_ref[...])
    hf = _dot3(wf_ref[...], z_ref[0]) * decay
    hb = _dot3(wb_ref[...], z_ref[0]) * decay
    lane = lax.broadcasted_iota(I32, hb.shape, 1)
    hb = jnp.where(lane == 0, 0.0, hb)
    norm = jnp.sum(jnp.abs(hf), axis=-1, keepdims=True) + jnp.sum(jnp.abs(hb), axis=-1, keepdims=True) + EPS
    hf = hf / norm
    hb = hb / norm
    nrow = hf.shape[0] // SUBLANES
    nch = seq // LANES
    lane1 = lax.broadcasted_iota(I32, (hf.shape[0], LANES), 1)
    mirror = (LANES - lane1) % LANES
    for j in range(nch):
        o_ref[:, j, :, :] = hf[:, LANES * j:LANES * (j + 1)].reshape(nrow, SUBLANES, LANES)
        src = nch - 1 - j
        back = jnp.take_along_axis(hb[:, LANES * src:LANES * (src + 1)], mirror, axis=1)
        first = hb[:, LANES * (src + 1):LANES * (src + 1) + 1] if j > 0 else 0.0
        o_ref[:, nch + j, :, :] = jnp.where(lane1 == 0, first, back).reshape(nrow, SUBLANES, LANES)


def _filt_time(z2t, w3f, w3b, delta_rows, t_f):
    nrows = w3f.shape[0]
    seq = z2t.shape[2]
    rb = RB_FILT
    return pl.pallas_call(
        _filt_time_body,
        grid=(nrows // rb,),
        in_specs=[
            pl.BlockSpec((1, HY_FFN, seq), lambda r: (0, 0, 0)),
            pl.BlockSpec((rb, HY_FFN), lambda r: (r, 0)),
            pl.BlockSpec((rb, HY_FFN), lambda r: (r, 0)),
            pl.BlockSpec((rb, 1), lambda r: (r, 0)),
            pl.BlockSpec((1, seq), lambda r: (0, 0)),
        ],
        out_specs=pl.BlockSpec((rb // SUBLANES, 2 * seq // LANES, SUBLANES, LANES), lambda r: (r, 0, 0, 0)),
        out_shape=jax.ShapeDtypeStruct((nrows // SUBLANES, 2 * seq // LANES, SUBLANES, LANES), F32),
        compiler_params=_params(("parallel",)),
        name="hyena_filter_time",
    )(z2t, w3f, w3b, delta_rows, t_f)


def _filt_fft_body(k_ref, f1_ref, f2_ref, tr_ref, ti_ref, kr_ref, ki_ref):
    tr = tr_ref[...]
    ti = ti_ref[...]
    for g in range(k_ref.shape[0]):
        x3 = jnp.swapaxes(k_ref[g], 0, 1)
        xr, xi = _fft_fwd(x3, f1_ref[...], f2_ref[...], tr, ti)
        kr_ref[SUBLANES * g:SUBLANES * (g + 1)] = xr.astype(BF16).reshape(SUBLANES, FFT_R, FFT_R)
        ki_ref[SUBLANES * g:SUBLANES * (g + 1)] = xi.astype(BF16).reshape(SUBLANES, FFT_R, FFT_R)


def _filt_fft(ktile, tables):
    n8, r, _, _ = ktile.shape
    nrows = n8 * SUBLANES
    rf = RF_KF
    out_spec = pl.BlockSpec((rf, r, r), lambda i: (i, 0, 0))
    consts = [tables[k] for k in ("f1", "f2", "tr16", "ti16")]
    return pl.pallas_call(
        _filt_fft_body,
        grid=(nrows // rf,),
        in_specs=[pl.BlockSpec((rf // SUBLANES, r, SUBLANES, LANES), lambda i: (i, 0, 0, 0))] + _const_specs(consts, 1),
        out_specs=[out_spec, out_spec],
        out_shape=[jax.ShapeDtypeStruct((nrows, r, r), BF16)] * 2,
        compiler_params=_params(("parallel",)),
        name="hyena_filter_fft",
    )(ktile, *consts)


def _conv3_time(x, w):
    lane = lax.broadcasted_iota(I32, x.shape, 2)
    zero = jnp.zeros_like(x[:1])
    xm = pltpu.roll(x, 1, axis=2)
    xm = jnp.where(lane == 0, jnp.concatenate([zero, xm[:-1]], axis=0), xm)
    xp = pltpu.roll(x, LANES - 1, axis=2)
    xp = jnp.where(lane == LANES - 1, jnp.concatenate([xp[1:], zero], axis=0), xp)
    return w[0] * xm + w[1] * x + w[2] * xp + w[3]


def _hyena_body(sk_ref, v_ref, x1_ref, x2_ref, wv_ref, w1_ref, w2_ref, k0r_ref, k0i_ref, k1r_ref, k1i_ref,
                f1p_ref, f2_ref, f2c_ref, f3p_ref, tr_ref, ti_ref, o_ref, *, rc):
    ct = pl.program_id(0)
    tr = tr_ref[...]
    ti = ti_ref[...]
    np_ = v_ref.shape[2]
    nb = v_ref.shape[0]

    def load(ref, w_ref, g):
        parts = [jnp.swapaxes(_conv3_time(ref[bb, g], w_ref[:, g]), 0, 1) for bb in range(nb)]
        return jnp.concatenate(parts, axis=1)

    def long_conv(z3, kr_ref, ki_ref, g):
        r = FFT_R
        gsub = SUBLANES // HY_SPLIT
        wd = tr.dtype

        def split(m):
            return m[:, :r].astype(wd).reshape(gsub, r, r), m[:, r:].astype(wd).reshape(gsub, r, r)

        def join_t(re, im):
            return jnp.concatenate([jnp.swapaxes(re, 1, 2), jnp.swapaxes(im, 1, 2)], axis=-1).reshape(
                gsub * r, 2 * r).astype(BF16)

        subs = range(HY_SPLIT)
        zs = [z3[s * gsub:(s + 1) * gsub] for s in subs]
        a = [_dot(jnp.swapaxes(z.astype(wd), 1, 2).reshape(gsub * r, r).astype(BF16), f1p_ref[...]) for z in zs]
        p = []
        for m in a:
            ar, ai = split(m)
            p.append(join_t(ar * tr - ai * ti, ar * ti + ai * tr))
        x = [_dot(m, f2_ref[...]) for m in p]
        y = []
        for s, m in zip(subs, x):
            rows = pl.ds(g * SUBLANES + s * gsub, gsub)
            kr = kr_ref[rows].reshape(gsub * r, r)
            ki = ki_ref[rows].reshape(gsub * r, r)
            xr, xi = m[:, :r].astype(wd), m[:, r:].astype(wd)
            y.append(jnp.concatenate([xr * kr - xi * ki, xr * ki + xi * kr], axis=-1).astype(BF16))
        b = [_dot(m, f2c_ref[...]) for m in y]
        c = []
        for m in b:
            br, bi = split(m)
            c.append(join_t(br * tr + bi * ti, bi * tr - br * ti))
        yt = [_dot(m, f3p_ref[...]) for m in c]
        return jnp.concatenate([jnp.swapaxes(m.reshape(gsub, r, r), 1, 2) for m in yt], axis=0)

    def group(g, carry):
        cbase = ct * rc + g * SUBLANES

        def skip(z3, order):
            return jnp.stack([sk_ref[order, cbase + ci] * z3[ci] for ci in range(SUBLANES)], axis=0)

        v3 = load(v_ref, wv_ref, g)
        z1 = load(x1_ref, w1_ref, g) * (long_conv(v3, k0r_ref, k0i_ref, g) + skip(v3, 0))
        out = load(x2_ref, w2_ref, g) * (long_conv(z1, k1r_ref, k1i_ref, g) + skip(z1, 1))
        for bb in range(nb):
            o_ref[bb, g] = jnp.swapaxes(out[:, bb * np_:(bb + 1) * np_, :], 0, 1)
        return carry

    lax.fori_loop(0, rc // SUBLANES, group, 0)


def _hyena(hy5, cwt, skip, kfr, kfi, tables):
    bsz, n8, np_, _, _ = hy5.shape
    assert bsz % 2 == 0 and 2 * np_ == FFT_R
    rc = RC_HY
    r8 = rc // SUBLANES
    nct = D_HY // rc
    r = FFT_R

    def xspec(off):
        return pl.BlockSpec((2, r8, np_, SUBLANES, LANES), lambda c, b: (b, c + off * nct, 0, 0, 0))

    def wspec(off):
        return pl.BlockSpec((HY_CONV + 1, r8, SUBLANES, LANES), lambda c, b: (0, c + off * nct, 0, 0))

    def kspec(order):
        return pl.BlockSpec((rc, r, r), lambda c, b: (c + order * nct, 0, 0))

    consts = [tables[k] for k in ("f1p", "f2", "f2c", "f3p", "tr16", "ti16")]
    return pl.pallas_call(
        functools.partial(_hyena_body, rc=rc),
        grid=(nct, bsz // 2),
        in_specs=[pl.BlockSpec(memory_space=pltpu.SMEM), xspec(0), xspec(1), xspec(2), wspec(0), wspec(1), wspec(2),
                  kspec(0), kspec(0), kspec(1), kspec(1)] + _const_specs(consts, 2),
        out_specs=pl.BlockSpec((2, r8, np_, SUBLANES, LANES), lambda c, b: (b, c, 0, 0, 0)),
        out_shape=jax.ShapeDtypeStruct((bsz, D_HY // SUBLANES, np_, SUBLANES, LANES), F32),
        compiler_params=_params(("parallel", "parallel")),
        name="hyena",
    )(skip, hy5, hy5, hy5, cwt, cwt, cwt, kfr, kfi, kfr, kfi, *consts)


def _gelu_tanh(x):
    return 0.5 * x * (1.0 + jnp.tanh(math.sqrt(2.0 / math.pi) * (x + 0.044715 * (x * x * x))))


def _mix_body(hf_ref, hb_ref, gate_ref, yhy_ref, x_ref, gl_ref, gh_ref, wl_ref, wh_ref, gffn_ref,
              wrh_ref, wrl_ref, rb_ref, tri_ref, ltri_ref, xmid_ref, h_ref, routec_ref, routet_ref, cnt_ref):
    tm = x_ref.shape[1]
    y_lru = (hf_ref[0, 0] + hb_ref[0, 0]) * _gelu_tanh(gate_ref[0])
    nl = _rms(y_lru, gl_ref[...]).astype(BF16)
    gh = gh_ref[...]
    nts = []
    for j in range(tm // LANES):
        yt = yhy_ref[0, :, j, :, :].reshape(D_HY, LANES)
        nts.append((yt * lax.rsqrt(jnp.mean(yt * yt, axis=0, keepdims=True) + EPS) * gh).astype(BF16))
    nt = jnp.concatenate(nts, axis=1)
    mix = _dot(nl, wl_ref[...]) + lax.dot_general(nt, wh_ref[...], (((0,), (0,)), ((), ())),
                                                   preferred_element_type=F32)
    xmid_ref[0] = x_ref[0] + mix
    h = _rms(xmid_ref[0], gffn_ref[...])
    h_ref[...] = h.astype(BF16)

    hh, hl = _split_bf16(h)
    logits = _dot(hh, wrh_ref[...]) + _dot(hl, wrh_ref[...]) + _dot(hh, wrl_ref[...]) + rb_ref[...]
    lt = logits.T[:ROUTE_ROWS]
    row = lax.broadcasted_iota(I32, lt.shape, 0)
    big = jnp.int32(1 << 20)
    is_g = row < N_GROUPS
    lg = jnp.where(is_g, lt, NEG_BIG)
    mg = jnp.max(lg, axis=0, keepdims=True)
    gidx = jnp.min(jnp.where(lg == mg, row, big), axis=0, keepdims=True)
    g_p = 1.0 / jnp.sum(jnp.where(is_g, jnp.exp(lg - mg), 0.0), axis=0, keepdims=True)
    e_lo = N_GROUPS + EXPERTS_PER_GROUP * gidx
    is_e = jnp.logical_and(row >= e_lo, row < e_lo + EXPERTS_PER_GROUP)
    le = jnp.where(is_e, lt, NEG_BIG)
    m1 = jnp.max(le, axis=0, keepdims=True)
    i1 = jnp.min(jnp.where(le == m1, row, big), axis=0, keepdims=True)
    le2 = jnp.where(row == i1, NEG_BIG, le)
    m2 = jnp.max(le2, axis=0, keepdims=True)
    i2 = jnp.min(jnp.where(le2 == m2, row, big), axis=0, keepdims=True)
    ratio = jnp.exp(m2 - m1)
    gate1 = g_p / (1.0 + ratio)
    gate2 = g_p * ratio / (1.0 + ratio)
    oh1 = row == i1
    oh2 = row == i2
    oh = jnp.where(oh1, 1.0, jnp.where(oh2, 1.0, 0.0))
    before = _dot(oh.astype(BF16), tri_ref[...])
    cnt = jnp.sum(oh, axis=1, keepdims=True)
    nchunk = jnp.floor((cnt + (ROW_CHUNK - 1)) * (1.0 / ROW_CHUNK))
    run_start = ROW_CHUNK * _dot(ltri_ref[...], jnp.broadcast_to(nchunk, (ROUTE_ROWS, LANES)).astype(BF16))[:, 0:1]
    pos = run_start + before
    l0 = jnp.sum(jnp.where(oh1, pos, 0.0), axis=0, keepdims=True)
    l1 = jnp.sum(jnp.where(oh2, pos, 0.0), axis=0, keepdims=True)
    cnt_ref[0] = cnt
    route_t = jnp.concatenate([l0, l1, gate1, gate2, jnp.zeros((SUBLANES - 4, tm), F32)], axis=0)
    routet_ref[0] = route_t
    routec_ref[...] = jnp.concatenate([route_t, jnp.zeros((LANES - SUBLANES, tm), F32)], axis=0).T[:, :SUBLANES]


def _mix_router(h4, lru, yhy5, x, gl, gh, wl, wh, gffn, wrh, wrl, rb, tri, ltri):
    bsz, seq, d = x.shape
    tm = TM_MIX
    nt = seq // tm
    n_tok = bsz * seq
    c2 = lambda b, i: (0, 0)
    return pl.pallas_call(
        _mix_body,
        grid=(bsz, nt),
        in_specs=[
            pl.BlockSpec((1, 1, tm, D_LRU), lambda b, i: (0, b, i, 0)),
            pl.BlockSpec((1, 1, tm, D_LRU), lambda b, i: (1, b, i, 0)),
            pl.BlockSpec((1, tm, D_LRU), lambda b, i: (b, i, 1)),
            pl.BlockSpec((1, D_HY // SUBLANES, tm // LANES, SUBLANES, LANES), lambda b, i: (b, 0, i, 0, 0)),
            pl.BlockSpec((1, tm, d), lambda b, i: (b, i, 0)),
            pl.BlockSpec((1, D_LRU), c2),
            pl.BlockSpec((D_HY, 1), c2),
            pl.BlockSpec((D_LRU, d), c2),
            pl.BlockSpec((D_HY, d), c2),
            pl.BlockSpec((1, d), c2),
            pl.BlockSpec((d, LANES), c2),
            pl.BlockSpec((d, LANES), c2),
            pl.BlockSpec((1, LANES), c2),
            pl.BlockSpec((tm, tm), c2),
            pl.BlockSpec((ROUTE_ROWS, ROUTE_ROWS), c2),
        ],
        out_specs=[
            pl.BlockSpec((1, tm, d), lambda b, i: (b, i, 0)),
            pl.BlockSpec((tm, d), lambda b, i: (b * nt + i, 0)),
            pl.BlockSpec((tm, SUBLANES), lambda b, i: (b * nt + i, 0)),
            pl.BlockSpec((1, SUBLANES, tm), lambda b, i: (b * nt + i, 0, 0)),
            pl.BlockSpec((1, ROUTE_ROWS, 1), lambda b, i: (b * nt + i, 0, 0)),
        ],
        out_shape=[
            jax.ShapeDtypeStruct((bsz, seq, d), F32),
            jax.ShapeDtypeStruct((n_tok, d), BF16),
            jax.ShapeDtypeStruct((n_tok, SUBLANES), F32),
            jax.ShapeDtypeStruct((n_tok // tm, SUBLANES, tm), F32),
            jax.ShapeDtypeStruct((n_tok // tm, ROUTE_ROWS, 1), F32),
        ],
        compiler_params=_params(("parallel", "parallel")),
        name="mix_router",
    )(h4, h4, lru, yhy5, x, gl, gh, wl, wh, gffn, wrh, wrl, rb, tri, ltri)


def _start_chunks(i, tch_ref, dst_ref, make_copy):
    n = tch_ref[i]

    def start(k, priority):
        make_copy(k * ROW_CHUNK, dst_ref[i * MAX_CHUNKS + k], ROW_CHUNK).start(priority=priority)

    def per_pair(k2, c):
        start(2 * k2, 0)
        start(2 * k2 + 1, 1)
        return c

    lax.fori_loop(0, n // 2, per_pair, 0)

    @pl.when(n % 2 == 1)
    def _():
        start(n - 1, 0)


def _wait_chunks(count, make_copy):
    nbits = MAX_CHUNKS.bit_length()
    for j in reversed(range(nbits)):
        @pl.when((count >> j) & 1 == 1)
        def _():
            make_copy(0, 0, ROW_CHUNK << j).wait()


def _dispatch_body(dst_ref, tch_ref, tn_ref, ts_ref, nu_ref, h_ref, routet_ref, xb_ref,
                   buf, zbuf, sem, zsem):
    i = pl.program_id(0)
    n_tiles = pl.num_programs(0)
    tm = h_ref.shape[0]
    lbuf = buf.shape[1]
    n_blocks = xb_ref.shape[0] // MOE_BLK
    slot = i % 2

    def aligned(r):
        return pl.ds(pl.multiple_of(r, ROW_CHUNK), ROW_CHUNK)

    @pl.when(i == 0)
    def _():
        zbuf[...] = jnp.zeros_like(zbuf)

        def zero_chunk(r):
            return pltpu.make_async_copy(zbuf.at[pl.ds(0, ROW_CHUNK)], xb_ref.at[aligned(r)], zsem)

        def zero_block(j):
            return pltpu.make_async_copy(zbuf, xb_ref.at[pl.ds(pl.multiple_of(j * MOE_BLK, MOE_BLK), MOE_BLK)], zsem)

        def per_expert(e, total):
            def per_chunk(k, c):
                zero_chunk(ts_ref[e] + k * ROW_CHUNK).start()
                return c

            lax.fori_loop(0, tn_ref[e], per_chunk, 0)
            return total + tn_ref[e]

        total = lax.fori_loop(0, N_EXPERTS, per_expert, 0)

        def start_block(j, c):
            zero_block(j).start()
            return c

        lax.fori_loop(nu_ref[0], n_blocks, start_block, 0)

        def wait_chunk(k, c):
            zero_chunk(0).wait()
            return c

        lax.fori_loop(0, total, wait_chunk, 0)

        def wait_block(j, c):
            zero_block(0).wait()
            return c

        lax.fori_loop(nu_ref[0], n_blocks, wait_block, 0)

    l0t = routet_ref[0, 0:1, :].astype(I32)
    l1t = routet_ref[0, 1:2, :].astype(I32)
    row = lax.broadcasted_iota(I32, (lbuf, tm), 0)
    perm = jnp.where(row == l0t, 1.0, jnp.where(row == l1t, 1.0, 0.0)).astype(BF16)
    srt = _dot(perm, h_ref[...]).astype(BF16)

    def copy_from(s):
        def make_copy(src_row, dst_row, rows):
            return pltpu.make_async_copy(buf.at[s, pl.ds(pl.multiple_of(src_row, ROW_CHUNK), rows)],
                                         xb_ref.at[pl.ds(pl.multiple_of(dst_row, ROW_CHUNK), rows)], sem.at[s])
        return make_copy

    @pl.when(i >= 2)
    def _():
        _wait_chunks(tch_ref[i - 2], copy_from(slot))

    buf[slot] = srt
    _start_chunks(i, tch_ref, dst_ref, copy_from(slot))

    @pl.when(i == n_tiles - 1)
    def _():
        @pl.when(i >= 1)
        def _():
            _wait_chunks(tch_ref[i - 1], copy_from(1 - slot))

        _wait_chunks(tch_ref[i], copy_from(slot))


def _dispatch(chunk_dst, tile_nch, tail_n, tail_s, n_used, h, route_t, n_slots):
    n_tok, d = h.shape
    tm = TM_MIX
    grid_spec = pltpu.PrefetchScalarGridSpec(
        num_scalar_prefetch=5,
        grid=(n_tok // tm,),
        in_specs=[
            pl.BlockSpec((tm, d), lambda i, *_: (i, 0)),
            pl.BlockSpec((1, SUBLANES, tm), lambda i, *_: (i, 0, 0)),
        ],
        out_specs=pl.BlockSpec(memory_space=pl.ANY),
        scratch_shapes=[pltpu.VMEM((2, LOCAL_ROWS, d), BF16), pltpu.VMEM((MOE_BLK, d), BF16),
                        pltpu.SemaphoreType.DMA((2,)), pltpu.SemaphoreType.DMA(())],
    )
    return pl.pallas_call(
        _dispatch_body,
        grid_spec=grid_spec,
        out_shape=jax.ShapeDtypeStruct((n_slots, d), BF16),
        compiler_params=_params(("arbitrary",)),
        name="moe_dispatch",
    )(chunk_dst, tile_nch, tail_n, tail_s, n_used, h, route_t)


def _expert_body(be_ref, fl_ref, sw_ref, nu_ref, xb_ref, w1_ref, w3_ref, w2_ref, yb_ref,
                 w1_sc, w3_sc, w2_sc, w1_ring, w3_ring, w2_ring, sem):
    j = pl.program_id(0)
    n_used = nu_ref[0]
    used = j < n_used

    def weight_copies(jj):
        e = be_ref[jj]
        s = sw_ref[jj] % WEIGHT_SLOTS
        return [pltpu.make_async_copy(w_ref.at[e], ring.at[s], sem.at[s])
                for w_ref, ring in ((w1_ref, w1_ring), (w3_ref, w3_ring), (w2_ref, w2_ring))]

    def fetch_if_first(jj):
        @pl.when(jnp.logical_and(jj < n_used, fl_ref[jnp.minimum(jj, pl.num_programs(0) - 1)] == 1))
        def _():
            for cp in weight_copies(jj):
                cp.start()

    @pl.when(j == 0)
    def _():
        for jj in range(WEIGHT_LOOKAHEAD):
            fetch_if_first(jnp.int32(jj))

    fetch_if_first(j + WEIGHT_LOOKAHEAD)

    @pl.when(jnp.logical_and(used, fl_ref[j] == 1))
    def _():
        for cp in weight_copies(j):
            cp.wait()
        s = sw_ref[j] % WEIGHT_SLOTS
        w1_sc[...] = w1_ring[s].astype(BF16)
        w3_sc[...] = w3_ring[s].astype(BF16)
        w2_sc[...] = w2_ring[s].astype(BF16)

    @pl.when(used)
    def _():
        xb = xb_ref[...]
        a = _dot(xb, w1_sc[...])
        b = _dot(xb, w3_sc[...])
        act = (a * _sigmoid(a) * b).astype(BF16)
        yb_ref[...] = _dot(act, w2_sc[...]).astype(BF16)

    @pl.when(jnp.logical_not(used))
    def _():
        yb_ref[...] = jnp.zeros_like(yb_ref)


def _experts(block_e, first_flag, n_used, xb, w1, w3, w2):
    n_slots, d = xb.shape
    blk = MOE_BLK
    de = w1.shape[2]
    switch_no = jnp.cumsum(first_flag) - 1
    hbm = pl.BlockSpec(memory_space=pl.ANY)
    grid_spec = pltpu.PrefetchScalarGridSpec(
        num_scalar_prefetch=4,
        grid=(n_slots // blk,),
        in_specs=[pl.BlockSpec((blk, d), lambda j, *_: (j, 0)), hbm, hbm, hbm],
        out_specs=pl.BlockSpec((blk, d), lambda j, *_: (j, 0)),
        scratch_shapes=[pltpu.VMEM((d, de), BF16), pltpu.VMEM((d, de), BF16), pltpu.VMEM((de, d), BF16),
                        pltpu.VMEM((WEIGHT_SLOTS, d, de), F32), pltpu.VMEM((WEIGHT_SLOTS, d, de), F32),
                        pltpu.VMEM((WEIGHT_SLOTS, de, d), F32), pltpu.SemaphoreType.DMA((WEIGHT_SLOTS,))],
    )
    return pl.pallas_call(
        _expert_body,
        grid_spec=grid_spec,
        out_shape=jax.ShapeDtypeStruct((n_slots, d), BF16),
        compiler_params=_params(("arbitrary",)),
        name="moe_experts",
    )(block_e, first_flag, switch_no.astype(I32), n_used, xb, w1, w3, w2)


def _combine_body(dst_ref, tch_ref, xmid_ref, route_ref, g_ref, yb_ref, o_ref, buf, sem):
    i = pl.program_id(0)
    n_tiles = pl.num_programs(0)
    tm = xmid_ref.shape[0]
    lbuf = buf.shape[1]
    slot = i % 2

    def copy_into(s):
        def make_copy(local_row, slot_row, rows):
            return pltpu.make_async_copy(yb_ref.at[pl.ds(pl.multiple_of(slot_row, ROW_CHUNK), rows)],
                                         buf.at[s, pl.ds(pl.multiple_of(local_row, ROW_CHUNK), rows)], sem.at[s])
        return make_copy

    @pl.when(i == 0)
    def _():
        buf[...] = jnp.zeros_like(buf)
        _start_chunks(i, tch_ref, dst_ref, copy_into(slot))

    @pl.when(i + 1 < n_tiles)
    def _():
        _start_chunks(i + 1, tch_ref, dst_ref, copy_into(1 - slot))

    route = route_ref[...]
    l0 = route[:, 0:1].astype(I32)
    l1 = route[:, 1:2].astype(I32)
    lane = lax.broadcasted_iota(I32, (tm, lbuf), 1)
    gmat = jnp.where(lane == l0, route[:, 2:3], jnp.where(lane == l1, route[:, 3:4], 0.0)).astype(BF16)
    _wait_chunks(tch_ref[i], copy_into(slot))
    y = xmid_ref[...] + _dot(gmat, buf[slot])
    o_ref[...] = _rms(y, g_ref[...])


def _combine(chunk_dst, tile_nch, xmid, route, g, yb):
    n_tok, d = xmid.shape
    tm = TM_MIX
    grid_spec = pltpu.PrefetchScalarGridSpec(
        num_scalar_prefetch=2,
        grid=(n_tok // tm,),
        in_specs=[
            pl.BlockSpec((tm, d), lambda i, *_: (i, 0)),
            pl.BlockSpec((tm, SUBLANES), lambda i, *_: (i, 0)),
            pl.BlockSpec((1, d), lambda i, *_: (0, 0)),
            pl.BlockSpec(memory_space=pl.ANY),
        ],
        out_specs=pl.BlockSpec((tm, d), lambda i, *_: (i, 0)),
        scratch_shapes=[pltpu.VMEM((2, LOCAL_ROWS, d), BF16), pltpu.SemaphoreType.DMA((2,))],
    )
    return pl.pallas_call(
        _combine_body,
        grid_spec=grid_spec,
        out_shape=jax.ShapeDtypeStruct((n_tok, d), F32),
        compiler_params=_params(("arbitrary",)),
        name="moe_combine",
    )(chunk_dst, tile_nch, xmid, route, g, yb)


def _filter_features(seq):
    pos = jnp.arange(seq, dtype=F32)
    t = jnp.linspace(0.0, 1.0, seq, dtype=F32)
    w = (2.0 * math.pi / seq) * pos
    bands = jnp.linspace(1e-4, HY_BANDS - 1, HY_BANDS, dtype=F32)
    ang = w[:, None] * bands[None, :]
    feats = jnp.concatenate([t[:, None], jnp.cos(ang), jnp.sin(ang)], axis=-1)
    featst = jnp.pad(feats, ((0, 0), (0, HY_EMB_PAD - HY_EMB))).T[None]
    return featst, t[None, :]


def _gate_weights(wr, wi):
    half = D_LRU // 2
    hph = half // LRU_HEAD_DIM
    eye = jnp.eye(hph, dtype=wr.dtype)[:, None, :, None]

    def block_diag(w):
        return (eye * w[:, :, None, :]).reshape(half, half)

    out = [jnp.concatenate([block_diag(wr[hh * hph:(hh + 1) * hph]), block_diag(wi[hh * hph:(hh + 1) * hph])], axis=1)
           for hh in range(2)]
    return jnp.stack(out, axis=0).astype(BF16)


def _layer(x, l, p):
    bsz, seq, d = x.shape
    n_tok = bsz * seq
    row = lambda a: a.reshape(1, -1).astype(F32)
    col = lambda a: a.reshape(-1, 1).astype(F32)

    w_in = p["w_in"][l]
    wl = w_in[:, :2 * D_LRU].astype(BF16)
    wht = w_in[:, 2 * D_LRU:].T.astype(BF16)
    lru, hy5 = _inproj(x, row(p["norm_mix_g"][l]), wl, wht)

    wg = jnp.stack([_gate_weights(p["lru_wr_f"][l], p["lru_wi_f"][l]),
                    _gate_weights(p["lru_wr_b"][l], p["lru_wi_b"][l])], axis=0)
    stack2 = lambda a, b: jnp.stack([a.reshape(1, -1), b.reshape(1, -1)], axis=0).astype(F32)
    h4 = _lru(lru, p["lru_conv_w"][l], row(p["lru_conv_b"][l]), wg,
              stack2(p["lru_br_f"][l], p["lru_br_b"][l]), stack2(p["lru_bi_f"][l], p["lru_bi_b"][l]),
              stack2(p["lru_lambda_f"][l], p["lru_lambda_b"][l]))

    tables = _fft_tables()
    featst, t_f = _filter_features(seq)
    w1t = jnp.pad(p["hy_filt_w1"][l].T, ((0, 0), (0, HY_EMB_PAD - HY_EMB)))
    z2t = _filt_mlp(featst, w1t, col(p["hy_filt_b1"][l]), col(p["hy_filt_freq1"][l]),
                    p["hy_filt_w2"][l].T, col(p["hy_filt_b2"][l]), col(p["hy_filt_freq2"][l]))
    w3 = p["hy_filt_w3"][l]
    nfr = HY_ORDER * D_HY
    deltas = jnp.abs(jnp.linspace(math.log(HY_DECAY_TARGET) / HY_SLOW_DECAY_PCT,
                                  math.log(HY_DECAY_TARGET) / HY_FAST_DECAY_PCT, D_HY, dtype=F32))
    ktile = _filt_time(z2t, w3[:, :nfr].T, w3[:, nfr:].T, col(jnp.tile(deltas, HY_ORDER)), t_f)
    kfr, kfi = _filt_fft(ktile, tables)
    cwt = jnp.concatenate([p["hy_conv_w"][l], p["hy_conv_b"][l][None, :]], axis=0)
    cwt = jnp.broadcast_to(cwt.reshape(HY_CONV + 1, -1, SUBLANES, 1), (HY_CONV + 1, cwt.shape[1] // SUBLANES, SUBLANES, LANES))
    yhy5 = _hyena(hy5, cwt, p["hy_skip"][l], kfr, kfi, tables)

    w_out = p["w_out"][l]
    wr_cat = jnp.concatenate([p["router_group_w"][l], p["router_expert_w"][l]], axis=1)
    wr_cat = jnp.pad(wr_cat, ((0, 0), (0, LANES - wr_cat.shape[1])))
    wrh, wrl = _split_bf16(wr_cat)
    rb = jnp.concatenate([p["router_group_b"][l], p["router_expert_b"][l]])
    rb = jnp.pad(rb, (0, LANES - rb.shape[0])).reshape(1, LANES)
    earlier = lambda n: (jnp.arange(n)[:, None] < jnp.arange(n)[None, :]).astype(BF16)
    xmid, h, route, route_t, tile_cnt = _mix_router(
        h4, lru, yhy5, x, row(p["grp_norm_lru_g"][l]), col(p["grp_norm_hy_g"][l]),
        w_out[:D_LRU].astype(BF16), w_out[D_LRU:].astype(BF16), row(p["norm_ffn_g"][l]), wrh, wrl, rb,
        earlier(TM_MIX), earlier(ROUTE_ROWS).T)

    blk = MOE_BLK
    ch = ROW_CHUNK
    n_tiles = n_tok // TM_MIX
    n_blocks = (2 * n_tok + n_tiles * N_EXPERTS * (ch - 1)) // blk + N_EXPERTS
    cnt = tile_cnt[:, N_GROUPS:N_GROUPS + N_EXPERTS, 0].astype(I32)
    run = (cnt + ch - 1) // ch * ch
    tot_e = jnp.sum(run, axis=0)
    padded_e = (tot_e + blk - 1) // blk * blk
    pad_end = jnp.cumsum(padded_e)
    pad_start = pad_end - padded_e
    gstart = pad_start[None, :] + jnp.cumsum(run, axis=0) - run
    lstart = jnp.cumsum(run, axis=1) - run
    flat = lambda a: a.reshape(-1).astype(I32)
    tile_nch = flat(jnp.sum(run, axis=1) // ch)
    nch_end = jnp.cumsum(run // ch, axis=1)
    k = jnp.arange(MAX_CHUNKS, dtype=I32)
    base = gstart - lstart
    step = base[:, 1:] - base[:, :-1]
    passed = (k[None, :, None] >= nch_end[:, None, :-1]).astype(I32)
    chunk_dst = flat(base[:, :1] + jnp.sum(passed * step[:, None, :], axis=2) + k[None, :] * ch)
    tail_n = flat((padded_e - tot_e) // ch)
    tail_s = flat(pad_start + tot_e)
    block_start = jnp.arange(n_blocks, dtype=I32) * blk
    block_e = jnp.minimum(jnp.sum(block_start[:, None] >= pad_end[None, :], axis=1), N_EXPERTS - 1).astype(I32)
    first_flag = jnp.concatenate([jnp.ones((1,), I32), (block_e[1:] != block_e[:-1]).astype(I32)])
    n_used = (pad_end[-1:] // blk).astype(I32)

    xb = _dispatch(chunk_dst, tile_nch, tail_n, tail_s, n_used, h, route_t, n_blocks * blk)
    yb = _experts(block_e, first_flag, n_used, xb, p["exp_w1"][l], p["exp_w3"][l], p["exp_w2"][l])
    return xmid.reshape(n_tok, d), route, (chunk_dst, tile_nch), yb


def kernel(x, norm_mix_g, w_in, lru_conv_w, lru_conv_b, lru_wr_f, lru_br_f, lru_wi_f, lru_bi_f, lru_lambda_f, lru_wr_b, lru_br_b, lru_wi_b, lru_bi_b, lru_lambda_b, hy_conv_w, hy_conv_b, hy_filt_w1, hy_filt_b1, hy_filt_freq1, hy_filt_w2, hy_filt_b2, hy_filt_freq2, hy_filt_w3, hy_skip, grp_norm_lru_g, grp_norm_hy_g, w_out, norm_ffn_g, router_group_w, router_group_b, router_expert_w, router_expert_b, exp_w1, exp_w3, exp_w2, norm_final_g):
    p = dict(locals())
    bsz, seq, d = x.shape
    assert d == D_MODEL and 2 * seq == FFT_R * FFT_R and w_in.shape[0] == 1
    xmid, route, (chunk_dst, tile_nch), yb = _layer(x, 0, p)
    out = _combine(chunk_dst, tile_nch, xmid, route, norm_final_g.reshape(1, d), yb)
    return out.reshape(bsz, seq, d)
```

```python
import functools
import math

import numpy as np
import jax
import jax.numpy as jnp
from jax import lax
from jax.experimental import pallas as pl
from jax.experimental.pallas import tpu as pltpu

F32 = jnp.float32
BF16 = jnp.bfloat16
I32 = jnp.int32

D_MODEL = 1024
D_LRU = 512
D_HY = 512
LRU_HEADS = 8
LRU_HEAD_DIM = D_LRU // LRU_HEADS
LRU_CONV = 4
LRU_C = 8.0
HY_ORDER = 2
HY_CONV = 3
HY_BANDS = 16
HY_EMB = 2 * HY_BANDS + 1
HY_EMB_PAD = 40
HY_FFN = 64
HY_FAST_DECAY_PCT = 0.3
HY_SLOW_DECAY_PCT = 1.5
HY_DECAY_TARGET = 1e-2
N_GROUPS = 4
EXPERTS_PER_GROUP = 8
N_EXPERTS = N_GROUPS * EXPERTS_PER_GROUP
D_EXPERT = D_MODEL // 2
EPS = 1e-6

LANES = 128
SUBLANES = 8
FFT_R = 128
VMEM_LIMIT = 56 * 1024 * 1024

TM_IN = 1024
TC_LRU = 1024
RC_HY = 32
RF_KF = 128
HY_SPLIT = 2
RB_FILT = 128
TM_MIX = 512
MOE_BLK = 512
WEIGHT_LOOKAHEAD = 2
WEIGHT_SLOTS = WEIGHT_LOOKAHEAD + 1
ROW_CHUNK = 16
LOCAL_ROWS = -(-(2 * TM_MIX + N_EXPERTS * (ROW_CHUNK - 1)) // LANES) * LANES
MAX_CHUNKS = LOCAL_ROWS // ROW_CHUNK
ROUTE_ROWS = -(-(N_GROUPS + N_EXPERTS) // SUBLANES) * SUBLANES
NEG_BIG = -1e30


def _params(sem, vmem=VMEM_LIMIT):
    return pltpu.CompilerParams(dimension_semantics=sem, vmem_limit_bytes=vmem)


def _rms(x, g):
    return x * lax.rsqrt(jnp.mean(x * x, axis=-1, keepdims=True) + EPS) * g


def _sigmoid(x):
    return 1.0 / (1.0 + jnp.exp(-x))


def _split_bf16(a):
    hi = a.astype(BF16)
    lo = (a - hi.astype(F32)).astype(BF16)
    return hi, lo


def _dot(a, b):
    return jnp.dot(a, b, preferred_element_type=F32)


def _dot3(a, b):
    ah, al = _split_bf16(a)
    bh, bl = _split_bf16(b)
    return _dot(ah, bh) + _dot(al, bh) + _dot(ah, bl)


def _inproj_body(x_ref, g_ref, wl_ref, wht_ref, lru_ref, hy_ref):
    hn = _rms(x_ref[0], g_ref[...]).astype(BF16)
    lru_ref[0] = _dot(hn, wl_ref[...])
    hyt = lax.dot_general(wht_ref[...], hn, (((1,), (1,)), ((), ())), preferred_element_type=F32)
    nrow = hyt.shape[0] // SUBLANES
    for j in range(hyt.shape[1] // LANES):
        hy_ref[0, :, j, :, :] = hyt[:, LANES * j:LANES * (j + 1)].reshape(nrow, SUBLANES, LANES)


def _inproj(x, g, wl, wht):
    bsz, seq, d = x.shape
    nl = wl.shape[1]
    nh = wht.shape[0]
    tm = TM_IN
    return pl.pallas_call(
        _inproj_body,
        grid=(bsz, seq // tm),
        in_specs=[
            pl.BlockSpec((1, tm, d), lambda b, i: (b, i, 0)),
            pl.BlockSpec((1, d), lambda b, i: (0, 0)),
            pl.BlockSpec((d, nl), lambda b, i: (0, 0)),
            pl.BlockSpec((nh, d), lambda b, i: (0, 0)),
        ],
        out_specs=[
            pl.BlockSpec((1, tm, nl), lambda b, i: (b, i, 0)),
            pl.BlockSpec((1, nh // SUBLANES, tm // LANES, SUBLANES, LANES), lambda b, i: (b, 0, i, 0, 0)),
        ],
        out_shape=[
            jax.ShapeDtypeStruct((bsz, seq, nl), F32),
            jax.ShapeDtypeStruct((bsz, nh // SUBLANES, seq // LANES, SUBLANES, LANES), F32),
        ],
        compiler_params=_params(("parallel", "parallel")),
        name="inproj",
    )(x, g, wl, wht)


def _sqrt_one_minus_sq(a, neg_log_a):
    om = jnp.maximum(jnp.tanh(neg_log_a) * (1.0 + a * a), 1e-30)
    return om * lax.rsqrt(om)


def _group_scan(a3, b3, reverse):
    sub = lax.broadcasted_iota(I32, a3.shape, 1)
    for s in (1, 2, 4):
        if reverse:
            a_sh = pltpu.roll(a3, SUBLANES - s, axis=1)
            b_sh = pltpu.roll(b3, SUBLANES - s, axis=1)
            m = sub < SUBLANES - s
        else:
            a_sh = pltpu.roll(a3, s, axis=1)
            b_sh = pltpu.roll(b3, s, axis=1)
            m = sub >= s
        b3 = jnp.where(m, a3 * b_sh + b3, b3)
        a3 = jnp.where(m, a3 * a_sh, a3)
    return a3, b3


def _lru_body(cur_ref, prev_ref, next_ref, cw_ref, cb_ref, wg_ref, br_ref, bi_ref, lam_ref, o_ref,
              carry_sc, *, nt, tc):
    d = pl.program_id(1)
    i = pl.program_id(2)
    c = jnp.where(d == 0, i, nt - 1 - i)
    ns = SUBLANES
    ng = tc // ns
    half = D_LRU // 2

    @pl.when(i == 0)
    def _():
        carry_sc[...] = jnp.zeros_like(carry_sc)

    xs = jnp.swapaxes(cur_ref[0].reshape(ns, ng, D_LRU), 0, 1)
    prev = jnp.where(c > 0, prev_ref[0], 0.0)
    nxt = jnp.where(c < nt - 1, next_ref[0], 0.0)
    sub = lax.broadcasted_iota(I32, (ns, D_LRU), 0)

    def prev_segment(slab, halo_row):
        return jnp.where(sub == 0, halo_row, pltpu.roll(slab, 1, axis=0))

    def next_segment(slab, halo_row):
        return jnp.where(sub == ns - 1, halo_row, pltpu.roll(slab, ns - 1, axis=0))

    xext = jnp.concatenate([prev_segment(xs[ng - 2], prev[ns - 2:ns - 1])[None],
                            prev_segment(xs[ng - 1], prev[ns - 1:ns])[None], xs,
                            next_segment(xs[0], nxt[0:1])[None]], axis=0)
    cw = cw_ref[...]
    xc = cb_ref[...] + cw[0:1] * xext[0:ng]
    for k in range(1, LRU_CONV):
        xc = xc + cw[k:k + 1] * xext[k:k + ng]
    xc = xc.reshape(tc, D_LRU)

    lam = lam_ref[0]
    nlam = -lam
    softplus = jnp.maximum(nlam, 0.0) + jnp.log1p(jnp.exp(-jnp.abs(nlam)))
    half_c = (0.5 * LRU_C) * softplus

    def gates(hh):
        sl = slice(half * hh, half * (hh + 1))
        xh = xc[:, sl]
        logits = _dot(xh.astype(BF16), wg_ref[0, hh])
        tr_ = jnp.tanh(logits[:, :half] + br_ref[0][:, sl])
        gi = 0.5 + 0.5 * jnp.tanh(logits[:, half:] + bi_ref[0][:, sl])
        neg_log_a = half_c[:, sl] * (1.0 + tr_)
        a = jnp.exp(-neg_log_a)
        b = _sqrt_one_minus_sq(a, neg_log_a) * (gi * xh)
        return a.reshape(ng, ns, half), b.reshape(ng, ns, half)

    def run(reverse):
        subh = lax.broadcasted_iota(I32, (ns, half), 0)
        for hh in range(2):
            sl = slice(half * hh, half * (hh + 1))
            a3, b3 = gates(hh)
            h = jnp.zeros((ns, half), F32)
            p = jnp.ones((ns, half), F32)
            hs = [None] * ng
            ps = [None] * ng
            for g in (range(ng - 1, -1, -1) if reverse else range(ng)):
                h = a3[g] * h + b3[g]
                p = a3[g] * p
                hs[g] = h
                ps[g] = p
            ac, hc = _group_scan(p[None], h[None], reverse)
            c_in = carry_sc[:, sl]
            end = ac[0] * c_in + hc[0]
            if reverse:
                seg_in = jnp.where(subh == ns - 1, c_in, pltpu.roll(end, ns - 1, axis=0))
                carry_sc[:, sl] = jnp.broadcast_to(end[0:1], end.shape)
            else:
                seg_in = jnp.where(subh == 0, c_in, pltpu.roll(end, 1, axis=0))
                carry_sc[:, sl] = jnp.broadcast_to(end[ns - 1:ns], end.shape)
            h3 = jnp.stack(hs, axis=0) + jnp.stack(ps, axis=0) * seg_in[None]
            o_ref[0, 0, :, sl] = jnp.swapaxes(h3, 0, 1).reshape(tc, half)

    @pl.when(d == 0)
    def _():
        run(False)

    @pl.when(d == 1)
    def _():
        run(True)


def _lru(lru, cw, cb, wg, br, bi, lam):
    bsz, seq, _ = lru.shape
    tc = TC_LRU
    nt = seq // tc
    r8 = tc // SUBLANES
    nrow8 = seq // SUBLANES

    def cidx(d, i):
        return jnp.where(d == 0, i, nt - 1 - i)

    return pl.pallas_call(
        functools.partial(_lru_body, nt=nt, tc=tc),
        grid=(bsz, 2, nt),
        in_specs=[
            pl.BlockSpec((1, tc, D_LRU), lambda b, d, i: (b, cidx(d, i), 0)),
            pl.BlockSpec((1, SUBLANES, D_LRU), lambda b, d, i: (b, jnp.maximum(cidx(d, i) * r8 - 1, 0), 0)),
            pl.BlockSpec((1, SUBLANES, D_LRU), lambda b, d, i: (b, jnp.minimum((cidx(d, i) + 1) * r8, nrow8 - 1), 0)),
            pl.BlockSpec((LRU_CONV, D_LRU), lambda b, d, i: (0, 0)),
            pl.BlockSpec((1, D_LRU), lambda b, d, i: (0, 0)),
            pl.BlockSpec((1, 2, D_LRU // 2, D_LRU), lambda b, d, i: (d, 0, 0, 0)),
            pl.BlockSpec((1, 1, D_LRU), lambda b, d, i: (d, 0, 0)),
            pl.BlockSpec((1, 1, D_LRU), lambda b, d, i: (d, 0, 0)),
            pl.BlockSpec((1, 1, D_LRU), lambda b, d, i: (d, 0, 0)),
        ],
        out_specs=pl.BlockSpec((1, 1, tc, D_LRU), lambda b, d, i: (d, b, cidx(d, i), 0)),
        out_shape=jax.ShapeDtypeStruct((2, bsz, seq, D_LRU), F32),
        scratch_shapes=[pltpu.VMEM((SUBLANES, D_LRU), F32)],
        compiler_params=_params(("parallel", "arbitrary", "arbitrary")),
        name="rg_lru",
    )(lru, lru, lru, cw, cb, wg, br, bi, lam)


def _fft_tables():
    r = FFT_R
    n = r * r
    idx = np.arange(r, dtype=np.float64)
    ang = 2.0 * np.pi * np.outer(idx, idx) / r
    wr, wi = np.cos(ang), -np.sin(ang)
    angt = 2.0 * np.pi * np.outer(idx, idx) / n
    tr, ti = np.cos(angt), -np.sin(angt)
    h = r // 2
    f1 = np.concatenate([wr, wi], axis=1)
    f1p = np.block([[wr[:h], wi[:h]], [-wi[:h], wr[:h]]])
    f2 = np.block([[wr, wi], [-wi, wr]])
    f2c = np.block([[wr, -wi], [wi, wr]])
    f3p = np.block([[wr[:, :h], -wi[:, :h]], [wi[:, :h], wr[:, :h]]]) / n
    as_bf16 = lambda a: jnp.asarray(a, F32).astype(BF16)
    return dict(f1=as_bf16(f1), f1p=as_bf16(f1p), f2=as_bf16(f2), f2c=as_bf16(f2c), f3p=as_bf16(f3p),
                tr16=as_bf16(tr), ti16=as_bf16(ti))


def _const_specs(arrays, nargs):
    return [pl.BlockSpec(a.shape, (lambda nd: (lambda *_: (0,) * nd))(a.ndim)) for a in arrays]


def _fft_fwd(x3, f1, f2, tr, ti):
    g = x3.shape[0]
    r = FFT_R
    wd = tr.dtype
    xt = jnp.swapaxes(x3.astype(wd), 1, 2).reshape(g * r, r)
    a = _dot(xt.astype(BF16), f1)
    ar = a[:, :r].astype(wd).reshape(g, r, r)
    ai = a[:, r:].astype(wd).reshape(g, r, r)
    pr = ar * tr - ai * ti
    pi = ar * ti + ai * tr
    pt = jnp.concatenate([jnp.swapaxes(pr, 1, 2), jnp.swapaxes(pi, 1, 2)], axis=-1).reshape(g * r, 2 * r)
    x = _dot(pt.astype(BF16), f2)
    return x[:, :r], x[:, r:]


def _filt_mlp_body(ft_ref, w1_ref, b1_ref, f1_ref, w2_ref, b2_ref, f2_ref, o_ref):
    z = jnp.sin(f1_ref[...] * (_dot3(w1_ref[...], ft_ref[0]) + b1_ref[...]))
    o_ref[0] = jnp.sin(f2_ref[...] * (_dot3(w2_ref[...], z) + b2_ref[...]))


def _filt_mlp(featst, w1t, b1, f1, w2t, b2, f2):
    nd, ke, seq = featst.shape
    col = lambda d: (0, 0)
    return pl.pallas_call(
        _filt_mlp_body,
        grid=(nd,),
        in_specs=[
            pl.BlockSpec((1, ke, seq), lambda d: (d, 0, 0)),
            pl.BlockSpec((HY_FFN, ke), col),
            pl.BlockSpec((HY_FFN, 1), col),
            pl.BlockSpec((HY_FFN, 1), col),
            pl.BlockSpec((HY_FFN, HY_FFN), col),
            pl.BlockSpec((HY_FFN, 1), col),
            pl.BlockSpec((HY_FFN, 1), col),
        ],
        out_specs=pl.BlockSpec((1, HY_FFN, seq), lambda d: (d, 0, 0)),
        out_shape=jax.ShapeDtypeStruct((nd, HY_FFN, seq), F32),
        compiler_params=_params(("parallel",)),
        name="hyena_filter_mlp",
    )(featst, w1t, b1, f1, w2t, b2, f2)


def _filt_time_body(z_ref, wf_ref, wb_ref, dl_ref, tf_ref, o_ref):
    seq = z_ref.shape[2]
    decay = jnp.exp(-tf_ref[...] * dl_ref[...])
    hf = _dot3(wf_ref[...], z_ref[0]) * decay
    hb = _dot3(wb_ref[...], z_ref[0]) * decay
    lane = lax.broadcasted_iota(I32, hb.shape, 1)
    hb = jnp.where(lane == 0, 0.0, hb)
    norm = jnp.sum(jnp.abs(hf), axis=-1, keepdims=True) + jnp.sum(jnp.abs(hb), axis=-1, keepdims=True) + EPS
    hf = hf / norm
    hb = hb / norm
    nrow = hf.shape[0] // SUBLANES
    nch = seq // LANES
    lane1 = lax.broadcasted_iota(I32, (hf.shape[0], LANES), 1)
    mirror = (LANES - lane1) % LANES
    for j in range(nch):
        o_ref[:, j, :, :] = hf[:, LANES * j:LANES * (j + 1)].reshape(nrow, SUBLANES, LANES)
        src = nch - 1 - j
        back = jnp.take_along_axis(hb[:, LANES * src:LANES * (src + 1)], mirror, axis=1)
        first = hb[:, LANES * (src + 1):LANES * (src + 1) + 1] if j > 0 else 0.0
        o_ref[:, nch + j, :, :] = jnp.where(lane1 == 0, first, back).reshape(nrow, SUBLANES, LANES)


def _filt_time(z2t, w3f, w3b, delta_rows, t_f):
    nrows = w3f.shape[0]
    seq = z2t.shape[2]
    rb = RB_FILT
    return pl.pallas_call(
        _filt_time_body,
        grid=(nrows // rb,),
        in_specs=[
            pl.BlockSpec((1, HY_FFN, seq), lambda r: (0, 0, 0)),
            pl.BlockSpec((rb, HY_FFN), lambda r: (r, 0)),
            pl.BlockSpec((rb, HY_FFN), lambda r: (r, 0)),
            pl.BlockSpec((rb, 1), lambda r: (r, 0)),
            pl.BlockSpec((1, seq), lambda r: (0, 0)),
        ],
        out_specs=pl.BlockSpec((rb // SUBLANES, 2 * seq // LANES, SUBLANES, LANES), lambda r: (r, 0, 0, 0)),
        out_shape=jax.ShapeDtypeStruct((nrows // SUBLANES, 2 * seq // LANES, SUBLANES, LANES), F32),
        compiler_params=_params(("parallel",)),
        name="hyena_filter_time",
    )(z2t, w3f, w3b, delta_rows, t_f)


def _filt_fft_body(k_ref, f1_ref, f2_ref, tr_ref, ti_ref, kr_ref, ki_ref):
    tr = tr_ref[...]
    ti = ti_ref[...]
    for g in range(k_ref.shape[0]):
        x3 = jnp.swapaxes(k_ref[g], 0, 1)
        xr, xi = _fft_fwd(x3, f1_ref[...], f2_ref[...], tr, ti)
        kr_ref[SUBLANES * g:SUBLANES * (g + 1)] = xr.astype(BF16).reshape(SUBLANES, FFT_R, FFT_R)
        ki_ref[SUBLANES * g:SUBLANES * (g + 1)] = xi.astype(BF16).reshape(SUBLANES, FFT_R, FFT_R)


def _filt_fft(ktile, tables):
    n8, r, _, _ = ktile.shape
    nrows = n8 * SUBLANES
    rf = RF_KF
    out_spec = pl.BlockSpec((rf, r, r), lambda i: (i, 0, 0))
    consts = [tables[k] for k in ("f1", "f2", "tr16", "ti16")]
    return pl.pallas_call(
        _filt_fft_body,
        grid=(nrows // rf,),
        in_specs=[pl.BlockSpec((rf // SUBLANES, r, SUBLANES, LANES), lambda i: (i, 0, 0, 0))] + _const_specs(consts, 1),
        out_specs=[out_spec, out_spec],
        out_shape=[jax.ShapeDtypeStruct((nrows, r, r), BF16)] * 2,
        compiler_params=_params(("parallel",)),
        name="hyena_filter_fft",
    )(ktile, *consts)


def _conv3_time(x, w):
    lane = lax.broadcasted_iota(I32, x.shape, 2)
    zero = jnp.zeros_like(x[:1])
    xm = pltpu.roll(x, 1, axis=2)
    xm = jnp.where(lane == 0, jnp.concatenate([zero, xm[:-1]], axis=0), xm)
    xp = pltpu.roll(x, LANES - 1, axis=2)
    xp = jnp.where(lane == LANES - 1, jnp.concatenate([xp[1:], zero], axis=0), xp)
    return w[0] * xm + w[1] * x + w[2] * xp + w[3]


def _hyena_body(sk_ref, v_ref, x1_ref, x2_ref, wv_ref, w1_ref, w2_ref, k0r_ref, k0i_ref, k1r_ref, k1i_ref,
                f1p_ref, f2_ref, f2c_ref, f3p_ref, tr_ref, ti_ref, o_ref, *, rc):
    ct = pl.program_id(0)
    tr = tr_ref[...]
    ti = ti_ref[...]
    np_ = v_ref.shape[2]
    nb = v_ref.shape[0]

    def load(ref, w_ref, g):
        parts = [jnp.swapaxes(_conv3_time(ref[bb, g], w_ref[:, g]), 0, 1) for bb in range(nb)]
        return jnp.concatenate(parts, axis=1)

    def long_conv(z3, kr_ref, ki_ref, g):
        r = FFT_R
        gsub = SUBLANES // HY_SPLIT
        wd = tr.dtype

        def split(m):
            return m[:, :r].astype(wd).reshape(gsub, r, r), m[:, r:].astype(wd).reshape(gsub, r, r)

        def join_t(re, im):
            return jnp.concatenate([jnp.swapaxes(re, 1, 2), jnp.swapaxes(im, 1, 2)], axis=-1).reshape(
                gsub * r, 2 * r).astype(BF16)

        subs = range(HY_SPLIT)
        zs = [z3[s * gsub:(s + 1) * gsub] for s in subs]
        a = [_dot(jnp.swapaxes(z.astype(wd), 1, 2).reshape(gsub * r, r).astype(BF16), f1p_ref[...]) for z in zs]
        p = []
        for m in a:
            ar, ai = split(m)
            p.append(join_t(ar * tr - ai * ti, ar * ti + ai * tr))
        x = [_dot(m, f2_ref[...]) for m in p]
        y = []
        for s, m in zip(subs, x):
            rows = pl.ds(g * SUBLANES + s * gsub, gsub)
            kr = kr_ref[rows].reshape(gsub * r, r)
            ki = ki_ref[rows].reshape(gsub * r, r)
            xr, xi = m[:, :r].astype(wd), m[:, r:].astype(wd)
            y.append(jnp.concatenate([xr * kr - xi * ki, xr * ki + xi * kr], axis=-1).astype(BF16))
        b = [_dot(m, f2c_ref[...]) for m in y]
        c = []
        for m in b:
            br, bi = split(m)
            c.append(join_t(br * tr + bi * ti, bi * tr - br * ti))
        yt = [_dot(m, f3p_ref[...]) for m in c]
        return jnp.concatenate([jnp.swapaxes(m.reshape(gsub, r, r), 1, 2) for m in yt], axis=0)

    def group(g, carry):
        cbase = ct * rc + g * SUBLANES

        def skip(z3, order):
            return jnp.stack([sk_ref[order, cbase + ci] * z3[ci] for ci in range(SUBLANES)], axis=0)

        v3 = load(v_ref, wv_ref, g)
        z1 = load(x1_ref, w1_ref, g) * (long_conv(v3, k0r_ref, k0i_ref, g) + skip(v3, 0))
        out = load(x2_ref, w2_ref, g) * (long_conv(z1, k1r_ref, k1i_ref, g) + skip(z1, 1))
        for bb in range(nb):
            o_ref[bb, g] = jnp.swapaxes(out[:, bb * np_:(bb + 1) * np_, :], 0, 1)
        return carry

    lax.fori_loop(0, rc // SUBLANES, group, 0)


def _hyena(hy5, cwt, skip, kfr, kfi, tables):
    bsz, n8, np_, _, _ = hy5.shape
    assert bsz % 2 == 0 and 2 * np_ == FFT_R
    rc = RC_HY
    r8 = rc // SUBLANES
    nct = D_HY // rc
    r = FFT_R

    def xspec(off):
        return pl.BlockSpec((2, r8, np_, SUBLANES, LANES), lambda c, b: (b, c + off * nct, 0, 0, 0))

    def wspec(off):
        return pl.BlockSpec((HY_CONV + 1, r8, SUBLANES, LANES), lambda c, b: (0, c + off * nct, 0, 0))

    def kspec(order):
        return pl.BlockSpec((rc, r, r), lambda c, b: (c + order * nct, 0, 0))

    consts = [tables[k] for k in ("f1p", "f2", "f2c", "f3p", "tr16", "ti16")]
    return pl.pallas_call(
        functools.partial(_hyena_body, rc=rc),
        grid=(nct, bsz // 2),
        in_specs=[pl.BlockSpec(memory_space=pltpu.SMEM), xspec(0), xspec(1), xspec(2), wspec(0), wspec(1), wspec(2),
                  kspec(0), kspec(0), kspec(1), kspec(1)] + _const_specs(consts, 2),
        out_specs=pl.BlockSpec((2, r8, np_, SUBLANES, LANES), lambda c, b: (b, c, 0, 0, 0)),
        out_shape=jax.ShapeDtypeStruct((bsz, D_HY // SUBLANES, np_, SUBLANES, LANES), F32),
        compiler_params=_params(("parallel", "parallel")),
        name="hyena",
    )(skip, hy5, hy5, hy5, cwt, cwt, cwt, kfr, kfi, kfr, kfi, *consts)


def _gelu_tanh(x):
    return 0.5 * x * (1.0 + jnp.tanh(math.sqrt(2.0 / math.pi) * (x + 0.044715 * (x * x * x))))


def _mix_body(hf_ref, hb_ref, gate_ref, yhy_ref, x_ref, gl_ref, gh_ref, wl_ref, wh_ref, gffn_ref,
              wrh_ref, wrl_ref, rb_ref, tri_ref, ltri_ref, xmid_ref, h_ref, routec_ref, routet_ref, cnt_ref):
    tm = x_ref.shape[1]
    y_lru = (hf_ref[0, 0] + hb_ref[0, 0]) * _gelu_tanh(gate_ref[0])
    nl = _rms(y_lru, gl_ref[...]).astype(BF16)
    gh = gh_ref[...]
    nts = []
    for j in range(tm // LANES):
        yt = yhy_ref[0, :, j, :, :].reshape(D_HY, LANES)
        nts.append((yt * lax.rsqrt(jnp.mean(yt * yt, axis=0, keepdims=True) + EPS) * gh).astype(BF16))
    nt = jnp.concatenate(nts, axis=1)
    mix = _dot(nl, wl_ref[...]) + lax.dot_general(nt, wh_ref[...], (((0,), (0,)), ((), ())),
                                                   preferred_element_type=F32)
    xmid_ref[0] = x_ref[0] + mix
    h = _rms(xmid_ref[0], gffn_ref[...])
    h_ref[...] = h.astype(BF16)

    hh, hl = _split_bf16(h)
    logits = _dot(hh, wrh_ref[...]) + _dot(hl, wrh_ref[...]) + _dot(hh, wrl_ref[...]) + rb_ref[...]
    lt = logits.T[:ROUTE_ROWS]
    row = lax.broadcasted_iota(I32, lt.shape, 0)
    big = jnp.int32(1 << 20)
    is_g = row < N_GROUPS
    lg = jnp.where(is_g, lt, NEG_BIG)
    mg = jnp.max(lg, axis=0, keepdims=True)
    gidx = jnp.min(jnp.where(lg == mg, row, big), axis=0, keepdims=True)
    g_p = 1.0 / jnp.sum(jnp.where(is_g, jnp.exp(lg - mg), 0.0), axis=0, keepdims=True)
    e_lo = N_GROUPS + EXPERTS_PER_GROUP * gidx
    is_e = jnp.logical_and(row >= e_lo, row < e_lo + EXPERTS_PER_GROUP)
    le = jnp.where(is_e, lt, NEG_BIG)
    m1 = jnp.max(le, axis=0, keepdims=True)
    i1 = jnp.min(jnp.where(le == m1, row, big), axis=0, keepdims=True)
    le2 = jnp.where(row == i1, NEG_BIG, le)
    m2 = jnp.max(le2, axis=0, keepdims=True)
    i2 = jnp.min(jnp.where(le2 == m2, row, big), axis=0, keepdims=True)
    ratio = jnp.exp(m2 - m1)
    gate1 = g_p / (1.0 + ratio)
    gate2 = g_p * ratio / (1.0 + ratio)
    oh1 = row == i1
    oh2 = row == i2
    oh = jnp.where(oh1, 1.0, jnp.where(oh2, 1.0, 0.0))
    before = _dot(oh.astype(BF16), tri_ref[...])
    cnt = jnp.sum(oh, axis=1, keepdims=True)
    nchunk = jnp.floor((cnt + (ROW_CHUNK - 1)) * (1.0 / ROW_CHUNK))
    run_start = ROW_CHUNK * _dot(ltri_ref[...], jnp.broadcast_to(nchunk, (ROUTE_ROWS, LANES)).astype(BF16))[:, 0:1]
    pos = run_start + before
    l0 = jnp.sum(jnp.where(oh1, pos, 0.0), axis=0, keepdims=True)
    l1 = jnp.sum(jnp.where(oh2, pos, 0.0), axis=0, keepdims=True)
    cnt_ref[0] = cnt
    route_t = jnp.concatenate([l0, l1, gate1, gate2, jnp.zeros((SUBLANES - 4, tm), F32)], axis=0)
    routet_ref[0] = route_t
    routec_ref[...] = jnp.concatenate([route_t, jnp.zeros((LANES - SUBLANES, tm), F32)], axis=0).T[:, :SUBLANES]


def _mix_router(h4, lru, yhy5, x, gl, gh, wl, wh, gffn, wrh, wrl, rb, tri, ltri):
    bsz, seq, d = x.shape
    tm = TM_MIX
    nt = seq // tm
    n_tok = bsz * seq
    c2 = lambda b, i: (0, 0)
    return pl.pallas_call(
        _mix_body,
        grid=(bsz, nt),
        in_specs=[
            pl.BlockSpec((1, 1, tm, D_LRU), lambda b, i: (0, b, i, 0)),
            pl.BlockSpec((1, 1, tm, D_LRU), lambda b, i: (1, b, i, 0)),
            pl.BlockSpec((1, tm, D_LRU), lambda b, i: (b, i, 1)),
            pl.BlockSpec((1, D_HY // SUBLANES, tm // LANES, SUBLANES, LANES), lambda b, i: (b, 0, i, 0, 0)),
            pl.BlockSpec((1, tm, d), lambda b, i: (b, i, 0)),
            pl.BlockSpec((1, D_LRU), c2),
            pl.BlockSpec((D_HY, 1), c2),
            pl.BlockSpec((D_LRU, d), c2),
            pl.BlockSpec((D_HY, d), c2),
            pl.BlockSpec((1, d), c2),
            pl.BlockSpec((d, LANES), c2),
            pl.BlockSpec((d, LANES), c2),
            pl.BlockSpec((1, LANES), c2),
            pl.BlockSpec((tm, tm), c2),
            pl.BlockSpec((ROUTE_ROWS, ROUTE_ROWS), c2),
        ],
        out_specs=[
            pl.BlockSpec((1, tm, d), lambda b, i: (b, i, 0)),
            pl.BlockSpec((tm, d), lambda b, i: (b * nt + i, 0)),
            pl.BlockSpec((tm, SUBLANES), lambda b, i: (b * nt + i, 0)),
            pl.BlockSpec((1, SUBLANES, tm), lambda b, i: (b * nt + i, 0, 0)),
            pl.BlockSpec((1, ROUTE_ROWS, 1), lambda b, i: (b * nt + i, 0, 0)),
        ],
        out_shape=[
            jax.ShapeDtypeStruct((bsz, seq, d), F32),
            jax.ShapeDtypeStruct((n_tok, d), BF16),
            jax.ShapeDtypeStruct((n_tok, SUBLANES), F32),
            jax.ShapeDtypeStruct((n_tok // tm, SUBLANES, tm), F32),
            jax.ShapeDtypeStruct((n_tok // tm, ROUTE_ROWS, 1), F32),
        ],
        compiler_params=_params(("parallel", "parallel")),
        name="mix_router",
    )(h4, h4, lru, yhy5, x, gl, gh, wl, wh, gffn, wrh, wrl, rb, tri, ltri)


def _start_chunks(i, tch_ref, dst_ref, make_copy):
    n = tch_ref[i]

    def start(k, priority):
        make_copy(k * ROW_CHUNK, dst_ref[i * MAX_CHUNKS + k], ROW_CHUNK).start(priority=priority)

    def per_pair(k2, c):
        start(2 * k2, 0)
        start(2 * k2 + 1, 1)
        return c

    lax.fori_loop(0, n // 2, per_pair, 0)

    @pl.when(n % 2 == 1)
    def _():
        start(n - 1, 0)


def _wait_chunks(count, make_copy):
    nbits = MAX_CHUNKS.bit_length()
    for j in reversed(range(nbits)):
        @pl.when((count >> j) & 1 == 1)
        def _():
            make_copy(0, 0, ROW_CHUNK << j).wait()


def _dispatch_body(dst_ref, tch_ref, tn_ref, ts_ref, nu_ref, h_ref, routet_ref, xb_ref,
                   buf, zbuf, sem, zsem):
    i = pl.program_id(0)
    n_tiles = pl.num_programs(0)
    tm = h_ref.shape[0]
    lbuf = buf.shape[1]
    n_blocks = xb_ref.shape[0] // MOE_BLK
    slot = i % 2

    def aligned(r):
        return pl.ds(pl.multiple_of(r, ROW_CHUNK), ROW_CHUNK)

    @pl.when(i == 0)
    def _():
        zbuf[...] = jnp.zeros_like(zbuf)

        def zero_chunk(r):
            return pltpu.make_async_copy(zbuf.at[pl.ds(0, ROW_CHUNK)], xb_ref.at[aligned(r)], zsem)

        def zero_block(j):
            return pltpu.make_async_copy(zbuf, xb_ref.at[pl.ds(pl.multiple_of(j * MOE_BLK, MOE_BLK), MOE_BLK)], zsem)

        def per_expert(e, total):
            def per_chunk(k, c):
                zero_chunk(ts_ref[e] + k * ROW_CHUNK).start()
                return c

            lax.fori_loop(0, tn_ref[e], per_chunk, 0)
            return total + tn_ref[e]

        total = lax.fori_loop(0, N_EXPERTS, per_expert, 0)

        def start_block(j, c):
            zero_block(j).start()
            return c

        lax.fori_loop(nu_ref[0], n_blocks, start_block, 0)

        def wait_chunk(k, c):
            zero_chunk(0).wait()
            return c

        lax.fori_loop(0, total, wait_chunk, 0)

        def wait_block(j, c):
            zero_block(0).wait()
            return c

        lax.fori_loop(nu_ref[0], n_blocks, wait_block, 0)

    l0t = routet_ref[0, 0:1, :].astype(I32)
    l1t = routet_ref[0, 1:2, :].astype(I32)
    row = lax.broadcasted_iota(I32, (lbuf, tm), 0)
    perm = jnp.where(row == l0t, 1.0, jnp.where(row == l1t, 1.0, 0.0)).astype(BF16)
    srt = _dot(perm, h_ref[...]).astype(BF16)

    def copy_from(s):
        def make_copy(src_row, dst_row, rows):
            return pltpu.make_async_copy(buf.at[s, pl.ds(pl.multiple_of(src_row, ROW_CHUNK), rows)],
                                         xb_ref.at[pl.ds(pl.multiple_of(dst_row, ROW_CHUNK), rows)], sem.at[s])
        return make_copy

    @pl.when(i >= 2)
    def _():
        _wait_chunks(tch_ref[i - 2], copy_from(slot))

    buf[slot] = srt
    _start_chunks(i, tch_ref, dst_ref, copy_from(slot))

    @pl.when(i == n_tiles - 1)
    def _():
        @pl.when(i >= 1)
        def _():
            _wait_chunks(tch_ref[i - 1], copy_from(1 - slot))

        _wait_chunks(tch_ref[i], copy_from(slot))


def _dispatch(chunk_dst, tile_nch, tail_n, tail_s, n_used, h, route_t, n_slots):
    n_tok, d = h.shape
    tm = TM_MIX
    grid_spec = pltpu.PrefetchScalarGridSpec(
        num_scalar_prefetch=5,
        grid=(n_tok // tm,),
        in_specs=[
            pl.BlockSpec((tm, d), lambda i, *_: (i, 0)),
            pl.BlockSpec((1, SUBLANES, tm), lambda i, *_: (i, 0, 0)),
        ],
        out_specs=pl.BlockSpec(memory_space=pl.ANY),
        scratch_shapes=[pltpu.VMEM((2, LOCAL_ROWS, d), BF16), pltpu.VMEM((MOE_BLK, d), BF16),
                        pltpu.SemaphoreType.DMA((2,)), pltpu.SemaphoreType.DMA(())],
    )
    return pl.pallas_call(
        _dispatch_body,
        grid_spec=grid_spec,
        out_shape=jax.ShapeDtypeStruct((n_slots, d), BF16),
        compiler_params=_params(("arbitrary",)),
        name="moe_dispatch",
    )(chunk_dst, tile_nch, tail_n, tail_s, n_used, h, route_t)


def _expert_body(be_ref, fl_ref, sw_ref, nu_ref, xb_ref, w1_ref, w3_ref, w2_ref, yb_ref,
                 w1_sc, w3_sc, w2_sc, w1_ring, w3_ring, w2_ring, sem):
    j = pl.program_id(0)
    n_used = nu_ref[0]
    used = j < n_used

    def weight_copies(jj):
        e = be_ref[jj]
        s = sw_ref[jj] % WEIGHT_SLOTS
        return [pltpu.make_async_copy(w_ref.at[e], ring.at[s], sem.at[s])
                for w_ref, ring in ((w1_ref, w1_ring), (w3_ref, w3_ring), (w2_ref, w2_ring))]

    def fetch_if_first(jj):
        @pl.when(jnp.logical_and(jj < n_used, fl_ref[jnp.minimum(jj, pl.num_programs(0) - 1)] == 1))
        def _():
            for cp in weight_copies(jj):
                cp.start()

    @pl.when(j == 0)
    def _():
        for jj in range(WEIGHT_LOOKAHEAD):
            fetch_if_first(jnp.int32(jj))

    fetch_if_first(j + WEIGHT_LOOKAHEAD)

    @pl.when(jnp.logical_and(used, fl_ref[j] == 1))
    def _():
        for cp in weight_copies(j):
            cp.wait()
        s = sw_ref[j] % WEIGHT_SLOTS
        w1_sc[...] = w1_ring[s].astype(BF16)
        w3_sc[...] = w3_ring[s].astype(BF16)
        w2_sc[...] = w2_ring[s].astype(BF16)

    @pl.when(used)
    def _():
        xb = xb_ref[...]
        a = _dot(xb, w1_sc[...])
        b = _dot(xb, w3_sc[...])
        act = (a * _sigmoid(a) * b).astype(BF16)
        yb_ref[...] = _dot(act, w2_sc[...]).astype(BF16)

    @pl.when(jnp.logical_not(used))
    def _():
        yb_ref[...] = jnp.zeros_like(yb_ref)


def _experts(block_e, first_flag, n_used, xb, w1, w3, w2):
    n_slots, d = xb.shape
    blk = MOE_BLK
    de = w1.shape[2]
    switch_no = jnp.cumsum(first_flag) - 1
    hbm = pl.BlockSpec(memory_space=pl.ANY)
    grid_spec = pltpu.PrefetchScalarGridSpec(
        num_scalar_prefetch=4,
        grid=(n_slots // blk,),
        in_specs=[pl.BlockSpec((blk, d), lambda j, *_: (j, 0)), hbm, hbm, hbm],
        out_specs=pl.BlockSpec((blk, d), lambda j, *_: (j, 0)),
        scratch_shapes=[pltpu.VMEM((d, de), BF16), pltpu.VMEM((d, de), BF16), pltpu.VMEM((de, d), BF16),
                        pltpu.VMEM((WEIGHT_SLOTS, d, de), F32), pltpu.VMEM((WEIGHT_SLOTS, d, de), F32),
                        pltpu.VMEM((WEIGHT_SLOTS, de, d), F32), pltpu.SemaphoreType.DMA((WEIGHT_SLOTS,))],
    )
    return pl.pallas_call(
        _expert_body,
        grid_spec=grid_spec,
        out_shape=jax.ShapeDtypeStruct((n_slots, d), BF16),
        compiler_params=_params(("arbitrary",)),
        name="moe_experts",
    )(block_e, first_flag, switch_no.astype(I32), n_used, xb, w1, w3, w2)


def _combine_body(dst_ref, tch_ref, xmid_ref, route_ref, g_ref, yb_ref, o_ref, buf, sem):
    i = pl.program_id(0)
    n_tiles = pl.num_programs(0)
    tm = xmid_ref.shape[0]
    lbuf = buf.shape[1]
    slot = i % 2

    def copy_into(s):
        def make_copy(local_row, slot_row, rows):
            return pltpu.make_async_copy(yb_ref.at[pl.ds(pl.multiple_of(slot_row, ROW_CHUNK), rows)],
                                         buf.at[s, pl.ds(pl.multiple_of(local_row, ROW_CHUNK), rows)], sem.at[s])
        return make_copy

    @pl.when(i == 0)
    def _():
        buf[...] = jnp.zeros_like(buf)
        _start_chunks(i, tch_ref, dst_ref, copy_into(slot))

    @pl.when(i + 1 < n_tiles)
    def _():
        _start_chunks(i + 1, tch_ref, dst_ref, copy_into(1 - slot))

    route = route_ref[...]
    l0 = route[:, 0:1].astype(I32)
    l1 = route[:, 1:2].astype(I32)
    lane = lax.broadcasted_iota(I32, (tm, lbuf), 1)
    gmat = jnp.where(lane == l0, route[:, 2:3], jnp.where(lane == l1, route[:, 3:4], 0.0)).astype(BF16)
    _wait_chunks(tch_ref[i], copy_into(slot))
    y = xmid_ref[...] + _dot(gmat, buf[slot])
    o_ref[...] = _rms(y, g_ref[...])


def _combine(chunk_dst, tile_nch, xmid, route, g, yb):
    n_tok, d = xmid.shape
    tm = TM_MIX
    grid_spec = pltpu.PrefetchScalarGridSpec(
        num_scalar_prefetch=2,
        grid=(n_tok // tm,),
        in_specs=[
            pl.BlockSpec((tm, d), lambda i, *_: (i, 0)),
            pl.BlockSpec((tm, SUBLANES), lambda i, *_: (i, 0)),
            pl.BlockSpec((1, d), lambda i, *_: (0, 0)),
            pl.BlockSpec(memory_space=pl.ANY),
        ],
        out_specs=pl.BlockSpec((tm, d), lambda i, *_: (i, 0)),
        scratch_shapes=[pltpu.VMEM((2, LOCAL_ROWS, d), BF16), pltpu.SemaphoreType.DMA((2,))],
    )
    return pl.pallas_call(
        _combine_body,
        grid_spec=grid_spec,
        out_shape=jax.ShapeDtypeStruct((n_tok, d), F32),
        compiler_params=_params(("arbitrary",)),
        name="moe_combine",
    )(chunk_dst, tile_nch, xmid, route, g, yb)


def _filter_features(seq):
    pos = jnp.arange(seq, dtype=F32)
    t = jnp.linspace(0.0, 1.0, seq, dtype=F32)
    w = (2.0 * math.pi / seq) * pos
    bands = jnp.linspace(1e-4, HY_BANDS - 1, HY_BANDS, dtype=F32)
    ang = w[:, None] * bands[None, :]
    feats = jnp.concatenate([t[:, None], jnp.cos(ang), jnp.sin(ang)], axis=-1)
    featst = jnp.pad(feats, ((0, 0), (0, HY_EMB_PAD - HY_EMB))).T[None]
    return featst, t[None, :]


def _gate_weights(wr, wi):
    half = D_LRU // 2
    hph = half // LRU_HEAD_DIM
    eye = 0.5 * jnp.eye(hph, dtype=wr.dtype)[:, None, :, None]

    def block_diag(w):
        return (eye * w[:, :, None, :]).reshape(half, half)

    out = [jnp.concatenate([block_diag(wr[hh * hph:(hh + 1) * hph]), block_diag(wi[hh * hph:(hh + 1) * hph])], axis=1)
           for hh in range(2)]
    return jnp.stack(out, axis=0).astype(BF16)


def _layer(x, l, p):
    bsz, seq, d = x.shape
    n_tok = bsz * seq
    row = lambda a: a.reshape(1, -1).astype(F32)
    col = lambda a: a.reshape(-1, 1).astype(F32)

    w_in = p["w_in"][l]
    wl = w_in[:, :2 * D_LRU].astype(BF16)
    wht = w_in[:, 2 * D_LRU:].T.astype(BF16)
    lru, hy5 = _inproj(x, row(p["norm_mix_g"][l]), wl, wht)

    wg = jnp.stack([_gate_weights(p["lru_wr_f"][l], p["lru_wi_f"][l]),
                    _gate_weights(p["lru_wr_b"][l], p["lru_wi_b"][l])], axis=0)
    stack2 = lambda a, b: jnp.stack([a.reshape(1, -1), b.reshape(1, -1)], axis=0).astype(F32)
    h4 = _lru(lru, p["lru_conv_w"][l], row(p["lru_conv_b"][l]), wg,
              0.5 * stack2(p["lru_br_f"][l], p["lru_br_b"][l]), 0.5 * stack2(p["lru_bi_f"][l], p["lru_bi_b"][l]),
              stack2(p["lru_lambda_f"][l], p["lru_lambda_b"][l]))

    tables = _fft_tables()
    featst, t_f = _filter_features(seq)
    w1t = jnp.pad(p["hy_filt_w1"][l].T, ((0, 0), (0, HY_EMB_PAD - HY_EMB)))
    z2t = _filt_mlp(featst, w1t, col(p["hy_filt_b1"][l]), col(p["hy_filt_freq1"][l]),
                    p["hy_filt_w2"][l].T, col(p["hy_filt_b2"][l]), col(p["hy_filt_freq2"][l]))
    w3 = p["hy_filt_w3"][l]
    nfr = HY_ORDER * D_HY
    deltas = jnp.abs(jnp.linspace(math.log(HY_DECAY_TARGET) / HY_SLOW_DECAY_PCT,
                                  math.log(HY_DECAY_TARGET) / HY_FAST_DECAY_PCT, D_HY, dtype=F32))
    ktile = _filt_time(z2t, w3[:, :nfr].T, w3[:, nfr:].T, col(jnp.tile(deltas, HY_ORDER)), t_f)
    kfr, kfi = _filt_fft(ktile, tables)
    cwt = jnp.concatenate([p["hy_conv_w"][l], p["hy_conv_b"][l][None, :]], axis=0)
    cwt = jnp.broadcast_to(cwt.reshape(HY_CONV + 1, -1, SUBLANES, 1), (HY_CONV + 1, cwt.shape[1] // SUBLANES, SUBLANES, LANES))
    yhy5 = _hyena(hy5, cwt, p["hy_skip"][l], kfr, kfi, tables)

    w_out = p["w_out"][l]
    wr_cat = jnp.concatenate([p["router_group_w"][l], p["router_expert_w"][l]], axis=1)
    wr_cat = jnp.pad(wr_cat, ((0, 0), (0, LANES - wr_cat.shape[1])))
    wrh, wrl = _split_bf16(wr_cat)
    rb = jnp.concatenate([p["router_group_b"][l], p["router_expert_b"][l]])
    rb = jnp.pad(rb, (0, LANES - rb.shape[0])).reshape(1, LANES)
    earlier = lambda n: (jnp.arange(n)[:, None] < jnp.arange(n)[None, :]).astype(BF16)
    xmid, h, route, route_t, tile_cnt = _mix_router(
        h4, lru, yhy5, x, row(p["grp_norm_lru_g"][l]), col(p["grp_norm_hy_g"][l]),
        w_out[:D_LRU].astype(BF16), w_out[D_LRU:].astype(BF16), row(p["norm_ffn_g"][l]), wrh, wrl, rb,
        earlier(TM_MIX), earlier(ROUTE_ROWS).T)

    blk = MOE_BLK
    ch = ROW_CHUNK
    n_tiles = n_tok // TM_MIX
    n_blocks = (2 * n_tok + n_tiles * N_EXPERTS * (ch - 1)) // blk + N_EXPERTS
    cnt = tile_cnt[:, N_GROUPS:N_GROUPS + N_EXPERTS, 0].astype(I32)
    run = (cnt + ch - 1) // ch * ch
    tot_e = jnp.sum(run, axis=0)
    padded_e = (tot_e + blk - 1) // blk * blk
    pad_end = jnp.cumsum(padded_e)
    pad_start = pad_end - padded_e
    gstart = pad_start[None, :] + jnp.cumsum(run, axis=0) - run
    lstart = jnp.cumsum(run, axis=1) - run
    flat = lambda a: a.reshape(-1).astype(I32)
    tile_nch = flat(jnp.sum(run, axis=1) // ch)
    nch_end = jnp.cumsum(run // ch, axis=1)
    k = jnp.arange(MAX_CHUNKS, dtype=I32)
    base = gstart - lstart
    step = base[:, 1:] - base[:, :-1]
    passed = (k[None, :, None] >= nch_end[:, None, :-1]).astype(I32)
    chunk_dst = flat(base[:, :1] + jnp.sum(passed * step[:, None, :], axis=2) + k[None, :] * ch)
    tail_n = flat((padded_e - tot_e) // ch)
    tail_s = flat(pad_start + tot_e)
    block_start = jnp.arange(n_blocks, dtype=I32) * blk
    block_e = jnp.minimum(jnp.sum(block_start[:, None] >= pad_end[None, :], axis=1), N_EXPERTS - 1).astype(I32)
    first_flag = jnp.concatenate([jnp.ones((1,), I32), (block_e[1:] != block_e[:-1]).astype(I32)])
    n_used = (pad_end[-1:] // blk).astype(I32)

    xb = _dispatch(chunk_dst, tile_nch, tail_n, tail_s, n_used, h, route_t, n_blocks * blk)
    yb = _experts(block_e, first_flag, n_used, xb, p["exp_w1"][l], p["exp_w3"][l], p["exp_w2"][l])
    return xmid.reshape(n_tok, d), route, (chunk_dst, tile_nch), yb


def kernel(x, norm_mix_g, w_in, lru_conv_w, lru_conv_b, lru_wr_f, lru_br_f, lru_wi_f, lru_bi_f, lru_lambda_f, lru_wr_b, lru_br_b, lru_wi_b, lru_bi_b, lru_lambda_b, hy_conv_w, hy_conv_b, hy_filt_w1, hy_filt_b1, hy_filt_freq1, hy_filt_w2, hy_filt_b2, hy_filt_freq2, hy_filt_w3, hy_skip, grp_norm_lru_g, grp_norm_hy_g, w_out, norm_ffn_g, router_group_w, router_group_b, router_expert_w, router_expert_b, exp_w1, exp_w3, exp_w2, norm_final_g):
    p = dict(locals())
    bsz, seq, d = x.shape
    assert d == D_MODEL and 2 * seq == FFT_R * FFT_R and w_in.shape[0] == 1
    xmid, route, (chunk_dst, tile_nch), yb = _layer(x, 0, p)
    out = _combine(chunk_dst, tile_nch, xmid, route, norm_final_g.reshape(1, d), yb)
    return out.reshape(bsz, seq, d)
```

```python
import functools
import math

import numpy as np
import jax
import jax.numpy as jnp
from jax import lax
from jax.experimental import pallas as pl
from jax.experimental.pallas import tpu as pltpu

F32 = jnp.float32
BF16 = jnp.bfloat16
I32 = jnp.int32

D_MODEL = 1024
D_LRU = 512
D_HY = 512
LRU_HEADS = 8
LRU_HEAD_DIM = D_LRU // LRU_HEADS
LRU_CONV = 4
LRU_C = 8.0
HY_ORDER = 2
HY_CONV = 3
HY_BANDS = 16
HY_EMB = 2 * HY_BANDS + 1
HY_EMB_PAD = 40
HY_FFN = 64
HY_FAST_DECAY_PCT = 0.3
HY_SLOW_DECAY_PCT = 1.5
HY_DECAY_TARGET = 1e-2
N_GROUPS = 4
EXPERTS_PER_GROUP = 8
N_EXPERTS = N_GROUPS * EXPERTS_PER_GROUP
D_EXPERT = D_MODEL // 2
EPS = 1e-6

LANES = 128
SUBLANES = 8
FFT_R = 128
VMEM_LIMIT = 56 * 1024 * 1024

TM_IN = 1024
TC_LRU = 1024
RC_HY = 32
RF_KF = 128
HY_SPLIT = 2
RB_FILT = 128
TM_MIX = 512
MOE_BLK = 512
WEIGHT_LOOKAHEAD = 2
WEIGHT_SLOTS = WEIGHT_LOOKAHEAD + 1
ROW_CHUNK = 16
LOCAL_ROWS = -(-(2 * TM_MIX + N_EXPERTS * (ROW_CHUNK - 1)) // LANES) * LANES
MAX_CHUNKS = LOCAL_ROWS // ROW_CHUNK
ROUTE_ROWS = -(-(N_GROUPS + N_EXPERTS) // SUBLANES) * SUBLANES
NEG_BIG = -1e30


def _params(sem, vmem=VMEM_LIMIT):
    return pltpu.CompilerParams(dimension_semantics=sem, vmem_limit_bytes=vmem)


def _rms(x, g):
    return x * lax.rsqrt(jnp.mean(x * x, axis=-1, keepdims=True) + EPS) * g


def _sigmoid(x):
    return 1.0 / (1.0 + jnp.exp(-x))


def _split_bf16(a):
    hi = a.astype(BF16)
    lo = (a - hi.astype(F32)).astype(BF16)
    return hi, lo


def _dot(a, b, out=F32):
    return jnp.dot(a, b, preferred_element_type=F32).astype(out)


def _dot3(a, b):
    ah, al = _split_bf16(a)
    bh, bl = _split_bf16(b)
    return _dot(ah, bh) + _dot(al, bh) + _dot(ah, bl)


def _inproj_body(x_ref, g_ref, wl_ref, wht_ref, lru_ref, hy_ref):
    hn = _rms(x_ref[0], g_ref[...]).astype(BF16)
    lru_ref[0] = _dot(hn, wl_ref[...])
    hyt = lax.dot_general(wht_ref[...], hn, (((1,), (1,)), ((), ())), preferred_element_type=F32)
    nrow = hyt.shape[0] // SUBLANES
    for j in range(hyt.shape[1] // LANES):
        hy_ref[0, :, j, :, :] = hyt[:, LANES * j:LANES * (j + 1)].reshape(nrow, SUBLANES, LANES)


def _inproj(x, g, wl, wht):
    bsz, seq, d = x.shape
    nl = wl.shape[1]
    nh = wht.shape[0]
    tm = TM_IN
    return pl.pallas_call(
        _inproj_body,
        grid=(bsz, seq // tm),
        in_specs=[
            pl.BlockSpec((1, tm, d), lambda b, i: (b, i, 0)),
            pl.BlockSpec((1, d), lambda b, i: (0, 0)),
            pl.BlockSpec((d, nl), lambda b, i: (0, 0)),
            pl.BlockSpec((nh, d), lambda b, i: (0, 0)),
        ],
        out_specs=[
            pl.BlockSpec((1, tm, nl), lambda b, i: (b, i, 0)),
            pl.BlockSpec((1, nh // SUBLANES, tm // LANES, SUBLANES, LANES), lambda b, i: (b, 0, i, 0, 0)),
        ],
        out_shape=[
            jax.ShapeDtypeStruct((bsz, seq, nl), F32),
            jax.ShapeDtypeStruct((bsz, nh // SUBLANES, seq // LANES, SUBLANES, LANES), F32),
        ],
        compiler_params=_params(("parallel", "parallel")),
        name="inproj",
    )(x, g, wl, wht)


def _sqrt_one_minus_sq(a, neg_log_a):
    om = jnp.maximum(jnp.tanh(neg_log_a) * (1.0 + a * a), 1e-30)
    return om * lax.rsqrt(om)


def _group_scan(a3, b3, reverse):
    sub = lax.broadcasted_iota(I32, a3.shape, 1)
    for s in (1, 2, 4):
        if reverse:
            a_sh = pltpu.roll(a3, SUBLANES - s, axis=1)
            b_sh = pltpu.roll(b3, SUBLANES - s, axis=1)
            m = sub < SUBLANES - s
        else:
            a_sh = pltpu.roll(a3, s, axis=1)
            b_sh = pltpu.roll(b3, s, axis=1)
            m = sub >= s
        b3 = jnp.where(m, a3 * b_sh + b3, b3)
        a3 = jnp.where(m, a3 * a_sh, a3)
    return a3, b3


def _lru_body(cur_ref, prev_ref, next_ref, cw_ref, cb_ref, wg_ref, br_ref, bi_ref, lam_ref, o_ref,
              carry_sc, *, nt, tc):
    d = pl.program_id(1)
    i = pl.program_id(2)
    c = jnp.where(d == 0, i, nt - 1 - i)
    ns = SUBLANES
    ng = tc // ns
    half = D_LRU // 2

    @pl.when(i == 0)
    def _():
        carry_sc[...] = jnp.zeros_like(carry_sc)

    xs = jnp.swapaxes(cur_ref[0].reshape(ns, ng, D_LRU), 0, 1)
    prev = jnp.where(c > 0, prev_ref[0], 0.0)
    nxt = jnp.where(c < nt - 1, next_ref[0], 0.0)
    sub = lax.broadcasted_iota(I32, (ns, D_LRU), 0)

    def prev_segment(slab, halo_row):
        return jnp.where(sub == 0, halo_row, pltpu.roll(slab, 1, axis=0))

    def next_segment(slab, halo_row):
        return jnp.where(sub == ns - 1, halo_row, pltpu.roll(slab, ns - 1, axis=0))

    xext = jnp.concatenate([prev_segment(xs[ng - 2], prev[ns - 2:ns - 1])[None],
                            prev_segment(xs[ng - 1], prev[ns - 1:ns])[None], xs,
                            next_segment(xs[0], nxt[0:1])[None]], axis=0)
    cw = cw_ref[...]
    xc = cb_ref[...] + cw[0:1] * xext[0:ng]
    for k in range(1, LRU_CONV):
        xc = xc + cw[k:k + 1] * xext[k:k + ng]
    xc = xc.reshape(tc, D_LRU)

    lam = lam_ref[0]
    nlam = -lam
    softplus = jnp.maximum(nlam, 0.0) + jnp.log1p(jnp.exp(-jnp.abs(nlam)))
    half_c = (0.5 * LRU_C) * softplus

    def gates(hh):
        sl = slice(half * hh, half * (hh + 1))
        xh = xc[:, sl]
        logits = _dot(xh.astype(BF16), wg_ref[0, hh])
        tr_ = jnp.tanh(logits[:, :half] + br_ref[0][:, sl])
        gi = 0.5 + 0.5 * jnp.tanh(logits[:, half:] + bi_ref[0][:, sl])
        neg_log_a = half_c[:, sl] * (1.0 + tr_)
        a = jnp.exp(-neg_log_a)
        b = _sqrt_one_minus_sq(a, neg_log_a) * (gi * xh)
        return a.reshape(ng, ns, half), b.reshape(ng, ns, half)

    def run(reverse):
        subh = lax.broadcasted_iota(I32, (ns, half), 0)
        for hh in range(2):
            sl = slice(half * hh, half * (hh + 1))
            a3, b3 = gates(hh)
            h = jnp.zeros((ns, half), F32)
            p = jnp.ones((ns, half), F32)
            hs = [None] * ng
            ps = [None] * ng
            for g in (range(ng - 1, -1, -1) if reverse else range(ng)):
                h = a3[g] * h + b3[g]
                p = a3[g] * p
                hs[g] = h
                ps[g] = p
            ac, hc = _group_scan(p[None], h[None], reverse)
            c_in = carry_sc[:, sl]
            end = ac[0] * c_in + hc[0]
            if reverse:
                seg_in = jnp.where(subh == ns - 1, c_in, pltpu.roll(end, ns - 1, axis=0))
                carry_sc[:, sl] = jnp.broadcast_to(end[0:1], end.shape)
            else:
                seg_in = jnp.where(subh == 0, c_in, pltpu.roll(end, 1, axis=0))
                carry_sc[:, sl] = jnp.broadcast_to(end[ns - 1:ns], end.shape)
            h3 = jnp.stack(hs, axis=0) + jnp.stack(ps, axis=0) * seg_in[None]
            o_ref[0, 0, :, sl] = jnp.swapaxes(h3, 0, 1).reshape(tc, half)

    @pl.when(d == 0)
    def _():
        run(False)

    @pl.when(d == 1)
    def _():
        run(True)


def _lru(lru, cw, cb, wg, br, bi, lam):
    bsz, seq, _ = lru.shape
    tc = TC_LRU
    nt = seq // tc
    r8 = tc // SUBLANES
    nrow8 = seq // SUBLANES

    def cidx(d, i):
        return jnp.where(d == 0, i, nt - 1 - i)

    return pl.pallas_call(
        functools.partial(_lru_body, nt=nt, tc=tc),
        grid=(bsz, 2, nt),
        in_specs=[
            pl.BlockSpec((1, tc, D_LRU), lambda b, d, i: (b, cidx(d, i), 0)),
            pl.BlockSpec((1, SUBLANES, D_LRU), lambda b, d, i: (b, jnp.maximum(cidx(d, i) * r8 - 1, 0), 0)),
            pl.BlockSpec((1, SUBLANES, D_LRU), lambda b, d, i: (b, jnp.minimum((cidx(d, i) + 1) * r8, nrow8 - 1), 0)),
            pl.BlockSpec((LRU_CONV, D_LRU), lambda b, d, i: (0, 0)),
            pl.BlockSpec((1, D_LRU), lambda b, d, i: (0, 0)),
            pl.BlockSpec((1, 2, D_LRU // 2, D_LRU), lambda b, d, i: (d, 0, 0, 0)),
            pl.BlockSpec((1, 1, D_LRU), lambda b, d, i: (d, 0, 0)),
            pl.BlockSpec((1, 1, D_LRU), lambda b, d, i: (d, 0, 0)),
            pl.BlockSpec((1, 1, D_LRU), lambda b, d, i: (d, 0, 0)),
        ],
        out_specs=pl.BlockSpec((1, 1, tc, D_LRU), lambda b, d, i: (d, b, cidx(d, i), 0)),
        out_shape=jax.ShapeDtypeStruct((2, bsz, seq, D_LRU), F32),
        scratch_shapes=[pltpu.VMEM((SUBLANES, D_LRU), F32)],
        compiler_params=_params(("parallel", "arbitrary", "arbitrary")),
        name="rg_lru",
    )(lru, lru, lru, cw, cb, wg, br, bi, lam)


def _fft_tables():
    r = FFT_R
    n = r * r
    idx = np.arange(r, dtype=np.float64)
    ang = 2.0 * np.pi * np.outer(idx, idx) / r
    wr, wi = np.cos(ang), -np.sin(ang)
    angt = 2.0 * np.pi * np.outer(idx, idx) / n
    tr, ti = np.cos(angt), -np.sin(angt)
    h = r // 2
    f1 = np.concatenate([wr, wi], axis=1)
    f1p = np.block([[wr[:h], wi[:h]], [-wi[:h], wr[:h]]])
    f2 = np.block([[wr, wi], [-wi, wr]])
    f2c = np.block([[wr, -wi], [wi, wr]])
    f3p = np.block([[wr[:, :h], -wi[:, :h]], [wi[:, :h], wr[:, :h]]]) / n
    as_bf16 = lambda a: jnp.asarray(a, F32).astype(BF16)
    return dict(f1=as_bf16(f1), f1p=as_bf16(f1p), f2=as_bf16(f2), f2c=as_bf16(f2c), f3p=as_bf16(f3p),
                tr16=as_bf16(tr), ti16=as_bf16(ti))


def _const_specs(arrays, nargs):
    return [pl.BlockSpec(a.shape, (lambda nd: (lambda *_: (0,) * nd))(a.ndim)) for a in arrays]


def _fft_fwd(x3, f1, f2, tr, ti):
    g = x3.shape[0]
    r = FFT_R
    wd = tr.dtype
    xt = jnp.swapaxes(x3.astype(wd), 1, 2).reshape(g * r, r)
    a = _dot(xt.astype(BF16), f1, wd)
    ar = a[:, :r].reshape(g, r, r)
    ai = a[:, r:].reshape(g, r, r)
    pr = ar * tr - ai * ti
    pi = ar * ti + ai * tr
    pt = jnp.concatenate([jnp.swapaxes(pr, 1, 2), jnp.swapaxes(pi, 1, 2)], axis=-1).reshape(g * r, 2 * r)
    x = _dot(pt.astype(BF16), f2, BF16)
    return x[:, :r], x[:, r:]


def _filt_mlp_body(ft_ref, w1_ref, b1_ref, f1_ref, w2_ref, b2_ref, f2_ref, o_ref):
    z = jnp.sin(f1_ref[...] * (_dot3(w1_ref[...], ft_ref[0]) + b1_ref[...]))
    o_ref[0] = jnp.sin(f2_ref[...] * (_dot3(w2_ref[...], z) + b2_ref[...]))


def _filt_mlp(featst, w1t, b1, f1, w2t, b2, f2):
    nd, ke, seq = featst.shape
    col = lambda d: (0, 0)
    return pl.pallas_call(
        _filt_mlp_body,
        grid=(nd,),
        in_specs=[
            pl.BlockSpec((1, ke, seq), lambda d: (d, 0, 0)),
            pl.BlockSpec((HY_FFN, ke), col),
            pl.BlockSpec((HY_FFN, 1), col),
            pl.BlockSpec((HY_FFN, 1), col),
            pl.BlockSpec((HY_FFN, HY_FFN), col),
            pl.BlockSpec((HY_FFN, 1), col),
            pl.BlockSpec((HY_FFN, 1), col),
        ],
        out_specs=pl.BlockSpec((1, HY_FFN, seq), lambda d: (d, 0, 0)),
        out_shape=jax.ShapeDtypeStruct((nd, HY_FFN, seq), F32),
        compiler_params=_params(("parallel",)),
        name="hyena_filter_mlp",
    )(featst, w1t, b1, f1, w2t, b2, f2)


def _filt_time_body(z_ref, wf_ref, wb_ref, dl_ref, tf_ref, o_ref):
    seq = z_ref.shape[2]
    decay = jnp.exp(-tf_ref[...] * dl_ref[...])
    hf = _dot3(wf_ref[...], z_ref[0]) * decay
    hb = _dot3(wb_ref[...], z_ref[0]) * decay
    lane = lax.broadcasted_iota(I32, hb.shape, 1)
    hb = jnp.where(lane == 0, 0.0, hb)
    norm = jnp.sum(jnp.abs(hf), axis=-1, keepdims=True) + jnp.sum(jnp.abs(hb), axis=-1, keepdims=True) + EPS
    hf = hf / norm
    hb = hb / norm
    nrow = hf.shape[0] // SUBLANES
    nch = seq // LANES
    lane1 = lax.broadcasted_iota(I32, (hf.shape[0], LANES), 1)
    mirror = (LANES - lane1) % LANES
    for j in range(nch):
        o_ref[:, j, :, :] = hf[:, LANES * j:LANES * (j + 1)].reshape(nrow, SUBLANES, LANES)
        src = nch - 1 - j
        back = jnp.take_along_axis(hb[:, LANES * src:LANES * (src + 1)], mirror, axis=1)
        first = hb[:, LANES * (src + 1):LANES * (src + 1) + 1] if j > 0 else 0.0
        o_ref[:, nch + j, :, :] = jnp.where(lane1 == 0, first, back).reshape(nrow, SUBLANES, LANES)


def _filt_time(z2t, w3f, w3b, delta_rows, t_f):
    nrows = w3f.shape[0]
    seq = z2t.shape[2]
    rb = RB_FILT
    return pl.pallas_call(
        _filt_time_body,
        grid=(nrows // rb,),
        in_specs=[
            pl.BlockSpec((1, HY_FFN, seq), lambda r: (0, 0, 0)),
            pl.BlockSpec((rb, HY_FFN), lambda r: (r, 0)),
            pl.BlockSpec((rb, HY_FFN), lambda r: (r, 0)),
            pl.BlockSpec((rb, 1), lambda r: (r, 0)),
            pl.BlockSpec((1, seq), lambda r: (0, 0)),
        ],
        out_specs=pl.BlockSpec((rb // SUBLANES, 2 * seq // LANES, SUBLANES, LANES), lambda r: (r, 0, 0, 0)),
        out_shape=jax.ShapeDtypeStruct((nrows // SUBLANES, 2 * seq // LANES, SUBLANES, LANES), F32),
        compiler_params=_params(("parallel",)),
        name="hyena_filter_time",
    )(z2t, w3f, w3b, delta_rows, t_f)


def _filt_fft_body(k_ref, f1_ref, f2_ref, tr_ref, ti_ref, kr_ref, ki_ref):
    tr = tr_ref[...]
    ti = ti_ref[...]
    for g in range(k_ref.shape[0]):
        x3 = jnp.swapaxes(k_ref[g], 0, 1)
        xr, xi = _fft_fwd(x3, f1_ref[...], f2_ref[...], tr, ti)
        kr_ref[SUBLANES * g:SUBLANES * (g + 1)] = xr.reshape(SUBLANES, FFT_R, FFT_R)
        ki_ref[SUBLANES * g:SUBLANES * (g + 1)] = xi.reshape(SUBLANES, FFT_R, FFT_R)


def _filt_fft(ktile, tables):
    n8, r, _, _ = ktile.shape
    nrows = n8 * SUBLANES
    rf = RF_KF
    out_spec = pl.BlockSpec((rf, r, r), lambda i: (i, 0, 0))
    consts = [tables[k] for k in ("f1", "f2", "tr16", "ti16")]
    return pl.pallas_call(
        _filt_fft_body,
        grid=(nrows // rf,),
        in_specs=[pl.BlockSpec((rf // SUBLANES, r, SUBLANES, LANES), lambda i: (i, 0, 0, 0))] + _const_specs(consts, 1),
        out_specs=[out_spec, out_spec],
        out_shape=[jax.ShapeDtypeStruct((nrows, r, r), BF16)] * 2,
        compiler_params=_params(("parallel",)),
        name="hyena_filter_fft",
    )(ktile, *consts)


def _conv3_time(x, w):
    lane = lax.broadcasted_iota(I32, x.shape, 2)
    zero = jnp.zeros_like(x[:1])
    xm = pltpu.roll(x, 1, axis=2)
    xm = jnp.where(lane == 0, jnp.concatenate([zero, xm[:-1]], axis=0), xm)
    xp = pltpu.roll(x, LANES - 1, axis=2)
    xp = jnp.where(lane == LANES - 1, jnp.concatenate([xp[1:], zero], axis=0), xp)
    return w[0] * xm + w[1] * x + w[2] * xp + w[3]


def _hyena_body(sk_ref, v_ref, x1_ref, x2_ref, wv_ref, w1_ref, w2_ref, k0r_ref, k0i_ref, k1r_ref, k1i_ref,
                f1p_ref, f2_ref, f2c_ref, f3p_ref, tr_ref, ti_ref, o_ref, *, rc):
    ct = pl.program_id(0)
    tr = tr_ref[...]
    ti = ti_ref[...]
    np_ = v_ref.shape[2]
    nb = v_ref.shape[0]

    def load(ref, w_ref, g):
        parts = [jnp.swapaxes(_conv3_time(ref[bb, g], w_ref[:, g]), 0, 1) for bb in range(nb)]
        return jnp.concatenate(parts, axis=1)

    def long_conv(z3, kr_ref, ki_ref, g):
        r = FFT_R
        gsub = SUBLANES // HY_SPLIT
        wd = tr.dtype

        def split(m):
            return m[:, :r].reshape(gsub, r, r), m[:, r:].reshape(gsub, r, r)

        def join_t(re, im):
            return jnp.concatenate([jnp.swapaxes(re, 1, 2), jnp.swapaxes(im, 1, 2)], axis=-1).reshape(
                gsub * r, 2 * r).astype(BF16)

        subs = range(HY_SPLIT)
        zs = [z3[s * gsub:(s + 1) * gsub] for s in subs]
        a = [_dot(jnp.swapaxes(z.astype(wd), 1, 2).reshape(gsub * r, r).astype(BF16), f1p_ref[...], wd) for z in zs]
        p = []
        for m in a:
            ar, ai = split(m)
            p.append(join_t(ar * tr - ai * ti, ar * ti + ai * tr))
        x = [_dot(m, f2_ref[...], wd) for m in p]
        y = []
        for s, m in zip(subs, x):
            rows = pl.ds(g * SUBLANES + s * gsub, gsub)
            kr = kr_ref[rows].reshape(gsub * r, r)
            ki = ki_ref[rows].reshape(gsub * r, r)
            xr, xi = m[:, :r], m[:, r:]
            y.append(jnp.concatenate([xr * kr - xi * ki, xr * ki + xi * kr], axis=-1).astype(BF16))
        b = [_dot(m, f2c_ref[...], wd) for m in y]
        c = []
        for m in b:
            br, bi = split(m)
            c.append(join_t(br * tr + bi * ti, bi * tr - br * ti))
        yt = [_dot(m, f3p_ref[...]) for m in c]
        return jnp.concatenate([jnp.swapaxes(m.reshape(gsub, r, r), 1, 2) for m in yt], axis=0)

    def group(g, carry):
        cbase = ct * rc + g * SUBLANES

        def skip(z3, order):
            return jnp.stack([sk_ref[order, cbase + ci] * z3[ci] for ci in range(SUBLANES)], axis=0)

        v3 = load(v_ref, wv_ref, g)
        z1 = load(x1_ref, w1_ref, g) * (long_conv(v3, k0r_ref, k0i_ref, g) + skip(v3, 0))
        out = load(x2_ref, w2_ref, g) * (long_conv(z1, k1r_ref, k1i_ref, g) + skip(z1, 1))
        for bb in range(nb):
            o_ref[bb, g] = jnp.swapaxes(out[:, bb * np_:(bb + 1) * np_, :], 0, 1)
        return carry

    lax.fori_loop(0, rc // SUBLANES, group, 0)


def _hyena(hy5, cwt, skip, kfr, kfi, tables):
    bsz, n8, np_, _, _ = hy5.shape
    assert bsz % 2 == 0 and 2 * np_ == FFT_R
    rc = RC_HY
    r8 = rc // SUBLANES
    nct = D_HY // rc
    r = FFT_R

    def xspec(off):
        return pl.BlockSpec((2, r8, np_, SUBLANES, LANES), lambda c, b: (b, c + off * nct, 0, 0, 0))

    def wspec(off):
        return pl.BlockSpec((HY_CONV + 1, r8, SUBLANES, LANES), lambda c, b: (0, c + off * nct, 0, 0))

    def kspec(order):
        return pl.BlockSpec((rc, r, r), lambda c, b: (c + order * nct, 0, 0))

    consts = [tables[k] for k in ("f1p", "f2", "f2c", "f3p", "tr16", "ti16")]
    return pl.pallas_call(
        functools.partial(_hyena_body, rc=rc),
        grid=(nct, bsz // 2),
        in_specs=[pl.BlockSpec(memory_space=pltpu.SMEM), xspec(0), xspec(1), xspec(2), wspec(0), wspec(1), wspec(2),
                  kspec(0), kspec(0), kspec(1), kspec(1)] + _const_specs(consts, 2),
        out_specs=pl.BlockSpec((2, r8, np_, SUBLANES, LANES), lambda c, b: (b, c, 0, 0, 0)),
        out_shape=jax.ShapeDtypeStruct((bsz, D_HY // SUBLANES, np_, SUBLANES, LANES), F32),
        compiler_params=_params(("parallel", "parallel")),
        name="hyena",
    )(skip, hy5, hy5, hy5, cwt, cwt, cwt, kfr, kfi, kfr, kfi, *consts)


def _gelu_tanh(x):
    return 0.5 * x * (1.0 + jnp.tanh(math.sqrt(2.0 / math.pi) * (x + 0.044715 * (x * x * x))))


def _mix_body(hf_ref, hb_ref, gate_ref, yhy_ref, x_ref, gl_ref, gh_ref, wl_ref, wh_ref, gffn_ref,
              wrh_ref, wrl_ref, rb_ref, tri_ref, ltri_ref, xmid_ref, h_ref, routec_ref, routet_ref, cnt_ref):
    tm = x_ref.shape[1]
    y_lru = (hf_ref[0, 0] + hb_ref[0, 0]) * _gelu_tanh(gate_ref[0])
    nl = _rms(y_lru, gl_ref[...]).astype(BF16)
    gh = gh_ref[...]
    nts = []
    for j in range(tm // LANES):
        yt = yhy_ref[0, :, j, :, :].reshape(D_HY, LANES)
        nts.append((yt * lax.rsqrt(jnp.mean(yt * yt, axis=0, keepdims=True) + EPS) * gh).astype(BF16))
    nt = jnp.concatenate(nts, axis=1)
    mix = _dot(nl, wl_ref[...]) + lax.dot_general(nt, wh_ref[...], (((0,), (0,)), ((), ())),
                                                   preferred_element_type=F32)
    xmid_ref[0] = x_ref[0] + mix
    h = _rms(xmid_ref[0], gffn_ref[...])
    h_ref[...] = h.astype(BF16)

    hh, hl = _split_bf16(h)
    logits = _dot(hh, wrh_ref[...]) + _dot(hl, wrh_ref[...]) + _dot(hh, wrl_ref[...]) + rb_ref[...]
    lt = logits.T[:ROUTE_ROWS]
    row = lax.broadcasted_iota(I32, lt.shape, 0)
    big = jnp.int32(1 << 20)
    is_g = row < N_GROUPS
    lg = jnp.where(is_g, lt, NEG_BIG)
    mg = jnp.max(lg, axis=0, keepdims=True)
    gidx = jnp.min(jnp.where(lg == mg, row, big), axis=0, keepdims=True)
    g_p = 1.0 / jnp.sum(jnp.where(is_g, jnp.exp(lg - mg), 0.0), axis=0, keepdims=True)
    e_lo = N_GROUPS + EXPERTS_PER_GROUP * gidx
    is_e = jnp.logical_and(row >= e_lo, row < e_lo + EXPERTS_PER_GROUP)
    le = jnp.where(is_e, lt, NEG_BIG)
    m1 = jnp.max(le, axis=0, keepdims=True)
    i1 = jnp.min(jnp.where(le == m1, row, big), axis=0, keepdims=True)
    le2 = jnp.where(row == i1, NEG_BIG, le)
    m2 = jnp.max(le2, axis=0, keepdims=True)
    i2 = jnp.min(jnp.where(le2 == m2, row, big), axis=0, keepdims=True)
    ratio = jnp.exp(m2 - m1)
    gate1 = g_p / (1.0 + ratio)
    gate2 = g_p * ratio / (1.0 + ratio)
    oh1 = row == i1
    oh2 = row == i2
    oh = jnp.where(oh1, 1.0, jnp.where(oh2, 1.0, 0.0))
    before = _dot(oh.astype(BF16), tri_ref[...])
    cnt = jnp.sum(oh, axis=1, keepdims=True)
    nchunk = jnp.floor((cnt + (ROW_CHUNK - 1)) * (1.0 / ROW_CHUNK))
    run_start = ROW_CHUNK * _dot(ltri_ref[...], jnp.broadcast_to(nchunk, (ROUTE_ROWS, LANES)).astype(BF16))[:, 0:1]
    pos = run_start + before
    l0 = jnp.sum(jnp.where(oh1, pos, 0.0), axis=0, keepdims=True)
    l1 = jnp.sum(jnp.where(oh2, pos, 0.0), axis=0, keepdims=True)
    cnt_ref[0] = cnt
    route_t = jnp.concatenate([l0, l1, gate1, gate2, jnp.zeros((SUBLANES - 4, tm), F32)], axis=0)
    routet_ref[0] = route_t
    routec_ref[...] = jnp.concatenate([route_t, jnp.zeros((LANES - SUBLANES, tm), F32)], axis=0).T[:, :SUBLANES]


def _mix_router(h4, lru, yhy5, x, gl, gh, wl, wh, gffn, wrh, wrl, rb, tri, ltri):
    bsz, seq, d = x.shape
    tm = TM_MIX
    nt = seq // tm
    n_tok = bsz * seq
    c2 = lambda b, i: (0, 0)
    return pl.pallas_call(
        _mix_body,
        grid=(bsz, nt),
        in_specs=[
            pl.BlockSpec((1, 1, tm, D_LRU), lambda b, i: (0, b, i, 0)),
            pl.BlockSpec((1, 1, tm, D_LRU), lambda b, i: (1, b, i, 0)),
            pl.BlockSpec((1, tm, D_LRU), lambda b, i: (b, i, 1)),
            pl.BlockSpec((1, D_HY // SUBLANES, tm // LANES, SUBLANES, LANES), lambda b, i: (b, 0, i, 0, 0)),
            pl.BlockSpec((1, tm, d), lambda b, i: (b, i, 0)),
            pl.BlockSpec((1, D_LRU), c2),
            pl.BlockSpec((D_HY, 1), c2),
            pl.BlockSpec((D_LRU, d), c2),
            pl.BlockSpec((D_HY, d), c2),
            pl.BlockSpec((1, d), c2),
            pl.BlockSpec((d, LANES), c2),
            pl.BlockSpec((d, LANES), c2),
            pl.BlockSpec((1, LANES), c2),
            pl.BlockSpec((tm, tm), c2),
            pl.BlockSpec((ROUTE_ROWS, ROUTE_ROWS), c2),
        ],
        out_specs=[
            pl.BlockSpec((1, tm, d), lambda b, i: (b, i, 0)),
            pl.BlockSpec((tm, d), lambda b, i: (b * nt + i, 0)),
            pl.BlockSpec((tm, SUBLANES), lambda b, i: (b * nt + i, 0)),
            pl.BlockSpec((1, SUBLANES, tm), lambda b, i: (b * nt + i, 0, 0)),
            pl.BlockSpec((1, ROUTE_ROWS, 1), lambda b, i: (b * nt + i, 0, 0)),
        ],
        out_shape=[
            jax.ShapeDtypeStruct((bsz, seq, d), F32),
            jax.ShapeDtypeStruct((n_tok, d), BF16),
            jax.ShapeDtypeStruct((n_tok, SUBLANES), F32),
            jax.ShapeDtypeStruct((n_tok // tm, SUBLANES, tm), F32),
            jax.ShapeDtypeStruct((n_tok // tm, ROUTE_ROWS, 1), F32),
        ],
        compiler_params=_params(("parallel", "parallel")),
        name="mix_router",
    )(h4, h4, lru, yhy5, x, gl, gh, wl, wh, gffn, wrh, wrl, rb, tri, ltri)


def _start_chunks(i, tch_ref, dst_ref, make_copy):
    n = tch_ref[i]

    def start(k, priority):
        make_copy(k * ROW_CHUNK, dst_ref[i * MAX_CHUNKS + k], ROW_CHUNK).start(priority=priority)

    def per_pair(k2, c):
        start(2 * k2, 0)
        start(2 * k2 + 1, 1)
        return c

    lax.fori_loop(0, n // 2, per_pair, 0)

    @pl.when(n % 2 == 1)
    def _():
        start(n - 1, 0)


def _wait_chunks(count, make_copy):
    nbits = MAX_CHUNKS.bit_length()
    for j in reversed(range(nbits)):
        @pl.when((count >> j) & 1 == 1)
        def _():
            make_copy(0, 0, ROW_CHUNK << j).wait()


def _dispatch_body(dst_ref, tch_ref, tn_ref, ts_ref, nu_ref, h_ref, routet_ref, xb_ref,
                   buf, zbuf, sem, zsem):
    i = pl.program_id(0)
    n_tiles = pl.num_programs(0)
    tm = h_ref.shape[0]
    lbuf = buf.shape[1]
    n_blocks = xb_ref.shape[0] // MOE_BLK
    slot = i % 2

    def aligned(r):
        return pl.ds(pl.multiple_of(r, ROW_CHUNK), ROW_CHUNK)

    def zero_chunk(r):
        return pltpu.make_async_copy(zbuf.at[pl.ds(0, ROW_CHUNK)], xb_ref.at[aligned(r)], zsem)

    def zero_block(j):
        return pltpu.make_async_copy(zbuf, xb_ref.at[pl.ds(pl.multiple_of(j * MOE_BLK, MOE_BLK), MOE_BLK)], zsem)

    def for_each_zero_copy(act):
        def per_expert(e, c):
            def per_chunk(k, c2):
                act(zero_chunk(ts_ref[e] + k * ROW_CHUNK))
                return c2

            lax.fori_loop(0, tn_ref[e], per_chunk, 0)
            return c

        lax.fori_loop(0, N_EXPERTS, per_expert, 0)

        def per_block(j, c):
            act(zero_block(j))
            return c

        lax.fori_loop(nu_ref[0], n_blocks, per_block, 0)

    @pl.when(i == 0)
    def _():
        zbuf[...] = jnp.zeros_like(zbuf)
        for_each_zero_copy(lambda cp: cp.start())

    l0t = routet_ref[0, 0:1, :].astype(I32)
    l1t = routet_ref[0, 1:2, :].astype(I32)
    row = lax.broadcasted_iota(I32, (lbuf, tm), 0)
    perm = jnp.where(row == l0t, 1.0, jnp.where(row == l1t, 1.0, 0.0)).astype(BF16)
    srt = _dot(perm, h_ref[...], BF16)

    def copy_from(s):
        def make_copy(src_row, dst_row, rows):
            return pltpu.make_async_copy(buf.at[s, pl.ds(pl.multiple_of(src_row, ROW_CHUNK), rows)],
                                         xb_ref.at[pl.ds(pl.multiple_of(dst_row, ROW_CHUNK), rows)], sem.at[s])
        return make_copy

    @pl.when(i >= 2)
    def _():
        _wait_chunks(tch_ref[i - 2], copy_from(slot))

    buf[slot] = srt
    _start_chunks(i, tch_ref, dst_ref, copy_from(slot))

    @pl.when(i == n_tiles - 1)
    def _():
        @pl.when(i >= 1)
        def _():
            _wait_chunks(tch_ref[i - 1], copy_from(1 - slot))

        _wait_chunks(tch_ref[i], copy_from(slot))
        for_each_zero_copy(lambda cp: cp.wait())


def _dispatch(chunk_dst, tile_nch, tail_n, tail_s, n_used, h, route_t, n_slots):
    n_tok, d = h.shape
    tm = TM_MIX
    grid_spec = pltpu.PrefetchScalarGridSpec(
        num_scalar_prefetch=5,
        grid=(n_tok // tm,),
        in_specs=[
            pl.BlockSpec((tm, d), lambda i, *_: (i, 0)),
            pl.BlockSpec((1, SUBLANES, tm), lambda i, *_: (i, 0, 0)),
        ],
        out_specs=pl.BlockSpec(memory_space=pl.ANY),
        scratch_shapes=[pltpu.VMEM((2, LOCAL_ROWS, d), BF16), pltpu.VMEM((MOE_BLK, d), BF16),
                        pltpu.SemaphoreType.DMA((2,)), pltpu.SemaphoreType.DMA(())],
    )
    return pl.pallas_call(
        _dispatch_body,
        grid_spec=grid_spec,
        out_shape=jax.ShapeDtypeStruct((n_slots, d), BF16),
        compiler_params=_params(("arbitrary",)),
        name="moe_dispatch",
    )(chunk_dst, tile_nch, tail_n, tail_s, n_used, h, route_t)


def _expert_body(be_ref, fl_ref, sw_ref, nu_ref, xb_ref, w1_ref, w3_ref, w2_ref, yb_ref,
                 w1_sc, w3_sc, w2_sc, w1_ring, w3_ring, w2_ring, sem):
    j = pl.program_id(0)
    n_used = nu_ref[0]
    used = j < n_used

    def weight_copies(jj):
        e = be_ref[jj]
        s = sw_ref[jj] % WEIGHT_SLOTS
        return [pltpu.make_async_copy(w_ref.at[e], ring.at[s], sem.at[s])
                for w_ref, ring in ((w1_ref, w1_ring), (w3_ref, w3_ring), (w2_ref, w2_ring))]

    def fetch_if_first(jj):
        @pl.when(jnp.logical_and(jj < n_used, fl_ref[jnp.minimum(jj, pl.num_programs(0) - 1)] == 1))
        def _():
            for cp in weight_copies(jj):
                cp.start()

    @pl.when(j == 0)
    def _():
        for jj in range(WEIGHT_LOOKAHEAD):
            fetch_if_first(jnp.int32(jj))

    fetch_if_first(j + WEIGHT_LOOKAHEAD)

    @pl.when(jnp.logical_and(used, fl_ref[j] == 1))
    def _():
        for cp in weight_copies(j):
            cp.wait()
        s = sw_ref[j] % WEIGHT_SLOTS
        w1_sc[...] = w1_ring[s].astype(BF16)
        w3_sc[...] = w3_ring[s].astype(BF16)
        w2_sc[...] = w2_ring[s].astype(BF16)

    @pl.when(used)
    def _():
        xb = xb_ref[...]
        a = _dot(xb, w1_sc[...])
        b = _dot(xb, w3_sc[...])
        act = (a * _sigmoid(a) * b).astype(BF16)
        yb_ref[...] = _dot(act, w2_sc[...], BF16)

    @pl.when(jnp.logical_not(used))
    def _():
        yb_ref[...] = jnp.zeros_like(yb_ref)


def _experts(block_e, first_flag, n_used, xb, w1, w3, w2):
    n_slots, d = xb.shape
    blk = MOE_BLK
    de = w1.shape[2]
    switch_no = jnp.cumsum(first_flag) - 1
    hbm = pl.BlockSpec(memory_space=pl.ANY)
    grid_spec = pltpu.PrefetchScalarGridSpec(
        num_scalar_prefetch=4,
        grid=(n_slots // blk,),
        in_specs=[pl.BlockSpec((blk, d), lambda j, be, fl, sw, nu: (jnp.minimum(j, nu[0] - 1), 0)), hbm, hbm, hbm],
        out_specs=pl.BlockSpec((blk, d), lambda j, *_: (j, 0)),
        scratch_shapes=[pltpu.VMEM((d, de), BF16), pltpu.VMEM((d, de), BF16), pltpu.VMEM((de, d), BF16),
                        pltpu.VMEM((WEIGHT_SLOTS, d, de), F32), pltpu.VMEM((WEIGHT_SLOTS, d, de), F32),
                        pltpu.VMEM((WEIGHT_SLOTS, de, d), F32), pltpu.SemaphoreType.DMA((WEIGHT_SLOTS,))],
    )
    return pl.pallas_call(
        _expert_body,
        grid_spec=grid_spec,
        out_shape=jax.ShapeDtypeStruct((n_slots, d), BF16),
        compiler_params=_params(("arbitrary",)),
        name="moe_experts",
    )(block_e, first_flag, switch_no.astype(I32), n_used, xb, w1, w3, w2)


def _combine_body(dst_ref, tch_ref, xmid_ref, route_ref, g_ref, yb_ref, o_ref, buf, sem):
    i = pl.program_id(0)
    n_tiles = pl.num_programs(0)
    tm = xmid_ref.shape[0]
    lbuf = buf.shape[1]
    slot = i % 2

    def copy_into(s):
        def make_copy(local_row, slot_row, rows):
            return pltpu.make_async_copy(yb_ref.at[pl.ds(pl.multiple_of(slot_row, ROW_CHUNK), rows)],
                                         buf.at[s, pl.ds(pl.multiple_of(local_row, ROW_CHUNK), rows)], sem.at[s])
        return make_copy

    @pl.when(i == 0)
    def _():
        buf[...] = jnp.zeros_like(buf)
        _start_chunks(i, tch_ref, dst_ref, copy_into(slot))

    @pl.when(i + 1 < n_tiles)
    def _():
        _start_chunks(i + 1, tch_ref, dst_ref, copy_into(1 - slot))

    route = route_ref[...]
    l0 = route[:, 0:1].astype(I32)
    l1 = route[:, 1:2].astype(I32)
    lane = lax.broadcasted_iota(I32, (tm, lbuf), 1)
    gmat = jnp.where(lane == l0, route[:, 2:3], jnp.where(lane == l1, route[:, 3:4], 0.0)).astype(BF16)
    _wait_chunks(tch_ref[i], copy_into(slot))
    y = xmid_ref[...] + _dot(gmat, buf[slot])
    o_ref[...] = _rms(y, g_ref[...])


def _combine(chunk_dst, tile_nch, xmid, route, g, yb):
    n_tok, d = xmid.shape
    tm = TM_MIX
    grid_spec = pltpu.PrefetchScalarGridSpec(
        num_scalar_prefetch=2,
        grid=(n_tok // tm,),
        in_specs=[
            pl.BlockSpec((tm, d), lambda i, *_: (i, 0)),
            pl.BlockSpec((tm, SUBLANES), lambda i, *_: (i, 0)),
            pl.BlockSpec((1, d), lambda i, *_: (0, 0)),
            pl.BlockSpec(memory_space=pl.ANY),
        ],
        out_specs=pl.BlockSpec((tm, d), lambda i, *_: (i, 0)),
        scratch_shapes=[pltpu.VMEM((2, LOCAL_ROWS, d), BF16), pltpu.SemaphoreType.DMA((2,))],
    )
    return pl.pallas_call(
        _combine_body,
        grid_spec=grid_spec,
        out_shape=jax.ShapeDtypeStruct((n_tok, d), F32),
        compiler_params=_params(("arbitrary",)),
        name="moe_combine",
    )(chunk_dst, tile_nch, xmid, route, g, yb)


def _filter_features(seq):
    pos = jnp.arange(seq, dtype=F32)
    t = jnp.linspace(0.0, 1.0, seq, dtype=F32)
    w = (2.0 * math.pi / seq) * pos
    bands = jnp.linspace(1e-4, HY_BANDS - 1, HY_BANDS, dtype=F32)
    ang = w[:, None] * bands[None, :]
    feats = jnp.concatenate([t[:, None], jnp.cos(ang), jnp.sin(ang)], axis=-1)
    featst = jnp.pad(feats, ((0, 0), (0, HY_EMB_PAD - HY_EMB))).T[None]
    return featst, t[None, :]


def _gate_weights(wr, wi):
    half = D_LRU // 2
    hph = half // LRU_HEAD_DIM
    eye = 0.5 * jnp.eye(hph, dtype=wr.dtype)[:, None, :, None]

    def block_diag(w):
        return (eye * w[:, :, None, :]).reshape(half, half)

    out = [jnp.concatenate([block_diag(wr[hh * hph:(hh + 1) * hph]), block_diag(wi[hh * hph:(hh + 1) * hph])], axis=1)
           for hh in range(2)]
    return jnp.stack(out, axis=0).astype(BF16)


def _layer(x, l, p):
    bsz, seq, d = x.shape
    n_tok = bsz * seq
    row = lambda a: a.reshape(1, -1).astype(F32)
    col = lambda a: a.reshape(-1, 1).astype(F32)

    w_in = p["w_in"][l]
    wl = w_in[:, :2 * D_LRU].astype(BF16)
    wht = w_in[:, 2 * D_LRU:].T.astype(BF16)
    lru, hy5 = _inproj(x, row(p["norm_mix_g"][l]), wl, wht)

    wg = jnp.stack([_gate_weights(p["lru_wr_f"][l], p["lru_wi_f"][l]),
                    _gate_weights(p["lru_wr_b"][l], p["lru_wi_b"][l])], axis=0)
    stack2 = lambda a, b: jnp.stack([a.reshape(1, -1), b.reshape(1, -1)], axis=0).astype(F32)
    h4 = _lru(lru, p["lru_conv_w"][l], row(p["lru_conv_b"][l]), wg,
              0.5 * stack2(p["lru_br_f"][l], p["lru_br_b"][l]), 0.5 * stack2(p["lru_bi_f"][l], p["lru_bi_b"][l]),
              stack2(p["lru_lambda_f"][l], p["lru_lambda_b"][l]))

    tables = _fft_tables()
    featst, t_f = _filter_features(seq)
    w1t = jnp.pad(p["hy_filt_w1"][l].T, ((0, 0), (0, HY_EMB_PAD - HY_EMB)))
    z2t = _filt_mlp(featst, w1t, col(p["hy_filt_b1"][l]), col(p["hy_filt_freq1"][l]),
                    p["hy_filt_w2"][l].T, col(p["hy_filt_b2"][l]), col(p["hy_filt_freq2"][l]))
    w3 = p["hy_filt_w3"][l]
    nfr = HY_ORDER * D_HY
    deltas = jnp.abs(jnp.linspace(math.log(HY_DECAY_TARGET) / HY_SLOW_DECAY_PCT,
                                  math.log(HY_DECAY_TARGET) / HY_FAST_DECAY_PCT, D_HY, dtype=F32))
    ktile = _filt_time(z2t, w3[:, :nfr].T, w3[:, nfr:].T, col(jnp.tile(deltas, HY_ORDER)), t_f)
    kfr, kfi = _filt_fft(ktile, tables)
    cwt = jnp.concatenate([p["hy_conv_w"][l], p["hy_conv_b"][l][None, :]], axis=0)
    cwt = jnp.broadcast_to(cwt.reshape(HY_CONV + 1, -1, SUBLANES, 1), (HY_CONV + 1, cwt.shape[1] // SUBLANES, SUBLANES, LANES))
    yhy5 = _hyena(hy5, cwt, p["hy_skip"][l], kfr, kfi, tables)

    w_out = p["w_out"][l]
    wr_cat = jnp.concatenate([p["router_group_w"][l], p["router_expert_w"][l]], axis=1)
    wr_cat = jnp.pad(wr_cat, ((0, 0), (0, LANES - wr_cat.shape[1])))
    wrh, wrl = _split_bf16(wr_cat)
    rb = jnp.concatenate([p["router_group_b"][l], p["router_expert_b"][l]])
    rb = jnp.pad(rb, (0, LANES - rb.shape[0])).reshape(1, LANES)
    earlier = lambda n: (jnp.arange(n)[:, None] < jnp.arange(n)[None, :]).astype(BF16)
    xmid, h, route, route_t, tile_cnt = _mix_router(
        h4, lru, yhy5, x, row(p["grp_norm_lru_g"][l]), col(p["grp_norm_hy_g"][l]),
        w_out[:D_LRU].astype(BF16), w_out[D_LRU:].astype(BF16), row(p["norm_ffn_g"][l]), wrh, wrl, rb,
        earlier(TM_MIX), earlier(ROUTE_ROWS).T)

    blk = MOE_BLK
    ch = ROW_CHUNK
    n_tiles = n_tok // TM_MIX
    n_blocks = (2 * n_tok + n_tiles * N_EXPERTS * (ch - 1)) // blk + N_EXPERTS
    cnt = tile_cnt[:, N_GROUPS:N_GROUPS + N_EXPERTS, 0].astype(I32)
    run = (cnt + ch - 1) // ch * ch
    tot_e = jnp.sum(run, axis=0)
    padded_e = (tot_e + blk - 1) // blk * blk
    pad_end = jnp.cumsum(padded_e)
    pad_start = pad_end - padded_e
    gstart = pad_start[None, :] + jnp.cumsum(run, axis=0) - run
    lstart = jnp.cumsum(run, axis=1) - run
    flat = lambda a: a.reshape(-1).astype(I32)
    tile_nch = flat(jnp.sum(run, axis=1) // ch)
    nch_end = jnp.cumsum(run // ch, axis=1)
    k = jnp.arange(MAX_CHUNKS, dtype=I32)
    base = gstart - lstart
    step = base[:, 1:] - base[:, :-1]
    passed = (k[None, :, None] >= nch_end[:, None, :-1]).astype(I32)
    chunk_dst = flat(base[:, :1] + jnp.sum(passed * step[:, None, :], axis=2) + k[None, :] * ch)
    tail_n = flat((padded_e - tot_e) // ch)
    tail_s = flat(pad_start + tot_e)
    block_start = jnp.arange(n_blocks, dtype=I32) * blk
    block_e = jnp.minimum(jnp.sum(block_start[:, None] >= pad_end[None, :], axis=1), N_EXPERTS - 1).astype(I32)
    first_flag = jnp.concatenate([jnp.ones((1,), I32), (block_e[1:] != block_e[:-1]).astype(I32)])
    n_used = (pad_end[-1:] // blk).astype(I32)

    xb = _dispatch(chunk_dst, tile_nch, tail_n, tail_s, n_used, h, route_t, n_blocks * blk)
    yb = _experts(block_e, first_flag, n_used, xb, p["exp_w1"][l], p["exp_w3"][l], p["exp_w2"][l])
    return xmid.reshape(n_tok, d), route, (chunk_dst, tile_nch), yb


def kernel(x, norm_mix_g, w_in, lru_conv_w, lru_conv_b, lru_wr_f, lru_br_f, lru_wi_f, lru_bi_f, lru_lambda_f, lru_wr_b, lru_br_b, lru_wi_b, lru_bi_b, lru_lambda_b, hy_conv_w, hy_conv_b, hy_filt_w1, hy_filt_b1, hy_filt_freq1, hy_filt_w2, hy_filt_b2, hy_filt_freq2, hy_filt_w3, hy_skip, grp_norm_lru_g, grp_norm_hy_g, w_out, norm_ffn_g, router_group_w, router_group_b, router_expert_w, router_expert_b, exp_w1, exp_w3, exp_w2, norm_final_g):
    p = dict(locals())
    bsz, seq, d = x.shape
    assert d == D_MODEL and 2 * seq == FFT_R * FFT_R and w_in.shape[0] == 1
    xmid, route, (chunk_dst, tile_nch), yb = _layer(x, 0, p)
    out = _combine(chunk_dst, tile_nch, xmid, route, norm_final_g.reshape(1, d), yb)
    return out.reshape(bsz, seq, d)
```

```python
import functools
import math

import numpy as np
import jax
import jax.numpy as jnp
from jax import lax
from jax.experimental import pallas as pl
from jax.experimental.pallas import tpu as pltpu

F32 = jnp.float32
BF16 = jnp.bfloat16
I32 = jnp.int32

D_MODEL = 1024
D_LRU = 512
D_HY = 512
LRU_HEADS = 8
LRU_HEAD_DIM = D_LRU // LRU_HEADS
LRU_CONV = 4
LRU_C = 8.0
HY_ORDER = 2
HY_CONV = 3
HY_BANDS = 16
HY_EMB = 2 * HY_BANDS + 1
HY_EMB_PAD = 40
HY_FFN = 64
HY_FAST_DECAY_PCT = 0.3
HY_SLOW_DECAY_PCT = 1.5
HY_DECAY_TARGET = 1e-2
N_GROUPS = 4
EXPERTS_PER_GROUP = 8
N_EXPERTS = N_GROUPS * EXPERTS_PER_GROUP
D_EXPERT = D_MODEL // 2
EPS = 1e-6

LANES = 128
SUBLANES = 8
FFT_R = 128
VMEM_LIMIT = 56 * 1024 * 1024

TM_IN = 1024
TC_LRU = 2048
RC_HY = 32
RF_KF = 128
HY_SPLIT = 2
RB_FILT = 128
TM_MIX = 512
MOE_BLK = 512
WEIGHT_LOOKAHEAD = 2
WEIGHT_SLOTS = WEIGHT_LOOKAHEAD + 1
ROW_CHUNK = 16
LOCAL_ROWS = -(-(2 * TM_MIX + N_EXPERTS * (ROW_CHUNK - 1)) // LANES) * LANES
MAX_CHUNKS = LOCAL_ROWS // ROW_CHUNK
ROUTE_ROWS = -(-(N_GROUPS + N_EXPERTS) // SUBLANES) * SUBLANES
NEG_BIG = -1e30


def _params(sem, vmem=VMEM_LIMIT):
    return pltpu.CompilerParams(dimension_semantics=sem, vmem_limit_bytes=vmem)


def _rms(x, g):
    return x * lax.rsqrt(jnp.mean(x * x, axis=-1, keepdims=True) + EPS) * g


def _sigmoid(x):
    return 1.0 / (1.0 + jnp.exp(-x))


def _split_bf16(a):
    hi = a.astype(BF16)
    lo = (a - hi.astype(F32)).astype(BF16)
    return hi, lo


def _dot(a, b, out=F32):
    return jnp.dot(a, b, preferred_element_type=F32).astype(out)


def _dot3(a, b):
    ah, al = _split_bf16(a)
    bh, bl = _split_bf16(b)
    return _dot(ah, bh) + _dot(al, bh) + _dot(ah, bl)


def _inproj_body(x_ref, g_ref, wl_ref, wht_ref, lru_ref, hy_ref):
    hn = _rms(x_ref[0], g_ref[...]).astype(BF16)
    lru_ref[0] = _dot(hn, wl_ref[...])
    hyt = lax.dot_general(wht_ref[...], hn, (((1,), (1,)), ((), ())), preferred_element_type=F32)
    nrow = hyt.shape[0] // SUBLANES
    for j in range(hyt.shape[1] // LANES):
        hy_ref[0, :, j, :, :] = hyt[:, LANES * j:LANES * (j + 1)].reshape(nrow, SUBLANES, LANES)


def _inproj(x, g, wl, wht):
    bsz, seq, d = x.shape
    nl = wl.shape[1]
    nh = wht.shape[0]
    tm = TM_IN
    return pl.pallas_call(
        _inproj_body,
        grid=(bsz, seq // tm),
        in_specs=[
            pl.BlockSpec((1, tm, d), lambda b, i: (b, i, 0)),
            pl.BlockSpec((1, d), lambda b, i: (0, 0)),
            pl.BlockSpec((d, nl), lambda b, i: (0, 0)),
            pl.BlockSpec((nh, d), lambda b, i: (0, 0)),
        ],
        out_specs=[
            pl.BlockSpec((1, tm, nl), lambda b, i: (b, i, 0)),
            pl.BlockSpec((1, nh // SUBLANES, tm // LANES, SUBLANES, LANES), lambda b, i: (b, 0, i, 0, 0)),
        ],
        out_shape=[
            jax.ShapeDtypeStruct((bsz, seq, nl), F32),
            jax.ShapeDtypeStruct((bsz, nh // SUBLANES, seq // LANES, SUBLANES, LANES), F32),
        ],
        compiler_params=_params(("parallel", "parallel")),
        name="inproj",
    )(x, g, wl, wht)


def _sqrt_one_minus_sq(a, neg_log_a):
    om = jnp.maximum(jnp.tanh(neg_log_a) * (1.0 + a * a), 1e-30)
    return om * lax.rsqrt(om)


def _group_scan(a3, b3, reverse):
    sub = lax.broadcasted_iota(I32, a3.shape, 1)
    for s in (1, 2, 4):
        if reverse:
            a_sh = pltpu.roll(a3, SUBLANES - s, axis=1)
            b_sh = pltpu.roll(b3, SUBLANES - s, axis=1)
            m = sub < SUBLANES - s
        else:
            a_sh = pltpu.roll(a3, s, axis=1)
            b_sh = pltpu.roll(b3, s, axis=1)
            m = sub >= s
        b3 = jnp.where(m, a3 * b_sh + b3, b3)
        a3 = jnp.where(m, a3 * a_sh, a3)
    return a3, b3


def _lru_body(cur_ref, prev_ref, next_ref, cw_ref, cb_ref, wg_ref, br_ref, bi_ref, lam_ref, o_ref,
              carry_sc, *, nt, tc):
    d = pl.program_id(1)
    i = pl.program_id(2)
    c = jnp.where(d == 0, i, nt - 1 - i)
    ns = SUBLANES
    ng = tc // ns
    half = D_LRU // 2

    @pl.when(i == 0)
    def _():
        carry_sc[...] = jnp.zeros_like(carry_sc)

    xs = jnp.swapaxes(cur_ref[0].reshape(ns, ng, D_LRU), 0, 1)
    prev = jnp.where(c > 0, prev_ref[0], 0.0)
    nxt = jnp.where(c < nt - 1, next_ref[0], 0.0)
    sub = lax.broadcasted_iota(I32, (ns, D_LRU), 0)

    def prev_segment(slab, halo_row):
        return jnp.where(sub == 0, halo_row, pltpu.roll(slab, 1, axis=0))

    def next_segment(slab, halo_row):
        return jnp.where(sub == ns - 1, halo_row, pltpu.roll(slab, ns - 1, axis=0))

    xext = jnp.concatenate([prev_segment(xs[ng - 2], prev[ns - 2:ns - 1])[None],
                            prev_segment(xs[ng - 1], prev[ns - 1:ns])[None], xs,
                            next_segment(xs[0], nxt[0:1])[None]], axis=0)
    cw = cw_ref[...]
    xc = cb_ref[...] + cw[0:1] * xext[0:ng]
    for k in range(1, LRU_CONV):
        xc = xc + cw[k:k + 1] * xext[k:k + ng]
    xc = xc.reshape(tc, D_LRU)

    lam = lam_ref[0]
    nlam = -lam
    softplus = jnp.maximum(nlam, 0.0) + jnp.log1p(jnp.exp(-jnp.abs(nlam)))
    half_c = (0.5 * LRU_C) * softplus

    def gates(hh):
        sl = slice(half * hh, half * (hh + 1))
        xh = xc[:, sl]
        logits = _dot(xh.astype(BF16), wg_ref[0, hh])
        tr_ = jnp.tanh(logits[:, :half] + br_ref[0][:, sl])
        gi = 0.5 + 0.5 * jnp.tanh(logits[:, half:] + bi_ref[0][:, sl])
        neg_log_a = half_c[:, sl] * (1.0 + tr_)
        a = jnp.exp(-neg_log_a)
        b = _sqrt_one_minus_sq(a, neg_log_a) * (gi * xh)
        return a.reshape(ng, ns, half), b.reshape(ng, ns, half)

    def run(reverse):
        subh = lax.broadcasted_iota(I32, (ns, half), 0)
        for hh in range(2):
            sl = slice(half * hh, half * (hh + 1))
            a3, b3 = gates(hh)
            h = jnp.zeros((ns, half), F32)
            p = jnp.ones((ns, half), F32)
            hs = [None] * ng
            ps = [None] * ng
            for g in (range(ng - 1, -1, -1) if reverse else range(ng)):
                h = a3[g] * h + b3[g]
                p = a3[g] * p
                hs[g] = h
                ps[g] = p
            ac, hc = _group_scan(p[None], h[None], reverse)
            c_in = carry_sc[:, sl]
            end = ac[0] * c_in + hc[0]
            if reverse:
                seg_in = jnp.where(subh == ns - 1, c_in, pltpu.roll(end, ns - 1, axis=0))
                carry_sc[:, sl] = jnp.broadcast_to(end[0:1], end.shape)
            else:
                seg_in = jnp.where(subh == 0, c_in, pltpu.roll(end, 1, axis=0))
                carry_sc[:, sl] = jnp.broadcast_to(end[ns - 1:ns], end.shape)
            h3 = jnp.stack(hs, axis=0) + jnp.stack(ps, axis=0) * seg_in[None]
            o_ref[0, 0, :, sl] = jnp.swapaxes(h3, 0, 1).reshape(tc, half)

    @pl.when(d == 0)
    def _():
        run(False)

    @pl.when(d == 1)
    def _():
        run(True)


def _lru(lru, cw, cb, wg, br, bi, lam):
    bsz, seq, _ = lru.shape
    tc = TC_LRU
    nt = seq // tc
    r8 = tc // SUBLANES
    nrow8 = seq // SUBLANES

    def cidx(d, i):
        return jnp.where(d == 0, i, nt - 1 - i)

    return pl.pallas_call(
        functools.partial(_lru_body, nt=nt, tc=tc),
        grid=(bsz, 2, nt),
        in_specs=[
            pl.BlockSpec((1, tc, D_LRU), lambda b, d, i: (b, cidx(d, i), 0)),
            pl.BlockSpec((1, SUBLANES, D_LRU), lambda b, d, i: (b, jnp.maximum(cidx(d, i) * r8 - 1, 0), 0)),
            pl.BlockSpec((1, SUBLANES, D_LRU), lambda b, d, i: (b, jnp.minimum((cidx(d, i) + 1) * r8, nrow8 - 1), 0)),
            pl.BlockSpec((LRU_CONV, D_LRU), lambda b, d, i: (0, 0)),
            pl.BlockSpec((1, D_LRU), lambda b, d, i: (0, 0)),
            pl.BlockSpec((1, 2, D_LRU // 2, D_LRU), lambda b, d, i: (d, 0, 0, 0)),
            pl.BlockSpec((1, 1, D_LRU), lambda b, d, i: (d, 0, 0)),
            pl.BlockSpec((1, 1, D_LRU), lambda b, d, i: (d, 0, 0)),
            pl.BlockSpec((1, 1, D_LRU), lambda b, d, i: (d, 0, 0)),
        ],
        out_specs=pl.BlockSpec((1, 1, tc, D_LRU), lambda b, d, i: (d, b, cidx(d, i), 0)),
        out_shape=jax.ShapeDtypeStruct((2, bsz, seq, D_LRU), F32),
        scratch_shapes=[pltpu.VMEM((SUBLANES, D_LRU), F32)],
        compiler_params=_params(("parallel", "arbitrary", "arbitrary")),
        name="rg_lru",
    )(lru, lru, lru, cw, cb, wg, br, bi, lam)


def _fft_tables():
    r = FFT_R
    n = r * r
    idx = np.arange(r, dtype=np.float64)
    ang = 2.0 * np.pi * np.outer(idx, idx) / r
    wr, wi = np.cos(ang), -np.sin(ang)
    angt = 2.0 * np.pi * np.outer(idx, idx) / n
    tr, ti = np.cos(angt), -np.sin(angt)
    h = r // 2
    f1 = np.concatenate([wr, wi], axis=1)
    f1p = np.block([[wr[:h], wi[:h]], [-wi[:h], wr[:h]]])
    f2 = np.block([[wr, wi], [-wi, wr]])
    f2c = np.block([[wr, -wi], [wi, wr]])
    f3p = np.block([[wr[:, :h], -wi[:, :h]], [wi[:, :h], wr[:, :h]]]) / n
    as_bf16 = lambda a: jnp.asarray(a, F32).astype(BF16)
    return dict(f1=as_bf16(f1), f1p=as_bf16(f1p), f2=as_bf16(f2), f2c=as_bf16(f2c), f3p=as_bf16(f3p),
                tr16=as_bf16(tr), ti16=as_bf16(ti))


def _const_specs(arrays, nargs):
    return [pl.BlockSpec(a.shape, (lambda nd: (lambda *_: (0,) * nd))(a.ndim)) for a in arrays]


def _fft_fwd(x3, f1, f2, tr, ti):
    g = x3.shape[0]
    r = FFT_R
    wd = tr.dtype
    xt = jnp.swapaxes(x3.astype(wd), 1, 2).reshape(g * r, r)
    a = _dot(xt.astype(BF16), f1, wd)
    ar = a[:, :r].reshape(g, r, r)
    ai = a[:, r:].reshape(g, r, r)
    pr = ar * tr - ai * ti
    pi = ar * ti + ai * tr
    pt = jnp.concatenate([jnp.swapaxes(pr, 1, 2), jnp.swapaxes(pi, 1, 2)], axis=-1).reshape(g * r, 2 * r)
    x = _dot(pt.astype(BF16), f2, BF16)
    return x[:, :r], x[:, r:]


def _filt_mlp_body(ft_ref, w1_ref, b1_ref, f1_ref, w2_ref, b2_ref, f2_ref, o_ref):
    z = jnp.sin(f1_ref[...] * (_dot3(w1_ref[...], ft_ref[0]) + b1_ref[...]))
    o_ref[0] = jnp.sin(f2_ref[...] * (_dot3(w2_ref[...], z) + b2_ref[...]))


def _filt_mlp(featst, w1t, b1, f1, w2t, b2, f2):
    nd, ke, seq = featst.shape
    col = lambda d: (0, 0)
    return pl.pallas_call(
        _filt_mlp_body,
        grid=(nd,),
        in_specs=[
            pl.BlockSpec((1, ke, seq), lambda d: (d, 0, 0)),
            pl.BlockSpec((HY_FFN, ke), col),
            pl.BlockSpec((HY_FFN, 1), col),
            pl.BlockSpec((HY_FFN, 1), col),
            pl.BlockSpec((HY_FFN, HY_FFN), col),
            pl.BlockSpec((HY_FFN, 1), col),
            pl.BlockSpec((HY_FFN, 1), col),
        ],
        out_specs=pl.BlockSpec((1, HY_FFN, seq), lambda d: (d, 0, 0)),
        out_shape=jax.ShapeDtypeStruct((nd, HY_FFN, seq), F32),
        compiler_params=_params(("parallel",)),
        name="hyena_filter_mlp",
    )(featst, w1t, b1, f1, w2t, b2, f2)


def _filt_time_body(z_ref, wf_ref, wb_ref, dl_ref, tf_ref, o_ref):
    seq = z_ref.shape[2]
    decay = jnp.exp(-tf_ref[...] * dl_ref[...])
    hf = _dot3(wf_ref[...], z_ref[0]) * decay
    hb = _dot3(wb_ref[...], z_ref[0]) * decay
    lane = lax.broadcasted_iota(I32, hb.shape, 1)
    hb = jnp.where(lane == 0, 0.0, hb)
    norm = jnp.sum(jnp.abs(hf), axis=-1, keepdims=True) + jnp.sum(jnp.abs(hb), axis=-1, keepdims=True) + EPS
    hf = hf / norm
    hb = hb / norm
    nrow = hf.shape[0] // SUBLANES
    nch = seq // LANES
    lane1 = lax.broadcasted_iota(I32, (hf.shape[0], LANES), 1)
    mirror = (LANES - lane1) % LANES
    for j in range(nch):
        o_ref[:, j, :, :] = hf[:, LANES * j:LANES * (j + 1)].reshape(nrow, SUBLANES, LANES)
        src = nch - 1 - j
        back = jnp.take_along_axis(hb[:, LANES * src:LANES * (src + 1)], mirror, axis=1)
        first = hb[:, LANES * (src + 1):LANES * (src + 1) + 1] if j > 0 else 0.0
        o_ref[:, nch + j, :, :] = jnp.where(lane1 == 0, first, back).reshape(nrow, SUBLANES, LANES)


def _filt_time(z2t, w3f, w3b, delta_rows, t_f):
    nrows = w3f.shape[0]
    seq = z2t.shape[2]
    rb = RB_FILT
    return pl.pallas_call(
        _filt_time_body,
        grid=(nrows // rb,),
        in_specs=[
            pl.BlockSpec((1, HY_FFN, seq), lambda r: (0, 0, 0)),
            pl.BlockSpec((rb, HY_FFN), lambda r: (r, 0)),
            pl.BlockSpec((rb, HY_FFN), lambda r: (r, 0)),
            pl.BlockSpec((rb, 1), lambda r: (r, 0)),
            pl.BlockSpec((1, seq), lambda r: (0, 0)),
        ],
        out_specs=pl.BlockSpec((rb // SUBLANES, 2 * seq // LANES, SUBLANES, LANES), lambda r: (r, 0, 0, 0)),
        out_shape=jax.ShapeDtypeStruct((nrows // SUBLANES, 2 * seq // LANES, SUBLANES, LANES), F32),
        compiler_params=_params(("parallel",)),
        name="hyena_filter_time",
    )(z2t, w3f, w3b, delta_rows, t_f)


def _filt_fft_body(k_ref, f1_ref, f2_ref, tr_ref, ti_ref, kr_ref, ki_ref):
    tr = tr_ref[...]
    ti = ti_ref[...]
    for g in range(k_ref.shape[0]):
        x3 = jnp.swapaxes(k_ref[g], 0, 1)
        xr, xi = _fft_fwd(x3, f1_ref[...], f2_ref[...], tr, ti)
        kr_ref[SUBLANES * g:SUBLANES * (g + 1)] = xr.reshape(SUBLANES, FFT_R, FFT_R)
        ki_ref[SUBLANES * g:SUBLANES * (g + 1)] = xi.reshape(SUBLANES, FFT_R, FFT_R)


def _filt_fft(ktile, tables):
    n8, r, _, _ = ktile.shape
    nrows = n8 * SUBLANES
    rf = RF_KF
    out_spec = pl.BlockSpec((rf, r, r), lambda i: (i, 0, 0))
    consts = [tables[k] for k in ("f1", "f2", "tr16", "ti16")]
    return pl.pallas_call(
        _filt_fft_body,
        grid=(nrows // rf,),
        in_specs=[pl.BlockSpec((rf // SUBLANES, r, SUBLANES, LANES), lambda i: (i, 0, 0, 0))] + _const_specs(consts, 1),
        out_specs=[out_spec, out_spec],
        out_shape=[jax.ShapeDtypeStruct((nrows, r, r), BF16)] * 2,
        compiler_params=_params(("parallel",)),
        name="hyena_filter_fft",
    )(ktile, *consts)


def _conv3_time(x, w):
    lane = lax.broadcasted_iota(I32, x.shape, 2)
    zero = jnp.zeros_like(x[:1])
    xm = pltpu.roll(x, 1, axis=2)
    xm = jnp.where(lane == 0, jnp.concatenate([zero, xm[:-1]], axis=0), xm)
    xp = pltpu.roll(x, LANES - 1, axis=2)
    xp = jnp.where(lane == LANES - 1, jnp.concatenate([xp[1:], zero], axis=0), xp)
    return w[0] * xm + w[1] * x + w[2] * xp + w[3]


def _hyena_body(sk_ref, v_ref, x1_ref, x2_ref, wv_ref, w1_ref, w2_ref, k0r_ref, k0i_ref, k1r_ref, k1i_ref,
                f1p_ref, f2_ref, f2c_ref, f3p_ref, tr_ref, ti_ref, o_ref, *, rc):
    ct = pl.program_id(0)
    tr = tr_ref[...]
    ti = ti_ref[...]
    np_ = v_ref.shape[2]
    nb = v_ref.shape[0]

    def load(ref, w_ref, g):
        parts = [jnp.swapaxes(_conv3_time(ref[bb, g], w_ref[:, g]), 0, 1) for bb in range(nb)]
        return jnp.concatenate(parts, axis=1)

    def long_conv(z3, kr_ref, ki_ref, g):
        r = FFT_R
        gsub = SUBLANES // HY_SPLIT
        wd = tr.dtype

        def split(m):
            return m[:, :r].reshape(gsub, r, r), m[:, r:].reshape(gsub, r, r)

        def join_t(re, im):
            return jnp.concatenate([jnp.swapaxes(re, 1, 2), jnp.swapaxes(im, 1, 2)], axis=-1).reshape(
                gsub * r, 2 * r).astype(BF16)

        subs = range(HY_SPLIT)
        zs = [z3[s * gsub:(s + 1) * gsub] for s in subs]
        a = [_dot(jnp.swapaxes(z.astype(wd), 1, 2).reshape(gsub * r, r).astype(BF16), f1p_ref[...], wd) for z in zs]
        p = []
        for m in a:
            ar, ai = split(m)
            p.append(join_t(ar * tr - ai * ti, ar * ti + ai * tr))
        x = [_dot(m, f2_ref[...], wd) for m in p]
        y = []
        for s, m in zip(subs, x):
            rows = pl.ds(g * SUBLANES + s * gsub, gsub)
            kr = kr_ref[rows].reshape(gsub * r, r)
            ki = ki_ref[rows].reshape(gsub * r, r)
            xr, xi = m[:, :r], m[:, r:]
            y.append(jnp.concatenate([xr * kr - xi * ki, xr * ki + xi * kr], axis=-1).astype(BF16))
        b = [_dot(m, f2c_ref[...], wd) for m in y]
        c = []
        for m in b:
            br, bi = split(m)
            c.append(join_t(br * tr + bi * ti, bi * tr - br * ti))
        yt = [_dot(m, f3p_ref[...]) for m in c]
        return jnp.concatenate([jnp.swapaxes(m.reshape(gsub, r, r), 1, 2) for m in yt], axis=0)

    def group(g, carry):
        cbase = ct * rc + g * SUBLANES

        def skip(z3, order):
            return jnp.stack([sk_ref[order, cbase + ci] * z3[ci] for ci in range(SUBLANES)], axis=0)

        v3 = load(v_ref, wv_ref, g)
        z1 = load(x1_ref, w1_ref, g) * (long_conv(v3, k0r_ref, k0i_ref, g) + skip(v3, 0))
        out = load(x2_ref, w2_ref, g) * (long_conv(z1, k1r_ref, k1i_ref, g) + skip(z1, 1))
        for bb in range(nb):
            o_ref[bb, g] = jnp.swapaxes(out[:, bb * np_:(bb + 1) * np_, :], 0, 1)
        return carry

    lax.fori_loop(0, rc // SUBLANES, group, 0)


def _hyena(hy5, cwt, skip, kfr, kfi, tables):
    bsz, n8, np_, _, _ = hy5.shape
    assert bsz % 2 == 0 and 2 * np_ == FFT_R
    rc = RC_HY
    r8 = rc // SUBLANES
    nct = D_HY // rc
    r = FFT_R

    def xspec(off):
        return pl.BlockSpec((2, r8, np_, SUBLANES, LANES), lambda c, b: (b, c + off * nct, 0, 0, 0))

    def wspec(off):
        return pl.BlockSpec((HY_CONV + 1, r8, SUBLANES, LANES), lambda c, b: (0, c + off * nct, 0, 0))

    def kspec(order):
        return pl.BlockSpec((rc, r, r), lambda c, b: (c + order * nct, 0, 0))

    consts = [tables[k] for k in ("f1p", "f2", "f2c", "f3p", "tr16", "ti16")]
    return pl.pallas_call(
        functools.partial(_hyena_body, rc=rc),
        grid=(nct, bsz // 2),
        in_specs=[pl.BlockSpec(memory_space=pltpu.SMEM), xspec(0), xspec(1), xspec(2), wspec(0), wspec(1), wspec(2),
                  kspec(0), kspec(0), kspec(1), kspec(1)] + _const_specs(consts, 2),
        out_specs=pl.BlockSpec((2, r8, np_, SUBLANES, LANES), lambda c, b: (b, c, 0, 0, 0)),
        out_shape=jax.ShapeDtypeStruct((bsz, D_HY // SUBLANES, np_, SUBLANES, LANES), F32),
        compiler_params=_params(("parallel", "parallel")),
        name="hyena",
    )(skip, hy5, hy5, hy5, cwt, cwt, cwt, kfr, kfi, kfr, kfi, *consts)


def _gelu_tanh(x):
    return 0.5 * x * (1.0 + jnp.tanh(math.sqrt(2.0 / math.pi) * (x + 0.044715 * (x * x * x))))


def _mix_body(hf_ref, hb_ref, gate_ref, yhy_ref, x_ref, gl_ref, gh_ref, wl_ref, wh_ref, gffn_ref,
              wrh_ref, wrl_ref, rb_ref, tri_ref, ltri_ref, xmid_ref, h_ref, routec_ref, routet_ref, cnt_ref):
    tm = x_ref.shape[1]
    y_lru = (hf_ref[0, 0] + hb_ref[0, 0]) * _gelu_tanh(gate_ref[0])
    nl = _rms(y_lru, gl_ref[...]).astype(BF16)
    gh = gh_ref[...]
    nts = []
    for j in range(tm // LANES):
        yt = yhy_ref[0, :, j, :, :].reshape(D_HY, LANES)
        nts.append((yt * lax.rsqrt(jnp.mean(yt * yt, axis=0, keepdims=True) + EPS) * gh).astype(BF16))
    nt = jnp.concatenate(nts, axis=1)
    mix = _dot(nl, wl_ref[...]) + lax.dot_general(nt, wh_ref[...], (((0,), (0,)), ((), ())),
                                                   preferred_element_type=F32)
    xmid_ref[0] = x_ref[0] + mix
    h = _rms(xmid_ref[0], gffn_ref[...])
    h_ref[...] = h.astype(BF16)

    hh, hl = _split_bf16(h)
    logits = _dot(hh, wrh_ref[...]) + _dot(hl, wrh_ref[...]) + _dot(hh, wrl_ref[...]) + rb_ref[...]
    lt = logits.T[:ROUTE_ROWS]
    row = lax.broadcasted_iota(I32, lt.shape, 0)
    big = jnp.int32(1 << 20)
    is_g = row < N_GROUPS
    lg = jnp.where(is_g, lt, NEG_BIG)
    mg = jnp.max(lg, axis=0, keepdims=True)
    gidx = jnp.min(jnp.where(lg == mg, row, big), axis=0, keepdims=True)
    g_p = 1.0 / jnp.sum(jnp.where(is_g, jnp.exp(lg - mg), 0.0), axis=0, keepdims=True)
    e_lo = N_GROUPS + EXPERTS_PER_GROUP * gidx
    is_e = jnp.logical_and(row >= e_lo, row < e_lo + EXPERTS_PER_GROUP)
    le = jnp.where(is_e, lt, NEG_BIG)
    m1 = jnp.max(le, axis=0, keepdims=True)
    i1 = jnp.min(jnp.where(le == m1, row, big), axis=0, keepdims=True)
    le2 = jnp.where(row == i1, NEG_BIG, le)
    m2 = jnp.max(le2, axis=0, keepdims=True)
    i2 = jnp.min(jnp.where(le2 == m2, row, big), axis=0, keepdims=True)
    ratio = jnp.exp(m2 - m1)
    gate1 = g_p / (1.0 + ratio)
    gate2 = g_p * ratio / (1.0 + ratio)
    oh1 = row == i1
    oh2 = row == i2
    oh = jnp.where(oh1, 1.0, jnp.where(oh2, 1.0, 0.0))
    before = _dot(oh.astype(BF16), tri_ref[...])
    cnt = jnp.sum(oh, axis=1, keepdims=True)
    nchunk = jnp.floor((cnt + (ROW_CHUNK - 1)) * (1.0 / ROW_CHUNK))
    run_start = ROW_CHUNK * _dot(ltri_ref[...], jnp.broadcast_to(nchunk, (ROUTE_ROWS, LANES)).astype(BF16))[:, 0:1]
    pos = run_start + before
    l0 = jnp.sum(jnp.where(oh1, pos, 0.0), axis=0, keepdims=True)
    l1 = jnp.sum(jnp.where(oh2, pos, 0.0), axis=0, keepdims=True)
    cnt_ref[0] = cnt
    route_t = jnp.concatenate([l0, l1, gate1, gate2, jnp.zeros((SUBLANES - 4, tm), F32)], axis=0)
    routet_ref[0] = route_t
    routec_ref[...] = jnp.concatenate([route_t, jnp.zeros((LANES - SUBLANES, tm), F32)], axis=0).T[:, :SUBLANES]


def _mix_router(h4, lru, yhy5, x, gl, gh, wl, wh, gffn, wrh, wrl, rb, tri, ltri):
    bsz, seq, d = x.shape
    tm = TM_MIX
    nt = seq // tm
    n_tok = bsz * seq
    c2 = lambda b, i: (0, 0)
    return pl.pallas_call(
        _mix_body,
        grid=(bsz, nt),
        in_specs=[
            pl.BlockSpec((1, 1, tm, D_LRU), lambda b, i: (0, b, i, 0)),
            pl.BlockSpec((1, 1, tm, D_LRU), lambda b, i: (1, b, i, 0)),
            pl.BlockSpec((1, tm, D_LRU), lambda b, i: (b, i, 1)),
            pl.BlockSpec((1, D_HY // SUBLANES, tm // LANES, SUBLANES, LANES), lambda b, i: (b, 0, i, 0, 0)),
            pl.BlockSpec((1, tm, d), lambda b, i: (b, i, 0)),
            pl.BlockSpec((1, D_LRU), c2),
            pl.BlockSpec((D_HY, 1), c2),
            pl.BlockSpec((D_LRU, d), c2),
            pl.BlockSpec((D_HY, d), c2),
            pl.BlockSpec((1, d), c2),
            pl.BlockSpec((d, LANES), c2),
            pl.BlockSpec((d, LANES), c2),
            pl.BlockSpec((1, LANES), c2),
            pl.BlockSpec((tm, tm), c2),
            pl.BlockSpec((ROUTE_ROWS, ROUTE_ROWS), c2),
        ],
        out_specs=[
            pl.BlockSpec((1, tm, d), lambda b, i: (b, i, 0)),
            pl.BlockSpec((tm, d), lambda b, i: (b * nt + i, 0)),
            pl.BlockSpec((tm, SUBLANES), lambda b, i: (b * nt + i, 0)),
            pl.BlockSpec((1, SUBLANES, tm), lambda b, i: (b * nt + i, 0, 0)),
            pl.BlockSpec((1, ROUTE_ROWS, 1), lambda b, i: (b * nt + i, 0, 0)),
        ],
        out_shape=[
            jax.ShapeDtypeStruct((bsz, seq, d), F32),
            jax.ShapeDtypeStruct((n_tok, d), BF16),
            jax.ShapeDtypeStruct((n_tok, SUBLANES), F32),
            jax.ShapeDtypeStruct((n_tok // tm, SUBLANES, tm), F32),
            jax.ShapeDtypeStruct((n_tok // tm, ROUTE_ROWS, 1), F32),
        ],
        compiler_params=_params(("parallel", "parallel")),
        name="mix_router",
    )(h4, h4, lru, yhy5, x, gl, gh, wl, wh, gffn, wrh, wrl, rb, tri, ltri)


def _start_chunks(i, tch_ref, dst_ref, make_copy):
    n = tch_ref[i]

    def start(k, priority):
        make_copy(k * ROW_CHUNK, dst_ref[i * MAX_CHUNKS + k], ROW_CHUNK).start(priority=priority)

    def per_pair(k2, c):
        start(2 * k2, 0)
        start(2 * k2 + 1, 1)
        return c

    lax.fori_loop(0, n // 2, per_pair, 0)

    @pl.when(n % 2 == 1)
    def _():
        start(n - 1, 0)


def _wait_chunks(count, make_copy):
    nbits = MAX_CHUNKS.bit_length()
    for j in reversed(range(nbits)):
        @pl.when((count >> j) & 1 == 1)
        def _():
            make_copy(0, 0, ROW_CHUNK << j).wait()


def _dispatch_body(dst_ref, tch_ref, tn_ref, ts_ref, nu_ref, h_ref, routet_ref, xb_ref,
                   buf, zbuf, sem, zsem):
    i = pl.program_id(0)
    n_tiles = pl.num_programs(0)
    tm = h_ref.shape[0]
    lbuf = buf.shape[1]
    n_blocks = xb_ref.shape[0] // MOE_BLK
    slot = i % 2

    def aligned(r):
        return pl.ds(pl.multiple_of(r, ROW_CHUNK), ROW_CHUNK)

    def zero_chunk(r):
        return pltpu.make_async_copy(zbuf.at[pl.ds(0, ROW_CHUNK)], xb_ref.at[aligned(r)], zsem)

    def zero_block(j):
        return pltpu.make_async_copy(zbuf, xb_ref.at[pl.ds(pl.multiple_of(j * MOE_BLK, MOE_BLK), MOE_BLK)], zsem)

    def for_each_zero_copy(act):
        def per_expert(e, c):
            def per_chunk(k, c2):
                act(zero_chunk(ts_ref[e] + k * ROW_CHUNK))
                return c2

            lax.fori_loop(0, tn_ref[e], per_chunk, 0)
            return c

        lax.fori_loop(0, N_EXPERTS, per_expert, 0)

        def per_block(j, c):
            act(zero_block(j))
            return c

        lax.fori_loop(nu_ref[0], n_blocks, per_block, 0)

    @pl.when(i == 0)
    def _():
        zbuf[...] = jnp.zeros_like(zbuf)
        for_each_zero_copy(lambda cp: cp.start())

    l0t = routet_ref[0, 0:1, :].astype(I32)
    l1t = routet_ref[0, 1:2, :].astype(I32)
    row = lax.broadcasted_iota(I32, (lbuf, tm), 0)
    perm = jnp.where(row == l0t, 1.0, jnp.where(row == l1t, 1.0, 0.0)).astype(BF16)
    srt = _dot(perm, h_ref[...], BF16)

    def copy_from(s):
        def make_copy(src_row, dst_row, rows):
            return pltpu.make_async_copy(buf.at[s, pl.ds(pl.multiple_of(src_row, ROW_CHUNK), rows)],
                                         xb_ref.at[pl.ds(pl.multiple_of(dst_row, ROW_CHUNK), rows)], sem.at[s])
        return make_copy

    @pl.when(i >= 2)
    def _():
        _wait_chunks(tch_ref[i - 2], copy_from(slot))

    buf[slot] = srt
    _start_chunks(i, tch_ref, dst_ref, copy_from(slot))

    @pl.when(i == n_tiles - 1)
    def _():
        @pl.when(i >= 1)
        def _():
            _wait_chunks(tch_ref[i - 1], copy_from(1 - slot))

        _wait_chunks(tch_ref[i], copy_from(slot))
        for_each_zero_copy(lambda cp: cp.wait())


def _dispatch(chunk_dst, tile_nch, tail_n, tail_s, n_used, h, route_t, n_slots):
    n_tok, d = h.shape
    tm = TM_MIX
    grid_spec = pltpu.PrefetchScalarGridSpec(
        num_scalar_prefetch=5,
        grid=(n_tok // tm,),
        in_specs=[
            pl.BlockSpec((tm, d), lambda i, *_: (i, 0)),
            pl.BlockSpec((1, SUBLANES, tm), lambda i, *_: (i, 0, 0)),
        ],
        out_specs=pl.BlockSpec(memory_space=pl.ANY),
        scratch_shapes=[pltpu.VMEM((2, LOCAL_ROWS, d), BF16), pltpu.VMEM((MOE_BLK, d), BF16),
                        pltpu.SemaphoreType.DMA((2,)), pltpu.SemaphoreType.DMA(())],
    )
    return pl.pallas_call(
        _dispatch_body,
        grid_spec=grid_spec,
        out_shape=jax.ShapeDtypeStruct((n_slots, d), BF16),
        compiler_params=_params(("arbitrary",)),
        name="moe_dispatch",
    )(chunk_dst, tile_nch, tail_n, tail_s, n_used, h, route_t)


def _expert_body(be_ref, fl_ref, sw_ref, nu_ref, xb_ref, w1_ref, w3_ref, w2_ref, yb_ref,
                 w1_sc, w3_sc, w2_sc, w1_ring, w3_ring, w2_ring, sem):
    j = pl.program_id(0)
    n_used = nu_ref[0]
    used = j < n_used

    def weight_copies(jj):
        e = be_ref[jj]
        s = sw_ref[jj] % WEIGHT_SLOTS
        return [pltpu.make_async_copy(w_ref.at[e], ring.at[s], sem.at[s])
                for w_ref, ring in ((w1_ref, w1_ring), (w3_ref, w3_ring), (w2_ref, w2_ring))]

    def fetch_if_first(jj):
        @pl.when(jnp.logical_and(jj < n_used, fl_ref[jnp.minimum(jj, pl.num_programs(0) - 1)] == 1))
        def _():
            for cp in weight_copies(jj):
                cp.start()

    @pl.when(j == 0)
    def _():
        for jj in range(WEIGHT_LOOKAHEAD):
            fetch_if_first(jnp.int32(jj))

    fetch_if_first(j + WEIGHT_LOOKAHEAD)

    @pl.when(jnp.logical_and(used, fl_ref[j] == 1))
    def _():
        for cp in weight_copies(j):
            cp.wait()
        s = sw_ref[j] % WEIGHT_SLOTS
        w1_sc[...] = w1_ring[s].astype(BF16)
        w3_sc[...] = w3_ring[s].astype(BF16)
        w2_sc[...] = w2_ring[s].astype(BF16)

    @pl.when(used)
    def _():
        xb = xb_ref[...]
        a = _dot(xb, w1_sc[...])
        b = _dot(xb, w3_sc[...])
        act = (a * _sigmoid(a) * b).astype(BF16)
        yb_ref[...] = _dot(act, w2_sc[...], BF16)

    @pl.when(jnp.logical_not(used))
    def _():
        yb_ref[...] = jnp.zeros_like(yb_ref)


def _experts(block_e, first_flag, n_used, xb, w1, w3, w2):
    n_slots, d = xb.shape
    blk = MOE_BLK
    de = w1.shape[2]
    switch_no = jnp.cumsum(first_flag) - 1
    hbm = pl.BlockSpec(memory_space=pl.ANY)
    grid_spec = pltpu.PrefetchScalarGridSpec(
        num_scalar_prefetch=4,
        grid=(n_slots // blk,),
        in_specs=[pl.BlockSpec((blk, d), lambda j, be, fl, sw, nu: (jnp.minimum(j, nu[0] - 1), 0)), hbm, hbm, hbm],
        out_specs=pl.BlockSpec((blk, d), lambda j, *_: (j, 0)),
        scratch_shapes=[pltpu.VMEM((d, de), BF16), pltpu.VMEM((d, de), BF16), pltpu.VMEM((de, d), BF16),
                        pltpu.VMEM((WEIGHT_SLOTS, d, de), F32), pltpu.VMEM((WEIGHT_SLOTS, d, de), F32),
                        pltpu.VMEM((WEIGHT_SLOTS, de, d), F32), pltpu.SemaphoreType.DMA((WEIGHT_SLOTS,))],
    )
    return pl.pallas_call(
        _expert_body,
        grid_spec=grid_spec,
        out_shape=jax.ShapeDtypeStruct((n_slots, d), BF16),
        compiler_params=_params(("arbitrary",)),
        name="moe_experts",
    )(block_e, first_flag, switch_no.astype(I32), n_used, xb, w1, w3, w2)


def _combine_body(dst_ref, tch_ref, xmid_ref, route_ref, g_ref, yb_ref, o_ref, buf, sem):
    i = pl.program_id(0)
    n_tiles = pl.num_programs(0)
    tm = xmid_ref.shape[0]
    lbuf = buf.shape[1]
    slot = i % 2

    def copy_into(s):
        def make_copy(local_row, slot_row, rows):
            return pltpu.make_async_copy(yb_ref.at[pl.ds(pl.multiple_of(slot_row, ROW_CHUNK), rows)],
                                         buf.at[s, pl.ds(pl.multiple_of(local_row, ROW_CHUNK), rows)], sem.at[s])
        return make_copy

    @pl.when(i == 0)
    def _():
        buf[...] = jnp.zeros_like(buf)
        _start_chunks(i, tch_ref, dst_ref, copy_into(slot))

    @pl.when(i + 1 < n_tiles)
    def _():
        _start_chunks(i + 1, tch_ref, dst_ref, copy_into(1 - slot))

    route = route_ref[...]
    l0 = route[:, 0:1].astype(I32)
    l1 = route[:, 1:2].astype(I32)
    lane = lax.broadcasted_iota(I32, (tm, lbuf), 1)
    gmat = jnp.where(lane == l0, route[:, 2:3], jnp.where(lane == l1, route[:, 3:4], 0.0)).astype(BF16)
    _wait_chunks(tch_ref[i], copy_into(slot))
    y = xmid_ref[...] + _dot(gmat, buf[slot])
    o_ref[...] = _rms(y, g_ref[...])


def _combine(chunk_dst, tile_nch, xmid, route, g, yb):
    n_tok, d = xmid.shape
    tm = TM_MIX
    grid_spec = pltpu.PrefetchScalarGridSpec(
        num_scalar_prefetch=2,
        grid=(n_tok // tm,),
        in_specs=[
            pl.BlockSpec((tm, d), lambda i, *_: (i, 0)),
            pl.BlockSpec((tm, SUBLANES), lambda i, *_: (i, 0)),
            pl.BlockSpec((1, d), lambda i, *_: (0, 0)),
            pl.BlockSpec(memory_space=pl.ANY),
        ],
        out_specs=pl.BlockSpec((tm, d), lambda i, *_: (i, 0)),
        scratch_shapes=[pltpu.VMEM((2, LOCAL_ROWS, d), BF16), pltpu.SemaphoreType.DMA((2,))],
    )
    return pl.pallas_call(
        _combine_body,
        grid_spec=grid_spec,
        out_shape=jax.ShapeDtypeStruct((n_tok, d), F32),
        compiler_params=_params(("arbitrary",)),
        name="moe_combine",
    )(chunk_dst, tile_nch, xmid, route, g, yb)


def _filter_features(seq):
    pos = jnp.arange(seq, dtype=F32)
    t = jnp.linspace(0.0, 1.0, seq, dtype=F32)
    w = (2.0 * math.pi / seq) * pos
    bands = jnp.linspace(1e-4, HY_BANDS - 1, HY_BANDS, dtype=F32)
    ang = w[:, None] * bands[None, :]
    feats = jnp.concatenate([t[:, None], jnp.cos(ang), jnp.sin(ang)], axis=-1)
    featst = jnp.pad(feats, ((0, 0), (0, HY_EMB_PAD - HY_EMB))).T[None]
    return featst, t[None, :]


def _gate_weights(wr, wi):
    half = D_LRU // 2
    hph = half // LRU_HEAD_DIM
    eye = 0.5 * jnp.eye(hph, dtype=wr.dtype)[:, None, :, None]

    def block_diag(w):
        return (eye * w[:, :, None, :]).reshape(half, half)

    out = [jnp.concatenate([block_diag(wr[hh * hph:(hh + 1) * hph]), block_diag(wi[hh * hph:(hh + 1) * hph])], axis=1)
           for hh in range(2)]
    return jnp.stack(out, axis=0).astype(BF16)


def _layer(x, l, p):
    bsz, seq, d = x.shape
    n_tok = bsz * seq
    row = lambda a: a.reshape(1, -1).astype(F32)
    col = lambda a: a.reshape(-1, 1).astype(F32)

    w_in = p["w_in"][l]
    wl = w_in[:, :2 * D_LRU].astype(BF16)
    wht = w_in[:, 2 * D_LRU:].T.astype(BF16)
    lru, hy5 = _inproj(x, row(p["norm_mix_g"][l]), wl, wht)

    wg = jnp.stack([_gate_weights(p["lru_wr_f"][l], p["lru_wi_f"][l]),
                    _gate_weights(p["lru_wr_b"][l], p["lru_wi_b"][l])], axis=0)
    stack2 = lambda a, b: jnp.stack([a.reshape(1, -1), b.reshape(1, -1)], axis=0).astype(F32)
    h4 = _lru(lru, p["lru_conv_w"][l], row(p["lru_conv_b"][l]), wg,
              0.5 * stack2(p["lru_br_f"][l], p["lru_br_b"][l]), 0.5 * stack2(p["lru_bi_f"][l], p["lru_bi_b"][l]),
              stack2(p["lru_lambda_f"][l], p["lru_lambda_b"][l]))

    tables = _fft_tables()
    featst, t_f = _filter_features(seq)
    w1t = jnp.pad(p["hy_filt_w1"][l].T, ((0, 0), (0, HY_EMB_PAD - HY_EMB)))
    z2t = _filt_mlp(featst, w1t, col(p["hy_filt_b1"][l]), col(p["hy_filt_freq1"][l]),
                    p["hy_filt_w2"][l].T, col(p["hy_filt_b2"][l]), col(p["hy_filt_freq2"][l]))
    w3 = p["hy_filt_w3"][l]
    nfr = HY_ORDER * D_HY
    deltas = jnp.abs(jnp.linspace(math.log(HY_DECAY_TARGET) / HY_SLOW_DECAY_PCT,
                                  math.log(HY_DECAY_TARGET) / HY_FAST_DECAY_PCT, D_HY, dtype=F32))
    ktile = _filt_time(z2t, w3[:, :nfr].T, w3[:, nfr:].T, col(jnp.tile(deltas, HY_ORDER)), t_f)
    kfr, kfi = _filt_fft(ktile, tables)
    cwt = jnp.concatenate([p["hy_conv_w"][l], p["hy_conv_b"][l][None, :]], axis=0)
    cwt = jnp.broadcast_to(cwt.reshape(HY_CONV + 1, -1, SUBLANES, 1), (HY_CONV + 1, cwt.shape[1] // SUBLANES, SUBLANES, LANES))
    yhy5 = _hyena(hy5, cwt, p["hy_skip"][l], kfr, kfi, tables)

    w_out = p["w_out"][l]
    wr_cat = jnp.concatenate([p["router_group_w"][l], p["router_expert_w"][l]], axis=1)
    wr_cat = jnp.pad(wr_cat, ((0, 0), (0, LANES - wr_cat.shape[1])))
    wrh, wrl = _split_bf16(wr_cat)
    rb = jnp.concatenate([p["router_group_b"][l], p["router_expert_b"][l]])
    rb = jnp.pad(rb, (0, LANES - rb.shape[0])).reshape(1, LANES)
    earlier = lambda n: (jnp.arange(n)[:, None] < jnp.arange(n)[None, :]).astype(BF16)
    xmid, h, route, route_t, tile_cnt = _mix_router(
        h4, lru, yhy5, x, row(p["grp_norm_lru_g"][l]), col(p["grp_norm_hy_g"][l]),
        w_out[:D_LRU].astype(BF16), w_out[D_LRU:].astype(BF16), row(p["norm_ffn_g"][l]), wrh, wrl, rb,
        earlier(TM_MIX), earlier(ROUTE_ROWS).T)

    blk = MOE_BLK
    ch = ROW_CHUNK
    n_tiles = n_tok // TM_MIX
    n_blocks = (2 * n_tok + n_tiles * N_EXPERTS * (ch - 1)) // blk + N_EXPERTS
    cnt = tile_cnt[:, N_GROUPS:N_GROUPS + N_EXPERTS, 0].astype(I32)
    run = (cnt + ch - 1) // ch * ch
    tot_e = jnp.sum(run, axis=0)
    padded_e = (tot_e + blk - 1) // blk * blk
    pad_end = jnp.cumsum(padded_e)
    pad_start = pad_end - padded_e
    gstart = pad_start[None, :] + jnp.cumsum(run, axis=0) - run
    lstart = jnp.cumsum(run, axis=1) - run
    flat = lambda a: a.reshape(-1).astype(I32)
    tile_nch = flat(jnp.sum(run, axis=1) // ch)
    nch_end = jnp.cumsum(run // ch, axis=1)
    k = jnp.arange(MAX_CHUNKS, dtype=I32)
    base = gstart - lstart
    step = base[:, 1:] - base[:, :-1]
    passed = (k[None, :, None] >= nch_end[:, None, :-1]).astype(I32)
    chunk_dst = flat(base[:, :1] + jnp.sum(passed * step[:, None, :], axis=2) + k[None, :] * ch)
    tail_n = flat((padded_e - tot_e) // ch)
    tail_s = flat(pad_start + tot_e)
    block_start = jnp.arange(n_blocks, dtype=I32) * blk
    block_e = jnp.minimum(jnp.sum(block_start[:, None] >= pad_end[None, :], axis=1), N_EXPERTS - 1).astype(I32)
    first_flag = jnp.concatenate([jnp.ones((1,), I32), (block_e[1:] != block_e[:-1]).astype(I32)])
    n_used = (pad_end[-1:] // blk).astype(I32)

    xb = _dispatch(chunk_dst, tile_nch, tail_n, tail_s, n_used, h, route_t, n_blocks * blk)
    yb = _experts(block_e, first_flag, n_used, xb, p["exp_w1"][l], p["exp_w3"][l], p["exp_w2"][l])
    return xmid.reshape(n_tok, d), route, (chunk_dst, tile_nch), yb


def kernel(x, norm_mix_g, w_in, lru_conv_w, lru_conv_b, lru_wr_f, lru_br_f, lru_wi_f, lru_bi_f, lru_lambda_f, lru_wr_b, lru_br_b, lru_wi_b, lru_bi_b, lru_lambda_b, hy_conv_w, hy_conv_b, hy_filt_w1, hy_filt_b1, hy_filt_freq1, hy_filt_w2, hy_filt_b2, hy_filt_freq2, hy_filt_w3, hy_skip, grp_norm_lru_g, grp_norm_hy_g, w_out, norm_ffn_g, router_group_w, router_group_b, router_expert_w, router_expert_b, exp_w1, exp_w3, exp_w2, norm_final_g):
    p = dict(locals())
    bsz, seq, d = x.shape
    assert d == D_MODEL and 2 * seq == FFT_R * FFT_R and w_in.shape[0] == 1
    xmid, route, (chunk_dst, tile_nch), yb = _layer(x, 0, p)
    out = _combine(chunk_dst, tile_nch, xmid, route, norm_final_g.reshape(1, d), yb)
    return out.reshape(bsz, seq, d)
```

```python
import functools
import math

import numpy as np
import jax
import jax.numpy as jnp
from jax import lax
from jax.experimental import pallas as pl
from jax.experimental.pallas import tpu as pltpu

F32 = jnp.float32
BF16 = jnp.bfloat16
I32 = jnp.int32

D_MODEL = 1024
D_LRU = 512
D_HY = 512
LRU_HEADS = 8
LRU_HEAD_DIM = D_LRU // LRU_HEADS
LRU_CONV = 4
LRU_C = 8.0
HY_ORDER = 2
HY_CONV = 3
HY_BANDS = 16
HY_EMB = 2 * HY_BANDS + 1
HY_FFN = 64
HY_FAST_DECAY_PCT = 0.3
HY_SLOW_DECAY_PCT = 1.5
HY_DECAY_TARGET = 1e-2
N_GROUPS = 4
EXPERTS_PER_GROUP = 8
N_EXPERTS = N_GROUPS * EXPERTS_PER_GROUP
D_EXPERT = D_MODEL // 2
EPS = 1e-6

LANES = 128
SUBLANES = 8
V7X_VMEM_BYTES = 64 * 1024 * 1024
VMEM_LIMIT = V7X_VMEM_BYTES * 7 // 8
FFT_R = LANES
HY_EMB_PAD = -(-HY_EMB // SUBLANES) * SUBLANES

TM_IN = 1024
TC_LRU = 1024
RC_HY = 32
RF_KF = 128
HY_SPLIT = 2
RB_FILT = 128
TM_MIX = 512
MOE_BLK = 512
WEIGHT_LOOKAHEAD = 2
WEIGHT_SLOTS = WEIGHT_LOOKAHEAD + 1
ROW_CHUNK = 16
LOCAL_ROWS = -(-(2 * TM_MIX + N_EXPERTS * (ROW_CHUNK - 1)) // LANES) * LANES
MAX_CHUNKS = LOCAL_ROWS // ROW_CHUNK
ROUTE_ROWS = -(-(N_GROUPS + N_EXPERTS) // SUBLANES) * SUBLANES
NEG_BIG = -1e30
NO_INDEX = 1 << 20
TINY = 1e-30


def _params(sem, vmem=VMEM_LIMIT):
    return pltpu.CompilerParams(dimension_semantics=sem, vmem_limit_bytes=vmem)


def _rms(x, g):
    return x * lax.rsqrt(jnp.mean(x * x, axis=-1, keepdims=True) + EPS) * g


def _sigmoid(x):
    return 1.0 / (1.0 + jnp.exp(-x))


def _split_bf16(a):
    hi = a.astype(BF16)
    lo = (a - hi.astype(F32)).astype(BF16)
    return hi, lo


def _dot(a, b, out=F32):
    return jnp.dot(a, b, preferred_element_type=F32).astype(out)


def _dot3(a, b):
    ah, al = _split_bf16(a)
    bh, bl = _split_bf16(b)
    return _dot(ah, bh) + _dot(al, bh) + _dot(ah, bl)


def _inproj_body(x_ref, g_ref, wl_ref, wht_ref, lru_ref, hy_ref):
    hn = _rms(x_ref[0], g_ref[...]).astype(BF16)
    lru_ref[0] = _dot(hn, wl_ref[...])
    hyt = lax.dot_general(wht_ref[...], hn, (((1,), (1,)), ((), ())), preferred_element_type=F32)
    nrow = hyt.shape[0] // SUBLANES
    for j in range(hyt.shape[1] // LANES):
        hy_ref[0, :, j, :, :] = hyt[:, LANES * j:LANES * (j + 1)].reshape(nrow, SUBLANES, LANES)


def _inproj(x, g, wl, wht):
    bsz, seq, d = x.shape
    nl = wl.shape[1]
    nh = wht.shape[0]
    tm = TM_IN
    return pl.pallas_call(
        _inproj_body,
        grid=(bsz, seq // tm),
        in_specs=[
            pl.BlockSpec((1, tm, d), lambda b, i: (b, i, 0)),
            pl.BlockSpec((1, d), lambda b, i: (0, 0)),
            pl.BlockSpec((d, nl), lambda b, i: (0, 0)),
            pl.BlockSpec((nh, d), lambda b, i: (0, 0)),
        ],
        out_specs=[
            pl.BlockSpec((1, tm, nl), lambda b, i: (b, i, 0)),
            pl.BlockSpec((1, nh // SUBLANES, tm // LANES, SUBLANES, LANES), lambda b, i: (b, 0, i, 0, 0)),
        ],
        out_shape=[
            jax.ShapeDtypeStruct((bsz, seq, nl), F32),
            jax.ShapeDtypeStruct((bsz, nh // SUBLANES, seq // LANES, SUBLANES, LANES), F32),
        ],
        compiler_params=_params(("parallel", "parallel")),
        name="inproj",
    )(x, g, wl, wht)


def _sqrt_one_minus_sq(a, neg_log_a):
    om = jnp.maximum(jnp.tanh(neg_log_a) * (1.0 + a * a), TINY)
    return om * lax.rsqrt(om)


def _group_scan(a3, b3, reverse):
    sub = lax.broadcasted_iota(I32, a3.shape, 1)
    for s in (1, 2, 4):
        if reverse:
            a_sh = pltpu.roll(a3, SUBLANES - s, axis=1)
            b_sh = pltpu.roll(b3, SUBLANES - s, axis=1)
            m = sub < SUBLANES - s
        else:
            a_sh = pltpu.roll(a3, s, axis=1)
            b_sh = pltpu.roll(b3, s, axis=1)
            m = sub >= s
        b3 = jnp.where(m, a3 * b_sh + b3, b3)
        a3 = jnp.where(m, a3 * a_sh, a3)
    return a3, b3


def _lru_body(cur_ref, prev_ref, next_ref, cw_ref, cb_ref, wg_ref, br_ref, bi_ref, lam_ref, o_ref,
              carry_sc, *, nt, tc):
    d = pl.program_id(1)
    i = pl.program_id(2)
    c = jnp.where(d == 0, i, nt - 1 - i)
    ns = SUBLANES
    ng = tc // ns
    half = D_LRU // 2

    @pl.when(i == 0)
    def _():
        carry_sc[...] = jnp.zeros_like(carry_sc)

    xs = jnp.swapaxes(cur_ref[0].reshape(ns, ng, D_LRU), 0, 1)
    prev = jnp.where(c > 0, prev_ref[0], 0.0)
    nxt = jnp.where(c < nt - 1, next_ref[0], 0.0)
    sub = lax.broadcasted_iota(I32, (ns, D_LRU), 0)

    def prev_segment(slab, halo_row):
        return jnp.where(sub == 0, halo_row, pltpu.roll(slab, 1, axis=0))

    def next_segment(slab, halo_row):
        return jnp.where(sub == ns - 1, halo_row, pltpu.roll(slab, ns - 1, axis=0))

    xext = jnp.concatenate([prev_segment(xs[ng - 2], prev[ns - 2:ns - 1])[None],
                            prev_segment(xs[ng - 1], prev[ns - 1:ns])[None], xs,
                            next_segment(xs[0], nxt[0:1])[None]], axis=0)
    cw = cw_ref[...]
    xc = cb_ref[...] + cw[0:1] * xext[0:ng]
    for k in range(1, LRU_CONV):
        xc = xc + cw[k:k + 1] * xext[k:k + ng]
    xc = xc.reshape(tc, D_LRU)

    lam = lam_ref[0]
    nlam = -lam
    softplus = jnp.maximum(nlam, 0.0) + jnp.log1p(jnp.exp(-jnp.abs(nlam)))
    half_c = (0.5 * LRU_C) * softplus

    def gates(hh):
        sl = slice(half * hh, half * (hh + 1))
        xh = xc[:, sl]
        logits = _dot(xh.astype(BF16), wg_ref[0, hh])
        tr_ = jnp.tanh(logits[:, :half] + br_ref[0][:, sl])
        gi = 0.5 + 0.5 * jnp.tanh(logits[:, half:] + bi_ref[0][:, sl])
        neg_log_a = half_c[:, sl] * (1.0 + tr_)
        a = jnp.exp(-neg_log_a)
        b = _sqrt_one_minus_sq(a, neg_log_a) * (gi * xh)
        return a.reshape(ng, ns, half), b.reshape(ng, ns, half)

    def run(reverse):
        subh = lax.broadcasted_iota(I32, (ns, half), 0)
        for hh in range(2):
            sl = slice(half * hh, half * (hh + 1))
            a3, b3 = gates(hh)
            h = jnp.zeros((ns, half), F32)
            p = jnp.ones((ns, half), F32)
            hs = [None] * ng
            ps = [None] * ng
            for g in (range(ng - 1, -1, -1) if reverse else range(ng)):
                h = a3[g] * h + b3[g]
                p = a3[g] * p
                hs[g] = h
                ps[g] = p
            ac, hc = _group_scan(p[None], h[None], reverse)
            c_in = carry_sc[:, sl]
            end = ac[0] * c_in + hc[0]
            if reverse:
                seg_in = jnp.where(subh == ns - 1, c_in, pltpu.roll(end, ns - 1, axis=0))
                carry_sc[:, sl] = jnp.broadcast_to(end[0:1], end.shape)
            else:
                seg_in = jnp.where(subh == 0, c_in, pltpu.roll(end, 1, axis=0))
                carry_sc[:, sl] = jnp.broadcast_to(end[ns - 1:ns], end.shape)
            h3 = jnp.stack(hs, axis=0) + jnp.stack(ps, axis=0) * seg_in[None]
            o_ref[0, 0, :, sl] = jnp.swapaxes(h3, 0, 1).reshape(tc, half)

    @pl.when(d == 0)
    def _():
        run(False)

    @pl.when(d == 1)
    def _():
        run(True)


def _lru(lru, cw, cb, wg, br, bi, lam):
    bsz, seq, _ = lru.shape
    tc = TC_LRU
    nt = seq // tc
    r8 = tc // SUBLANES
    nrow8 = seq // SUBLANES

    def cidx(d, i):
        return jnp.where(d == 0, i, nt - 1 - i)

    return pl.pallas_call(
        functools.partial(_lru_body, nt=nt, tc=tc),
        grid=(bsz, 2, nt),
        in_specs=[
            pl.BlockSpec((1, tc, D_LRU), lambda b, d, i: (b, cidx(d, i), 0)),
            pl.BlockSpec((1, SUBLANES, D_LRU), lambda b, d, i: (b, jnp.maximum(cidx(d, i) * r8 - 1, 0), 0)),
            pl.BlockSpec((1, SUBLANES, D_LRU), lambda b, d, i: (b, jnp.minimum((cidx(d, i) + 1) * r8, nrow8 - 1), 0)),
            pl.BlockSpec((LRU_CONV, D_LRU), lambda b, d, i: (0, 0)),
            pl.BlockSpec((1, D_LRU), lambda b, d, i: (0, 0)),
            pl.BlockSpec((1, 2, D_LRU // 2, D_LRU), lambda b, d, i: (d, 0, 0, 0)),
            pl.BlockSpec((1, 1, D_LRU), lambda b, d, i: (d, 0, 0)),
            pl.BlockSpec((1, 1, D_LRU), lambda b, d, i: (d, 0, 0)),
            pl.BlockSpec((1, 1, D_LRU), lambda b, d, i: (d, 0, 0)),
        ],
        out_specs=pl.BlockSpec((1, 1, tc, D_LRU), lambda b, d, i: (d, b, cidx(d, i), 0)),
        out_shape=jax.ShapeDtypeStruct((2, bsz, seq, D_LRU), F32),
        scratch_shapes=[pltpu.VMEM((SUBLANES, D_LRU), F32)],
        compiler_params=_params(("parallel", "arbitrary", "arbitrary")),
        name="rg_lru",
    )(lru, lru, lru, cw, cb, wg, br, bi, lam)


def _fft_tables():
    r = FFT_R
    n = r * r
    idx = np.arange(r, dtype=np.float64)
    ang = 2.0 * np.pi * np.outer(idx, idx) / r
    wr, wi = np.cos(ang), -np.sin(ang)
    angt = 2.0 * np.pi * np.outer(idx, idx) / n
    tr, ti = np.cos(angt), -np.sin(angt)
    h = r // 2
    f1 = np.concatenate([wr, wi], axis=1)
    f1p = np.block([[wr[:h], wi[:h]], [-wi[:h], wr[:h]]])
    f2 = np.block([[wr, wi], [-wi, wr]])
    f2c = np.block([[wr, -wi], [wi, wr]])
    f3p = np.block([[wr[:, :h], -wi[:, :h]], [wi[:, :h], wr[:, :h]]]) / n
    as_bf16 = lambda a: jnp.asarray(a, F32).astype(BF16)
    return dict(f1=as_bf16(f1), f1p=as_bf16(f1p), f2=as_bf16(f2), f2c=as_bf16(f2c), f3p=as_bf16(f3p),
                tr16=as_bf16(tr), ti16=as_bf16(ti))


def _const_specs(arrays, nargs):
    return [pl.BlockSpec(a.shape, (lambda nd: (lambda *_: (0,) * nd))(a.ndim)) for a in arrays]


def _fft_fwd(x3, f1, f2, tr, ti):
    g = x3.shape[0]
    r = FFT_R
    wd = tr.dtype
    xt = jnp.swapaxes(x3.astype(wd), 1, 2).reshape(g * r, r)
    a = _dot(xt.astype(BF16), f1, wd)
    ar = a[:, :r].reshape(g, r, r)
    ai = a[:, r:].reshape(g, r, r)
    pr = ar * tr - ai * ti
    pi = ar * ti + ai * tr
    pt = jnp.concatenate([jnp.swapaxes(pr, 1, 2), jnp.swapaxes(pi, 1, 2)], axis=-1).reshape(g * r, 2 * r)
    x = _dot(pt.astype(BF16), f2, BF16)
    return x[:, :r], x[:, r:]


def _filt_mlp_body(ft_ref, w1_ref, b1_ref, f1_ref, w2_ref, b2_ref, f2_ref, o_ref):
    z = jnp.sin(f1_ref[...] * (_dot3(w1_ref[...], ft_ref[0]) + b1_ref[...]))
    o_ref[0] = jnp.sin(f2_ref[...] * (_dot3(w2_ref[...], z) + b2_ref[...]))


def _filt_mlp(featst, w1t, b1, f1, w2t, b2, f2):
    nd, ke, seq = featst.shape
    col = lambda d: (0, 0)
    return pl.pallas_call(
        _filt_mlp_body,
        grid=(nd,),
        in_specs=[
            pl.BlockSpec((1, ke, seq), lambda d: (d, 0, 0)),
            pl.BlockSpec((HY_FFN, ke), col),
            pl.BlockSpec((HY_FFN, 1), col),
            pl.BlockSpec((HY_FFN, 1), col),
            pl.BlockSpec((HY_FFN, HY_FFN), col),
            pl.BlockSpec((HY_FFN, 1), col),
            pl.BlockSpec((HY_FFN, 1), col),
        ],
        out_specs=pl.BlockSpec((1, HY_FFN, seq), lambda d: (d, 0, 0)),
        out_shape=jax.ShapeDtypeStruct((nd, HY_FFN, seq), F32),
        compiler_params=_params(("parallel",)),
        name="hyena_filter_mlp",
    )(featst, w1t, b1, f1, w2t, b2, f2)


def _filt_time_body(z_ref, wf_ref, wb_ref, dl_ref, tf_ref, o_ref):
    seq = z_ref.shape[2]
    decay = jnp.exp(-tf_ref[...] * dl_ref[...])
    hf = _dot3(wf_ref[...], z_ref[0]) * decay
    hb = _dot3(wb_ref[...], z_ref[0]) * decay
    lane = lax.broadcasted_iota(I32, hb.shape, 1)
    hb = jnp.where(lane == 0, 0.0, hb)
    norm = jnp.sum(jnp.abs(hf), axis=-1, keepdims=True) + jnp.sum(jnp.abs(hb), axis=-1, keepdims=True) + EPS
    hf = hf / norm
    hb = hb / norm
    nrow = hf.shape[0] // SUBLANES
    nch = seq // LANES
    lane1 = lax.broadcasted_iota(I32, (hf.shape[0], LANES), 1)
    mirror = (LANES - lane1) % LANES
    for j in range(nch):
        o_ref[:, j, :, :] = hf[:, LANES * j:LANES * (j + 1)].reshape(nrow, SUBLANES, LANES)
        src = nch - 1 - j
        back = jnp.take_along_axis(hb[:, LANES * src:LANES * (src + 1)], mirror, axis=1)
        first = hb[:, LANES * (src + 1):LANES * (src + 1) + 1] if j > 0 else 0.0
        o_ref[:, nch + j, :, :] = jnp.where(lane1 == 0, first, back).reshape(nrow, SUBLANES, LANES)


def _filt_time(z2t, w3f, w3b, delta_rows, t_f):
    nrows = w3f.shape[0]
    seq = z2t.shape[2]
    rb = RB_FILT
    return pl.pallas_call(
        _filt_time_body,
        grid=(nrows // rb,),
        in_specs=[
            pl.BlockSpec((1, HY_FFN, seq), lambda r: (0, 0, 0)),
            pl.BlockSpec((rb, HY_FFN), lambda r: (r, 0)),
            pl.BlockSpec((rb, HY_FFN), lambda r: (r, 0)),
            pl.BlockSpec((rb, 1), lambda r: (r, 0)),
            pl.BlockSpec((1, seq), lambda r: (0, 0)),
        ],
        out_specs=pl.BlockSpec((rb // SUBLANES, 2 * seq // LANES, SUBLANES, LANES), lambda r: (r, 0, 0, 0)),
        out_shape=jax.ShapeDtypeStruct((nrows // SUBLANES, 2 * seq // LANES, SUBLANES, LANES), F32),
        compiler_params=_params(("parallel",)),
        name="hyena_filter_time",
    )(z2t, w3f, w3b, delta_rows, t_f)


def _filt_fft_body(k_ref, f1_ref, f2_ref, tr_ref, ti_ref, kr_ref, ki_ref):
    tr = tr_ref[...]
    ti = ti_ref[...]
    for g in range(k_ref.shape[0]):
        x3 = jnp.swapaxes(k_ref[g], 0, 1)
        xr, xi = _fft_fwd(x3, f1_ref[...], f2_ref[...], tr, ti)
        kr_ref[SUBLANES * g:SUBLANES * (g + 1)] = xr.reshape(SUBLANES, FFT_R, FFT_R)
        ki_ref[SUBLANES * g:SUBLANES * (g + 1)] = xi.reshape(SUBLANES, FFT_R, FFT_R)


def _filt_fft(ktile, tables):
    n8, r, _, _ = ktile.shape
    nrows = n8 * SUBLANES
    rf = RF_KF
    out_spec = pl.BlockSpec((rf, r, r), lambda i: (i, 0, 0))
    consts = [tables[k] for k in ("f1", "f2", "tr16", "ti16")]
    return pl.pallas_call(
        _filt_fft_body,
        grid=(nrows // rf,),
        in_specs=[pl.BlockSpec((rf // SUBLANES, r, SUBLANES, LANES), lambda i: (i, 0, 0, 0))] + _const_specs(consts, 1),
        out_specs=[out_spec, out_spec],
        out_shape=[jax.ShapeDtypeStruct((nrows, r, r), BF16)] * 2,
        compiler_params=_params(("parallel",)),
        name="hyena_filter_fft",
    )(ktile, *consts)


def _conv3_time(x, w):
    lane = lax.broadcasted_iota(I32, x.shape, 2)
    zero = jnp.zeros_like(x[:1])
    xm = pltpu.roll(x, 1, axis=2)
    xm = jnp.where(lane == 0, jnp.concatenate([zero, xm[:-1]], axis=0), xm)
    xp = pltpu.roll(x, LANES - 1, axis=2)
    xp = jnp.where(lane == LANES - 1, jnp.concatenate([xp[1:], zero], axis=0), xp)
    return w[0] * xm + w[1] * x + w[2] * xp + w[3]


def _hyena_body(sk_ref, v_ref, x1_ref, x2_ref, wv_ref, w1_ref, w2_ref, k0r_ref, k0i_ref, k1r_ref, k1i_ref,
                f1p_ref, f2_ref, f2c_ref, f3p_ref, tr_ref, ti_ref, o_ref, *, rc):
    ct = pl.program_id(0)
    tr = tr_ref[...]
    ti = ti_ref[...]
    np_ = v_ref.shape[2]
    nb = v_ref.shape[0]

    def load(ref, w_ref, g):
        parts = [jnp.swapaxes(_conv3_time(ref[bb, g], w_ref[:, g]), 0, 1) for bb in range(nb)]
        return jnp.concatenate(parts, axis=1)

    def long_conv(z3, kr_ref, ki_ref, g):
        r = FFT_R
        gsub = SUBLANES // HY_SPLIT
        wd = tr.dtype

        def split(m):
            return m[:, :r].reshape(gsub, r, r), m[:, r:].reshape(gsub, r, r)

        def join_t(re, im):
            return jnp.concatenate([jnp.swapaxes(re, 1, 2), jnp.swapaxes(im, 1, 2)], axis=-1).reshape(
                gsub * r, 2 * r).astype(BF16)

        subs = range(HY_SPLIT)
        zs = [z3[s * gsub:(s + 1) * gsub] for s in subs]
        a = [_dot(jnp.swapaxes(z.astype(wd), 1, 2).reshape(gsub * r, r).astype(BF16), f1p_ref[...], wd) for z in zs]
        p = []
        for m in a:
            ar, ai = split(m)
            p.append(join_t(ar * tr - ai * ti, ar * ti + ai * tr))
        x = [_dot(m, f2_ref[...], wd) for m in p]
        y = []
        for s, m in zip(subs, x):
            rows = pl.ds(g * SUBLANES + s * gsub, gsub)
            kr = kr_ref[rows].reshape(gsub * r, r)
            ki = ki_ref[rows].reshape(gsub * r, r)
            xr, xi = m[:, :r], m[:, r:]
            y.append(jnp.concatenate([xr * kr - xi * ki, xr * ki + xi * kr], axis=-1).astype(BF16))
        b = [_dot(m, f2c_ref[...], wd) for m in y]
        c = []
        for m in b:
            br, bi = split(m)
            c.append(join_t(br * tr + bi * ti, bi * tr - br * ti))
        yt = [_dot(m, f3p_ref[...]) for m in c]
        return jnp.concatenate([jnp.swapaxes(m.reshape(gsub, r, r), 1, 2) for m in yt], axis=0)

    def group(g, carry):
        cbase = ct * rc + g * SUBLANES

        def skip(z3, order):
            return jnp.stack([sk_ref[order, cbase + ci] * z3[ci] for ci in range(SUBLANES)], axis=0)

        v3 = load(v_ref, wv_ref, g)
        z1 = load(x1_ref, w1_ref, g) * (long_conv(v3, k0r_ref, k0i_ref, g) + skip(v3, 0))
        out = load(x2_ref, w2_ref, g) * (long_conv(z1, k1r_ref, k1i_ref, g) + skip(z1, 1))
        for bb in range(nb):
            o_ref[bb, g] = jnp.swapaxes(out[:, bb * np_:(bb + 1) * np_, :], 0, 1)
        return carry

    lax.fori_loop(0, rc // SUBLANES, group, 0)


def _hyena(hy5, cwt, skip, kfr, kfi, tables):
    bsz, n8, np_, _, _ = hy5.shape
    assert bsz % 2 == 0 and 2 * np_ == FFT_R
    rc = RC_HY
    r8 = rc // SUBLANES
    nct = D_HY // rc
    r = FFT_R

    def xspec(off):
        return pl.BlockSpec((2, r8, np_, SUBLANES, LANES), lambda c, b: (b, c + off * nct, 0, 0, 0))

    def wspec(off):
        return pl.BlockSpec((HY_CONV + 1, r8, SUBLANES, LANES), lambda c, b: (0, c + off * nct, 0, 0))

    def kspec(order):
        return pl.BlockSpec((rc, r, r), lambda c, b: (c + order * nct, 0, 0))

    consts = [tables[k] for k in ("f1p", "f2", "f2c", "f3p", "tr16", "ti16")]
    return pl.pallas_call(
        functools.partial(_hyena_body, rc=rc),
        grid=(nct, bsz // 2),
        in_specs=[pl.BlockSpec(memory_space=pltpu.SMEM), xspec(0), xspec(1), xspec(2), wspec(0), wspec(1), wspec(2),
                  kspec(0), kspec(0), kspec(1), kspec(1)] + _const_specs(consts, 2),
        out_specs=pl.BlockSpec((2, r8, np_, SUBLANES, LANES), lambda c, b: (b, c, 0, 0, 0)),
        out_shape=jax.ShapeDtypeStruct((bsz, D_HY // SUBLANES, np_, SUBLANES, LANES), F32),
        compiler_params=_params(("parallel", "parallel")),
        name="hyena",
    )(skip, hy5, hy5, hy5, cwt, cwt, cwt, kfr, kfi, kfr, kfi, *consts)


def _gelu_tanh(x):
    return 0.5 * x * (1.0 + jnp.tanh(math.sqrt(2.0 / math.pi) * (x + 0.044715 * (x * x * x))))


def _mix_body(hf_ref, hb_ref, gate_ref, yhy_ref, x_ref, gl_ref, gh_ref, wl_ref, wh_ref, gffn_ref,
              wrh_ref, wrl_ref, rb_ref, tri_ref, ltri_ref, xmid_ref, h_ref, routec_ref, routet_ref, cnt_ref):
    tm = x_ref.shape[1]
    y_lru = (hf_ref[0, 0] + hb_ref[0, 0]) * _gelu_tanh(gate_ref[0])
    nl = _rms(y_lru, gl_ref[...]).astype(BF16)
    gh = gh_ref[...]
    nts = []
    for j in range(tm // LANES):
        yt = yhy_ref[0, :, j, :, :].reshape(D_HY, LANES)
        nts.append((yt * lax.rsqrt(jnp.mean(yt * yt, axis=0, keepdims=True) + EPS) * gh).astype(BF16))
    nt = jnp.concatenate(nts, axis=1)
    mix = _dot(nl, wl_ref[...]) + lax.dot_general(nt, wh_ref[...], (((0,), (0,)), ((), ())),
                                                   preferred_element_type=F32)
    xmid_ref[0] = x_ref[0] + mix
    h = _rms(xmid_ref[0], gffn_ref[...])
    h_ref[...] = h.astype(BF16)

    hh, hl = _split_bf16(h)
    logits = _dot(hh, wrh_ref[...]) + _dot(hl, wrh_ref[...]) + _dot(hh, wrl_ref[...]) + rb_ref[...]
    lt = logits.T[:ROUTE_ROWS]
    row = lax.broadcasted_iota(I32, lt.shape, 0)
    big = jnp.int32(NO_INDEX)
    is_g = row < N_GROUPS
    lg = jnp.where(is_g, lt, NEG_BIG)
    mg = jnp.max(lg, axis=0, keepdims=True)
    gidx = jnp.min(jnp.where(lg == mg, row, big), axis=0, keepdims=True)
    g_p = 1.0 / jnp.sum(jnp.where(is_g, jnp.exp(lg - mg), 0.0), axis=0, keepdims=True)
    e_lo = N_GROUPS + EXPERTS_PER_GROUP * gidx
    is_e = jnp.logical_and(row >= e_lo, row < e_lo + EXPERTS_PER_GROUP)
    le = jnp.where(is_e, lt, NEG_BIG)
    m1 = jnp.max(le, axis=0, keepdims=True)
    i1 = jnp.min(jnp.where(le == m1, row, big), axis=0, keepdims=True)
    le2 = jnp.where(row == i1, NEG_BIG, le)
    m2 = jnp.max(le2, axis=0, keepdims=True)
    i2 = jnp.min(jnp.where(le2 == m2, row, big), axis=0, keepdims=True)
    ratio = jnp.exp(m2 - m1)
    gate1 = g_p / (1.0 + ratio)
    gate2 = g_p * ratio / (1.0 + ratio)
    oh1 = row == i1
    oh2 = row == i2
    oh = jnp.where(oh1, 1.0, jnp.where(oh2, 1.0, 0.0))
    before = _dot(oh.astype(BF16), tri_ref[...])
    cnt = jnp.sum(oh, axis=1, keepdims=True)
    nchunk = jnp.floor((cnt + (ROW_CHUNK - 1)) * (1.0 / ROW_CHUNK))
    run_start = ROW_CHUNK * _dot(ltri_ref[...], jnp.broadcast_to(nchunk, (ROUTE_ROWS, LANES)).astype(BF16))[:, 0:1]
    pos = run_start + before
    l0 = jnp.sum(jnp.where(oh1, pos, 0.0), axis=0, keepdims=True)
    l1 = jnp.sum(jnp.where(oh2, pos, 0.0), axis=0, keepdims=True)
    cnt_ref[0] = cnt
    route_t = jnp.concatenate([l0, l1, gate1, gate2, jnp.zeros((SUBLANES - 4, tm), F32)], axis=0)
    routet_ref[0] = route_t
    routec_ref[...] = jnp.concatenate([route_t, jnp.zeros((LANES - SUBLANES, tm), F32)], axis=0).T[:, :SUBLANES]


def _mix_router(h4, lru, yhy5, x, gl, gh, wl, wh, gffn, wrh, wrl, rb, tri, ltri):
    bsz, seq, d = x.shape
    tm = TM_MIX
    nt = seq // tm
    n_tok = bsz * seq
    c2 = lambda b, i: (0, 0)
    return pl.pallas_call(
        _mix_body,
        grid=(bsz, nt),
        in_specs=[
            pl.BlockSpec((1, 1, tm, D_LRU), lambda b, i: (0, b, i, 0)),
            pl.BlockSpec((1, 1, tm, D_LRU), lambda b, i: (1, b, i, 0)),
            pl.BlockSpec((1, tm, D_LRU), lambda b, i: (b, i, 1)),
            pl.BlockSpec((1, D_HY // SUBLANES, tm // LANES, SUBLANES, LANES), lambda b, i: (b, 0, i, 0, 0)),
            pl.BlockSpec((1, tm, d), lambda b, i: (b, i, 0)),
            pl.BlockSpec((1, D_LRU), c2),
            pl.BlockSpec((D_HY, 1), c2),
            pl.BlockSpec((D_LRU, d), c2),
            pl.BlockSpec((D_HY, d), c2),
            pl.BlockSpec((1, d), c2),
            pl.BlockSpec((d, LANES), c2),
            pl.BlockSpec((d, LANES), c2),
            pl.BlockSpec((1, LANES), c2),
            pl.BlockSpec((tm, tm), c2),
            pl.BlockSpec((ROUTE_ROWS, ROUTE_ROWS), c2),
        ],
        out_specs=[
            pl.BlockSpec((1, tm, d), lambda b, i: (b, i, 0)),
            pl.BlockSpec((tm, d), lambda b, i: (b * nt + i, 0)),
            pl.BlockSpec((tm, SUBLANES), lambda b, i: (b * nt + i, 0)),
            pl.BlockSpec((1, SUBLANES, tm), lambda b, i: (b * nt + i, 0, 0)),
            pl.BlockSpec((1, ROUTE_ROWS, 1), lambda b, i: (b * nt + i, 0, 0)),
        ],
        out_shape=[
            jax.ShapeDtypeStruct((bsz, seq, d), F32),
            jax.ShapeDtypeStruct((n_tok, d), BF16),
            jax.ShapeDtypeStruct((n_tok, SUBLANES), F32),
            jax.ShapeDtypeStruct((n_tok // tm, SUBLANES, tm), F32),
            jax.ShapeDtypeStruct((n_tok // tm, ROUTE_ROWS, 1), F32),
        ],
        compiler_params=_params(("parallel", "parallel")),
        name="mix_router",
    )(h4, h4, lru, yhy5, x, gl, gh, wl, wh, gffn, wrh, wrl, rb, tri, ltri)


def _start_chunks(i, tch_ref, dst_ref, make_copy):
    n = tch_ref[i]

    def start(k, priority):
        make_copy(k * ROW_CHUNK, dst_ref[i * MAX_CHUNKS + k], ROW_CHUNK).start(priority=priority)

    def per_pair(k2, c):
        start(2 * k2, 0)
        start(2 * k2 + 1, 1)
        return c

    lax.fori_loop(0, n // 2, per_pair, 0)

    @pl.when(n % 2 == 1)
    def _():
        start(n - 1, 0)


def _wait_chunks(count, make_copy):
    nbits = MAX_CHUNKS.bit_length()
    for j in reversed(range(nbits)):
        @pl.when((count >> j) & 1 == 1)
        def _():
            make_copy(0, 0, ROW_CHUNK << j).wait()


def _dispatch_body(dst_ref, tch_ref, tn_ref, ts_ref, nu_ref, h_ref, routet_ref, xb_ref,
                   buf, zbuf, sem, zsem):
    i = pl.program_id(0)
    n_tiles = pl.num_programs(0)
    tm = h_ref.shape[0]
    lbuf = buf.shape[1]
    n_blocks = xb_ref.shape[0] // MOE_BLK
    slot = i % 2

    def aligned(r):
        return pl.ds(pl.multiple_of(r, ROW_CHUNK), ROW_CHUNK)

    def zero_chunk(r):
        return pltpu.make_async_copy(zbuf.at[pl.ds(0, ROW_CHUNK)], xb_ref.at[aligned(r)], zsem)

    def zero_block(j):
        return pltpu.make_async_copy(zbuf, xb_ref.at[pl.ds(pl.multiple_of(j * MOE_BLK, MOE_BLK), MOE_BLK)], zsem)

    def for_each_zero_copy(act):
        def per_expert(e, c):
            def per_chunk(k, c2):
                act(zero_chunk(ts_ref[e] + k * ROW_CHUNK))
                return c2

            lax.fori_loop(0, tn_ref[e], per_chunk, 0)
            return c

        lax.fori_loop(0, N_EXPERTS, per_expert, 0)

        def per_block(j, c):
            act(zero_block(j))
            return c

        lax.fori_loop(nu_ref[0], n_blocks, per_block, 0)

    @pl.when(i == 0)
    def _():
        zbuf[...] = jnp.zeros_like(zbuf)
        for_each_zero_copy(lambda cp: cp.start())

    l0t = routet_ref[0, 0:1, :].astype(I32)
    l1t = routet_ref[0, 1:2, :].astype(I32)
    row = lax.broadcasted_iota(I32, (lbuf, tm), 0)
    perm = jnp.where(row == l0t, 1.0, jnp.where(row == l1t, 1.0, 0.0)).astype(BF16)
    srt = _dot(perm, h_ref[...], BF16)

    def copy_from(s):
        def make_copy(src_row, dst_row, rows):
            return pltpu.make_async_copy(buf.at[s, pl.ds(pl.multiple_of(src_row, ROW_CHUNK), rows)],
                                         xb_ref.at[pl.ds(pl.multiple_of(dst_row, ROW_CHUNK), rows)], sem.at[s])
        return make_copy

    @pl.when(i >= 2)
    def _():
        _wait_chunks(tch_ref[i - 2], copy_from(slot))

    buf[slot] = srt
    _start_chunks(i, tch_ref, dst_ref, copy_from(slot))

    @pl.when(i == n_tiles - 1)
    def _():
        @pl.when(i >= 1)
        def _():
            _wait_chunks(tch_ref[i - 1], copy_from(1 - slot))

        _wait_chunks(tch_ref[i], copy_from(slot))
        for_each_zero_copy(lambda cp: cp.wait())


def _dispatch(chunk_dst, tile_nch, tail_n, tail_s, n_used, h, route_t, n_slots):
    n_tok, d = h.shape
    tm = TM_MIX
    grid_spec = pltpu.PrefetchScalarGridSpec(
        num_scalar_prefetch=5,
        grid=(n_tok // tm,),
        in_specs=[
            pl.BlockSpec((tm, d), lambda i, *_: (i, 0)),
            pl.BlockSpec((1, SUBLANES, tm), lambda i, *_: (i, 0, 0)),
        ],
        out_specs=pl.BlockSpec(memory_space=pl.ANY),
        scratch_shapes=[pltpu.VMEM((2, LOCAL_ROWS, d), BF16), pltpu.VMEM((MOE_BLK, d), BF16),
                        pltpu.SemaphoreType.DMA((2,)), pltpu.SemaphoreType.DMA(())],
    )
    return pl.pallas_call(
        _dispatch_body,
        grid_spec=grid_spec,
        out_shape=jax.ShapeDtypeStruct((n_slots, d), BF16),
        compiler_params=_params(("arbitrary",)),
        name="moe_dispatch",
    )(chunk_dst, tile_nch, tail_n, tail_s, n_used, h, route_t)


def _expert_body(be_ref, fl_ref, sw_ref, nu_ref, xb_ref, w1_ref, w3_ref, w2_ref, yb_ref,
                 w1_sc, w3_sc, w2_sc, w1_ring, w3_ring, w2_ring, sem):
    j = pl.program_id(0)
    n_used = nu_ref[0]
    used = j < n_used

    def weight_copies(jj):
        e = be_ref[jj]
        s = sw_ref[jj] % WEIGHT_SLOTS
        return [pltpu.make_async_copy(w_ref.at[e], ring.at[s], sem.at[s])
                for w_ref, ring in ((w1_ref, w1_ring), (w3_ref, w3_ring), (w2_ref, w2_ring))]

    def fetch_if_first(jj):
        @pl.when(jnp.logical_and(jj < n_used, fl_ref[jnp.minimum(jj, pl.num_programs(0) - 1)] == 1))
        def _():
            for cp in weight_copies(jj):
                cp.start()

    @pl.when(j == 0)
    def _():
        for jj in range(WEIGHT_LOOKAHEAD):
            fetch_if_first(jnp.int32(jj))

    fetch_if_first(j + WEIGHT_LOOKAHEAD)

    @pl.when(jnp.logical_and(used, fl_ref[j] == 1))
    def _():
        for cp in weight_copies(j):
            cp.wait()
        s = sw_ref[j] % WEIGHT_SLOTS
        w1_sc[...] = w1_ring[s].astype(BF16)
        w3_sc[...] = w3_ring[s].astype(BF16)
        w2_sc[...] = w2_ring[s].astype(BF16)

    @pl.when(used)
    def _():
        xb = xb_ref[...]
        a = _dot(xb, w1_sc[...])
        b = _dot(xb, w3_sc[...])
        act = (a * _sigmoid(a) * b).astype(BF16)
        yb_ref[...] = _dot(act, w2_sc[...], BF16)

    @pl.when(jnp.logical_not(used))
    def _():
        yb_ref[...] = jnp.zeros_like(yb_ref)


def _experts(block_e, first_flag, n_used, xb, w1, w3, w2):
    n_slots, d = xb.shape
    blk = MOE_BLK
    de = w1.shape[2]
    switch_no = jnp.cumsum(first_flag) - 1
    hbm = pl.BlockSpec(memory_space=pl.ANY)
    grid_spec = pltpu.PrefetchScalarGridSpec(
        num_scalar_prefetch=4,
        grid=(n_slots // blk,),
        in_specs=[pl.BlockSpec((blk, d), lambda j, be, fl, sw, nu: (jnp.minimum(j, nu[0] - 1), 0)), hbm, hbm, hbm],
        out_specs=pl.BlockSpec((blk, d), lambda j, *_: (j, 0)),
        scratch_shapes=[pltpu.VMEM((d, de), BF16), pltpu.VMEM((d, de), BF16), pltpu.VMEM((de, d), BF16),
                        pltpu.VMEM((WEIGHT_SLOTS, d, de), F32), pltpu.VMEM((WEIGHT_SLOTS, d, de), F32),
                        pltpu.VMEM((WEIGHT_SLOTS, de, d), F32), pltpu.SemaphoreType.DMA((WEIGHT_SLOTS,))],
    )
    return pl.pallas_call(
        _expert_body,
        grid_spec=grid_spec,
        out_shape=jax.ShapeDtypeStruct((n_slots, d), BF16),
        compiler_params=_params(("arbitrary",)),
        name="moe_experts",
    )(block_e, first_flag, switch_no.astype(I32), n_used, xb, w1, w3, w2)


def _combine_body(dst_ref, tch_ref, xmid_ref, route_ref, g_ref, yb_ref, o_ref, buf, sem):
    i = pl.program_id(0)
    n_tiles = pl.num_programs(0)
    tm = xmid_ref.shape[0]
    lbuf = buf.shape[1]
    slot = i % 2

    def copy_into(s):
        def make_copy(local_row, slot_row, rows):
            return pltpu.make_async_copy(yb_ref.at[pl.ds(pl.multiple_of(slot_row, ROW_CHUNK), rows)],
                                         buf.at[s, pl.ds(pl.multiple_of(local_row, ROW_CHUNK), rows)], sem.at[s])
        return make_copy

    @pl.when(i == 0)
    def _():
        buf[...] = jnp.zeros_like(buf)
        _start_chunks(i, tch_ref, dst_ref, copy_into(slot))

    @pl.when(i + 1 < n_tiles)
    def _():
        _start_chunks(i + 1, tch_ref, dst_ref, copy_into(1 - slot))

    route = route_ref[...]
    l0 = route[:, 0:1].astype(I32)
    l1 = route[:, 1:2].astype(I32)
    lane = lax.broadcasted_iota(I32, (tm, lbuf), 1)
    gmat = jnp.where(lane == l0, route[:, 2:3], jnp.where(lane == l1, route[:, 3:4], 0.0)).astype(BF16)
    _wait_chunks(tch_ref[i], copy_into(slot))
    y = xmid_ref[...] + _dot(gmat, buf[slot])
    o_ref[...] = _rms(y, g_ref[...])


def _combine(chunk_dst, tile_nch, xmid, route, g, yb):
    n_tok, d = xmid.shape
    tm = TM_MIX
    grid_spec = pltpu.PrefetchScalarGridSpec(
        num_scalar_prefetch=2,
        grid=(n_tok // tm,),
        in_specs=[
            pl.BlockSpec((tm, d), lambda i, *_: (i, 0)),
            pl.BlockSpec((tm, SUBLANES), lambda i, *_: (i, 0)),
            pl.BlockSpec((1, d), lambda i, *_: (0, 0)),
            pl.BlockSpec(memory_space=pl.ANY),
        ],
        out_specs=pl.BlockSpec((tm, d), lambda i, *_: (i, 0)),
        scratch_shapes=[pltpu.VMEM((2, LOCAL_ROWS, d), BF16), pltpu.SemaphoreType.DMA((2,))],
    )
    return pl.pallas_call(
        _combine_body,
        grid_spec=grid_spec,
        out_shape=jax.ShapeDtypeStruct((n_tok, d), F32),
        compiler_params=_params(("arbitrary",)),
        name="moe_combine",
    )(chunk_dst, tile_nch, xmid, route, g, yb)


def _filter_features(seq):
    pos = jnp.arange(seq, dtype=F32)
    t = jnp.linspace(0.0, 1.0, seq, dtype=F32)
    w = (2.0 * math.pi / seq) * pos
    bands = jnp.linspace(1e-4, HY_BANDS - 1, HY_BANDS, dtype=F32)
    ang = w[:, None] * bands[None, :]
    feats = jnp.concatenate([t[:, None], jnp.cos(ang), jnp.sin(ang)], axis=-1)
    featst = jnp.pad(feats, ((0, 0), (0, HY_EMB_PAD - HY_EMB))).T[None]
    return featst, t[None, :]


def _gate_weights(wr, wi):
    half = D_LRU // 2
    hph = half // LRU_HEAD_DIM
    eye = 0.5 * jnp.eye(hph, dtype=wr.dtype)[:, None, :, None]

    def block_diag(w):
        return (eye * w[:, :, None, :]).reshape(half, half)

    out = [jnp.concatenate([block_diag(wr[hh * hph:(hh + 1) * hph]), block_diag(wi[hh * hph:(hh + 1) * hph])], axis=1)
           for hh in range(2)]
    return jnp.stack(out, axis=0).astype(BF16)


def _layer(x, l, p):
    bsz, seq, d = x.shape
    n_tok = bsz * seq
    row = lambda a: a.reshape(1, -1).astype(F32)
    col = lambda a: a.reshape(-1, 1).astype(F32)

    w_in = p["w_in"][l]
    wl = w_in[:, :2 * D_LRU].astype(BF16)
    wht = w_in[:, 2 * D_LRU:].T.astype(BF16)
    lru, hy5 = _inproj(x, row(p["norm_mix_g"][l]), wl, wht)

    wg = jnp.stack([_gate_weights(p["lru_wr_f"][l], p["lru_wi_f"][l]),
                    _gate_weights(p["lru_wr_b"][l], p["lru_wi_b"][l])], axis=0)
    stack2 = lambda a, b: jnp.stack([a.reshape(1, -1), b.reshape(1, -1)], axis=0).astype(F32)
    h4 = _lru(lru, p["lru_conv_w"][l], row(p["lru_conv_b"][l]), wg,
              0.5 * stack2(p["lru_br_f"][l], p["lru_br_b"][l]), 0.5 * stack2(p["lru_bi_f"][l], p["lru_bi_b"][l]),
              stack2(p["lru_lambda_f"][l], p["lru_lambda_b"][l]))

    tables = _fft_tables()
    featst, t_f = _filter_features(seq)
    w1t = jnp.pad(p["hy_filt_w1"][l].T, ((0, 0), (0, HY_EMB_PAD - HY_EMB)))
    z2t = _filt_mlp(featst, w1t, col(p["hy_filt_b1"][l]), col(p["hy_filt_freq1"][l]),
                    p["hy_filt_w2"][l].T, col(p["hy_filt_b2"][l]), col(p["hy_filt_freq2"][l]))
    w3 = p["hy_filt_w3"][l]
    nfr = HY_ORDER * D_HY
    deltas = jnp.abs(jnp.linspace(math.log(HY_DECAY_TARGET) / HY_SLOW_DECAY_PCT,
                                  math.log(HY_DECAY_TARGET) / HY_FAST_DECAY_PCT, D_HY, dtype=F32))
    ktile = _filt_time(z2t, w3[:, :nfr].T, w3[:, nfr:].T, col(jnp.tile(deltas, HY_ORDER)), t_f)
    kfr, kfi = _filt_fft(ktile, tables)
    cwt = jnp.concatenate([p["hy_conv_w"][l], p["hy_conv_b"][l][None, :]], axis=0)
    cwt = jnp.broadcast_to(cwt.reshape(HY_CONV + 1, -1, SUBLANES, 1), (HY_CONV + 1, cwt.shape[1] // SUBLANES, SUBLANES, LANES))
    yhy5 = _hyena(hy5, cwt, p["hy_skip"][l], kfr, kfi, tables)

    w_out = p["w_out"][l]
    wr_cat = jnp.concatenate([p["router_group_w"][l], p["router_expert_w"][l]], axis=1)
    wr_cat = jnp.pad(wr_cat, ((0, 0), (0, LANES - wr_cat.shape[1])))
    wrh, wrl = _split_bf16(wr_cat)
    rb = jnp.concatenate([p["router_group_b"][l], p["router_expert_b"][l]])
    rb = jnp.pad(rb, (0, LANES - rb.shape[0])).reshape(1, LANES)
    earlier = lambda n: (jnp.arange(n)[:, None] < jnp.arange(n)[None, :]).astype(BF16)
    xmid, h, route, route_t, tile_cnt = _mix_router(
        h4, lru, yhy5, x, row(p["grp_norm_lru_g"][l]), col(p["grp_norm_hy_g"][l]),
        w_out[:D_LRU].astype(BF16), w_out[D_LRU:].astype(BF16), row(p["norm_ffn_g"][l]), wrh, wrl, rb,
        earlier(TM_MIX), earlier(ROUTE_ROWS).T)

    blk = MOE_BLK
    ch = ROW_CHUNK
    n_tiles = n_tok // TM_MIX
    n_blocks = (2 * n_tok + n_tiles * N_EXPERTS * (ch - 1)) // blk + N_EXPERTS
    cnt = tile_cnt[:, N_GROUPS:N_GROUPS + N_EXPERTS, 0].astype(I32)
    run = (cnt + ch - 1) // ch * ch
    tot_e = jnp.sum(run, axis=0)
    padded_e = (tot_e + blk - 1) // blk * blk
    pad_end = jnp.cumsum(padded_e)
    pad_start = pad_end - padded_e
    gstart = pad_start[None, :] + jnp.cumsum(run, axis=0) - run
    lstart = jnp.cumsum(run, axis=1) - run
    flat = lambda a: a.reshape(-1).astype(I32)
    tile_nch = flat(jnp.sum(run, axis=1) // ch)
    nch_end = jnp.cumsum(run // ch, axis=1)
    k = jnp.arange(MAX_CHUNKS, dtype=I32)
    base = gstart - lstart
    step = base[:, 1:] - base[:, :-1]
    passed = (k[None, :, None] >= nch_end[:, None, :-1]).astype(I32)
    chunk_dst = flat(base[:, :1] + jnp.sum(passed * step[:, None, :], axis=2) + k[None, :] * ch)
    tail_n = flat((padded_e - tot_e) // ch)
    tail_s = flat(pad_start + tot_e)
    block_start = jnp.arange(n_blocks, dtype=I32) * blk
    block_e = jnp.minimum(jnp.sum(block_start[:, None] >= pad_end[None, :], axis=1), N_EXPERTS - 1).astype(I32)
    first_flag = jnp.concatenate([jnp.ones((1,), I32), (block_e[1:] != block_e[:-1]).astype(I32)])
    n_used = (pad_end[-1:] // blk).astype(I32)

    xb = _dispatch(chunk_dst, tile_nch, tail_n, tail_s, n_used, h, route_t, n_blocks * blk)
    yb = _experts(block_e, first_flag, n_used, xb, p["exp_w1"][l], p["exp_w3"][l], p["exp_w2"][l])
    return xmid.reshape(n_tok, d), route, (chunk_dst, tile_nch), yb


def kernel(x, norm_mix_g, w_in, lru_conv_w, lru_conv_b, lru_wr_f, lru_br_f, lru_wi_f, lru_bi_f, lru_lambda_f, lru_wr_b, lru_br_b, lru_wi_b, lru_bi_b, lru_lambda_b, hy_conv_w, hy_conv_b, hy_filt_w1, hy_filt_b1, hy_filt_freq1, hy_filt_w2, hy_filt_b2, hy_filt_freq2, hy_filt_w3, hy_skip, grp_norm_lru_g, grp_norm_hy_g, w_out, norm_ffn_g, router_group_w, router_group_b, router_expert_w, router_expert_b, exp_w1, exp_w3, exp_w2, norm_final_g):
    p = dict(locals())
    bsz, seq, d = x.shape
    assert d == D_MODEL and 2 * seq == FFT_R * FFT_R and w_in.shape[0] == 1
    xmid, route, (chunk_dst, tile_nch), yb = _layer(x, 0, p)
    out = _combine(chunk_dst, tile_nch, xmid, route, norm_final_g.reshape(1, d), yb)
    return out.reshape(bsz, seq, d)
```
